```python
import math
import jax, jax.numpy as jnp
from jax import lax
import numpy as np

D_MODEL = 1024
BATCH = 8
SEQ = 16384
DEPTH = 4

CHUNK = 64
N_MIXERS = 2
N_A = (DEPTH + 1) // 2
N_B = DEPTH // 2
N_HEADS_A = 16
HEAD_DIM_A = D_MODEL // N_HEADS_A
Q_BLOCK = 128
D_RNN = D_MODEL
N_BLOCKS_B = 8
BLOCK_B = D_RNN // N_BLOCKS_B
CONV_B = 4
LRU_C = 8.0
D_FF = 2816
CONV_F = 3
D_PLE = 256
LN_EPS = 1e-5
ALPHA = (2.0 * DEPTH) ** 0.25
BETA = (8.0 * DEPTH) ** -0.25

kernel_name = "fox_rglru_deepnorm_convffn_hybrid"


def layer_norm(x, g, b):
    xf = x.astype(jnp.float32)
    mu = jnp.mean(xf, axis=-1, keepdims=True)
    xc = xf - mu
    var = jnp.mean(xc * xc, axis=-1, keepdims=True)
    y = xc * lax.rsqrt(var + LN_EPS) * g.astype(jnp.float32) + b.astype(jnp.float32)
    return y.astype(x.dtype)


def causal_dwconv(x, w, b):
    k, c = w.shape
    y = lax.conv_general_dilated(
        x, w[:, None, :].astype(x.dtype), window_strides=(1,), padding=[(k - 1, 0)],
        dimension_numbers=("NWC", "WIO", "NWC"), feature_group_count=c)
    return y + b.astype(x.dtype)


def forgetting_attention(x, w_in, b_f, w_out):
    bsz, s, _ = x.shape
    h, dh = N_HEADS_A, HEAD_DIM_A
    proj = x @ w_in
    q, k, v, fg = jnp.split(proj, [D_MODEL, 2 * D_MODEL, 3 * D_MODEL], axis=-1)
    q = q.reshape(bsz, s, h, dh).transpose(0, 2, 1, 3)
    k = k.reshape(bsz, s, h, dh).transpose(0, 2, 1, 3)
    v = v.reshape(bsz, s, h, dh).transpose(0, 2, 1, 3)
    log_f = jax.nn.log_sigmoid((fg + b_f).astype(jnp.float32))
    c = jnp.cumsum(log_f, axis=1).transpose(0, 2, 1)
    n_blk = s // Q_BLOCK
    qb = q.reshape(bsz, h, n_blk, Q_BLOCK, dh).transpose(2, 0, 1, 3, 4)
    cb = c.reshape(bsz, h, n_blk, Q_BLOCK).transpose(2, 0, 1, 3)
    pos_k = jnp.arange(s)
    scale = 1.0 / math.sqrt(dh)

    def one_block(args):
        q_i, c_i, i = args
        logits = jnp.einsum("bhqd,bhkd->bhqk", q_i, k).astype(jnp.float32) * scale
        logits = logits + c_i[..., None] - c[:, :, None, :]
        pos_q = i * Q_BLOCK + jnp.arange(Q_BLOCK)
        mask = pos_k[None, :] <= pos_q[:, None]
        logits = jnp.where(mask, logits, -jnp.inf)
        probs = jax.nn.softmax(logits, axis=-1)
        return jnp.einsum("bhqk,bhkd->bhqd", probs.astype(v.dtype), v)

    o = lax.map(one_block, (qb, cb, jnp.arange(n_blk)))
    o = o.transpose(1, 0, 3, 2, 4).reshape(bsz, s, D_MODEL)
    return o @ w_out


def rglru_block(x, w_in, conv_w, conv_b, w_a, b_a, w_i, b_i, lam, w_out):
    bsz, s, _ = x.shape
    proj = x @ w_in
    xb, gb = jnp.split(proj, [D_RNN], axis=-1)
    xb = causal_dwconv(xb, conv_w, conv_b)
    xh = xb.reshape(bsz, s, N_BLOCKS_B, BLOCK_B)
    r = jax.nn.sigmoid(jnp.einsum("bsnc,ncd->bsnd", xh, w_a) + b_a).reshape(bsz, s, D_RNN)
    ig = jax.nn.sigmoid(jnp.einsum("bsnc,ncd->bsnd", xh, w_i) + b_i).reshape(bsz, s, D_RNN)
    log_a = -LRU_C * r.astype(jnp.float32) * jax.nn.softplus(-lam.astype(jnp.float32))
    a = jnp.exp(log_a)
    mult = jnp.sqrt(-jnp.expm1(2.0 * log_a))
    u = mult * (ig * xb).astype(jnp.float32)

    def combine(left, right):
        a1, b1 = left
        a2, b2 = right
        return a1 * a2, a2 * b1 + b2

    _, hseq = lax.associative_scan(combine, (a, u), axis=1)
    y = hseq.astype(x.dtype) * jax.nn.gelu(gb)
    return y @ w_out


def conv_ffn(x, w_up, conv_w, conv_b, w_down):
    hdn = x @ w_up
    hdn = causal_dwconv(hdn, conv_w, conv_b)
    val, gate = jnp.split(hdn, [D_FF], axis=-1)
    return (jax.nn.gelu(gate) * val) @ w_down


def _fwd_setup_inputs(seed: int = 0) -> dict:
    key = jax.random.key(seed)
    ks = jax.random.split(key, 32)
    f32 = jnp.float32
    nrm = lambda k, shape, sc: jax.random.normal(k, shape, f32) * sc
    x = nrm(ks[0], (BATCH, SEQ, D_MODEL), 1.0)
    p = nrm(ks[1], (DEPTH, BATCH, SEQ, D_PLE), 1.0)
    col_scale = jnp.concatenate([
        jnp.ones((2 * D_MODEL,), f32),
        jnp.full((D_MODEL,), BETA, f32),
        jnp.full((N_HEADS_A,), 0.1, f32)])
    a_w_in = nrm(ks[2], (N_A, D_MODEL, 3 * D_MODEL + N_HEADS_A), D_MODEL ** -0.5) * col_scale
    a_b_f = jnp.linspace(1.0, 6.0, N_HEADS_A, dtype=f32)[None, :] + nrm(ks[3], (N_A, N_HEADS_A), 0.1)
    a_w_out = nrm(ks[4], (N_A, D_MODEL, D_MODEL), D_MODEL ** -0.5 * BETA)
    b_w_in = nrm(ks[5], (N_B, D_MODEL, 2 * D_RNN), D_MODEL ** -0.5)
    b_conv_w = nrm(ks[6], (N_B, CONV_B, D_RNN), CONV_B ** -0.5)
    b_conv_b = nrm(ks[7], (N_B, D_RNN), 0.02)
    b_w_a = nrm(ks[8], (N_B, N_BLOCKS_B, BLOCK_B, BLOCK_B), BLOCK_B ** -0.5)
    b_b_a = nrm(ks[9], (N_B, N_BLOCKS_B, BLOCK_B), 0.02)
    b_w_i = nrm(ks[10], (N_B, N_BLOCKS_B, BLOCK_B, BLOCK_B), BLOCK_B ** -0.5)
    b_b_i = nrm(ks[11], (N_B, N_BLOCKS_B, BLOCK_B), 0.02)
    a_pow_c = jax.random.uniform(ks[12], (N_B, D_RNN), f32, 0.9, 0.999)
    a_base = a_pow_c ** (1.0 / LRU_C)
    b_lam = jnp.log(a_base) - jnp.log1p(-a_base)
    b_w_out = nrm(ks[13], (N_B, D_RNN, D_MODEL), D_RNN ** -0.5 * BETA)
    f_w_up = nrm(ks[14], (DEPTH, D_MODEL, 2 * D_FF), D_MODEL ** -0.5)
    f_conv_w = nrm(ks[15], (DEPTH, CONV_F, 2 * D_FF), CONV_F ** -0.5)
    f_conv_b = nrm(ks[16], (DEPTH, 2 * D_FF), 0.02)
    f_w_down = nrm(ks[17], (DEPTH, D_FF, D_MODEL), D_FF ** -0.5 * BETA)
    ln1_g = 1.0 + nrm(ks[18], (DEPTH, D_MODEL), 0.02)
    ln1_b = nrm(ks[19], (DEPTH, D_MODEL), 0.02)
    ln2_g = 1.0 + nrm(ks[20], (DEPTH, D_MODEL), 0.02)
    ln2_b = nrm(ks[21], (DEPTH, D_MODEL), 0.02)
    ple_w = nrm(ks[22], (DEPTH, D_PLE, D_MODEL), D_PLE ** -0.5 * BETA)
    ple_gate_w = nrm(ks[23], (DEPTH, D_MODEL, D_MODEL), D_MODEL ** -0.5)
    ple_gate_b = nrm(ks[24], (DEPTH, D_MODEL), 0.02)
    return {"x": x, "p": p,
            "a_w_in": a_w_in, "a_b_f": a_b_f, "a_w_out": a_w_out,
            "b_w_in": b_w_in, "b_conv_w": b_conv_w, "b_conv_b": b_conv_b,
            "b_w_a": b_w_a, "b_b_a": b_b_a, "b_w_i": b_w_i, "b_b_i": b_b_i,
            "b_lam": b_lam, "b_w_out": b_w_out,
            "f_w_up": f_w_up, "f_conv_w": f_conv_w, "f_conv_b": f_conv_b, "f_w_down": f_w_down,
            "ln1_g": ln1_g, "ln1_b": ln1_b, "ln2_g": ln2_g, "ln2_b": ln2_b,
            "ple_w": ple_w, "ple_gate_w": ple_gate_w, "ple_gate_b": ple_gate_b}


def _fwd_reference(x, p, a_w_in, a_b_f, a_w_out,
              b_w_in, b_conv_w, b_conv_b, b_w_a, b_b_a, b_w_i, b_b_i, b_lam, b_w_out,
              f_w_up, f_conv_w, f_conv_b, f_w_down,
              ln1_g, ln1_b, ln2_g, ln2_b,
              ple_w, ple_gate_w, ple_gate_b):
    for i in range(DEPTH):
        j = i // N_MIXERS
        if i % N_MIXERS == 0:
            m = forgetting_attention(x, a_w_in[j], a_b_f[j], a_w_out[j])
        else:
            m = rglru_block(x, b_w_in[j], b_conv_w[j], b_conv_b[j], b_w_a[j], b_b_a[j],
                            b_w_i[j], b_b_i[j], b_lam[j], b_w_out[j])
        x = layer_norm(ALPHA * x + m, ln1_g[i], ln1_b[i])
        ff = conv_ffn(x, f_w_up[i], f_conv_w[i], f_conv_b[i], f_w_down[i])
        x = layer_norm(ALPHA * x + ff, ln2_g[i], ln2_b[i])
        gate = jax.nn.sigmoid(x @ ple_gate_w[i] + ple_gate_b[i])
        x = x + gate * (p[i] @ ple_w[i])
    return x


import jax as _jax
import jax.numpy as _jnp

TWIN_FORMAT = 'train_step'
FWD_PARAMS = ['x', 'p', 'a_w_in', 'a_b_f', 'a_w_out', 'b_w_in', 'b_conv_w', 'b_conv_b', 'b_w_a', 'b_b_a', 'b_w_i', 'b_b_i', 'b_lam', 'b_w_out', 'f_w_up', 'f_conv_w', 'f_conv_b', 'f_w_down', 'ln1_g', 'ln1_b', 'ln2_g', 'ln2_b', 'ple_w', 'ple_gate_w', 'ple_gate_b']
TWIN_WEIGHTS = ['a_w_in', 'a_b_f', 'a_w_out', 'b_w_in', 'b_conv_w', 'b_conv_b', 'b_w_a', 'b_b_a', 'b_w_i', 'b_b_i', 'b_lam', 'b_w_out', 'f_w_up', 'f_conv_w', 'f_conv_b', 'f_w_down', 'ln1_g', 'ln1_b', 'ln2_g', 'ln2_b', 'ple_w', 'ple_gate_w', 'ple_gate_b']
TWIN_DIFF_INPUT = 'x'
TWIN_INPUTS = ['x', 'p', 'a_w_in', 'a_b_f', 'a_w_out', 'b_w_in', 'b_conv_w', 'b_conv_b', 'b_w_a', 'b_b_a', 'b_w_i', 'b_b_i', 'b_lam', 'b_w_out', 'f_w_up', 'f_conv_w', 'f_conv_b', 'f_w_down', 'ln1_g', 'ln1_b', 'ln2_g', 'ln2_b', 'ple_w', 'ple_gate_w', 'ple_gate_b', 'loss_target', 'm_a_w_in', 'm_a_b_f', 'm_a_w_out', 'm_b_w_in', 'm_b_conv_w', 'm_b_conv_b', 'm_b_w_a', 'm_b_b_a', 'm_b_w_i', 'm_b_b_i', 'm_b_lam', 'm_b_w_out', 'm_f_w_up', 'm_f_conv_w', 'm_f_conv_b', 'm_f_w_down', 'm_ln1_g', 'm_ln1_b', 'm_ln2_g', 'm_ln2_b', 'm_ple_w', 'm_ple_gate_w', 'm_ple_gate_b', 'v_a_w_in', 'v_a_b_f', 'v_a_w_out', 'v_b_w_in', 'v_b_conv_w', 'v_b_conv_b', 'v_b_w_a', 'v_b_b_a', 'v_b_w_i', 'v_b_b_i', 'v_b_lam', 'v_b_w_out', 'v_f_w_up', 'v_f_conv_w', 'v_f_conv_b', 'v_f_w_down', 'v_ln1_g', 'v_ln1_b', 'v_ln2_g', 'v_ln2_b', 'v_ple_w', 'v_ple_gate_w', 'v_ple_gate_b']
TWIN_OUTPUTS = ['loss', 'grad_x', 'grad_a_w_in', 'grad_a_b_f', 'grad_a_w_out', 'grad_b_w_in', 'grad_b_conv_w', 'grad_b_conv_b', 'grad_b_w_a', 'grad_b_b_a', 'grad_b_w_i', 'grad_b_b_i', 'grad_b_lam', 'grad_b_w_out', 'grad_f_w_up', 'grad_f_conv_w', 'grad_f_conv_b', 'grad_f_w_down', 'grad_ln1_g', 'grad_ln1_b', 'grad_ln2_g', 'grad_ln2_b', 'grad_ple_w', 'grad_ple_gate_w', 'grad_ple_gate_b', 'delta_a_w_in', 'delta_a_b_f', 'delta_a_w_out', 'delta_b_w_in', 'delta_b_conv_w', 'delta_b_conv_b', 'delta_b_w_a', 'delta_b_b_a', 'delta_b_w_i', 'delta_b_b_i', 'delta_b_lam', 'delta_b_w_out', 'delta_f_w_up', 'delta_f_conv_w', 'delta_f_conv_b', 'delta_f_w_down', 'delta_ln1_g', 'delta_ln1_b', 'delta_ln2_g', 'delta_ln2_b', 'delta_ple_w', 'delta_ple_gate_w', 'delta_ple_gate_b', 'new_m_a_w_in', 'new_m_a_b_f', 'new_m_a_w_out', 'new_m_b_w_in', 'new_m_b_conv_w', 'new_m_b_conv_b', 'new_m_b_w_a', 'new_m_b_b_a', 'new_m_b_w_i', 'new_m_b_b_i', 'new_m_b_lam', 'new_m_b_w_out', 'new_m_f_w_up', 'new_m_f_conv_w', 'new_m_f_conv_b', 'new_m_f_w_down', 'new_m_ln1_g', 'new_m_ln1_b', 'new_m_ln2_g', 'new_m_ln2_b', 'new_m_ple_w', 'new_m_ple_gate_w', 'new_m_ple_gate_b', 'new_v_a_w_in', 'new_v_a_b_f', 'new_v_a_w_out', 'new_v_b_w_in', 'new_v_b_conv_w', 'new_v_b_conv_b', 'new_v_b_w_a', 'new_v_b_b_a', 'new_v_b_w_i', 'new_v_b_b_i', 'new_v_b_lam', 'new_v_b_w_out', 'new_v_f_w_up', 'new_v_f_conv_w', 'new_v_f_conv_b', 'new_v_f_w_down', 'new_v_ln1_g', 'new_v_ln1_b', 'new_v_ln2_g', 'new_v_ln2_b', 'new_v_ple_w', 'new_v_ple_gate_w', 'new_v_ple_gate_b']
TWIN_LEAF_KINDS = {'loss': 'loss', 'grad_x': 'grad_x', 'grad_a_w_in': 'grad_w', 'grad_a_b_f': 'grad_w', 'grad_a_w_out': 'grad_w', 'grad_b_w_in': 'grad_w', 'grad_b_conv_w': 'grad_w', 'grad_b_conv_b': 'grad_w', 'grad_b_w_a': 'grad_w', 'grad_b_b_a': 'grad_w', 'grad_b_w_i': 'grad_w', 'grad_b_b_i': 'grad_w', 'grad_b_lam': 'grad_w', 'grad_b_w_out': 'grad_w', 'grad_f_w_up': 'grad_w', 'grad_f_conv_w': 'grad_w', 'grad_f_conv_b': 'grad_w', 'grad_f_w_down': 'grad_w', 'grad_ln1_g': 'grad_w', 'grad_ln1_b': 'grad_w', 'grad_ln2_g': 'grad_w', 'grad_ln2_b': 'grad_w', 'grad_ple_w': 'grad_w', 'grad_ple_gate_w': 'grad_w', 'grad_ple_gate_b': 'grad_w', 'delta_a_w_in': 'delta_w', 'delta_a_b_f': 'delta_w', 'delta_a_w_out': 'delta_w', 'delta_b_w_in': 'delta_w', 'delta_b_conv_w': 'delta_w', 'delta_b_conv_b': 'delta_w', 'delta_b_w_a': 'delta_w', 'delta_b_b_a': 'delta_w', 'delta_b_w_i': 'delta_w', 'delta_b_b_i': 'delta_w', 'delta_b_lam': 'delta_w', 'delta_b_w_out': 'delta_w', 'delta_f_w_up': 'delta_w', 'delta_f_conv_w': 'delta_w', 'delta_f_conv_b': 'delta_w', 'delta_f_w_down': 'delta_w', 'delta_ln1_g': 'delta_w', 'delta_ln1_b': 'delta_w', 'delta_ln2_g': 'delta_w', 'delta_ln2_b': 'delta_w', 'delta_ple_w': 'delta_w', 'delta_ple_gate_w': 'delta_w', 'delta_ple_gate_b': 'delta_w', 'new_m_a_w_in': 'new_m', 'new_m_a_b_f': 'new_m', 'new_m_a_w_out': 'new_m', 'new_m_b_w_in': 'new_m', 'new_m_b_conv_w': 'new_m', 'new_m_b_conv_b': 'new_m', 'new_m_b_w_a': 'new_m', 'new_m_b_b_a': 'new_m', 'new_m_b_w_i': 'new_m', 'new_m_b_b_i': 'new_m', 'new_m_b_lam': 'new_m', 'new_m_b_w_out': 'new_m', 'new_m_f_w_up': 'new_m', 'new_m_f_conv_w': 'new_m', 'new_m_f_conv_b': 'new_m', 'new_m_f_w_down': 'new_m', 'new_m_ln1_g': 'new_m', 'new_m_ln1_b': 'new_m', 'new_m_ln2_g': 'new_m', 'new_m_ln2_b': 'new_m', 'new_m_ple_w': 'new_m', 'new_m_ple_gate_w': 'new_m', 'new_m_ple_gate_b': 'new_m', 'new_v_a_w_in': 'new_v', 'new_v_a_b_f': 'new_v', 'new_v_a_w_out': 'new_v', 'new_v_b_w_in': 'new_v', 'new_v_b_conv_w': 'new_v', 'new_v_b_conv_b': 'new_v', 'new_v_b_w_a': 'new_v', 'new_v_b_b_a': 'new_v', 'new_v_b_w_i': 'new_v', 'new_v_b_b_i': 'new_v', 'new_v_b_lam': 'new_v', 'new_v_b_w_out': 'new_v', 'new_v_f_w_up': 'new_v', 'new_v_f_conv_w': 'new_v', 'new_v_f_conv_b': 'new_v', 'new_v_f_w_down': 'new_v', 'new_v_ln1_g': 'new_v', 'new_v_ln1_b': 'new_v', 'new_v_ln2_g': 'new_v', 'new_v_ln2_b': 'new_v', 'new_v_ple_w': 'new_v', 'new_v_ple_gate_w': 'new_v', 'new_v_ple_gate_b': 'new_v'}


def _forward(args):
    return _fwd_reference(*[args[k] for k in FWD_PARAMS])


def _output_shape():
    def fwd():
        inp = _fwd_setup_inputs(0)
        return _fwd_reference(*[inp[k] for k in FWD_PARAMS])
    out = _jax.eval_shape(fwd)
    return out.shape, out.dtype

N_MICROBATCH = 1
ADAM_LR = 0.001
ADAM_B1 = 0.9
ADAM_B2 = 0.999
ADAM_EPS = 1e-08
ADAM_WD = 0.01
ADAM_STEP = 10
PER_EXAMPLE_BATCH_AXIS = {'x': 0, 'p': 1, 'loss_target': 0}
SHARED_INPUTS = []
_WEIGHT_DTYPES = {'a_w_in': _jnp.float32, 'a_b_f': _jnp.float32, 'a_w_out': _jnp.float32, 'b_w_in': _jnp.float32, 'b_conv_w': _jnp.float32, 'b_conv_b': _jnp.float32, 'b_w_a': _jnp.float32, 'b_b_a': _jnp.float32, 'b_w_i': _jnp.float32, 'b_b_i': _jnp.float32, 'b_lam': _jnp.float32, 'b_w_out': _jnp.float32, 'f_w_up': _jnp.float32, 'f_conv_w': _jnp.float32, 'f_conv_b': _jnp.float32, 'f_w_down': _jnp.float32, 'ln1_g': _jnp.float32, 'ln1_b': _jnp.float32, 'ln2_g': _jnp.float32, 'ln2_b': _jnp.float32, 'ple_w': _jnp.float32, 'ple_gate_w': _jnp.float32, 'ple_gate_b': _jnp.float32}
MOMENT_SCALE = {'a_w_in': 1.902662e-02, 'a_b_f': 4.033810e-02, 'a_w_out': 3.004395e-02, 'b_w_in': 5.306107e-02, 'b_conv_w': 5.838722e-02, 'b_conv_b': 1.291743e+00, 'b_w_a': 2.673838e-02, 'b_b_a': 2.002526e-02, 'b_w_i': 4.952980e-02, 'b_b_i': 1.962117e-02, 'b_lam': 3.361665e-02, 'b_w_out': 1.746031e-01, 'f_w_up': 3.617982e-02, 'f_conv_w': 3.632833e-02, 'f_conv_b': 6.399600e-02, 'f_w_down': 1.408686e-01, 'ln1_g': 3.739433e+00, 'ln1_b': 2.599995e+00, 'ln2_g': 6.458220e+01, 'ln2_b': 4.486699e+00, 'ple_w': 3.067125e-01, 'ple_gate_w': 4.885048e-02, 'ple_gate_b': 1.128461e+00}


def _to_microbatches(a, axis):
    t = _jnp.moveaxis(a, axis, 0)
    t = t.reshape((N_MICROBATCH, t.shape[0] // N_MICROBATCH) + t.shape[1:])
    return _jnp.moveaxis(t, 1, axis + 1)


def setup_inputs(seed: int = 0) -> dict:
    inp = _fwd_setup_inputs(seed)
    key = _jax.random.fold_in(_jax.random.key(seed), 7919)
    shape, _ = _output_shape()
    out = dict(inp)
    out["loss_target"] = _jax.random.normal(_jax.random.fold_in(key, 0), shape, _jnp.float32)
    for i, name in enumerate(TWIN_WEIGHTS):
        w = inp[name].astype(_jnp.float32)
        if MOMENT_SCALE is None:
            s = _jnp.sqrt(_jnp.mean(_jnp.square(w)) + 1e-30)
        else:
            s = MOMENT_SCALE[name]
        km, kv = _jax.random.split(_jax.random.fold_in(key, i + 1))
        out[name] = w
        out["m_" + name] = s * _jax.random.normal(km, w.shape, _jnp.float32)
        out["v_" + name] = (s * s) * _jax.random.uniform(kv, w.shape, _jnp.float32, 0.5, 1.5)
    if N_MICROBATCH > 1:
        for name, axis in PER_EXAMPLE_BATCH_AXIS.items():
            out[name] = _to_microbatches(out[name], axis)
    return {'x': out['x'], 'p': out['p'], 'a_w_in': out['a_w_in'], 'a_b_f': out['a_b_f'], 'a_w_out': out['a_w_out'], 'b_w_in': out['b_w_in'], 'b_conv_w': out['b_conv_w'], 'b_conv_b': out['b_conv_b'], 'b_w_a': out['b_w_a'], 'b_b_a': out['b_b_a'], 'b_w_i': out['b_w_i'], 'b_b_i': out['b_b_i'], 'b_lam': out['b_lam'], 'b_w_out': out['b_w_out'], 'f_w_up': out['f_w_up'], 'f_conv_w': out['f_conv_w'], 'f_conv_b': out['f_conv_b'], 'f_w_down': out['f_w_down'], 'ln1_g': out['ln1_g'], 'ln1_b': out['ln1_b'], 'ln2_g': out['ln2_g'], 'ln2_b': out['ln2_b'], 'ple_w': out['ple_w'], 'ple_gate_w': out['ple_gate_w'], 'ple_gate_b': out['ple_gate_b'], 'loss_target': out['loss_target'], 'm_a_w_in': out['m_a_w_in'], 'm_a_b_f': out['m_a_b_f'], 'm_a_w_out': out['m_a_w_out'], 'm_b_w_in': out['m_b_w_in'], 'm_b_conv_w': out['m_b_conv_w'], 'm_b_conv_b': out['m_b_conv_b'], 'm_b_w_a': out['m_b_w_a'], 'm_b_b_a': out['m_b_b_a'], 'm_b_w_i': out['m_b_w_i'], 'm_b_b_i': out['m_b_b_i'], 'm_b_lam': out['m_b_lam'], 'm_b_w_out': out['m_b_w_out'], 'm_f_w_up': out['m_f_w_up'], 'm_f_conv_w': out['m_f_conv_w'], 'm_f_conv_b': out['m_f_conv_b'], 'm_f_w_down': out['m_f_w_down'], 'm_ln1_g': out['m_ln1_g'], 'm_ln1_b': out['m_ln1_b'], 'm_ln2_g': out['m_ln2_g'], 'm_ln2_b': out['m_ln2_b'], 'm_ple_w': out['m_ple_w'], 'm_ple_gate_w': out['m_ple_gate_w'], 'm_ple_gate_b': out['m_ple_gate_b'], 'v_a_w_in': out['v_a_w_in'], 'v_a_b_f': out['v_a_b_f'], 'v_a_w_out': out['v_a_w_out'], 'v_b_w_in': out['v_b_w_in'], 'v_b_conv_w': out['v_b_conv_w'], 'v_b_conv_b': out['v_b_conv_b'], 'v_b_w_a': out['v_b_w_a'], 'v_b_b_a': out['v_b_b_a'], 'v_b_w_i': out['v_b_w_i'], 'v_b_b_i': out['v_b_b_i'], 'v_b_lam': out['v_b_lam'], 'v_b_w_out': out['v_b_w_out'], 'v_f_w_up': out['v_f_w_up'], 'v_f_conv_w': out['v_f_conv_w'], 'v_f_conv_b': out['v_f_conv_b'], 'v_f_w_down': out['v_f_w_down'], 'v_ln1_g': out['v_ln1_g'], 'v_ln1_b': out['v_ln1_b'], 'v_ln2_g': out['v_ln2_g'], 'v_ln2_b': out['v_ln2_b'], 'v_ple_w': out['v_ple_w'], 'v_ple_gate_w': out['v_ple_gate_w'], 'v_ple_gate_b': out['v_ple_gate_b']}


def _loss(weights, diff, rest, loss_target):
    with _jax.named_scope("forward"):
        args = {**rest, TWIN_DIFF_INPUT: diff, **{k: w.astype(_WEIGHT_DTYPES[k]) for k, w in weights.items()}}
        y = _forward(args)
    with _jax.named_scope("loss_head"):
        err = _jnp.square(y.astype(_jnp.float32) - loss_target)
        return 0.5 * _jnp.sum(_jnp.mean(err, axis=-1)) if err.ndim else 0.5 * err


def _adamw(w, g, m, v):
    m = ADAM_B1 * m + (1.0 - ADAM_B1) * g
    v = ADAM_B2 * v + (1.0 - ADAM_B2) * _jnp.square(g)
    m_hat = m / (1.0 - ADAM_B1 ** ADAM_STEP)
    v_hat = v / (1.0 - ADAM_B2 ** ADAM_STEP)
    delta = -ADAM_LR * (m_hat / (_jnp.sqrt(v_hat) + ADAM_EPS) + ADAM_WD * w)
    return delta, m, v


def reference(x, p, a_w_in, a_b_f, a_w_out, b_w_in, b_conv_w, b_conv_b, b_w_a, b_b_a, b_w_i, b_b_i, b_lam, b_w_out, f_w_up, f_conv_w, f_conv_b, f_w_down, ln1_g, ln1_b, ln2_g, ln2_b, ple_w, ple_gate_w, ple_gate_b, loss_target, m_a_w_in, m_a_b_f, m_a_w_out, m_b_w_in, m_b_conv_w, m_b_conv_b, m_b_w_a, m_b_b_a, m_b_w_i, m_b_b_i, m_b_lam, m_b_w_out, m_f_w_up, m_f_conv_w, m_f_conv_b, m_f_w_down, m_ln1_g, m_ln1_b, m_ln2_g, m_ln2_b, m_ple_w, m_ple_gate_w, m_ple_gate_b, v_a_w_in, v_a_b_f, v_a_w_out, v_b_w_in, v_b_conv_w, v_b_conv_b, v_b_w_a, v_b_b_a, v_b_w_i, v_b_b_i, v_b_lam, v_b_w_out, v_f_w_up, v_f_conv_w, v_f_conv_b, v_f_w_down, v_ln1_g, v_ln1_b, v_ln2_g, v_ln2_b, v_ple_w, v_ple_gate_w, v_ple_gate_b):
    given = dict(x=x, p=p, a_w_in=a_w_in, a_b_f=a_b_f, a_w_out=a_w_out, b_w_in=b_w_in, b_conv_w=b_conv_w, b_conv_b=b_conv_b, b_w_a=b_w_a, b_b_a=b_b_a, b_w_i=b_w_i, b_b_i=b_b_i, b_lam=b_lam, b_w_out=b_w_out, f_w_up=f_w_up, f_conv_w=f_conv_w, f_conv_b=f_conv_b, f_w_down=f_w_down, ln1_g=ln1_g, ln1_b=ln1_b, ln2_g=ln2_g, ln2_b=ln2_b, ple_w=ple_w, ple_gate_w=ple_gate_w, ple_gate_b=ple_gate_b, loss_target=loss_target, m_a_w_in=m_a_w_in, m_a_b_f=m_a_b_f, m_a_w_out=m_a_w_out, m_b_w_in=m_b_w_in, m_b_conv_w=m_b_conv_w, m_b_conv_b=m_b_conv_b, m_b_w_a=m_b_w_a, m_b_b_a=m_b_b_a, m_b_w_i=m_b_w_i, m_b_b_i=m_b_b_i, m_b_lam=m_b_lam, m_b_w_out=m_b_w_out, m_f_w_up=m_f_w_up, m_f_conv_w=m_f_conv_w, m_f_conv_b=m_f_conv_b, m_f_w_down=m_f_w_down, m_ln1_g=m_ln1_g, m_ln1_b=m_ln1_b, m_ln2_g=m_ln2_g, m_ln2_b=m_ln2_b, m_ple_w=m_ple_w, m_ple_gate_w=m_ple_gate_w, m_ple_gate_b=m_ple_gate_b, v_a_w_in=v_a_w_in, v_a_b_f=v_a_b_f, v_a_w_out=v_a_w_out, v_b_w_in=v_b_w_in, v_b_conv_w=v_b_conv_w, v_b_conv_b=v_b_conv_b, v_b_w_a=v_b_w_a, v_b_b_a=v_b_b_a, v_b_w_i=v_b_w_i, v_b_b_i=v_b_b_i, v_b_lam=v_b_lam, v_b_w_out=v_b_w_out, v_f_w_up=v_f_w_up, v_f_conv_w=v_f_conv_w, v_f_conv_b=v_f_conv_b, v_f_w_down=v_f_w_down, v_ln1_g=v_ln1_g, v_ln1_b=v_ln1_b, v_ln2_g=v_ln2_g, v_ln2_b=v_ln2_b, v_ple_w=v_ple_w, v_ple_gate_w=v_ple_gate_w, v_ple_gate_b=v_ple_gate_b)
    weights = {n: given[n] for n in TWIN_WEIGHTS}
    shared = {n: given[n] for n in SHARED_INPUTS}
    per_example = {n: given[n] for n in ['x', 'p']}
    grad_fn = _jax.value_and_grad(_loss, argnums=(0, 1))

    def one_microbatch(ex, loss_target):
        ex = dict(ex)
        diff = ex.pop(TWIN_DIFF_INPUT)
        return grad_fn(weights, diff, {**shared, **ex}, loss_target)

    if N_MICROBATCH == 1:
        loss, (grad_w, grad_x) = one_microbatch(per_example, given["loss_target"])
    else:
        def body(carry, xs):
            loss_sum, grad_sum = carry
            l_k, (gw_k, gx_k) = one_microbatch(xs[0], xs[1])
            with _jax.named_scope("update"):
                return (loss_sum + l_k, _jax.tree.map(_jnp.add, grad_sum, gw_k)), gx_k

        init = (_jnp.zeros((), _jnp.float32), _jax.tree.map(_jnp.zeros_like, weights))
        (loss, grad_w), grad_x = _jax.lax.scan(body, init, (per_example, given["loss_target"]))
    with _jax.named_scope("update"):
        delta_w, new_m, new_v = {}, {}, {}
        for n in TWIN_WEIGHTS:
            delta_w[n], new_m[n], new_v[n] = _adamw(weights[n], grad_w[n], given["m_" + n], given["v_" + n])
    return (loss, grad_x, *[grad_w[n] for n in TWIN_WEIGHTS], *[delta_w[n] for n in TWIN_WEIGHTS],
            *[new_m[n] for n in TWIN_WEIGHTS], *[new_v[n] for n in TWIN_WEIGHTS])
```

```python
import functools
import math

import jax
import jax.numpy as jnp
from jax import lax
from jax.experimental import pallas as pl
from jax.experimental.pallas import tpu as pltpu

F32 = jnp.float32
BF16 = jnp.bfloat16

D_MODEL = 1024
DEPTH = 4
N_HEADS = 16
HEAD_DIM = 64
N_BLOCKS_B = 8
BLOCK_B = 128
CONV_B = 4
LRU_C = 8.0
D_FF = 2816
CONV_F = 3
D_PLE = 256
LN_EPS = 1e-5
ALPHA = (2.0 * DEPTH) ** 0.25
ATTN_SCALE = 1.0 / math.sqrt(HEAD_DIM)

ADAM_LR = 0.001
ADAM_B1 = 0.9
ADAM_B2 = 0.999
ADAM_EPS = 1e-08
ADAM_WD = 0.01
ADAM_STEP = 10

LANES = 128
SUBLANES = 8
VMEM_LIMIT = 52 * 1024 * 1024
NEG_BIG = -1e30
N_DEV = 8
N_CHIP = 4

WEIGHTS = ['a_w_in', 'a_b_f', 'a_w_out', 'b_w_in', 'b_conv_w', 'b_conv_b', 'b_w_a', 'b_b_a', 'b_w_i', 'b_b_i',
           'b_lam', 'b_w_out', 'f_w_up', 'f_conv_w', 'f_conv_b', 'f_w_down', 'ln1_g', 'ln1_b', 'ln2_g', 'ln2_b',
           'ple_w', 'ple_gate_w', 'ple_gate_b']
SHARD_AXIS = {'a_w_in': 2, 'a_w_out': 1, 'b_w_in': 2, 'b_conv_w': 2, 'b_conv_b': 1, 'b_lam': 1, 'b_w_out': 1,
              'f_w_up': 2, 'f_conv_w': 2, 'f_w_down': 1, 'ple_w': 2, 'ple_gate_w': 1}
SHARDED = [n for n in WEIGHTS if n in SHARD_AXIS]
REPLICATED = [n for n in WEIGHTS if n not in SHARD_AXIS]
GATHER_BF16 = ['a_w_in', 'a_w_out', 'b_w_in', 'b_w_out', 'f_w_up', 'f_w_down', 'ple_w', 'ple_gate_w']
GATHER_F32 = ['b_conv_w', 'b_conv_b', 'b_lam', 'f_conv_w']


def _cparams(sem, vmem=VMEM_LIMIT):
    return pltpu.CompilerParams(dimension_semantics=sem, vmem_limit_bytes=vmem)


def _tile(n, cap, q=LANES):
    best = None
    for t in range(q, min(n, cap) + 1, q):
        if n % t == 0:
            best = t
    return best if best is not None else n


def _sigmoid(x):
    return 1.0 / (1.0 + jnp.exp(-x))


_GELU_C = math.sqrt(2.0 / math.pi)


def _gelu_and_grad(x):
    x2 = x * x
    t = jnp.tanh(_GELU_C * (x + 0.044715 * x * x2))
    cdf = 0.5 * (1.0 + t)
    g = x * cdf
    dg = cdf + x * 0.5 * (1.0 - t * t) * _GELU_C * (1.0 + 3.0 * 0.044715 * x2)
    return g, dg


def _gelu(x):
    t = jnp.tanh(_GELU_C * (x + 0.044715 * x * x * x))
    return x * (0.5 * (1.0 + t))


def _log1p(u):
    w = 1.0 + u
    d = w - 1.0
    return jnp.where(d == 0.0, u, jnp.log(w) * (u / jnp.where(d == 0.0, 1.0, d)))


def _softplus(y):
    return jnp.maximum(y, 0.0) + _log1p(jnp.exp(-jnp.abs(y)))


def _log_sigmoid(z):
    return -_softplus(-z)


def _neg_expm1(x):
    poly = x * (1.0 + x * (1.0 / 2 + x * (1.0 / 6 + x * (1.0 / 24 + x * (1.0 / 120 + x * (1.0 / 720 + x * (1.0 / 5040)))))))
    return -jnp.where(x > -0.25, poly, jnp.exp(x) - 1.0)


def _split3(x):
    hi = x.astype(BF16)
    r1 = x - hi.astype(F32)
    mid = r1.astype(BF16)
    lo = (r1 - mid.astype(F32)).astype(BF16)
    return hi, mid, lo


def _shift_down(x, halo, k):
    rolled = pltpu.roll(x, k, axis=0)
    hal = pltpu.roll(halo, k, axis=0)
    r8 = lax.broadcasted_iota(jnp.int32, halo.shape, 0)
    head = jnp.where(r8 < k, hal, rolled[:SUBLANES])
    if x.shape[0] == SUBLANES:
        return head
    return jnp.concatenate([head, rolled[SUBLANES:]], axis=0)


def _shift_up(x, nxt, k):
    n = x.shape[0]
    rolled = pltpu.roll(x, n - k, axis=0)
    nx = pltpu.roll(nxt, SUBLANES - k, axis=0)
    r8 = lax.broadcasted_iota(jnp.int32, nxt.shape, 0)
    tail = jnp.where(r8 >= SUBLANES - k, nx, rolled[n - SUBLANES:])
    if n == SUBLANES:
        return tail
    return jnp.concatenate([rolled[:n - SUBLANES], tail], axis=0)


def _split_spec(arr_ndim, part_cols, br, bc, idx):
    if arr_ndim == 3:
        nbh = part_cols // bc
        return pl.BlockSpec((None, br, bc), lambda i, j, k: (lax.div(idx(i, j, k)[1], nbh), idx(i, j, k)[0],
                                                             lax.rem(idx(i, j, k)[1], nbh)))
    return pl.BlockSpec((br, bc), lambda i, j, k: idx(i, j, k))


def _dims(arr):
    if arr.ndim == 3:
        return arr.shape[1], arr.shape[0] * arr.shape[2], arr.shape[2]
    return arr.shape[0], arr.shape[1], arr.shape[1]


def _mm(a, b, *, ta=False, tb=False, out_dtype=F32, out_split=1, add=None, add_scale=1.0,
        tm_cap=1024, tn_cap=1024, tk_cap=1408, name):
    ar, ac, apart = _dims(a)
    br_, bc_, bpart = _dims(b)
    m, kdim = (ac, ar) if ta else (ar, ac)
    kdim_b, n = (bc_, br_) if tb else (br_, bc_)
    assert kdim == kdim_b, (name, a.shape, b.shape)
    tm = _tile(apart, tm_cap) if ta else _tile(m, tm_cap, SUBLANES)
    tn = _tile(n, tn_cap, SUBLANES) if tb else _tile(math.gcd(bpart, n // out_split), tn_cap)
    if ta:
        tk = _tile(kdim, 1024, 2 * SUBLANES)
    elif tb:
        tk = _tile(math.gcd(apart, bpart), tk_cap)
    else:
        tk = _tile(apart, tk_cap)
    assert m % tm == 0 and n % tn == 0 and kdim % tk == 0, (name, m, n, kdim, tm, tn, tk)
    nk = kdim // tk
    a_spec = (_split_spec(a.ndim, apart, tk, tm, lambda i, j, k: (k, i)) if ta
              else _split_spec(a.ndim, apart, tm, tk, lambda i, j, k: (i, k)))
    b_spec = (_split_spec(b.ndim, bpart, tn, tk, lambda i, j, k: (j, k)) if tb
              else _split_spec(b.ndim, bpart, tk, tn, lambda i, j, k: (k, j)))
    if out_split > 1:
        out_shape = jax.ShapeDtypeStruct((out_split, m, n // out_split), out_dtype)
        o_spec = _split_spec(3, n // out_split, tm, tn, lambda i, j, k: (i, j))
    else:
        out_shape = jax.ShapeDtypeStruct((m, n), out_dtype)
        o_spec = pl.BlockSpec((tm, tn), lambda i, j, k: (i, j))
    dn = (((0 if ta else 1,), (1 if tb else 0,)), ((), ()))
    in_specs = [a_spec, b_spec]
    args = [a, b]
    if add is not None:
        in_specs.append(pl.BlockSpec((tm, tn), lambda i, j, k: (i, j)))
        args.append(add)
    use_acc = nk > 1
    has_add = add is not None

    def body(*refs):
        a_ref, b_ref = refs[0], refs[1]
        add_ref = refs[2] if has_add else None
        o_ref = refs[3] if has_add else refs[2]
        part = lax.dot_general(a_ref[...], b_ref[...], dn, preferred_element_type=F32)

        def finish(acc):
            if has_add:
                acc = acc + add_scale * add_ref[...]
            o_ref[...] = acc.astype(out_dtype)

        if not use_acc:
            finish(part)
        else:
            acc_ref = refs[-1]
            k = pl.program_id(2)

            @pl.when(k == 0)
            def _():
                acc_ref[...] = part

            @pl.when(k > 0)
            def _():
                acc_ref[...] += part

            @pl.when(k == nk - 1)
            def _():
                finish(acc_ref[...])

    return pl.pallas_call(
        body, name=name, grid=(m // tm, n // tn, nk), in_specs=in_specs, out_specs=o_spec, out_shape=out_shape,
        scratch_shapes=[pltpu.VMEM((tm, tn), F32)] if use_acc else [],
        compiler_params=_cparams(("parallel", "parallel", "arbitrary")),
    )(*args)


def _ln_fwd(x, m, g, b, *, name, tm=512):
    t, d = x.shape

    def body(x_ref, m_ref, g_ref, b_ref, y_ref, yb_ref, xhat_ref, rstd_ref):
        z = ALPHA * x_ref[...] + m_ref[...]
        mu = jnp.mean(z, axis=-1, keepdims=True)
        zc = z - mu
        var = jnp.mean(zc * zc, axis=-1, keepdims=True)
        rstd = lax.rsqrt(var + LN_EPS)
        xhat = zc * rstd
        y = xhat * g_ref[...] + b_ref[...]
        y_ref[...] = y
        yb_ref[...] = y.astype(BF16)
        xhat_ref[...] = xhat
        rstd_ref[...] = rstd

    row = pl.BlockSpec((tm, d), lambda i: (i, 0))
    vec = pl.BlockSpec((1, d), lambda i: (0, 0))
    return pl.pallas_call(
        body, name=name, grid=(t // tm,), in_specs=[row, row, vec, vec],
        out_specs=[row, row, row, pl.BlockSpec((tm, 1), lambda i: (i, 0))],
        out_shape=[jax.ShapeDtypeStruct((t, d), F32), jax.ShapeDtypeStruct((t, d), BF16),
                   jax.ShapeDtypeStruct((t, d), F32), jax.ShapeDtypeStruct((t, 1), F32)],
        compiler_params=_cparams(("parallel",)),
    )(x, m, g, b)


def _ln_bwd(dy, xhat, rstd, g, *, name, tm=512):
    t, d = dy.shape

    def body(dy_ref, xhat_ref, rstd_ref, g_ref, dz_ref, dzb_ref, dg_ref, db_ref):
        @pl.when(pl.program_id(0) == 0)
        def _():
            dg_ref[...] = jnp.zeros_like(dg_ref)
            db_ref[...] = jnp.zeros_like(db_ref)

        dyv = dy_ref[...]
        xh = xhat_ref[...]
        dg_ref[...] += jnp.sum(dyv * xh, axis=0, keepdims=True)
        db_ref[...] += jnp.sum(dyv, axis=0, keepdims=True)
        dxh = dyv * g_ref[...]
        m1 = jnp.mean(dxh, axis=-1, keepdims=True)
        m2 = jnp.mean(dxh * xh, axis=-1, keepdims=True)
        dz = rstd_ref[...] * (dxh - m1 - xh * m2)
        dz_ref[...] = dz
        dzb_ref[...] = dz.astype(BF16)

    row = pl.BlockSpec((tm, d), lambda i: (i, 0))
    vec = pl.BlockSpec((1, d), lambda i: (0, 0))
    return pl.pallas_call(
        body, name=name, grid=(t // tm,), in_specs=[row, row, pl.BlockSpec((tm, 1), lambda i: (i, 0)), vec],
        out_specs=[row, row, vec, vec],
        out_shape=[jax.ShapeDtypeStruct((t, d), F32), jax.ShapeDtypeStruct((t, d), BF16),
                   jax.ShapeDtypeStruct((1, d), F32), jax.ShapeDtypeStruct((1, d), F32)],
        compiler_params=_cparams(("arbitrary",)),
    )(dy, xhat, rstd, g)


def _ple_fwd(x2, gl, pe, gate_b, *, name, tm=512):
    t, d = x2.shape

    def body(x_ref, gl_ref, pe_ref, b_ref, y_ref, yb_ref):
        y = x_ref[...] + _sigmoid(gl_ref[...] + b_ref[...]) * pe_ref[...]
        y_ref[...] = y
        yb_ref[...] = y.astype(BF16)

    row = pl.BlockSpec((tm, d), lambda i: (i, 0))
    vec = pl.BlockSpec((1, d), lambda i: (0, 0))
    return pl.pallas_call(
        body, name=name, grid=(t // tm,), in_specs=[row, row, row, vec], out_specs=[row, row],
        out_shape=[jax.ShapeDtypeStruct((t, d), F32), jax.ShapeDtypeStruct((t, d), BF16)],
        compiler_params=_cparams(("parallel",)),
    )(x2, gl, pe, gate_b)


def _ple_bwd(dx3, gl, pe, gate_b, *, name, tm=512):
    t, d = dx3.shape

    def body(dx_ref, gl_ref, pe_ref, b_ref, dgl_ref, dpe_ref, db_ref):
        @pl.when(pl.program_id(0) == 0)
        def _():
            db_ref[...] = jnp.zeros_like(db_ref)

        dx = dx_ref[...]
        gt = _sigmoid(gl_ref[...] + b_ref[...])
        dgl = dx * pe_ref[...] * gt * (1.0 - gt)
        db_ref[...] += jnp.sum(dgl, axis=0, keepdims=True)
        dgl_ref[...] = dgl.astype(BF16)
        dpe_ref[...] = (dx * gt).astype(BF16)

    row = pl.BlockSpec((tm, d), lambda i: (i, 0))
    vec = pl.BlockSpec((1, d), lambda i: (0, 0))
    return pl.pallas_call(
        body, name=name, grid=(t // tm,), in_specs=[row, row, row, vec], out_specs=[row, row, vec],
        out_shape=[jax.ShapeDtypeStruct((t, d), BF16), jax.ShapeDtypeStruct((t, d), BF16),
                   jax.ShapeDtypeStruct((1, d), F32)],
        compiler_params=_cparams(("arbitrary",)),
    )(dx3, gl, pe, gate_b)


def _loss_bwd(y, tgt, *, name, tm=512):
    t, d = y.shape

    def body(y_ref, t_ref, dy_ref, l_ref):
        @pl.when(pl.program_id(0) == 0)
        def _():
            l_ref[...] = jnp.zeros_like(l_ref)

        err = y_ref[...] - t_ref[...]
        dy_ref[...] = err * (1.0 / d)
        part = jnp.sum(jnp.sum(err * err, axis=0, keepdims=True), axis=1, keepdims=True) * (0.5 / d)
        l_ref[...] += jnp.broadcast_to(part, l_ref.shape)

    row = pl.BlockSpec((tm, d), lambda i: (i, 0))
    return pl.pallas_call(
        body, name=name, grid=(t // tm,), in_specs=[row, row],
        out_specs=[row, pl.BlockSpec((1, LANES), lambda i: (0, 0))],
        out_shape=[jax.ShapeDtypeStruct((t, d), F32), jax.ShapeDtypeStruct((1, LANES), F32)],
        compiler_params=_cparams(("arbitrary",)),
    )(y, tgt)


def _conv_causal(x, halo, w_ref, b, kw):
    acc = x * w_ref[kw - 1:kw, :] + b
    for k in range(kw - 1):
        acc = acc + _shift_down(x, halo, kw - 1 - k) * w_ref[k:k + 1, :]
    return acc


def _ffn_act_fwd(hdn, conv_w, conv_b, *, name, tm=512):
    _, t, f = hdn.shape
    tc = _tile(f, 1408)
    hb = tm // SUBLANES

    def body(h_ref, halo_ref, w_ref, b_ref, a_ref):
        first = pl.program_id(1) == 0
        parts = []
        for s in range(2):
            halo = jnp.where(first, 0.0, halo_ref[s])
            parts.append(_conv_causal(h_ref[s], halo, w_ref.at[s], b_ref[s], CONV_F))
        a_ref[...] = (_gelu(parts[1]) * parts[0]).astype(BF16)

    return pl.pallas_call(
        body, name=name, grid=(f // tc, t // tm),
        in_specs=[pl.BlockSpec((2, tm, tc), lambda j, i: (0, i, j)),
                  pl.BlockSpec((2, SUBLANES, tc), lambda j, i: (0, jnp.maximum(i * hb - 1, 0), j)),
                  pl.BlockSpec((2, CONV_F, tc), lambda j, i: (0, 0, j)),
                  pl.BlockSpec((2, 1, tc), lambda j, i: (0, 0, j))],
        out_specs=pl.BlockSpec((tm, tc), lambda j, i: (i, j)),
        out_shape=jax.ShapeDtypeStruct((t, f), BF16),
        compiler_params=_cparams(("parallel", "arbitrary")),
    )(hdn, hdn, conv_w, conv_b)


def _ffn_act_bwd(da, hdn, conv_w, conv_b, *, name, tm=256):
    _, t, f = hdn.shape
    tc = _tile(f, 1408)
    hb = tm // SUBLANES
    nr = t // tm

    def body(da_ref, h_ref, halo_ref, w_ref, b_ref, dh_ref, dw_ref, db_ref, carry_ref):
        i = pl.program_id(1)
        r = nr - 1 - i

        @pl.when(i == 0)
        def _():
            dw_ref[...] = jnp.zeros_like(dw_ref)
            db_ref[...] = jnp.zeros_like(db_ref)
            carry_ref[...] = jnp.zeros_like(carry_ref)

        hs, hcs = [], []
        for s in range(2):
            halo = jnp.where(r == 0, 0.0, halo_ref[s])
            hs.append((h_ref[s], halo))
            hcs.append(_conv_causal(h_ref[s], halo, w_ref.at[s], b_ref[s], CONV_F))
        val, gate = hcs
        g, dg = _gelu_and_grad(gate)
        dav = da_ref[...]
        dhc = [dav * g, dav * val * dg]
        for s in range(2):
            d = dhc[s]
            x, halo = hs[s]
            nxt = carry_ref[s]
            db_ref[s] += jnp.sum(d, axis=0, keepdims=True)
            out = d * w_ref[s, CONV_F - 1:CONV_F, :]
            dw_ref[s, CONV_F - 1:CONV_F, :] += jnp.sum(d * x, axis=0, keepdims=True)
            for k in range(CONV_F - 1):
                sh = CONV_F - 1 - k
                out = out + _shift_up(d, nxt, sh) * w_ref[s, k:k + 1, :]
                dw_ref[s, k:k + 1, :] += jnp.sum(d * _shift_down(x, halo, sh), axis=0, keepdims=True)
            dh_ref[s] = out.astype(BF16)
            carry_ref[s] = d[:SUBLANES]

    return pl.pallas_call(
        body, name=name, grid=(f // tc, nr),
        in_specs=[pl.BlockSpec((tm, tc), lambda j, i: (nr - 1 - i, j)),
                  pl.BlockSpec((2, tm, tc), lambda j, i: (0, nr - 1 - i, j)),
                  pl.BlockSpec((2, SUBLANES, tc), lambda j, i: (0, jnp.maximum((nr - 1 - i) * hb - 1, 0), j)),
                  pl.BlockSpec((2, CONV_F, tc), lambda j, i: (0, 0, j)),
                  pl.BlockSpec((2, 1, tc), lambda j, i: (0, 0, j))],
        out_specs=[pl.BlockSpec((2, tm, tc), lambda j, i: (0, nr - 1 - i, j)),
                   pl.BlockSpec((2, CONV_F, tc), lambda j, i: (0, 0, j)),
                   pl.BlockSpec((2, 1, tc), lambda j, i: (0, 0, j))],
        out_shape=[jax.ShapeDtypeStruct((2, t, f), BF16), jax.ShapeDtypeStruct((2, CONV_F, f), F32),
                   jax.ShapeDtypeStruct((2, 1, f), F32)],
        scratch_shapes=[pltpu.VMEM((2, SUBLANES, tc), F32)],
        compiler_params=_cparams(("arbitrary", "arbitrary")),
    )(da, hdn, hdn, conv_w, conv_b)


def _fgate_fwd(xb, w_f, b_f, *, name, tm=256):
    t, d = xb.shape

    def body(x_ref, w_ref, b_ref, z_ref, c_ref, carry_ref):
        @pl.when(pl.program_id(0) == 0)
        def _():
            carry_ref[...] = jnp.zeros_like(carry_ref)

        z = jnp.dot(x_ref[...], w_ref[...], preferred_element_type=F32) + b_ref[...]
        z_ref[...] = z
        ls = _log_sigmoid(z)
        rr = lax.broadcasted_iota(jnp.int32, (tm, tm), 0)
        cc = lax.broadcasted_iota(jnp.int32, (tm, tm), 1)
        tri = (cc <= rr).astype(BF16)
        cum = carry_ref[...]
        for piece in _split3(ls):
            cum = cum + jnp.dot(tri, piece, preferred_element_type=F32)
        c_ref[...] = cum
        carry_ref[...] = cum[tm - 1:tm, :]

    return pl.pallas_call(
        body, name=name, grid=(t // tm,),
        in_specs=[pl.BlockSpec((tm, d), lambda i: (i, 0)), pl.BlockSpec((d, LANES), lambda i: (0, 0)),
                  pl.BlockSpec((1, LANES), lambda i: (0, 0))],
        out_specs=[pl.BlockSpec((tm, LANES), lambda i: (i, 0)), pl.BlockSpec((tm, LANES), lambda i: (i, 0))],
        out_shape=[jax.ShapeDtypeStruct((t, LANES), F32), jax.ShapeDtypeStruct((t, LANES), F32)],
        scratch_shapes=[pltpu.VMEM((1, LANES), F32)],
        compiler_params=_cparams(("arbitrary",)),
    )(xb, w_f, b_f)


def _fgate_bwd(dc, z, *, name, tm=256):
    t = dc.shape[0]
    nr = t // tm

    def body(dc_ref, z_ref, dz_ref, db_ref, carry_ref):
        @pl.when(pl.program_id(0) == 0)
        def _():
            carry_ref[...] = jnp.zeros_like(carry_ref)
            db_ref[...] = jnp.zeros_like(db_ref)

        rr = lax.broadcasted_iota(jnp.int32, (tm, tm), 0)
        cc = lax.broadcasted_iota(jnp.int32, (tm, tm), 1)
        tri = (cc >= rr).astype(BF16)
        cum = carry_ref[...]
        for piece in _split3(dc_ref[...]):
            cum = cum + jnp.dot(tri, piece, preferred_element_type=F32)
        carry_ref[...] = cum[0:1, :]
        dz = cum * _sigmoid(-z_ref[...])
        db_ref[...] += jnp.sum(dz, axis=0, keepdims=True)
        dz_ref[...] = dz.astype(BF16)

    rev = pl.BlockSpec((tm, LANES), lambda i: (nr - 1 - i, 0))
    return pl.pallas_call(
        body, name=name, grid=(nr,), in_specs=[rev, rev],
        out_specs=[rev, pl.BlockSpec((1, LANES), lambda i: (0, 0))],
        out_shape=[jax.ShapeDtypeStruct((t, LANES), BF16), jax.ShapeDtypeStruct((1, LANES), F32)],
        scratch_shapes=[pltpu.VMEM((1, LANES), F32)],
        compiler_params=_cparams(("arbitrary",)),
    )(dc, z)


def _head_lane_select(blk, head):
    lane = lax.broadcasted_iota(jnp.int32, (1, LANES), 1)
    return jnp.sum(jnp.where(lane == head, blk, 0.0), axis=1, keepdims=True)


def _flash_fwd(qkv, c, crow, *, name, tq=512):
    _, t, d = qkv.shape
    nq = t // tq
    n_pairs = d // LANES

    def body(q_ref, k_ref, v_ref, c_ref, cr0_ref, cr1_ref, o_ref, lse_ref):
        hp = pl.program_id(0)
        i = pl.program_id(1)
        lane = lax.broadcasted_iota(jnp.int32, (1, LANES), 1)
        rr = lax.broadcasted_iota(jnp.int32, (tq, tq), 0)
        cc = lax.broadcasted_iota(jnp.int32, (tq, tq), 1)
        causal = cc <= rr
        q = q_ref[...]
        cblk = c_ref[...]
        outs, lses = [], []
        for hh, cr_ref in enumerate((cr0_ref, cr1_ref)):
            sel = (lane < HEAD_DIM) if hh == 0 else (lane >= HEAD_DIM)
            qh = jnp.where(sel, q, jnp.zeros_like(q)) * jnp.asarray(ATTN_SCALE, BF16)
            ci = _head_lane_select(cblk, 2 * hp + hh)

            def kv_step(j, carry, masked, qh=qh, ci=ci, cr_ref=cr_ref):
                m, l, acc = carry
                off = pl.multiple_of(j * tq, tq)
                kj = k_ref[pl.ds(off, tq), :]
                vj = v_ref[pl.ds(off, tq), :]
                s = lax.dot_general(qh, kj, (((1,), (1,)), ((), ())), preferred_element_type=F32)
                s = s + (ci - cr_ref[:, pl.ds(off, tq)])
                if masked:
                    s = jnp.where(causal, s, NEG_BIG)
                m_new = jnp.maximum(m, jnp.max(s, axis=1, keepdims=True))
                p = jnp.exp(s - m_new)
                corr = jnp.exp(m - m_new)
                l = l * corr + jnp.sum(p, axis=1, keepdims=True)
                acc = acc * corr + jnp.dot(p.astype(BF16), vj, preferred_element_type=F32)
                return m_new, l, acc

            init = (jnp.full((tq, 1), NEG_BIG, F32), jnp.zeros((tq, 1), F32), jnp.zeros((tq, LANES), F32))
            carry = lax.fori_loop(0, i, functools.partial(kv_step, masked=False), init)
            m, l, acc = kv_step(i, carry, True)
            outs.append(acc / l)
            lses.append(m + jnp.log(l))
        o_ref[...] = jnp.where(lane < HEAD_DIM, outs[0], outs[1]).astype(BF16)
        lse_ref[...] = jnp.where(lane == 0, lses[0], jnp.where(lane == 1, lses[1], 0.0))

    return pl.pallas_call(
        body, name=name, grid=(n_pairs, nq),
        in_specs=[pl.BlockSpec((None, tq, LANES), lambda h, i: (0, i, h)),
                  pl.BlockSpec((None, t, LANES), lambda h, i: (1, 0, h)),
                  pl.BlockSpec((None, t, LANES), lambda h, i: (2, 0, h)),
                  pl.BlockSpec((tq, LANES), lambda h, i: (i, 0)),
                  pl.BlockSpec((None, 1, t), lambda h, i: (2 * h, 0, 0)),
                  pl.BlockSpec((None, 1, t), lambda h, i: (2 * h + 1, 0, 0))],
        out_specs=[pl.BlockSpec((tq, LANES), lambda h, i: (i, h)),
                   pl.BlockSpec((None, tq, LANES), lambda h, i: (h, i, 0))],
        out_shape=[jax.ShapeDtypeStruct((t, d), BF16), jax.ShapeDtypeStruct((n_pairs, t, LANES), F32)],
        compiler_params=_cparams(("parallel", "arbitrary")),
    )(qkv, qkv, qkv, c, crow, crow)


def _attn_delta(o, do, *, name, tm=512):
    t, d = o.shape

    def body(o_ref, do_ref, out_ref):
        prod = o_ref[...].astype(F32) * do_ref[...].astype(F32)
        col = lax.broadcasted_iota(jnp.int32, (d, LANES), 0)
        head = lax.broadcasted_iota(jnp.int32, (d, LANES), 1)
        sel = (lax.div(col, HEAD_DIM) == head).astype(BF16)
        acc = jnp.zeros((tm, LANES), F32)
        for piece in _split3(prod):
            acc = acc + jnp.dot(piece, sel, preferred_element_type=F32)
        out_ref[...] = acc

    row = pl.BlockSpec((tm, d), lambda i: (i, 0))
    return pl.pallas_call(
        body, name=name, grid=(t // tm,), in_specs=[row, row],
        out_specs=pl.BlockSpec((tm, LANES), lambda i: (i, 0)),
        out_shape=jax.ShapeDtypeStruct((t, LANES), F32),
        compiler_params=_cparams(("parallel",)),
    )(o, do)


def _flash_bwd(qkv, do, c, crow, lse_row, delta_row, *, name, tq=512):
    _, t, d = qkv.shape
    nq = t // tq
    n_pairs = d // LANES

    def body(q_ref, do_ref, k_ref, v_ref, c_ref, cr0, cr1, ls0, ls1, dl0, dl1,
             dq_ref, dkv_ref, dc_ref, dcq_ref, dq_acc):
        hp = pl.program_id(0)
        j = pl.program_id(1)
        lane = lax.broadcasted_iota(jnp.int32, (1, LANES), 1)
        kr = lax.broadcasted_iota(jnp.int32, (tq, tq), 0)
        qc = lax.broadcasted_iota(jnp.int32, (tq, tq), 1)
        causal_t = kr <= qc

        @pl.when(j == 0)
        def _():
            dq_acc[...] = jnp.zeros_like(dq_acc)
            dcq_ref[...] = jnp.zeros_like(dcq_ref)

        k = k_ref[...]
        v = v_ref[...]
        cblk = c_ref[...]
        dks, dvs, dcs = [], [], []
        for hh, (cr_ref, ls_ref, dl_ref) in enumerate(((cr0, ls0, dl0), (cr1, ls1, dl1))):
            sel = (lane < HEAD_DIM) if hh == 0 else (lane >= HEAD_DIM)
            kh = jnp.where(sel, k, jnp.zeros_like(k)) * jnp.asarray(ATTN_SCALE, BF16)
            vh = jnp.where(sel, v, jnp.zeros_like(v))
            cj = _head_lane_select(cblk, 2 * hp + hh)

            def q_step(i, carry, masked, hh=hh, kh=kh, vh=vh, cj=cj, cr_ref=cr_ref, ls_ref=ls_ref, dl_ref=dl_ref):
                dk, dv, dcj = carry
                off = pl.multiple_of(i * tq, tq)
                qi = q_ref[pl.ds(off, tq), :]
                doi = do_ref[pl.ds(off, tq), :]
                st = lax.dot_general(kh, qi, (((1,), (1,)), ((), ())), preferred_element_type=F32)
                st = st + (cr_ref[:, pl.ds(off, tq)] - cj)
                if masked:
                    st = jnp.where(causal_t, st, NEG_BIG)
                pt = jnp.exp(st - ls_ref[:, pl.ds(off, tq)])
                dv = dv + jnp.dot(pt.astype(BF16), doi, preferred_element_type=F32)
                dpt = lax.dot_general(vh, doi, (((1,), (1,)), ((), ())), preferred_element_type=F32)
                dst = pt * (dpt - dl_ref[:, pl.ds(off, tq)])
                dcj = dcj + jnp.sum(dst, axis=1, keepdims=True)
                dcq_ref[hh:hh + 1, pl.ds(off, tq)] += jnp.sum(dst, axis=0, keepdims=True)
                dsb = dst.astype(BF16)
                dk = dk + jnp.dot(dsb, qi, preferred_element_type=F32)
                dq_acc[pl.ds(off, tq), :] += lax.dot_general(dsb, kh, (((0,), (0,)), ((), ())),
                                                             preferred_element_type=F32)
                return dk, dv, dcj

            init = (jnp.zeros((tq, LANES), F32), jnp.zeros((tq, LANES), F32), jnp.zeros((tq, 1), F32))
            carry = q_step(j, init, True)
            dk, dv, dcj = lax.fori_loop(j + 1, nq, functools.partial(q_step, masked=False), carry)
            dks.append(dk * ATTN_SCALE)
            dvs.append(dv)
            dcs.append(-dcj)
        dkv_ref[0] = jnp.where(lane < HEAD_DIM, dks[0], dks[1]).astype(BF16)
        dkv_ref[1] = jnp.where(lane < HEAD_DIM, dvs[0], dvs[1]).astype(BF16)
        dc_ref[...] = jnp.where(lane == 0, dcs[0], jnp.where(lane == 1, dcs[1], 0.0))

        @pl.when(j == nq - 1)
        def _():
            dq_ref[...] = dq_acc[...].astype(BF16)

    full = lambda part: pl.BlockSpec((None, t, LANES), lambda h, j: (part, 0, h))
    rowspec = lambda hh: pl.BlockSpec((None, 1, t), lambda h, j: (2 * h + hh, 0, 0))
    return pl.pallas_call(
        body, name=name, grid=(n_pairs, nq),
        in_specs=[full(0),
                  pl.BlockSpec((t, LANES), lambda h, j: (0, h)),
                  pl.BlockSpec((None, tq, LANES), lambda h, j: (1, j, h)),
                  pl.BlockSpec((None, tq, LANES), lambda h, j: (2, j, h)),
                  pl.BlockSpec((tq, LANES), lambda h, j: (j, 0)),
                  rowspec(0), rowspec(1), rowspec(0), rowspec(1), rowspec(0), rowspec(1)],
        out_specs=[pl.BlockSpec((t, LANES), lambda h, j: (0, h)),
                   pl.BlockSpec((2, tq, LANES), lambda h, j: (0, j, h)),
                   pl.BlockSpec((None, tq, LANES), lambda h, j: (h, j, 0)),
                   pl.BlockSpec((None, SUBLANES, t), lambda h, j: (h, 0, 0))],
        out_shape=[jax.ShapeDtypeStruct((t, d), BF16), jax.ShapeDtypeStruct((2, t, d), BF16),
                   jax.ShapeDtypeStruct((n_pairs, t, LANES), F32),
                   jax.ShapeDtypeStruct((n_pairs, SUBLANES, t), F32)],
        scratch_shapes=[pltpu.VMEM((t, LANES), F32)],
        compiler_params=_cparams(("parallel", "arbitrary")),
    )(qkv, do, qkv, qkv, c, crow, crow, lse_row, lse_row, delta_row, delta_row)


def _rows_from_lanes(x, n):
    return x[:, :n].T.reshape(n, 1, x.shape[0])


def _rows_from_pairs(x8):
    hp, t, _ = x8.shape
    return x8[:, :, :2].transpose(0, 2, 1).reshape(2 * hp, 1, t)


def _lanes_from_pairs(x8):
    hp, t, _ = x8.shape
    y = x8[:, :, :2].transpose(1, 0, 2).reshape(t, 2 * hp)
    return jnp.pad(y, ((0, 0), (0, LANES - 2 * hp)))


def _lanes_from_pair_rows(x8):
    hp, _, t = x8.shape
    y = x8[:, :2, :].reshape(2 * hp, t).T
    return jnp.pad(y, ((0, 0), (0, LANES - 2 * hp)))


def _block_diag_dot(xb, w_ref, transpose_w=False):
    outs = []
    for n in range(N_BLOCKS_B):
        xn = xb[:, n * BLOCK_B:(n + 1) * BLOCK_B]
        dn = (((1,), (1 if transpose_w else 0,)), ((), ()))
        outs.append(lax.dot_general(xn, w_ref[n], dn, preferred_element_type=F32))
    return jnp.concatenate(outs, axis=1)


def _rglru_fwd(proj, conv_w, conv_b, w_a, b_a, w_i, b_i, lam, *, name, tm=256):
    _, t, d = proj.shape
    hb = tm // SUBLANES
    ng = tm // SUBLANES

    def body(p_ref, halo_ref, cw_ref, cb_ref, wa_ref, ba_ref, wi_ref, bi_ref, lam_ref,
             xb_ref, r_ref, ig_ref, a_ref, h_ref, y_ref, u_scr, hc_scr):
        first = pl.program_id(0) == 0

        @pl.when(first)
        def _():
            hc_scr[...] = jnp.zeros_like(hc_scr)

        halo = jnp.where(first, 0.0, halo_ref[0])
        xb = _conv_causal(p_ref[0], halo, cw_ref, cb_ref[...], CONV_B)
        xb_ref[...] = xb
        xbb = xb.astype(BF16)
        r = _sigmoid(_block_diag_dot(xbb, wa_ref) + ba_ref[...])
        ig = _sigmoid(_block_diag_dot(xbb, wi_ref) + bi_ref[...])
        r_ref[...] = r
        ig_ref[...] = ig
        log_a = (-LRU_C) * r * _softplus(-lam_ref[...])
        a_ref[...] = jnp.exp(log_a)
        u_scr[...] = jnp.sqrt(_neg_expm1(2.0 * log_a)) * (ig * xb)

        ones8 = jnp.ones((SUBLANES, d), F32)
        zeros8 = jnp.zeros((SUBLANES, d), F32)

        def group(gi, hprev):
            off = pl.multiple_of(gi * SUBLANES, SUBLANES)
            a8 = a_ref[pl.ds(off, SUBLANES), :]
            u8 = u_scr[pl.ds(off, SUBLANES), :]
            for s in (1, 2, 4):
                u8 = a8 * _shift_down(u8, zeros8, s) + u8
                a8 = a8 * _shift_down(a8, ones8, s)
            h8 = a8 * hprev + u8
            h_ref[pl.ds(off, SUBLANES), :] = h8
            return h8[SUBLANES - 1:SUBLANES, :]

        hc_scr[...] = lax.fori_loop(0, ng, group, hc_scr[...])
        y_ref[...] = (h_ref[...] * _gelu(p_ref[1])).astype(BF16)

    row = pl.BlockSpec((tm, d), lambda i: (i, 0))
    vec = pl.BlockSpec((1, d), lambda i: (0, 0))
    wblk = pl.BlockSpec((N_BLOCKS_B, BLOCK_B, BLOCK_B), lambda i: (0, 0, 0))
    f32_td = jax.ShapeDtypeStruct((t, d), F32)
    return pl.pallas_call(
        body, name=name, grid=(t // tm,),
        in_specs=[pl.BlockSpec((2, tm, d), lambda i: (0, i, 0)),
                  pl.BlockSpec((1, SUBLANES, d), lambda i: (0, jnp.maximum(i * hb - 1, 0), 0)),
                  pl.BlockSpec((CONV_B, d), lambda i: (0, 0)), vec, wblk, vec, wblk, vec, vec],
        out_specs=[row, row, row, row, row, row],
        out_shape=[f32_td, f32_td, f32_td, f32_td, f32_td, jax.ShapeDtypeStruct((t, d), BF16)],
        scratch_shapes=[pltpu.VMEM((tm, d), F32), pltpu.VMEM((1, d), F32)],
        compiler_params=_cparams(("arbitrary",)),
    )(proj, proj, conv_w, conv_b, w_a, b_a, w_i, b_i, lam)


def _rglru_bwd(dy, proj, xb, r, ig, a, h, conv_w, w_a, w_i, lam, *, name, tm=256):
    _, t, d = proj.shape
    hb = tm // SUBLANES
    ng = tm // SUBLANES
    nr = t // tm

    def body(dy_ref, p_ref, phalo_ref, xb_ref, r_ref, ig_ref, a_ref, h_ref, hhalo_ref,
             cw_ref, wa_ref, wi_ref, lam_ref,
             dp_ref, dcw_ref, dcb_ref, dba_ref, dbi_ref, dlam_ref, dwa_ref, dwi_ref,
             g_scr, carry_g, carry_dxb):
        i = pl.program_id(0)
        rpos = nr - 1 - i

        @pl.when(i == 0)
        def _():
            for ref in (dcw_ref, dcb_ref, dba_ref, dbi_ref, dlam_ref, dwa_ref, dwi_ref, carry_g, carry_dxb):
                ref[...] = jnp.zeros_like(ref)

        gb = p_ref[1]
        gel, dgel = _gelu_and_grad(gb)
        dyv = dy_ref[...]
        hv = h_ref[...]
        dp_ref[1] = (dyv * hv * dgel).astype(BF16)
        g_scr[...] = dyv * gel
        av = a_ref[...]

        ones8 = jnp.ones((SUBLANES, d), F32)
        zeros8 = jnp.zeros((SUBLANES, d), F32)

        def group(gi, cin):
            off = pl.multiple_of((ng - 1 - gi) * SUBLANES, SUBLANES)
            g8 = g_scr[pl.ds(off, SUBLANES), :]
            a8 = a_ref[pl.ds(off, SUBLANES), :]
            b8 = _shift_up(a8, ones8, 1)
            row8 = lax.broadcasted_iota(jnp.int32, (SUBLANES, d), 0)
            g8 = g8 + jnp.where(row8 == SUBLANES - 1, cin, 0.0)
            b8 = jnp.where(row8 == SUBLANES - 1, 0.0, b8)
            for s in (1, 2, 4):
                g8 = g8 + b8 * _shift_up(g8, zeros8, s)
                b8 = b8 * _shift_up(b8, zeros8, s)
            g_scr[pl.ds(off, SUBLANES), :] = g8
            return a8[0:1, :] * g8[0:1, :]

        carry_g[...] = lax.fori_loop(0, ng, group, carry_g[...])

        du = g_scr[...]
        hhalo = jnp.where(rpos == 0, 0.0, hhalo_ref[...])
        hprev = _shift_down(hv, hhalo, 1)
        da = du * hprev
        rv = r_ref[...]
        igv = ig_ref[...]
        xbv = xb_ref[...]
        sp = _softplus(-lam_ref[...])
        log_a = (-LRU_C) * rv * sp
        mult = jnp.sqrt(_neg_expm1(2.0 * log_a))
        dmult = du * (igv * xbv)
        dig = du * mult * xbv
        dxb = du * mult * igv
        dlog_a = da * av - dmult * (av * av) / mult
        dr = dlog_a * ((-LRU_C) * sp)
        dsp = jnp.sum(dlog_a * ((-LRU_C) * rv), axis=0, keepdims=True)
        dlam_ref[...] += dsp * (-_sigmoid(-lam_ref[...]))
        dra = dr * rv * (1.0 - rv)
        dia = dig * igv * (1.0 - igv)
        dba_ref[...] += jnp.sum(dra, axis=0, keepdims=True)
        dbi_ref[...] += jnp.sum(dia, axis=0, keepdims=True)
        drab = dra.astype(BF16)
        diab = dia.astype(BF16)
        xbb = xbv.astype(BF16)
        dxb = dxb + _block_diag_dot(drab, wa_ref, True) + _block_diag_dot(diab, wi_ref, True)
        tn = (((0,), (0,)), ((), ()))
        for n in range(N_BLOCKS_B):
            sl = slice(n * BLOCK_B, (n + 1) * BLOCK_B)
            dwa_ref[n] += lax.dot_general(xbb[:, sl], drab[:, sl], tn, preferred_element_type=F32)
            dwi_ref[n] += lax.dot_general(xbb[:, sl], diab[:, sl], tn, preferred_element_type=F32)

        xpre = p_ref[0]
        phalo = jnp.where(rpos == 0, 0.0, phalo_ref[0])
        nxt = carry_dxb[...]
        dcb_ref[...] += jnp.sum(dxb, axis=0, keepdims=True)
        out = dxb * cw_ref[CONV_B - 1:CONV_B, :]
        dcw_ref[CONV_B - 1:CONV_B, :] += jnp.sum(dxb * xpre, axis=0, keepdims=True)
        for k in range(CONV_B - 1):
            sh = CONV_B - 1 - k
            out = out + _shift_up(dxb, nxt, sh) * cw_ref[k:k + 1, :]
            dcw_ref[k:k + 1, :] += jnp.sum(dxb * _shift_down(xpre, phalo, sh), axis=0, keepdims=True)
        dp_ref[0] = out.astype(BF16)
        carry_dxb[...] = dxb[:SUBLANES]

    rev = pl.BlockSpec((tm, d), lambda i: (nr - 1 - i, 0))
    halo8 = pl.BlockSpec((SUBLANES, d), lambda i: (jnp.maximum((nr - 1 - i) * hb - 1, 0), 0))
    vec = pl.BlockSpec((1, d), lambda i: (0, 0))
    wblk = pl.BlockSpec((N_BLOCKS_B, BLOCK_B, BLOCK_B), lambda i: (0, 0, 0))
    vec_shape = jax.ShapeDtypeStruct((1, d), F32)
    w_shape = jax.ShapeDtypeStruct((N_BLOCKS_B, BLOCK_B, BLOCK_B), F32)
    return pl.pallas_call(
        body, name=name, grid=(nr,),
        in_specs=[rev, pl.BlockSpec((2, tm, d), lambda i: (0, nr - 1 - i, 0)),
                  pl.BlockSpec((1, SUBLANES, d), lambda i: (0, jnp.maximum((nr - 1 - i) * hb - 1, 0), 0)),
                  rev, rev, rev, rev, rev, halo8,
                  pl.BlockSpec((CONV_B, d), lambda i: (0, 0)), wblk, wblk, vec],
        out_specs=[pl.BlockSpec((2, tm, d), lambda i: (0, nr - 1 - i, 0)),
                   pl.BlockSpec((CONV_B, d), lambda i: (0, 0)), vec, vec, vec, vec, wblk, wblk],
        out_shape=[jax.ShapeDtypeStruct((2, t, d), BF16), jax.ShapeDtypeStruct((CONV_B, d), F32),
                   vec_shape, vec_shape, vec_shape, vec_shape, w_shape, w_shape],
        scratch_shapes=[pltpu.VMEM((tm, d), F32), pltpu.VMEM((1, d), F32), pltpu.VMEM((SUBLANES, d), F32)],
        compiler_params=_cparams(("arbitrary",)),
    )(dy, proj, proj, xb, r, ig, a, h, h, conv_w, w_a, w_i, lam)


def _split_cols(w, s):
    k, c = w.shape[-2:]
    return jnp.moveaxis(w.reshape(w.shape[:-2] + (k, s, c // s)), -2, -3)


def _merge_cols(w):
    s, k, c = w.shape
    return jnp.moveaxis(w, 0, 1).reshape(k, s * c)


def _local_step(x, p, tgt, w):
    t = x.shape[0]
    bf = lambda v: v.astype(BF16)
    saved = []
    xcur = x
    xcur_b = bf(x)
    for i in range(DEPTH):
        j = i // 2
        L = f"l{i}_"
        sv = {'x_in_b': xcur_b}
        if i % 2 == 0:
            w_in = bf(w['a_w_in'][j])
            sv['w_qkv'] = w_in[:, :3 * D_MODEL]
            sv['w_f'] = jnp.pad(w_in[:, 3 * D_MODEL:], ((0, 0), (0, LANES - N_HEADS)))
            sv['w_out'] = bf(w['a_w_out'][j])
            b_f = jnp.pad(w['a_b_f'][j], (0, LANES - N_HEADS)).reshape(1, LANES)
            qkv = _mm(xcur_b, sv['w_qkv'], out_dtype=BF16, out_split=3, name=L + "qkv")
            z, c = _fgate_fwd(xcur_b, sv['w_f'], b_f, name=L + "fgate")
            crow = _rows_from_lanes(c, N_HEADS)
            o, lse8 = _flash_fwd(qkv, c, crow, name=L + "flash_fwd")
            mix = _mm(o, sv['w_out'], name=L + "attn_out")
            sv.update(qkv=qkv, z=z, c=c, crow=crow, o=o, lse_row=_rows_from_pairs(lse8))
        else:
            sv['w_in'] = bf(w['b_w_in'][j])
            sv['w_out'] = bf(w['b_w_out'][j])
            sv['conv_w'] = w['b_conv_w'][j]
            sv['w_a'] = bf(w['b_w_a'][j])
            sv['w_i'] = bf(w['b_w_i'][j])
            sv['lam'] = w['b_lam'][j].reshape(1, D_MODEL)
            proj = _mm(xcur_b, sv['w_in'], out_split=2, name=L + "rg_in")
            xb, r, ig, a, h, y = _rglru_fwd(
                proj, sv['conv_w'], w['b_conv_b'][j].reshape(1, D_MODEL), sv['w_a'],
                w['b_b_a'][j].reshape(1, D_MODEL), sv['w_i'], w['b_b_i'][j].reshape(1, D_MODEL), sv['lam'],
                name=L + "rglru_fwd")
            mix = _mm(y, sv['w_out'], name=L + "rg_out")
            sv.update(proj=proj, xb=xb, r=r, ig=ig, a=a, h=h, y=y)
        sv['ln1_g'] = w['ln1_g'][i].reshape(1, D_MODEL)
        x1, x1b, xhat1, rstd1 = _ln_fwd(xcur, mix, sv['ln1_g'], w['ln1_b'][i].reshape(1, D_MODEL), name=L + "ln1")
        sv['w_up'] = bf(w['f_w_up'][i])
        sv['w_down'] = bf(w['f_w_down'][i])
        sv['fconv_w'] = _split_cols(w['f_conv_w'][i], 2)
        sv['fconv_b'] = w['f_conv_b'][i].reshape(2, 1, D_FF)
        hdn = _mm(x1b, sv['w_up'], out_split=2, tn_cap=1408, name=L + "ffn_up")
        act = _ffn_act_fwd(hdn, sv['fconv_w'], sv['fconv_b'], name=L + "ffn_act")
        ff = _mm(act, sv['w_down'], tk_cap=2816, name=L + "ffn_down")
        sv['ln2_g'] = w['ln2_g'][i].reshape(1, D_MODEL)
        x2, x2b, xhat2, rstd2 = _ln_fwd(x1, ff, sv['ln2_g'], w['ln2_b'][i].reshape(1, D_MODEL), name=L + "ln2")
        sv['gate_w'] = bf(w['ple_gate_w'][i])
        sv['ple_w'] = bf(w['ple_w'][i])
        sv['gate_b'] = w['ple_gate_b'][i].reshape(1, D_MODEL)
        sv['p_b'] = bf(p[i])
        gl = _mm(x2b, sv['gate_w'], name=L + "ple_gate")
        pe = _mm(sv['p_b'], sv['ple_w'], name=L + "ple_emb")
        x3, x3b = _ple_fwd(x2, gl, pe, sv['gate_b'], name=L + "ple")
        sv.update(xhat1=xhat1, rstd1=rstd1, x1b=x1b, hdn=hdn, act=act, xhat2=xhat2, rstd2=rstd2, x2b=x2b,
                  gl=gl, pe=pe)
        saved.append(sv)
        xcur, xcur_b = x3, x3b

    dx, loss_row = _loss_bwd(xcur, tgt, name="loss")

    g = {n: [None] * w[n].shape[0] for n in WEIGHTS}
    for i in reversed(range(DEPTH)):
        j = i // 2
        L = f"l{i}b_"
        sv = saved[i]
        dgl, dpe, d_gate_b = _ple_bwd(dx, sv['gl'], sv['pe'], sv['gate_b'], name=L + "ple")
        g['ple_gate_b'][i] = d_gate_b[0]
        g['ple_w'][i] = _mm(sv['p_b'], dpe, ta=True, name=L + "ple_emb_dw")
        g['ple_gate_w'][i] = _mm(sv['x2b'], dgl, ta=True, name=L + "ple_gate_dw")
        dx2 = _mm(dgl, sv['gate_w'], tb=True, add=dx, name=L + "ple_gate_dx")
        dz2, dz2b, dg2, db2 = _ln_bwd(dx2, sv['xhat2'], sv['rstd2'], sv['ln2_g'], name=L + "ln2")
        g['ln2_g'][i], g['ln2_b'][i] = dg2[0], db2[0]
        g['f_w_down'][i] = _mm(sv['act'], dz2b, ta=True, tm_cap=1408, name=L + "ffn_down_dw")
        da = _mm(dz2b, sv['w_down'], tb=True, tn_cap=1408, name=L + "ffn_down_dx")
        dhdn, d_fcw, d_fcb = _ffn_act_bwd(da, sv['hdn'], sv['fconv_w'], sv['fconv_b'], name=L + "ffn_act")
        g['f_conv_w'][i] = _merge_cols(d_fcw)
        g['f_conv_b'][i] = d_fcb.reshape(2 * D_FF)
        g['f_w_up'][i] = _mm(sv['x1b'], dhdn, ta=True, tn_cap=1408, name=L + "ffn_up_dw")
        dx1 = _mm(dhdn, sv['w_up'], tb=True, add=dz2, add_scale=ALPHA, name=L + "ffn_up_dx")
        dz1, dz1b, dg1, db1 = _ln_bwd(dx1, sv['xhat1'], sv['rstd1'], sv['ln1_g'], name=L + "ln1")
        g['ln1_g'][i], g['ln1_b'][i] = dg1[0], db1[0]
        if i % 2 == 0:
            g['a_w_out'][j] = _mm(sv['o'], dz1b, ta=True, name=L + "attn_out_dw")
            do = _mm(dz1b, sv['w_out'], tb=True, out_dtype=BF16, name=L + "attn_out_dx")
            delta = _attn_delta(sv['o'], do, name=L + "delta")
            dq, dkv, dc8, dcq = _flash_bwd(sv['qkv'], do, sv['c'], sv['crow'], sv['lse_row'],
                                           _rows_from_lanes(delta, N_HEADS), name=L + "flash_bwd")
            dc = _lanes_from_pairs(dc8) + _lanes_from_pair_rows(dcq)
            dzf, d_b_f = _fgate_bwd(dc, sv['z'], name=L + "fgate")
            g['a_b_f'][j] = d_b_f[0, :N_HEADS]
            xb_in = sv['x_in_b']
            d_wq = _mm(xb_in, dq, ta=True, name=L + "q_dw")
            d_wkv = _mm(xb_in, dkv, ta=True, name=L + "kv_dw")
            d_wf = _mm(xb_in, dzf, ta=True, name=L + "f_dw")
            g['a_w_in'][j] = jnp.concatenate([d_wq, d_wkv, d_wf[:, :N_HEADS]], axis=1)
            dxa = _mm(dq, sv['w_qkv'][:, :D_MODEL], tb=True, add=dz1, add_scale=ALPHA, name=L + "q_dx")
            dxa = _mm(dkv, sv['w_qkv'][:, D_MODEL:], tb=True, add=dxa, name=L + "kv_dx")
            dx = _mm(dzf, sv['w_f'], tb=True, add=dxa, name=L + "f_dx")
        else:
            g['b_w_out'][j] = _mm(sv['y'], dz1b, ta=True, name=L + "rg_out_dw")
            dy = _mm(dz1b, sv['w_out'], tb=True, name=L + "rg_out_dx")
            dproj, d_cw, d_cb, d_ba, d_bi, d_lam, d_wa, d_wi = _rglru_bwd(
                dy, sv['proj'], sv['xb'], sv['r'], sv['ig'], sv['a'], sv['h'], sv['conv_w'], sv['w_a'], sv['w_i'],
                sv['lam'], name=L + "rglru_bwd")
            g['b_conv_w'][j], g['b_conv_b'][j] = d_cw, d_cb[0]
            g['b_b_a'][j] = d_ba.reshape(N_BLOCKS_B, BLOCK_B)
            g['b_b_i'][j] = d_bi.reshape(N_BLOCKS_B, BLOCK_B)
            g['b_lam'][j] = d_lam[0]
            g['b_w_a'][j], g['b_w_i'][j] = d_wa, d_wi
            g['b_w_in'][j] = _mm(sv['x_in_b'], dproj, ta=True, name=L + "rg_in_dw")
            dx = _mm(dproj, sv['w_in'], tb=True, add=dz1, add_scale=ALPHA, name=L + "rg_in_dx")
    grads = {n: jnp.stack(v) for n, v in g.items()}
    return loss_row, dx, grads


def _round_up(n, q):
    return -(-n // q) * q


def _pack(arrs, dtype, row_multiple, lead=0):
    pieces = []
    for a in arrs:
        flat = a.reshape(a.shape[:lead] + (-1,)).astype(dtype)
        n = flat.shape[-1]
        pieces.append(jnp.pad(flat, [(0, 0)] * lead + [(0, _round_up(n, LANES) - n)]))
    flat = jnp.concatenate(pieces, axis=-1)
    rows = _round_up(flat.shape[-1] // LANES, row_multiple)
    flat = jnp.pad(flat, [(0, 0)] * lead + [(0, rows * LANES - flat.shape[-1])])
    return flat.reshape(flat.shape[:lead] + (rows, LANES))


def _unpack(buf, shapes):
    lead = buf.shape[:-2]
    flat = buf.reshape(lead + (-1,))
    out, off = [], 0
    for shp in shapes:
        n = math.prod(shp)
        out.append(flat[..., off:off + n].reshape(lead + tuple(shp)))
        off += _round_up(n, LANES)
    return out


MESH = pl.DeviceIdType.MESH
ANY = pl.BlockSpec(memory_space=pl.ANY)


def _all_gather_shards(buf):
    rows, lanes = buf.shape
    half = rows // 2

    def body(in_ref, out_ref, send_sems, recv_sems, local_sem):
        x, y, c = lax.axis_index("x"), lax.axis_index("y"), lax.axis_index("c")
        sibling = (x, y, 1 - c)
        chips = [(1 - x, y), (x, 1 - y), (1 - x, 1 - y)]

        def piece(cx, cy, hc):
            return out_ref.at[2 * cx + cy, pl.ds(hc * half, half), :]

        def copy(k, src, dst, to):
            return pltpu.make_async_remote_copy(src_ref=src, dst_ref=dst, send_sem=send_sems.at[k],
                                                recv_sem=recv_sems.at[k], device_id=to, device_id_type=MESH)

        mine = pltpu.make_async_copy(in_ref, out_ref.at[2 * x + y], local_sem)
        mine.start()
        my_half = in_ref.at[pl.ds(c * half, half), :]
        first = [copy(k, my_half, piece(x, y, c), (*chip, c)) for k, chip in enumerate(chips)]
        for cp in first:
            cp.start()
        passed = [copy(3 + k, piece(*chip, c), piece(*chip, c), sibling) for k, chip in enumerate(chips)]
        for k, chip in enumerate(chips):
            copy(k, my_half, piece(*chip, c), (*chip, c)).wait_recv()
            passed[k].start()
        for k, chip in enumerate(chips):
            copy(3 + k, my_half, piece(*chip, 1 - c), sibling).wait_recv()
        for cp in first + passed:
            cp.wait_send()
        mine.wait()

    return pl.pallas_call(
        body, name="gather_weights", in_specs=[ANY], out_specs=ANY,
        out_shape=jax.ShapeDtypeStruct((N_CHIP, rows, lanes), buf.dtype),
        scratch_shapes=[pltpu.SemaphoreType.DMA((6,)), pltpu.SemaphoreType.DMA((6,)), pltpu.SemaphoreType.DMA],
    )(buf)


def _exchange_pieces(pieces):
    n, prow, lanes = pieces.shape

    def body(in_ref, out_ref, send_sems, recv_sems, local_sem):
        x, y, c = lax.axis_index("x"), lax.axis_index("y"), lax.axis_index("c")
        me = 4 * x + 2 * y + c
        mine = pltpu.make_async_copy(in_ref.at[me], out_ref.at[me], local_sem)
        mine.start()
        copies = []
        for k in range(1, N_DEV):
            px, py, pc = x ^ (k >> 2), y ^ ((k >> 1) & 1), c ^ (k & 1)
            peer = 4 * px + 2 * py + pc
            cp = pltpu.make_async_remote_copy(src_ref=in_ref.at[peer], dst_ref=out_ref.at[me],
                                              send_sem=send_sems.at[k - 1], recv_sem=recv_sems.at[k - 1],
                                              device_id=(px, py, pc), device_id_type=MESH)
            cp.start()
            copies.append(cp)
        for cp in copies:
            cp.wait()
        mine.wait()

    return pl.pallas_call(
        body, name="exchange_grads", in_specs=[ANY], out_specs=ANY,
        out_shape=jax.ShapeDtypeStruct((n, prow, lanes), pieces.dtype),
        scratch_shapes=[pltpu.SemaphoreType.DMA((N_DEV - 1,)), pltpu.SemaphoreType.DMA((N_DEV - 1,)),
                        pltpu.SemaphoreType.DMA],
    )(pieces)


def _sum_slots(slots, *, tm=512):
    n, prow, lanes = slots.shape
    tm = _tile(prow, tm, SUBLANES)

    def body(s_ref, o_ref):
        acc = s_ref[0]
        for s in range(1, n):
            acc = acc + s_ref[s]
        o_ref[...] = acc

    return pl.pallas_call(
        body, name="sum_grads", grid=(prow // tm,),
        in_specs=[pl.BlockSpec((n, tm, lanes), lambda i: (0, i, 0))],
        out_specs=pl.BlockSpec((tm, lanes), lambda i: (i, 0)),
        out_shape=jax.ShapeDtypeStruct((prow, lanes), F32),
        compiler_params=_cparams(("parallel",)),
    )(slots)


def _share_halves(red, half_rows):
    prow, lanes = red.shape
    rep_rows = prow - half_rows

    def body(in_ref, out_ref, send_sem, recv_sem, local_sems):
        x, y, c = lax.axis_index("x"), lax.axis_index("y"), lax.axis_index("c")
        my_rows = out_ref.at[pl.ds(c * half_rows, half_rows), :]
        own = pltpu.make_async_copy(in_ref.at[pl.ds(0, half_rows), :], my_rows, local_sems.at[0])
        rep = pltpu.make_async_copy(in_ref.at[pl.ds(half_rows, rep_rows), :],
                                    out_ref.at[pl.ds(2 * half_rows, rep_rows), :], local_sems.at[1])
        own.start()
        rep.start()
        cp = pltpu.make_async_remote_copy(src_ref=in_ref.at[pl.ds(0, half_rows), :], dst_ref=my_rows,
                                          send_sem=send_sem, recv_sem=recv_sem,
                                          device_id=(x, y, 1 - c), device_id_type=MESH)
        cp.start()
        cp.wait()
        own.wait()
        rep.wait()

    return pl.pallas_call(
        body, name="share_halves", in_specs=[ANY], out_specs=ANY,
        out_shape=jax.ShapeDtypeStruct((2 * half_rows + rep_rows, lanes), red.dtype),
        scratch_shapes=[pltpu.SemaphoreType.DMA, pltpu.SemaphoreType.DMA, pltpu.SemaphoreType.DMA((2,))],
    )(red)


def _adamw(wp, gp, mp, vp, *, tm=1024):
    rows, lanes = wp.shape
    tm = _tile(rows, tm, SUBLANES)
    c1 = 1.0 / (1.0 - ADAM_B1 ** ADAM_STEP)
    c2 = 1.0 / (1.0 - ADAM_B2 ** ADAM_STEP)

    def body(w_ref, g_ref, m_ref, v_ref, d_ref, nm_ref, nv_ref):
        g = g_ref[...]
        m = ADAM_B1 * m_ref[...] + (1.0 - ADAM_B1) * g
        v = ADAM_B2 * v_ref[...] + (1.0 - ADAM_B2) * (g * g)
        m_hat = m * c1
        v_hat = v * c2
        d_ref[...] = -ADAM_LR * (m_hat / (jnp.sqrt(v_hat) + ADAM_EPS) + ADAM_WD * w_ref[...])
        nm_ref[...] = m
        nv_ref[...] = v

    spec = pl.BlockSpec((tm, lanes), lambda i: (i, 0))
    shp = jax.ShapeDtypeStruct((rows, lanes), F32)
    return pl.pallas_call(
        body, name="adamw", grid=(rows // tm,), in_specs=[spec] * 4, out_specs=[spec] * 3,
        out_shape=[shp, shp, shp], compiler_params=_cparams(("parallel",)),
    )(wp, gp, mp, vp)


AG_ROW_MULT = 64
GRAD_ROW_MULT = 2048
REP_ROW_MULT = 512


def _shard_to_front(a, axis):
    n = a.shape[axis]
    a = a.reshape(a.shape[:axis] + (N_CHIP, n // N_CHIP) + a.shape[axis + 1:])
    return jnp.moveaxis(a, axis, 0)


def _shards_to_full(a, axis):
    a = jnp.moveaxis(a, 0, axis)
    return a.reshape(a.shape[:axis] + (a.shape[axis] * a.shape[axis + 1],) + a.shape[axis + 2:])


def kernel(x, p, a_w_in, a_b_f, a_w_out, b_w_in, b_conv_w, b_conv_b, b_w_a, b_b_a, b_w_i, b_b_i, b_lam, b_w_out, f_w_up, f_conv_w, f_conv_b, f_w_down, ln1_g, ln1_b, ln2_g, ln2_b, ple_w, ple_gate_w, ple_gate_b, loss_target, m_a_w_in, m_a_b_f, m_a_w_out, m_b_w_in, m_b_conv_w, m_b_conv_b, m_b_w_a, m_b_b_a, m_b_w_i, m_b_b_i, m_b_lam, m_b_w_out, m_f_w_up, m_f_conv_w, m_f_conv_b, m_f_w_down, m_ln1_g, m_ln1_b, m_ln2_g, m_ln2_b, m_ple_w, m_ple_gate_w, m_ple_gate_b, v_a_w_in, v_a_b_f, v_a_w_out, v_b_w_in, v_b_conv_w, v_b_conv_b, v_b_w_a, v_b_b_a, v_b_w_i, v_b_b_i, v_b_lam, v_b_w_out, v_f_w_up, v_f_conv_w, v_f_conv_b, v_f_w_down, v_ln1_g, v_ln1_b, v_ln2_g, v_ln2_b, v_ple_w, v_ple_gate_w, v_ple_gate_b):
    args = dict(locals())
    local_w = {n: args[n] for n in WEIGHTS}
    local_m = {n: args['m_' + n] for n in WEIGHTS}
    local_v = {n: args['v_' + n] for n in WEIGHTS}

    as_pairs = lambda a: lax.bitcast_convert_type(a, BF16)
    from_pairs = lambda a: lax.bitcast_convert_type(a, F32)
    send = [local_w[n] for n in GATHER_BF16] + [as_pairs(local_w[n]) for n in GATHER_F32]
    gathered = _all_gather_shards(_pack(send, BF16, AG_ROW_MULT))
    shapes = [local_w[n].shape for n in GATHER_BF16] + [local_w[n].shape + (2,) for n in GATHER_F32]
    parts = _unpack(gathered, shapes)
    full_w = {}
    for n, part in zip(GATHER_BF16 + GATHER_F32, parts):
        if n in GATHER_F32:
            part = from_pairs(part)
        full_w[n] = _shards_to_full(part, SHARD_AXIS[n])
    for n in REPLICATED:
        full_w[n] = local_w[n]

    loss_row, grad_x, grads = _local_step(x[0], p[:, 0], loss_target[0], full_w)
    loss = lax.psum(loss_row[0, 0], ("x", "y", "c"))

    shard_rows = _round_up(sum(_round_up(local_w[n].size, LANES) for n in SHARDED) // LANES, GRAD_ROW_MULT)
    half_rows = shard_rows // 2
    sharded_g = _pack([_shard_to_front(grads[n], SHARD_AXIS[n]) for n in SHARDED], F32, GRAD_ROW_MULT, lead=1)
    rep_g = _pack([grads[n] for n in REPLICATED], F32, REP_ROW_MULT)
    rep_rows = rep_g.shape[0]
    pieces = sharded_g.reshape(N_DEV, half_rows, LANES)
    pieces = jnp.concatenate([pieces, jnp.broadcast_to(rep_g, (N_DEV, rep_rows, LANES))], axis=1)
    reduced = _sum_slots(_exchange_pieces(pieces))
    g_packed = _share_halves(reduced, half_rows)

    def packed(d):
        return jnp.concatenate([_pack([d[n] for n in SHARDED], F32, GRAD_ROW_MULT),
                                _pack([d[n] for n in REPLICATED], F32, REP_ROW_MULT)], axis=0)

    delta_p, new_m_p, new_v_p = _adamw(packed(local_w), g_packed, packed(local_m), packed(local_v))

    def unpacked(buf):
        sh = _unpack(buf[:shard_rows], [local_w[n].shape for n in SHARDED])
        rp = _unpack(buf[shard_rows:], [local_w[n].shape for n in REPLICATED])
        d = dict(zip(SHARDED, sh))
        d.update(zip(REPLICATED, rp))
        return [d[n] for n in WEIGHTS]

    return (loss, grad_x[None], *unpacked(g_packed), *unpacked(delta_p), *unpacked(new_m_p), *unpacked(new_v_p))
```

```python
import functools
import math

import jax
import jax.numpy as jnp
from jax import lax
from jax.experimental import pallas as pl
from jax.experimental.pallas import tpu as pltpu

F32 = jnp.float32
BF16 = jnp.bfloat16

D_MODEL = 1024
DEPTH = 4
N_HEADS = 16
HEAD_DIM = 64
N_BLOCKS_B = 8
BLOCK_B = 128
CONV_B = 4
LRU_C = 8.0
D_FF = 2816
CONV_F = 3
D_PLE = 256
LN_EPS = 1e-5
ALPHA = (2.0 * DEPTH) ** 0.25
ATTN_SCALE = 1.0 / math.sqrt(HEAD_DIM)

ADAM_LR = 0.001
ADAM_B1 = 0.9
ADAM_B2 = 0.999
ADAM_EPS = 1e-08
ADAM_WD = 0.01
ADAM_STEP = 10

LANES = 128
SUBLANES = 8
VMEM_LIMIT = 52 * 1024 * 1024
NEG_BIG = -1e30
N_DEV = 8
N_CHIP = 4

WEIGHTS = ['a_w_in', 'a_b_f', 'a_w_out', 'b_w_in', 'b_conv_w', 'b_conv_b', 'b_w_a', 'b_b_a', 'b_w_i', 'b_b_i',
           'b_lam', 'b_w_out', 'f_w_up', 'f_conv_w', 'f_conv_b', 'f_w_down', 'ln1_g', 'ln1_b', 'ln2_g', 'ln2_b',
           'ple_w', 'ple_gate_w', 'ple_gate_b']
SHARD_AXIS = {'a_w_in': 2, 'a_w_out': 1, 'b_w_in': 2, 'b_conv_w': 2, 'b_conv_b': 1, 'b_lam': 1, 'b_w_out': 1,
              'f_w_up': 2, 'f_conv_w': 2, 'f_w_down': 1, 'ple_w': 2, 'ple_gate_w': 1}
TRANSPOSED = ('a_w_in',)
SHARDED = [n for n in WEIGHTS if n in SHARD_AXIS]
REPLICATED = [n for n in WEIGHTS if n not in SHARD_AXIS]
GATHER_BF16 = ['a_w_in', 'a_w_out', 'b_w_in', 'b_w_out', 'f_w_up', 'f_w_down', 'ple_w', 'ple_gate_w']
GATHER_F32 = ['b_conv_w', 'b_conv_b', 'b_lam', 'f_conv_w']


def _cparams(sem, vmem=VMEM_LIMIT):
    return pltpu.CompilerParams(dimension_semantics=sem, vmem_limit_bytes=vmem)


def _tile(n, cap, q=LANES):
    best = None
    for t in range(q, min(n, cap) + 1, q):
        if n % t == 0:
            best = t
    return best if best is not None else n


def _sigmoid(x):
    return 1.0 / (1.0 + jnp.exp(-x))


_GELU_C = math.sqrt(2.0 / math.pi)


def _gelu_and_grad(x):
    x2 = x * x
    t = jnp.tanh(_GELU_C * (x + 0.044715 * x * x2))
    cdf = 0.5 * (1.0 + t)
    g = x * cdf
    dg = cdf + x * 0.5 * (1.0 - t * t) * _GELU_C * (1.0 + 3.0 * 0.044715 * x2)
    return g, dg


def _gelu(x):
    t = jnp.tanh(_GELU_C * (x + 0.044715 * x * x * x))
    return x * (0.5 * (1.0 + t))


def _log1p(u):
    w = 1.0 + u
    d = w - 1.0
    return jnp.where(d == 0.0, u, jnp.log(w) * (u / jnp.where(d == 0.0, 1.0, d)))


def _softplus(y):
    return jnp.maximum(y, 0.0) + _log1p(jnp.exp(-jnp.abs(y)))


def _log_sigmoid(z):
    return -_softplus(-z)


def _neg_expm1(x):
    poly = x * (1.0 + x * (1.0 / 2 + x * (1.0 / 6 + x * (1.0 / 24 + x * (1.0 / 120 + x * (1.0 / 720 + x * (1.0 / 5040)))))))
    return -jnp.where(x > -0.25, poly, jnp.exp(x) - 1.0)


def _split3(x):
    hi = x.astype(BF16)
    r1 = x - hi.astype(F32)
    mid = r1.astype(BF16)
    lo = (r1 - mid.astype(F32)).astype(BF16)
    return hi, mid, lo


def _shift_down(x, halo, k):
    rolled = pltpu.roll(x, k, axis=0)
    hal = pltpu.roll(halo, k, axis=0)
    r8 = lax.broadcasted_iota(jnp.int32, halo.shape, 0)
    head = jnp.where(r8 < k, hal, rolled[:SUBLANES])
    if x.shape[0] == SUBLANES:
        return head
    return jnp.concatenate([head, rolled[SUBLANES:]], axis=0)


def _shift_up(x, nxt, k):
    n = x.shape[0]
    rolled = pltpu.roll(x, n - k, axis=0)
    nx = pltpu.roll(nxt, SUBLANES - k, axis=0)
    r8 = lax.broadcasted_iota(jnp.int32, nxt.shape, 0)
    tail = jnp.where(r8 >= SUBLANES - k, nx, rolled[n - SUBLANES:])
    if n == SUBLANES:
        return tail
    return jnp.concatenate([rolled[:n - SUBLANES], tail], axis=0)


def _split_spec(arr_ndim, part_cols, br, bc, idx):
    if arr_ndim == 3:
        nbh = part_cols // bc
        return pl.BlockSpec((None, br, bc), lambda i, j, k: (lax.div(idx(i, j, k)[1], nbh), idx(i, j, k)[0],
                                                             lax.rem(idx(i, j, k)[1], nbh)))
    return pl.BlockSpec((br, bc), lambda i, j, k: idx(i, j, k))


def _dims(arr):
    if arr.ndim == 3:
        return arr.shape[1], arr.shape[0] * arr.shape[2], arr.shape[2]
    return arr.shape[0], arr.shape[1], arr.shape[1]


def _mm(a, b, *, ta=False, tb=False, out_dtype=F32, out_split=1, add=None, add_scale=1.0,
        tm_cap=1024, tn_cap=1024, tk_cap=1408, name):
    ar, ac, apart = _dims(a)
    br_, bc_, bpart = _dims(b)
    m, kdim = (ac, ar) if ta else (ar, ac)
    kdim_b, n = (bc_, br_) if tb else (br_, bc_)
    assert kdim == kdim_b, (name, a.shape, b.shape)
    tm = _tile(apart, tm_cap) if ta else _tile(m, tm_cap, SUBLANES)
    tn = _tile(n, tn_cap, SUBLANES) if tb else _tile(math.gcd(bpart, n // out_split), tn_cap)
    if ta:
        tk = _tile(kdim, 1024, 2 * SUBLANES)
    elif tb:
        tk = _tile(math.gcd(apart, bpart), tk_cap)
    else:
        tk = _tile(apart, tk_cap)
    assert m % tm == 0 and n % tn == 0 and kdim % tk == 0, (name, m, n, kdim, tm, tn, tk)
    nk = kdim // tk
    a_spec = (_split_spec(a.ndim, apart, tk, tm, lambda i, j, k: (k, i)) if ta
              else _split_spec(a.ndim, apart, tm, tk, lambda i, j, k: (i, k)))
    b_spec = (_split_spec(b.ndim, bpart, tn, tk, lambda i, j, k: (j, k)) if tb
              else _split_spec(b.ndim, bpart, tk, tn, lambda i, j, k: (k, j)))
    if out_split > 1:
        out_shape = jax.ShapeDtypeStruct((out_split, m, n // out_split), out_dtype)
        o_spec = _split_spec(3, n // out_split, tm, tn, lambda i, j, k: (i, j))
    else:
        out_shape = jax.ShapeDtypeStruct((m, n), out_dtype)
        o_spec = pl.BlockSpec((tm, tn), lambda i, j, k: (i, j))
    dn = (((0 if ta else 1,), (1 if tb else 0,)), ((), ()))
    in_specs = [a_spec, b_spec]
    args = [a, b]
    if add is not None:
        in_specs.append(pl.BlockSpec((tm, tn), lambda i, j, k: (i, j)))
        args.append(add)
    use_acc = nk > 1
    has_add = add is not None

    def body(*refs):
        a_ref, b_ref = refs[0], refs[1]
        add_ref = refs[2] if has_add else None
        o_ref = refs[3] if has_add else refs[2]
        part = lax.dot_general(a_ref[...], b_ref[...], dn, preferred_element_type=F32)

        def finish(acc):
            if has_add:
                acc = acc + add_scale * add_ref[...]
            o_ref[...] = acc.astype(out_dtype)

        if not use_acc:
            finish(part)
        else:
            acc_ref = refs[-1]
            k = pl.program_id(2)

            @pl.when(k == 0)
            def _():
                acc_ref[...] = part

            @pl.when(k > 0)
            def _():
                acc_ref[...] += part

            @pl.when(k == nk - 1)
            def _():
                finish(acc_ref[...])

    return pl.pallas_call(
        body, name=name, grid=(m // tm, n // tn, nk), in_specs=in_specs, out_specs=o_spec, out_shape=out_shape,
        scratch_shapes=[pltpu.VMEM((tm, tn), F32)] if use_acc else [],
        compiler_params=_cparams(("parallel", "parallel", "arbitrary")),
    )(*args)


def _ln_fwd(x, m, g, b, *, name, tm=512):
    t, d = x.shape

    def body(x_ref, m_ref, g_ref, b_ref, y_ref, yb_ref, xhat_ref, rstd_ref):
        z = ALPHA * x_ref[...] + m_ref[...]
        mu = jnp.mean(z, axis=-1, keepdims=True)
        zc = z - mu
        var = jnp.mean(zc * zc, axis=-1, keepdims=True)
        rstd = lax.rsqrt(var + LN_EPS)
        xhat = zc * rstd
        y = xhat * g_ref[...] + b_ref[...]
        y_ref[...] = y
        yb_ref[...] = y.astype(BF16)
        xhat_ref[...] = xhat
        rstd_ref[...] = rstd

    row = pl.BlockSpec((tm, d), lambda i: (i, 0))
    vec = pl.BlockSpec((1, d), lambda i: (0, 0))
    return pl.pallas_call(
        body, name=name, grid=(t // tm,), in_specs=[row, row, vec, vec],
        out_specs=[row, row, row, pl.BlockSpec((tm, 1), lambda i: (i, 0))],
        out_shape=[jax.ShapeDtypeStruct((t, d), F32), jax.ShapeDtypeStruct((t, d), BF16),
                   jax.ShapeDtypeStruct((t, d), F32), jax.ShapeDtypeStruct((t, 1), F32)],
        compiler_params=_cparams(("parallel",)),
    )(x, m, g, b)


def _ln_bwd(dy, xhat, rstd, g, *, name, tm=512):
    t, d = dy.shape

    def body(dy_ref, xhat_ref, rstd_ref, g_ref, dz_ref, dzb_ref, dg_ref, db_ref):
        @pl.when(pl.program_id(0) == 0)
        def _():
            dg_ref[...] = jnp.zeros_like(dg_ref)
            db_ref[...] = jnp.zeros_like(db_ref)

        dyv = dy_ref[...]
        xh = xhat_ref[...]
        dg_ref[...] += jnp.sum(dyv * xh, axis=0, keepdims=True)
        db_ref[...] += jnp.sum(dyv, axis=0, keepdims=True)
        dxh = dyv * g_ref[...]
        m1 = jnp.mean(dxh, axis=-1, keepdims=True)
        m2 = jnp.mean(dxh * xh, axis=-1, keepdims=True)
        dz = rstd_ref[...] * (dxh - m1 - xh * m2)
        dz_ref[...] = dz
        dzb_ref[...] = dz.astype(BF16)

    row = pl.BlockSpec((tm, d), lambda i: (i, 0))
    vec = pl.BlockSpec((1, d), lambda i: (0, 0))
    return pl.pallas_call(
        body, name=name, grid=(t // tm,), in_specs=[row, row, pl.BlockSpec((tm, 1), lambda i: (i, 0)), vec],
        out_specs=[row, row, vec, vec],
        out_shape=[jax.ShapeDtypeStruct((t, d), F32), jax.ShapeDtypeStruct((t, d), BF16),
                   jax.ShapeDtypeStruct((1, d), F32), jax.ShapeDtypeStruct((1, d), F32)],
        compiler_params=_cparams(("arbitrary",)),
    )(dy, xhat, rstd, g)


def _ple_fwd(x2, gl, pe, gate_b, *, name, tm=512):
    t, d = x2.shape

    def body(x_ref, gl_ref, pe_ref, b_ref, y_ref, yb_ref):
        y = x_ref[...] + _sigmoid(gl_ref[...] + b_ref[...]) * pe_ref[...]
        y_ref[...] = y
        yb_ref[...] = y.astype(BF16)

    row = pl.BlockSpec((tm, d), lambda i: (i, 0))
    vec = pl.BlockSpec((1, d), lambda i: (0, 0))
    return pl.pallas_call(
        body, name=name, grid=(t // tm,), in_specs=[row, row, row, vec], out_specs=[row, row],
        out_shape=[jax.ShapeDtypeStruct((t, d), F32), jax.ShapeDtypeStruct((t, d), BF16)],
        compiler_params=_cparams(("parallel",)),
    )(x2, gl, pe, gate_b)


def _ple_bwd(dx3, gl, pe, gate_b, *, name, tm=512):
    t, d = dx3.shape

    def body(dx_ref, gl_ref, pe_ref, b_ref, dgl_ref, dpe_ref, db_ref):
        @pl.when(pl.program_id(0) == 0)
        def _():
            db_ref[...] = jnp.zeros_like(db_ref)

        dx = dx_ref[...]
        gt = _sigmoid(gl_ref[...] + b_ref[...])
        dgl = dx * pe_ref[...] * gt * (1.0 - gt)
        db_ref[...] += jnp.sum(dgl, axis=0, keepdims=True)
        dgl_ref[...] = dgl.astype(BF16)
        dpe_ref[...] = (dx * gt).astype(BF16)

    row = pl.BlockSpec((tm, d), lambda i: (i, 0))
    vec = pl.BlockSpec((1, d), lambda i: (0, 0))
    return pl.pallas_call(
        body, name=name, grid=(t // tm,), in_specs=[row, row, row, vec], out_specs=[row, row, vec],
        out_shape=[jax.ShapeDtypeStruct((t, d), BF16), jax.ShapeDtypeStruct((t, d), BF16),
                   jax.ShapeDtypeStruct((1, d), F32)],
        compiler_params=_cparams(("arbitrary",)),
    )(dx3, gl, pe, gate_b)


def _loss_bwd(y, tgt, *, name, tm=512):
    t, d = y.shape

    def body(y_ref, t_ref, dy_ref, l_ref):
        @pl.when(pl.program_id(0) == 0)
        def _():
            l_ref[...] = jnp.zeros_like(l_ref)

        err = y_ref[...] - t_ref[...]
        dy_ref[...] = err * (1.0 / d)
        part = jnp.sum(jnp.sum(err * err, axis=0, keepdims=True), axis=1, keepdims=True) * (0.5 / d)
        l_ref[...] += jnp.broadcast_to(part, l_ref.shape)

    row = pl.BlockSpec((tm, d), lambda i: (i, 0))
    return pl.pallas_call(
        body, name=name, grid=(t // tm,), in_specs=[row, row],
        out_specs=[row, pl.BlockSpec((1, LANES), lambda i: (0, 0))],
        out_shape=[jax.ShapeDtypeStruct((t, d), F32), jax.ShapeDtypeStruct((1, LANES), F32)],
        compiler_params=_cparams(("arbitrary",)),
    )(y, tgt)


def _conv_causal(x, halo, w_ref, b, kw):
    acc = x * w_ref[kw - 1:kw, :] + b
    for k in range(kw - 1):
        acc = acc + _shift_down(x, halo, kw - 1 - k) * w_ref[k:k + 1, :]
    return acc


def _ffn_act_fwd(hdn, conv_w, conv_b, *, name, tm=512):
    _, t, f = hdn.shape
    tc = _tile(f, 1408)
    hb = tm // SUBLANES

    def body(h_ref, halo_ref, w_ref, b_ref, a_ref):
        first = pl.program_id(1) == 0
        parts = []
        for s in range(2):
            halo = jnp.where(first, 0.0, halo_ref[s])
            parts.append(_conv_causal(h_ref[s], halo, w_ref.at[s], b_ref[s], CONV_F))
        a_ref[...] = (_gelu(parts[1]) * parts[0]).astype(BF16)

    return pl.pallas_call(
        body, name=name, grid=(f // tc, t // tm),
        in_specs=[pl.BlockSpec((2, tm, tc), lambda j, i: (0, i, j)),
                  pl.BlockSpec((2, SUBLANES, tc), lambda j, i: (0, jnp.maximum(i * hb - 1, 0), j)),
                  pl.BlockSpec((2, CONV_F, tc), lambda j, i: (0, 0, j)),
                  pl.BlockSpec((2, 1, tc), lambda j, i: (0, 0, j))],
        out_specs=pl.BlockSpec((tm, tc), lambda j, i: (i, j)),
        out_shape=jax.ShapeDtypeStruct((t, f), BF16),
        compiler_params=_cparams(("parallel", "arbitrary")),
    )(hdn, hdn, conv_w, conv_b)


def _ffn_act_bwd(da, hdn, conv_w, conv_b, *, name, tm=256):
    _, t, f = hdn.shape
    tc = _tile(f, 1408)
    hb = tm // SUBLANES
    nr = t // tm

    def body(da_ref, h_ref, halo_ref, w_ref, b_ref, dh_ref, dw_ref, db_ref, carry_ref):
        i = pl.program_id(1)
        r = nr - 1 - i

        @pl.when(i == 0)
        def _():
            dw_ref[...] = jnp.zeros_like(dw_ref)
            db_ref[...] = jnp.zeros_like(db_ref)
            carry_ref[...] = jnp.zeros_like(carry_ref)

        hs, hcs = [], []
        for s in range(2):
            halo = jnp.where(r == 0, 0.0, halo_ref[s])
            hs.append((h_ref[s], halo))
            hcs.append(_conv_causal(h_ref[s], halo, w_ref.at[s], b_ref[s], CONV_F))
        val, gate = hcs
        g, dg = _gelu_and_grad(gate)
        dav = da_ref[...]
        dhc = [dav * g, dav * val * dg]
        for s in range(2):
            d = dhc[s]
            x, halo = hs[s]
            nxt = carry_ref[s]
            db_ref[s] += jnp.sum(d, axis=0, keepdims=True)
            out = d * w_ref[s, CONV_F - 1:CONV_F, :]
            dw_ref[s, CONV_F - 1:CONV_F, :] += jnp.sum(d * x, axis=0, keepdims=True)
            for k in range(CONV_F - 1):
                sh = CONV_F - 1 - k
                out = out + _shift_up(d, nxt, sh) * w_ref[s, k:k + 1, :]
                dw_ref[s, k:k + 1, :] += jnp.sum(d * _shift_down(x, halo, sh), axis=0, keepdims=True)
            dh_ref[s] = out.astype(BF16)
            carry_ref[s] = d[:SUBLANES]

    return pl.pallas_call(
        body, name=name, grid=(f // tc, nr),
        in_specs=[pl.BlockSpec((tm, tc), lambda j, i: (nr - 1 - i, j)),
                  pl.BlockSpec((2, tm, tc), lambda j, i: (0, nr - 1 - i, j)),
                  pl.BlockSpec((2, SUBLANES, tc), lambda j, i: (0, jnp.maximum((nr - 1 - i) * hb - 1, 0), j)),
                  pl.BlockSpec((2, CONV_F, tc), lambda j, i: (0, 0, j)),
                  pl.BlockSpec((2, 1, tc), lambda j, i: (0, 0, j))],
        out_specs=[pl.BlockSpec((2, tm, tc), lambda j, i: (0, nr - 1 - i, j)),
                   pl.BlockSpec((2, CONV_F, tc), lambda j, i: (0, 0, j)),
                   pl.BlockSpec((2, 1, tc), lambda j, i: (0, 0, j))],
        out_shape=[jax.ShapeDtypeStruct((2, t, f), BF16), jax.ShapeDtypeStruct((2, CONV_F, f), F32),
                   jax.ShapeDtypeStruct((2, 1, f), F32)],
        scratch_shapes=[pltpu.VMEM((2, SUBLANES, tc), F32)],
        compiler_params=_cparams(("arbitrary", "arbitrary")),
    )(da, hdn, hdn, conv_w, conv_b)


def _fgate_fwd(xb, w_f, b_f, *, name, tm=256):
    t, d = xb.shape

    def body(x_ref, w_ref, b_ref, z_ref, c_ref, carry_ref):
        @pl.when(pl.program_id(0) == 0)
        def _():
            carry_ref[...] = jnp.zeros_like(carry_ref)

        z = jnp.dot(x_ref[...], w_ref[...], preferred_element_type=F32) + b_ref[...]
        z_ref[...] = z
        ls = _log_sigmoid(z)
        rr = lax.broadcasted_iota(jnp.int32, (tm, tm), 0)
        cc = lax.broadcasted_iota(jnp.int32, (tm, tm), 1)
        tri = (cc <= rr).astype(BF16)
        cum = carry_ref[...]
        for piece in _split3(ls):
            cum = cum + jnp.dot(tri, piece, preferred_element_type=F32)
        c_ref[...] = cum
        carry_ref[...] = cum[tm - 1:tm, :]

    return pl.pallas_call(
        body, name=name, grid=(t // tm,),
        in_specs=[pl.BlockSpec((tm, d), lambda i: (i, 0)), pl.BlockSpec((d, LANES), lambda i: (0, 0)),
                  pl.BlockSpec((1, LANES), lambda i: (0, 0))],
        out_specs=[pl.BlockSpec((tm, LANES), lambda i: (i, 0)), pl.BlockSpec((tm, LANES), lambda i: (i, 0))],
        out_shape=[jax.ShapeDtypeStruct((t, LANES), F32), jax.ShapeDtypeStruct((t, LANES), F32)],
        scratch_shapes=[pltpu.VMEM((1, LANES), F32)],
        compiler_params=_cparams(("arbitrary",)),
    )(xb, w_f, b_f)


def _fgate_bwd(dc, z, *, name, tm=256):
    t = dc.shape[0]
    nr = t // tm

    def body(dc_ref, z_ref, dz_ref, db_ref, carry_ref):
        @pl.when(pl.program_id(0) == 0)
        def _():
            carry_ref[...] = jnp.zeros_like(carry_ref)
            db_ref[...] = jnp.zeros_like(db_ref)

        rr = lax.broadcasted_iota(jnp.int32, (tm, tm), 0)
        cc = lax.broadcasted_iota(jnp.int32, (tm, tm), 1)
        tri = (cc >= rr).astype(BF16)
        cum = carry_ref[...]
        for piece in _split3(dc_ref[...]):
            cum = cum + jnp.dot(tri, piece, preferred_element_type=F32)
        carry_ref[...] = cum[0:1, :]
        dz = cum * _sigmoid(-z_ref[...])
        db_ref[...] += jnp.sum(dz, axis=0, keepdims=True)
        dz_ref[...] = dz.astype(BF16)

    rev = pl.BlockSpec((tm, LANES), lambda i: (nr - 1 - i, 0))
    return pl.pallas_call(
        body, name=name, grid=(nr,), in_specs=[rev, rev],
        out_specs=[rev, pl.BlockSpec((1, LANES), lambda i: (0, 0))],
        out_shape=[jax.ShapeDtypeStruct((t, LANES), BF16), jax.ShapeDtypeStruct((1, LANES), F32)],
        scratch_shapes=[pltpu.VMEM((1, LANES), F32)],
        compiler_params=_cparams(("arbitrary",)),
    )(dc, z)


def _head_lane_select(blk, head):
    lane = lax.broadcasted_iota(jnp.int32, (1, LANES), 1)
    return jnp.sum(jnp.where(lane == head, blk, 0.0), axis=1, keepdims=True)


def _flash_fwd(qkv, c, crow, *, name, tq=512):
    _, t, d = qkv.shape
    nq = t // tq
    n_pairs = d // LANES

    def body(q_ref, k_ref, v_ref, c_ref, cr0_ref, cr1_ref, o_ref, lse_ref):
        hp = pl.program_id(0)
        i = pl.program_id(1)
        lane = lax.broadcasted_iota(jnp.int32, (1, LANES), 1)
        rr = lax.broadcasted_iota(jnp.int32, (tq, tq), 0)
        cc = lax.broadcasted_iota(jnp.int32, (tq, tq), 1)
        causal = cc <= rr
        q = q_ref[...]
        cblk = c_ref[...]
        outs, lses = [], []
        for hh, cr_ref in enumerate((cr0_ref, cr1_ref)):
            sel = (lane < HEAD_DIM) if hh == 0 else (lane >= HEAD_DIM)
            qh = jnp.where(sel, q, jnp.zeros_like(q)) * jnp.asarray(ATTN_SCALE, BF16)
            ci = _head_lane_select(cblk, 2 * hp + hh)

            def kv_step(j, carry, masked, qh=qh, ci=ci, cr_ref=cr_ref):
                m, l, acc = carry
                off = pl.multiple_of(j * tq, tq)
                kj = k_ref[pl.ds(off, tq), :]
                vj = v_ref[pl.ds(off, tq), :]
                s = lax.dot_general(qh, kj, (((1,), (1,)), ((), ())), preferred_element_type=F32)
                s = s + (ci - cr_ref[:, pl.ds(off, tq)])
                if masked:
                    s = jnp.where(causal, s, NEG_BIG)
                m_new = jnp.maximum(m, jnp.max(s, axis=1, keepdims=True))
                p = jnp.exp(s - m_new)
                corr = jnp.exp(m - m_new)
                l = l * corr + jnp.sum(p, axis=1, keepdims=True)
                acc = acc * corr + jnp.dot(p.astype(BF16), vj, preferred_element_type=F32)
                return m_new, l, acc

            init = (jnp.full((tq, 1), NEG_BIG, F32), jnp.zeros((tq, 1), F32), jnp.zeros((tq, LANES), F32))
            carry = lax.fori_loop(0, i, functools.partial(kv_step, masked=False), init)
            m, l, acc = kv_step(i, carry, True)
            outs.append(acc / l)
            lses.append(m + jnp.log(l))
        o_ref[...] = jnp.where(lane < HEAD_DIM, outs[0], outs[1]).astype(BF16)
        lse_ref[...] = jnp.where(lane == 0, lses[0], jnp.where(lane == 1, lses[1], 0.0))

    return pl.pallas_call(
        body, name=name, grid=(n_pairs, nq),
        in_specs=[pl.BlockSpec((None, tq, LANES), lambda h, i: (0, i, h)),
                  pl.BlockSpec((None, t, LANES), lambda h, i: (1, 0, h)),
                  pl.BlockSpec((None, t, LANES), lambda h, i: (2, 0, h)),
                  pl.BlockSpec((tq, LANES), lambda h, i: (i, 0)),
                  pl.BlockSpec((None, 1, t), lambda h, i: (2 * h, 0, 0)),
                  pl.BlockSpec((None, 1, t), lambda h, i: (2 * h + 1, 0, 0))],
        out_specs=[pl.BlockSpec((tq, LANES), lambda h, i: (i, h)),
                   pl.BlockSpec((None, tq, LANES), lambda h, i: (h, i, 0))],
        out_shape=[jax.ShapeDtypeStruct((t, d), BF16), jax.ShapeDtypeStruct((n_pairs, t, LANES), F32)],
        compiler_params=_cparams(("parallel", "arbitrary")),
    )(qkv, qkv, qkv, c, crow, crow)


def _attn_delta(o, do, *, name, tm=512):
    t, d = o.shape

    def body(o_ref, do_ref, out_ref):
        prod = o_ref[...].astype(F32) * do_ref[...].astype(F32)
        col = lax.broadcasted_iota(jnp.int32, (d, LANES), 0)
        head = lax.broadcasted_iota(jnp.int32, (d, LANES), 1)
        sel = (lax.div(col, HEAD_DIM) == head).astype(BF16)
        acc = jnp.zeros((tm, LANES), F32)
        for piece in _split3(prod):
            acc = acc + jnp.dot(piece, sel, preferred_element_type=F32)
        out_ref[...] = acc

    row = pl.BlockSpec((tm, d), lambda i: (i, 0))
    return pl.pallas_call(
        body, name=name, grid=(t // tm,), in_specs=[row, row],
        out_specs=pl.BlockSpec((tm, LANES), lambda i: (i, 0)),
        out_shape=jax.ShapeDtypeStruct((t, LANES), F32),
        compiler_params=_cparams(("parallel",)),
    )(o, do)


def _flash_bwd(qkv, do, c, crow, lse_row, delta_row, *, name, tq=512):
    _, t, d = qkv.shape
    nq = t // tq
    n_pairs = d // LANES

    def body(q_ref, do_ref, k_ref, v_ref, c_ref, cr0, cr1, ls0, ls1, dl0, dl1,
             dq_ref, dkv_ref, dc_ref, dcq_ref, dq_acc):
        hp = pl.program_id(0)
        j = pl.program_id(1)
        lane = lax.broadcasted_iota(jnp.int32, (1, LANES), 1)
        kr = lax.broadcasted_iota(jnp.int32, (tq, tq), 0)
        qc = lax.broadcasted_iota(jnp.int32, (tq, tq), 1)
        causal_t = kr <= qc

        @pl.when(j == 0)
        def _():
            dq_acc[...] = jnp.zeros_like(dq_acc)
            dcq_ref[...] = jnp.zeros_like(dcq_ref)

        k = k_ref[...]
        v = v_ref[...]
        cblk = c_ref[...]
        dks, dvs, dcs = [], [], []
        for hh, (cr_ref, ls_ref, dl_ref) in enumerate(((cr0, ls0, dl0), (cr1, ls1, dl1))):
            sel = (lane < HEAD_DIM) if hh == 0 else (lane >= HEAD_DIM)
            kh = jnp.where(sel, k, jnp.zeros_like(k)) * jnp.asarray(ATTN_SCALE, BF16)
            vh = jnp.where(sel, v, jnp.zeros_like(v))
            cj = _head_lane_select(cblk, 2 * hp + hh)

            def q_step(i, carry, masked, hh=hh, kh=kh, vh=vh, cj=cj, cr_ref=cr_ref, ls_ref=ls_ref, dl_ref=dl_ref):
                dk, dv, dcj = carry
                off = pl.multiple_of(i * tq, tq)
                qi = q_ref[pl.ds(off, tq), :]
                doi = do_ref[pl.ds(off, tq), :]
                st = lax.dot_general(kh, qi, (((1,), (1,)), ((), ())), preferred_element_type=F32)
                st = st + (cr_ref[:, pl.ds(off, tq)] - cj)
                if masked:
                    st = jnp.where(causal_t, st, NEG_BIG)
                pt = jnp.exp(st - ls_ref[:, pl.ds(off, tq)])
                dv = dv + jnp.dot(pt.astype(BF16), doi, preferred_element_type=F32)
                dpt = lax.dot_general(vh, doi, (((1,), (1,)), ((), ())), preferred_element_type=F32)
                dst = pt * (dpt - dl_ref[:, pl.ds(off, tq)])
                dcj = dcj + jnp.sum(dst, axis=1, keepdims=True)
                dcq_ref[hh:hh + 1, pl.ds(off, tq)] += jnp.sum(dst, axis=0, keepdims=True)
                dsb = dst.astype(BF16)
                dk = dk + jnp.dot(dsb, qi, preferred_element_type=F32)
                dq_acc[pl.ds(off, tq), :] += lax.dot_general(dsb, kh, (((0,), (0,)), ((), ())),
                                                             preferred_element_type=F32)
                return dk, dv, dcj

            init = (jnp.zeros((tq, LANES), F32), jnp.zeros((tq, LANES), F32), jnp.zeros((tq, 1), F32))
            carry = q_step(j, init, True)
            dk, dv, dcj = lax.fori_loop(j + 1, nq, functools.partial(q_step, masked=False), carry)
            dks.append(dk * ATTN_SCALE)
            dvs.append(dv)
            dcs.append(-dcj)
        dkv_ref[0] = jnp.where(lane < HEAD_DIM, dks[0], dks[1]).astype(BF16)
        dkv_ref[1] = jnp.where(lane < HEAD_DIM, dvs[0], dvs[1]).astype(BF16)
        dc_ref[...] = jnp.where(lane == 0, dcs[0], jnp.where(lane == 1, dcs[1], 0.0))

        @pl.when(j == nq - 1)
        def _():
            dq_ref[...] = dq_acc[...].astype(BF16)

    full = lambda part: pl.BlockSpec((None, t, LANES), lambda h, j: (part, 0, h))
    rowspec = lambda hh: pl.BlockSpec((None, 1, t), lambda h, j: (2 * h + hh, 0, 0))
    return pl.pallas_call(
        body, name=name, grid=(n_pairs, nq),
        in_specs=[full(0),
                  pl.BlockSpec((t, LANES), lambda h, j: (0, h)),
                  pl.BlockSpec((None, tq, LANES), lambda h, j: (1, j, h)),
                  pl.BlockSpec((None, tq, LANES), lambda h, j: (2, j, h)),
                  pl.BlockSpec((tq, LANES), lambda h, j: (j, 0)),
                  rowspec(0), rowspec(1), rowspec(0), rowspec(1), rowspec(0), rowspec(1)],
        out_specs=[pl.BlockSpec((t, LANES), lambda h, j: (0, h)),
                   pl.BlockSpec((2, tq, LANES), lambda h, j: (0, j, h)),
                   pl.BlockSpec((None, tq, LANES), lambda h, j: (h, j, 0)),
                   pl.BlockSpec((None, SUBLANES, t), lambda h, j: (h, 0, 0))],
        out_shape=[jax.ShapeDtypeStruct((t, d), BF16), jax.ShapeDtypeStruct((2, t, d), BF16),
                   jax.ShapeDtypeStruct((n_pairs, t, LANES), F32),
                   jax.ShapeDtypeStruct((n_pairs, SUBLANES, t), F32)],
        scratch_shapes=[pltpu.VMEM((t, LANES), F32)],
        compiler_params=_cparams(("parallel", "arbitrary")),
    )(qkv, do, qkv, qkv, c, crow, crow, lse_row, lse_row, delta_row, delta_row)


def _rows_from_lanes(x, n):
    return x[:, :n].T.reshape(n, 1, x.shape[0])


def _rows_from_pairs(x8):
    hp, t, _ = x8.shape
    return x8[:, :, :2].transpose(0, 2, 1).reshape(2 * hp, 1, t)


def _lanes_from_pairs(x8):
    hp, t, _ = x8.shape
    y = x8[:, :, :2].transpose(1, 0, 2).reshape(t, 2 * hp)
    return jnp.pad(y, ((0, 0), (0, LANES - 2 * hp)))


def _lanes_from_pair_rows(x8):
    hp, _, t = x8.shape
    y = x8[:, :2, :].reshape(2 * hp, t).T
    return jnp.pad(y, ((0, 0), (0, LANES - 2 * hp)))


HM = 2 * HEAD_DIM
LANE_C, LANE_ONE, LANE_LSE = 64, 67, 70
Q_SUB = 256


def _three_parts(col, sign=1.0):
    col = sign * col
    hi = col.astype(BF16).astype(F32)
    r1 = col - hi
    mid = r1.astype(BF16).astype(F32)
    lo = (r1 - mid).astype(BF16).astype(F32)
    return hi, mid, lo


def _fill_lanes(base, lane, first, parts):
    out = base
    for n, part in enumerate(parts):
        out = jnp.where(lane == first + n, part, out)
    return out


def _hm_weight_t(w_t):
    rows, k = w_t.shape
    h = rows // HEAD_DIM
    return jnp.pad(w_t.reshape(h, HEAD_DIM, k), ((0, 0), (0, HM - HEAD_DIM), (0, 0))).reshape(h * HM, k)


def _hm_unpad_t(g_t):
    rows, k = g_t.shape
    h = rows // HM
    return g_t.reshape(h, HM, k)[:, :HEAD_DIM].reshape(h * HEAD_DIM, k)


def _attn_prep(qkv, c, *, name, tm=256):
    t, w = qkv.shape
    nh = w // (3 * HM)

    def body(x_ref, c_ref, o_ref):
        lane = lax.broadcasted_iota(jnp.int32, (1, HM), 1)
        cblk = c_ref[...]
        for h in range(nh):
            ch = cblk[:, h:h + 1]
            pos = _three_parts(ch)
            neg = _three_parts(ch, -1.0)
            for part in range(3):
                col = (part * nh + h) * HM
                x = x_ref[:, col:col + HM].astype(F32)
                if part == 0:
                    y = _fill_lanes(_fill_lanes(x * ATTN_SCALE, lane, LANE_C, pos), lane, LANE_ONE, (1.0, 1.0, 1.0))
                elif part == 1:
                    y = _fill_lanes(_fill_lanes(x, lane, LANE_C, (1.0, 1.0, 1.0)), lane, LANE_ONE, neg)
                    y = _fill_lanes(y, lane, LANE_LSE, (1.0, 1.0, 1.0))
                else:
                    y = _fill_lanes(x, lane, LANE_C, (1.0, 1.0, 1.0))
                o_ref[:, col:col + HM] = y.astype(BF16)

    return pl.pallas_call(
        body, name=name, grid=(t // tm,),
        in_specs=[pl.BlockSpec((tm, w), lambda i: (i, 0)), pl.BlockSpec((tm, LANES), lambda i: (i, 0))],
        out_specs=pl.BlockSpec((tm, w), lambda i: (i, 0)),
        out_shape=jax.ShapeDtypeStruct((t, w), BF16),
        compiler_params=_cparams(("parallel",)),
    )(qkv, c)


def _flash_fwd_hm(qkva, *, name, tq=1024):
    t, w = qkva.shape
    nh = w // (3 * HM)
    nq = t // tq
    nsub = tq // Q_SUB

    def body(q_ref, k_ref, v_ref, o_ref, qb_ref, s_scr, p_scr, m_scr, acc_scr):
        i = pl.program_id(1)
        lane = lax.broadcasted_iota(jnp.int32, (1, HM), 1)
        rr = lax.broadcasted_iota(jnp.int32, (Q_SUB, tq), 0)
        cc = lax.broadcasted_iota(jnp.int32, (Q_SUB, tq), 1)
        sub = lambda r: slice(r * Q_SUB, (r + 1) * Q_SUB)
        nt = (((1,), (1,)), ((), ()))

        def scores(j):
            kj = k_ref[pl.ds(pl.multiple_of(j * tq, tq), tq), :]
            for r in range(nsub):
                s_scr[sub(r), :] = lax.dot_general(q_ref[sub(r), :], kj, nt, preferred_element_type=F32)

        def step(jp, masked):
            vj = v_ref[pl.ds(pl.multiple_of(jp * tq, tq), tq), :]
            for r in range(nsub):
                pv = jnp.dot(p_scr[sub(r), :], vj, preferred_element_type=F32)
                if masked:
                    s_scr[sub(r), :] = jnp.where(cc <= rr + r * Q_SUB, s_scr[sub(r), :], NEG_BIG)
                m_old = m_scr[sub(r), :]
                m_new = jnp.maximum(m_old, jnp.max(s_scr[sub(r), :], axis=1, keepdims=True))
                m_scr[sub(r), :] = m_new
                p_scr[sub(r), :] = jnp.exp(s_scr[sub(r), :] - m_new).astype(BF16)
                acc_scr[sub(r), :] = (acc_scr[sub(r), :] + pv) * jnp.exp(m_old - m_new)

        def kv_step(j, carry):
            step(jnp.maximum(j - 1, 0), False)
            scores(j + 1)
            return carry

        p_scr[...] = jnp.zeros_like(p_scr)
        m_scr[...] = jnp.full_like(m_scr, NEG_BIG)
        acc_scr[...] = jnp.zeros_like(acc_scr)
        scores(0)
        lax.fori_loop(0, i, kv_step, 0)
        step(jnp.maximum(i - 1, 0), True)
        vi = v_ref[pl.ds(pl.multiple_of(i * tq, tq), tq), :]
        for r in range(nsub):
            acc = acc_scr[sub(r), :] + jnp.dot(p_scr[sub(r), :], vi, preferred_element_type=F32)
            l = jnp.sum(jnp.where(lane == LANE_C, acc, 0.0), axis=1, keepdims=True)
            o_ref[sub(r), :] = jnp.where(lane < HEAD_DIM, acc / l, 0.0).astype(BF16)
            lse = m_scr[sub(r), :] + jnp.log(l)
            qb = _fill_lanes(q_ref[sub(r), :].astype(F32), lane, LANE_LSE, _three_parts(lse, -1.0))
            qb_ref[sub(r), :] = qb.astype(BF16)

    blk = lambda part: pl.BlockSpec((t, HM), lambda h, i: (0, part * nh + h))
    tile = pl.BlockSpec((tq, HM), lambda h, i: (i, h))
    return pl.pallas_call(
        body, name=name, grid=(nh, nq), in_specs=[tile, blk(1), blk(2)], out_specs=[tile, tile],
        out_shape=[jax.ShapeDtypeStruct((t, nh * HM), BF16), jax.ShapeDtypeStruct((t, nh * HM), BF16)],
        scratch_shapes=[pltpu.VMEM((tq, tq), F32), pltpu.VMEM((tq, tq), BF16), pltpu.VMEM((tq, 1), F32),
                        pltpu.VMEM((tq, HM), F32)],
        compiler_params=_cparams(("parallel", "arbitrary")),
    )(qkva, qkva, qkva)


def _attn_prep_bwd(o, do, *, name, tm=512):
    t, w = o.shape
    nh = w // HM

    def body(o_ref, do_ref, out_ref):
        lane = lax.broadcasted_iota(jnp.int32, (1, HM), 1)
        for h in range(nh):
            cols = slice(h * HM, (h + 1) * HM)
            dov = do_ref[:, cols].astype(F32)
            delta = jnp.sum(o_ref[:, cols].astype(F32) * dov, axis=1, keepdims=True)
            out_ref[:, cols] = _fill_lanes(dov, lane, LANE_C, _three_parts(delta, -1.0)).astype(BF16)

    row = pl.BlockSpec((tm, w), lambda i: (i, 0))
    return pl.pallas_call(
        body, name=name, grid=(t // tm,), in_specs=[row, row], out_specs=row,
        out_shape=jax.ShapeDtypeStruct((t, w), BF16), compiler_params=_cparams(("parallel",)),
    )(o, do)


def _flash_bwd_hm(qb, doa, qkva, *, name, tq=512):
    t, w = qb.shape
    nh = w // HM
    nq = t // tq
    nsub = tq // Q_SUB
    grp = slice(LANE_C, LANE_C + SUBLANES)

    def body(q_ref, do_ref, k_ref, v_ref, dq_ref, dk_ref, dv_ref, dcq_ref, dcj_ref,
             dq_acc, st_scr, dpt_scr, pt_scr, ds_scr, dk_scr, dv_scr):
        j = pl.program_id(1)
        kr = lax.broadcasted_iota(jnp.int32, (tq, Q_SUB), 0)
        qc = lax.broadcasted_iota(jnp.int32, (tq, Q_SUB), 1)
        sub = lambda r: slice(r * Q_SUB, (r + 1) * Q_SUB)
        nt = (((1,), (1,)), ((), ()))
        tn = (((0,), (0,)), ((), ()))

        @pl.when(j == 0)
        def _():
            dq_acc[...] = jnp.zeros_like(dq_acc)

        def rows_of(i, r):
            return pl.ds(pl.multiple_of(i * tq + r * Q_SUB, Q_SUB), Q_SUB)

        def scores(i):
            for r in range(nsub):
                st_scr[:, sub(r)] = lax.dot_general(k_ref[...], q_ref[rows_of(i, r), :], nt, preferred_element_type=F32)
                dpt_scr[:, sub(r)] = lax.dot_general(v_ref[...], do_ref[rows_of(i, r), :], nt,
                                                     preferred_element_type=F32)

        def step(ip, masked):
            for r in range(nsub):
                dv_scr[...] += jnp.dot(pt_scr[:, sub(r)], do_ref[rows_of(ip, r), :], preferred_element_type=F32)
                dk_scr[...] += jnp.dot(ds_scr[:, sub(r)], q_ref[rows_of(ip, r), :], preferred_element_type=F32)
                dq_acc[rows_of(ip, r), :] += lax.dot_general(ds_scr[:, sub(r)], k_ref[...], tn,
                                                             preferred_element_type=F32)
            for r in range(nsub):
                st = st_scr[:, sub(r)]
                if masked:
                    st = jnp.where(kr <= qc + r * Q_SUB, st, NEG_BIG)
                pt = jnp.exp(st)
                pt_scr[:, sub(r)] = pt.astype(BF16)
                ds_scr[:, sub(r)] = (pt * dpt_scr[:, sub(r)]).astype(BF16)

        def q_step(i, carry):
            step(i - 1, False)
            scores(jnp.minimum(i + 1, nq - 1))
            return carry

        pt_scr[...] = jnp.zeros_like(pt_scr)
        ds_scr[...] = jnp.zeros_like(ds_scr)
        dk_scr[...] = jnp.zeros_like(dk_scr)
        dv_scr[...] = jnp.zeros_like(dv_scr)
        scores(j)
        step(j, True)
        scores(jnp.minimum(j + 1, nq - 1))
        lax.fori_loop(j + 1, nq, q_step, 0)
        for r in range(nsub):
            dv_scr[...] += jnp.dot(pt_scr[:, sub(r)], do_ref[rows_of(nq - 1, r), :], preferred_element_type=F32)
            dk_scr[...] += jnp.dot(ds_scr[:, sub(r)], q_ref[rows_of(nq - 1, r), :], preferred_element_type=F32)
            dq_acc[rows_of(nq - 1, r), :] += lax.dot_general(ds_scr[:, sub(r)], k_ref[...], tn,
                                                             preferred_element_type=F32)
        dk = dk_scr[...]
        dk_ref[...] = dk.astype(BF16)
        dv_ref[...] = dv_scr[...].astype(BF16)
        dcj_ref[...] = dk.T[grp, :]

        @pl.when(j == nq - 1)
        def _():
            dq_ref[...] = (dq_acc[...] * ATTN_SCALE).astype(BF16)
            for cidx in range(nq):
                rows = slice(cidx * tq, (cidx + 1) * tq)
                dcq_ref[:, rows] = dq_acc[rows, :].T[grp, :]

    full = pl.BlockSpec((t, HM), lambda h, j: (0, h))
    ktile = lambda part: pl.BlockSpec((tq, HM), lambda h, j: (j, part * nh + h))
    tile = pl.BlockSpec((tq, HM), lambda h, j: (j, h))
    hm_shape = jax.ShapeDtypeStruct((t, w), BF16)
    row_shape = jax.ShapeDtypeStruct((nh, SUBLANES, t), F32)
    return pl.pallas_call(
        body, name=name, grid=(nh, nq),
        in_specs=[pl.BlockSpec((t, HM), lambda h, j: (0, h), pipeline_mode=pl.Buffered(1)),
                  pl.BlockSpec((t, HM), lambda h, j: (0, h), pipeline_mode=pl.Buffered(1)), ktile(1), ktile(2)],
        out_specs=[full, tile, tile, pl.BlockSpec((None, SUBLANES, t), lambda h, j: (h, 0, 0)),
                   pl.BlockSpec((None, SUBLANES, tq), lambda h, j: (h, 0, j))],
        out_shape=[hm_shape, hm_shape, hm_shape, row_shape, row_shape],
        scratch_shapes=[pltpu.VMEM((t, HM), F32), pltpu.VMEM((tq, tq), F32), pltpu.VMEM((tq, tq), F32),
                        pltpu.VMEM((tq, tq), BF16), pltpu.VMEM((tq, tq), BF16), pltpu.VMEM((tq, HM), F32),
                        pltpu.VMEM((tq, HM), F32)],
        compiler_params=_cparams(("parallel", "arbitrary")),
    )(qb, doa, qkva, qkva)


def _block_diag_dot(xb, w_ref, transpose_w=False):
    outs = []
    for n in range(N_BLOCKS_B):
        xn = xb[:, n * BLOCK_B:(n + 1) * BLOCK_B]
        dn = (((1,), (1 if transpose_w else 0,)), ((), ()))
        outs.append(lax.dot_general(xn, w_ref[n], dn, preferred_element_type=F32))
    return jnp.concatenate(outs, axis=1)


def _rglru_fwd(proj, conv_w, conv_b, w_a, b_a, w_i, b_i, lam, *, name, tm=256):
    _, t, d = proj.shape
    hb = tm // SUBLANES
    ng = tm // SUBLANES

    def body(p_ref, halo_ref, cw_ref, cb_ref, wa_ref, ba_ref, wi_ref, bi_ref, lam_ref,
             xb_ref, r_ref, ig_ref, a_ref, h_ref, y_ref, u_scr, hc_scr):
        first = pl.program_id(0) == 0

        @pl.when(first)
        def _():
            hc_scr[...] = jnp.zeros_like(hc_scr)

        halo = jnp.where(first, 0.0, halo_ref[0])
        xb = _conv_causal(p_ref[0], halo, cw_ref, cb_ref[...], CONV_B)
        xb_ref[...] = xb
        xbb = xb.astype(BF16)
        r = _sigmoid(_block_diag_dot(xbb, wa_ref) + ba_ref[...])
        ig = _sigmoid(_block_diag_dot(xbb, wi_ref) + bi_ref[...])
        r_ref[...] = r
        ig_ref[...] = ig
        log_a = (-LRU_C) * r * _softplus(-lam_ref[...])
        a_ref[...] = jnp.exp(log_a)
        u_scr[...] = jnp.sqrt(_neg_expm1(2.0 * log_a)) * (ig * xb)

        ones8 = jnp.ones((SUBLANES, d), F32)
        zeros8 = jnp.zeros((SUBLANES, d), F32)

        def group(gi, hprev):
            off = pl.multiple_of(gi * SUBLANES, SUBLANES)
            a8 = a_ref[pl.ds(off, SUBLANES), :]
            u8 = u_scr[pl.ds(off, SUBLANES), :]
            for s in (1, 2, 4):
                u8 = a8 * _shift_down(u8, zeros8, s) + u8
                a8 = a8 * _shift_down(a8, ones8, s)
            h8 = a8 * hprev + u8
            h_ref[pl.ds(off, SUBLANES), :] = h8
            return h8[SUBLANES - 1:SUBLANES, :]

        hc_scr[...] = lax.fori_loop(0, ng, group, hc_scr[...])
        y_ref[...] = (h_ref[...] * _gelu(p_ref[1])).astype(BF16)

    row = pl.BlockSpec((tm, d), lambda i: (i, 0))
    vec = pl.BlockSpec((1, d), lambda i: (0, 0))
    wblk = pl.BlockSpec((N_BLOCKS_B, BLOCK_B, BLOCK_B), lambda i: (0, 0, 0))
    f32_td = jax.ShapeDtypeStruct((t, d), F32)
    return pl.pallas_call(
        body, name=name, grid=(t // tm,),
        in_specs=[pl.BlockSpec((2, tm, d), lambda i: (0, i, 0)),
                  pl.BlockSpec((1, SUBLANES, d), lambda i: (0, jnp.maximum(i * hb - 1, 0), 0)),
                  pl.BlockSpec((CONV_B, d), lambda i: (0, 0)), vec, wblk, vec, wblk, vec, vec],
        out_specs=[row, row, row, row, row, row],
        out_shape=[f32_td, f32_td, f32_td, f32_td, f32_td, jax.ShapeDtypeStruct((t, d), BF16)],
        scratch_shapes=[pltpu.VMEM((tm, d), F32), pltpu.VMEM((1, d), F32)],
        compiler_params=_cparams(("arbitrary",)),
    )(proj, proj, conv_w, conv_b, w_a, b_a, w_i, b_i, lam)


def _rglru_bwd(dy, proj, xb, r, ig, a, h, conv_w, w_a, w_i, lam, *, name, tm=256):
    _, t, d = proj.shape
    hb = tm // SUBLANES
    ng = tm // SUBLANES
    nr = t // tm

    def body(dy_ref, p_ref, phalo_ref, xb_ref, r_ref, ig_ref, a_ref, h_ref, hhalo_ref,
             cw_ref, wa_ref, wi_ref, lam_ref,
             dp_ref, dcw_ref, dcb_ref, dba_ref, dbi_ref, dlam_ref, dwa_ref, dwi_ref,
             g_scr, carry_g, carry_dxb):
        i = pl.program_id(0)
        rpos = nr - 1 - i

        @pl.when(i == 0)
        def _():
            for ref in (dcw_ref, dcb_ref, dba_ref, dbi_ref, dlam_ref, dwa_ref, dwi_ref, carry_g, carry_dxb):
                ref[...] = jnp.zeros_like(ref)

        gb = p_ref[1]
        gel, dgel = _gelu_and_grad(gb)
        dyv = dy_ref[...]
        hv = h_ref[...]
        dp_ref[1] = (dyv * hv * dgel).astype(BF16)
        g_scr[...] = dyv * gel
        av = a_ref[...]

        ones8 = jnp.ones((SUBLANES, d), F32)
        zeros8 = jnp.zeros((SUBLANES, d), F32)

        def group(gi, cin):
            off = pl.multiple_of((ng - 1 - gi) * SUBLANES, SUBLANES)
            g8 = g_scr[pl.ds(off, SUBLANES), :]
            a8 = a_ref[pl.ds(off, SUBLANES), :]
            b8 = _shift_up(a8, ones8, 1)
            row8 = lax.broadcasted_iota(jnp.int32, (SUBLANES, d), 0)
            g8 = g8 + jnp.where(row8 == SUBLANES - 1, cin, 0.0)
            b8 = jnp.where(row8 == SUBLANES - 1, 0.0, b8)
            for s in (1, 2, 4):
                g8 = g8 + b8 * _shift_up(g8, zeros8, s)
                b8 = b8 * _shift_up(b8, zeros8, s)
            g_scr[pl.ds(off, SUBLANES), :] = g8
            return a8[0:1, :] * g8[0:1, :]

        carry_g[...] = lax.fori_loop(0, ng, group, carry_g[...])

        du = g_scr[...]
        hhalo = jnp.where(rpos == 0, 0.0, hhalo_ref[...])
        hprev = _shift_down(hv, hhalo, 1)
        da = du * hprev
        rv = r_ref[...]
        igv = ig_ref[...]
        xbv = xb_ref[...]
        sp = _softplus(-lam_ref[...])
        log_a = (-LRU_C) * rv * sp
        mult = jnp.sqrt(_neg_expm1(2.0 * log_a))
        dmult = du * (igv * xbv)
        dig = du * mult * xbv
        dxb = du * mult * igv
        dlog_a = da * av - dmult * (av * av) / mult
        dr = dlog_a * ((-LRU_C) * sp)
        dsp = jnp.sum(dlog_a * ((-LRU_C) * rv), axis=0, keepdims=True)
        dlam_ref[...] += dsp * (-_sigmoid(-lam_ref[...]))
        dra = dr * rv * (1.0 - rv)
        dia = dig * igv * (1.0 - igv)
        dba_ref[...] += jnp.sum(dra, axis=0, keepdims=True)
        dbi_ref[...] += jnp.sum(dia, axis=0, keepdims=True)
        drab = dra.astype(BF16)
        diab = dia.astype(BF16)
        xbb = xbv.astype(BF16)
        dxb = dxb + _block_diag_dot(drab, wa_ref, True) + _block_diag_dot(diab, wi_ref, True)
        tn = (((0,), (0,)), ((), ()))
        for n in range(N_BLOCKS_B):
            sl = slice(n * BLOCK_B, (n + 1) * BLOCK_B)
            dwa_ref[n] += lax.dot_general(xbb[:, sl], drab[:, sl], tn, preferred_element_type=F32)
            dwi_ref[n] += lax.dot_general(xbb[:, sl], diab[:, sl], tn, preferred_element_type=F32)

        xpre = p_ref[0]
        phalo = jnp.where(rpos == 0, 0.0, phalo_ref[0])
        nxt = carry_dxb[...]
        dcb_ref[...] += jnp.sum(dxb, axis=0, keepdims=True)
        out = dxb * cw_ref[CONV_B - 1:CONV_B, :]
        dcw_ref[CONV_B - 1:CONV_B, :] += jnp.sum(dxb * xpre, axis=0, keepdims=True)
        for k in range(CONV_B - 1):
            sh = CONV_B - 1 - k
            out = out + _shift_up(dxb, nxt, sh) * cw_ref[k:k + 1, :]
            dcw_ref[k:k + 1, :] += jnp.sum(dxb * _shift_down(xpre, phalo, sh), axis=0, keepdims=True)
        dp_ref[0] = out.astype(BF16)
        carry_dxb[...] = dxb[:SUBLANES]

    rev = pl.BlockSpec((tm, d), lambda i: (nr - 1 - i, 0))
    halo8 = pl.BlockSpec((SUBLANES, d), lambda i: (jnp.maximum((nr - 1 - i) * hb - 1, 0), 0))
    vec = pl.BlockSpec((1, d), lambda i: (0, 0))
    wblk = pl.BlockSpec((N_BLOCKS_B, BLOCK_B, BLOCK_B), lambda i: (0, 0, 0))
    vec_shape = jax.ShapeDtypeStruct((1, d), F32)
    w_shape = jax.ShapeDtypeStruct((N_BLOCKS_B, BLOCK_B, BLOCK_B), F32)
    return pl.pallas_call(
        body, name=name, grid=(nr,),
        in_specs=[rev, pl.BlockSpec((2, tm, d), lambda i: (0, nr - 1 - i, 0)),
                  pl.BlockSpec((1, SUBLANES, d), lambda i: (0, jnp.maximum((nr - 1 - i) * hb - 1, 0), 0)),
                  rev, rev, rev, rev, rev, halo8,
                  pl.BlockSpec((CONV_B, d), lambda i: (0, 0)), wblk, wblk, vec],
        out_specs=[pl.BlockSpec((2, tm, d), lambda i: (0, nr - 1 - i, 0)),
                   pl.BlockSpec((CONV_B, d), lambda i: (0, 0)), vec, vec, vec, vec, wblk, wblk],
        out_shape=[jax.ShapeDtypeStruct((2, t, d), BF16), jax.ShapeDtypeStruct((CONV_B, d), F32),
                   vec_shape, vec_shape, vec_shape, vec_shape, w_shape, w_shape],
        scratch_shapes=[pltpu.VMEM((tm, d), F32), pltpu.VMEM((1, d), F32), pltpu.VMEM((SUBLANES, d), F32)],
        compiler_params=_cparams(("arbitrary",)),
    )(dy, proj, proj, xb, r, ig, a, h, h, conv_w, w_a, w_i, lam)


def _split_cols(w, s):
    k, c = w.shape[-2:]
    return jnp.moveaxis(w.reshape(w.shape[:-2] + (k, s, c // s)), -2, -3)


def _merge_cols(w):
    s, k, c = w.shape
    return jnp.moveaxis(w, 0, 1).reshape(k, s * c)


def _local_step(x, p, tgt, w):
    t = x.shape[0]
    bf = lambda v: v.astype(BF16)
    saved = []
    xcur = x
    xcur_b = bf(x)
    for i in range(DEPTH):
        j = i // 2
        L = f"l{i}_"
        sv = {'x_in_b': xcur_b}
        if i % 2 == 0:
            w_in_t = bf(w['a_w_in'][j])
            sv['w_qkv_t'] = jnp.concatenate(
                [_hm_weight_t(w_in_t[part * D_MODEL:(part + 1) * D_MODEL]) for part in range(3)])
            sv['w_f'] = jnp.pad(w_in_t[3 * D_MODEL:].T, ((0, 0), (0, LANES - N_HEADS)))
            sv['w_out'] = _hm_weight_t(bf(w['a_w_out'][j]))
            b_f = jnp.pad(w['a_b_f'][j], (0, LANES - N_HEADS)).reshape(1, LANES)
            qkv = _mm(xcur_b, sv['w_qkv_t'], tb=True, out_dtype=BF16, name=L + "qkv")
            z, c = _fgate_fwd(xcur_b, sv['w_f'], b_f, name=L + "fgate")
            qkva = _attn_prep(qkv, c, name=L + "attn_prep")
            o, qb = _flash_fwd_hm(qkva, name=L + "flash_fwd")
            mix = _mm(o, sv['w_out'], name=L + "attn_out")
            sv.update(qkva=qkva, z=z, o=o, qb=qb)
        else:
            sv['w_in'] = bf(w['b_w_in'][j])
            sv['w_out'] = bf(w['b_w_out'][j])
            sv['conv_w'] = w['b_conv_w'][j]
            sv['w_a'] = bf(w['b_w_a'][j])
            sv['w_i'] = bf(w['b_w_i'][j])
            sv['lam'] = w['b_lam'][j].reshape(1, D_MODEL)
            proj = _mm(xcur_b, sv['w_in'], out_split=2, name=L + "rg_in")
            xb, r, ig, a, h, y = _rglru_fwd(
                proj, sv['conv_w'], w['b_conv_b'][j].reshape(1, D_MODEL), sv['w_a'],
                w['b_b_a'][j].reshape(1, D_MODEL), sv['w_i'], w['b_b_i'][j].reshape(1, D_MODEL), sv['lam'],
                name=L + "rglru_fwd")
            mix = _mm(y, sv['w_out'], name=L + "rg_out")
            sv.update(proj=proj, xb=xb, r=r, ig=ig, a=a, h=h, y=y)
        sv['ln1_g'] = w['ln1_g'][i].reshape(1, D_MODEL)
        x1, x1b, xhat1, rstd1 = _ln_fwd(xcur, mix, sv['ln1_g'], w['ln1_b'][i].reshape(1, D_MODEL), name=L + "ln1")
        sv['w_up'] = bf(w['f_w_up'][i])
        sv['w_down'] = bf(w['f_w_down'][i])
        sv['fconv_w'] = _split_cols(w['f_conv_w'][i], 2)
        sv['fconv_b'] = w['f_conv_b'][i].reshape(2, 1, D_FF)
        hdn = _mm(x1b, sv['w_up'], out_split=2, tn_cap=1408, name=L + "ffn_up")
        act = _ffn_act_fwd(hdn, sv['fconv_w'], sv['fconv_b'], name=L + "ffn_act")
        ff = _mm(act, sv['w_down'], tk_cap=2816, name=L + "ffn_down")
        sv['ln2_g'] = w['ln2_g'][i].reshape(1, D_MODEL)
        x2, x2b, xhat2, rstd2 = _ln_fwd(x1, ff, sv['ln2_g'], w['ln2_b'][i].reshape(1, D_MODEL), name=L + "ln2")
        sv['gate_w'] = bf(w['ple_gate_w'][i])
        sv['ple_w'] = bf(w['ple_w'][i])
        sv['gate_b'] = w['ple_gate_b'][i].reshape(1, D_MODEL)
        sv['p_b'] = bf(p[i])
        gl = _mm(x2b, sv['gate_w'], name=L + "ple_gate")
        pe = _mm(sv['p_b'], sv['ple_w'], name=L + "ple_emb")
        x3, x3b = _ple_fwd(x2, gl, pe, sv['gate_b'], name=L + "ple")
        sv.update(xhat1=xhat1, rstd1=rstd1, x1b=x1b, hdn=hdn, act=act, xhat2=xhat2, rstd2=rstd2, x2b=x2b,
                  gl=gl, pe=pe)
        saved.append(sv)
        xcur, xcur_b = x3, x3b

    dx, loss_row = _loss_bwd(xcur, tgt, name="loss")

    g = {n: [None] * w[n].shape[0] for n in WEIGHTS}
    for i in reversed(range(DEPTH)):
        j = i // 2
        L = f"l{i}b_"
        sv = saved[i]
        dgl, dpe, d_gate_b = _ple_bwd(dx, sv['gl'], sv['pe'], sv['gate_b'], name=L + "ple")
        g['ple_gate_b'][i] = d_gate_b[0]
        g['ple_w'][i] = _mm(sv['p_b'], dpe, ta=True, name=L + "ple_emb_dw")
        g['ple_gate_w'][i] = _mm(sv['x2b'], dgl, ta=True, name=L + "ple_gate_dw")
        dx2 = _mm(dgl, sv['gate_w'], tb=True, add=dx, name=L + "ple_gate_dx")
        dz2, dz2b, dg2, db2 = _ln_bwd(dx2, sv['xhat2'], sv['rstd2'], sv['ln2_g'], name=L + "ln2")
        g['ln2_g'][i], g['ln2_b'][i] = dg2[0], db2[0]
        g['f_w_down'][i] = _mm(sv['act'], dz2b, ta=True, tm_cap=1408, name=L + "ffn_down_dw")
        da = _mm(dz2b, sv['w_down'], tb=True, tn_cap=1408, name=L + "ffn_down_dx")
        dhdn, d_fcw, d_fcb = _ffn_act_bwd(da, sv['hdn'], sv['fconv_w'], sv['fconv_b'], name=L + "ffn_act")
        g['f_conv_w'][i] = _merge_cols(d_fcw)
        g['f_conv_b'][i] = d_fcb.reshape(2 * D_FF)
        g['f_w_up'][i] = _mm(sv['x1b'], dhdn, ta=True, tn_cap=1408, name=L + "ffn_up_dw")
        dx1 = _mm(dhdn, sv['w_up'], tb=True, add=dz2, add_scale=ALPHA, name=L + "ffn_up_dx")
        dz1, dz1b, dg1, db1 = _ln_bwd(dx1, sv['xhat1'], sv['rstd1'], sv['ln1_g'], name=L + "ln1")
        g['ln1_g'][i], g['ln1_b'][i] = dg1[0], db1[0]
        if i % 2 == 0:
            g['a_w_out'][j] = _hm_unpad_t(_mm(sv['o'], dz1b, ta=True, name=L + "attn_out_dw"))
            do = _mm(dz1b, sv['w_out'], tb=True, out_dtype=BF16, name=L + "attn_out_dx")
            doa = _attn_prep_bwd(sv['o'], do, name=L + "attn_prep")
            dqkv = _flash_bwd_hm(sv['qb'], doa, sv['qkva'], name=L + "flash_bwd")
            dcq, dcj = dqkv[3], dqkv[4]
            dc = jnp.pad((dcq[:, 0, :] - dcj[:, LANE_ONE - LANE_C, :]).T, ((0, 0), (0, LANES - N_HEADS)))
            dzf, d_b_f = _fgate_bwd(dc, sv['z'], name=L + "fgate")
            g['a_b_f'][j] = d_b_f[0, :N_HEADS]
            xb_in = sv['x_in_b']
            d_w = [_hm_unpad_t(_mm(dqkv[part], xb_in, ta=True, name=L + "qkv"[part] + "_dw")) for part in range(3)]
            d_wf = _mm(xb_in, dzf, ta=True, name=L + "f_dw")
            g['a_w_in'][j] = jnp.concatenate(d_w + [d_wf[:, :N_HEADS].T], axis=0)
            dxa, scale = dz1, ALPHA
            for part in range(3):
                w_part = sv['w_qkv_t'][part * N_HEADS * HM:(part + 1) * N_HEADS * HM]
                dxa = _mm(dqkv[part], w_part, add=dxa, add_scale=scale, name=L + "qkv"[part] + "_dx")
                scale = 1.0
            dx = _mm(dzf, sv['w_f'], tb=True, add=dxa, name=L + "f_dx")
        else:
            g['b_w_out'][j] = _mm(sv['y'], dz1b, ta=True, name=L + "rg_out_dw")
            dy = _mm(dz1b, sv['w_out'], tb=True, name=L + "rg_out_dx")
            dproj, d_cw, d_cb, d_ba, d_bi, d_lam, d_wa, d_wi = _rglru_bwd(
                dy, sv['proj'], sv['xb'], sv['r'], sv['ig'], sv['a'], sv['h'], sv['conv_w'], sv['w_a'], sv['w_i'],
                sv['lam'], name=L + "rglru_bwd")
            g['b_conv_w'][j], g['b_conv_b'][j] = d_cw, d_cb[0]
            g['b_b_a'][j] = d_ba.reshape(N_BLOCKS_B, BLOCK_B)
            g['b_b_i'][j] = d_bi.reshape(N_BLOCKS_B, BLOCK_B)
            g['b_lam'][j] = d_lam[0]
            g['b_w_a'][j], g['b_w_i'][j] = d_wa, d_wi
            g['b_w_in'][j] = _mm(sv['x_in_b'], dproj, ta=True, name=L + "rg_in_dw")
            dx = _mm(dproj, sv['w_in'], tb=True, add=dz1, add_scale=ALPHA, name=L + "rg_in_dx")
    return loss_row, dx, g


def _round_up(n, q):
    return -(-n // q) * q


def _pack(arrs, dtype, row_multiple, lead=0):
    pieces = []
    for a in arrs:
        flat = a.reshape(a.shape[:lead] + (-1,)).astype(dtype)
        n = flat.shape[-1]
        pieces.append(jnp.pad(flat, [(0, 0)] * lead + [(0, _round_up(n, LANES) - n)]))
    flat = jnp.concatenate(pieces, axis=-1)
    rows = _round_up(flat.shape[-1] // LANES, row_multiple)
    flat = jnp.pad(flat, [(0, 0)] * lead + [(0, rows * LANES - flat.shape[-1])])
    return flat.reshape(flat.shape[:lead] + (rows, LANES))


def _unpack(buf, shapes):
    lead = buf.shape[:-2]
    flat = buf.reshape(lead + (-1,))
    out, off = [], 0
    for shp in shapes:
        n = math.prod(shp)
        out.append(flat[..., off:off + n].reshape(lead + tuple(shp)))
        off += _round_up(n, LANES)
    return out


MESH = pl.DeviceIdType.MESH
ANY = pl.BlockSpec(memory_space=pl.ANY)
N_CHUNK = 8


def _all_gather_shards(buf):
    rows, lanes = buf.shape
    half = rows // 2
    ch = half // N_CHUNK
    n_ici = 3 * N_CHUNK

    def body(in_ref, out_ref, send_sems, recv_sems, local_sem):
        x, y, c = lax.axis_index("x"), lax.axis_index("y"), lax.axis_index("c")
        sibling = (x, y, 1 - c)
        chips = [(1 - x, y), (x, 1 - y), (1 - x, 1 - y)]

        def piece(cx, cy, hc, q):
            return out_ref.at[2 * cx + cy, pl.ds(hc * half + q * ch, ch), :]

        def copy(k, src, dst, to):
            return pltpu.make_async_remote_copy(src_ref=src, dst_ref=dst, send_sem=send_sems.at[k],
                                                recv_sem=recv_sems.at[k], device_id=to, device_id_type=MESH)

        mine = pltpu.make_async_copy(in_ref, out_ref.at[2 * x + y], local_sem)
        mine.start()
        my_chunk = lambda q: in_ref.at[pl.ds(c * half + q * ch, ch), :]
        first, passed = [], []
        for k, chip in enumerate(chips):
            for q in range(N_CHUNK):
                first.append(copy(k * N_CHUNK + q, my_chunk(q), piece(x, y, c, q), (*chip, c)))
                passed.append(copy(n_ici + k * N_CHUNK + q, piece(*chip, c, q), piece(*chip, c, q), sibling))
        for cp in first:
            cp.start()
        for k, chip in enumerate(chips):
            for q in range(N_CHUNK):
                n = k * N_CHUNK + q
                copy(n, my_chunk(q), piece(*chip, c, q), (*chip, c)).wait_recv()
                passed[n].start()
        for k, chip in enumerate(chips):
            for q in range(N_CHUNK):
                copy(n_ici + k * N_CHUNK + q, my_chunk(q), piece(*chip, 1 - c, q), sibling).wait_recv()
        for cp in first + passed:
            cp.wait_send()
        mine.wait()

    return pl.pallas_call(
        body, name="gather_weights", in_specs=[ANY], out_specs=ANY,
        out_shape=jax.ShapeDtypeStruct((N_CHIP, rows, lanes), buf.dtype),
        scratch_shapes=[pltpu.SemaphoreType.DMA((2 * n_ici,)), pltpu.SemaphoreType.DMA((2 * n_ici,)),
                        pltpu.SemaphoreType.DMA],
    )(buf)


def _exchange_pieces(pieces):
    n, prow, lanes = pieces.shape

    def body(in_ref, out_ref, send_sems, recv_sems, local_sem):
        x, y, c = lax.axis_index("x"), lax.axis_index("y"), lax.axis_index("c")
        me = 4 * x + 2 * y + c
        mine = pltpu.make_async_copy(in_ref.at[me], out_ref.at[me], local_sem)
        mine.start()
        copies = []
        for k in range(1, N_DEV):
            px, py, pc = x ^ (k >> 2), y ^ ((k >> 1) & 1), c ^ (k & 1)
            peer = 4 * px + 2 * py + pc
            cp = pltpu.make_async_remote_copy(src_ref=in_ref.at[peer], dst_ref=out_ref.at[me],
                                              send_sem=send_sems.at[k - 1], recv_sem=recv_sems.at[k - 1],
                                              device_id=(px, py, pc), device_id_type=MESH)
            cp.start()
            copies.append(cp)
        for cp in copies:
            cp.wait()
        mine.wait()

    return pl.pallas_call(
        body, name="exchange_grads", in_specs=[ANY], out_specs=ANY,
        out_shape=jax.ShapeDtypeStruct((n, prow, lanes), pieces.dtype),
        scratch_shapes=[pltpu.SemaphoreType.DMA((N_DEV - 1,)), pltpu.SemaphoreType.DMA((N_DEV - 1,)),
                        pltpu.SemaphoreType.DMA],
    )(pieces)


def _sum_slots(slots, *, tm=512):
    n, prow, lanes = slots.shape
    tm = _tile(prow, tm, 2 * SUBLANES)

    def body(s_ref, o_ref):
        acc = s_ref[0].astype(F32)
        for s in range(1, n):
            acc = acc + s_ref[s].astype(F32)
        o_ref[...] = acc

    return pl.pallas_call(
        body, name="sum_grads", grid=(prow // tm,),
        in_specs=[pl.BlockSpec((n, tm, lanes), lambda i: (0, i, 0))],
        out_specs=pl.BlockSpec((tm, lanes), lambda i: (i, 0)),
        out_shape=jax.ShapeDtypeStruct((prow, lanes), F32),
        compiler_params=_cparams(("parallel",)),
    )(slots)


def _share_halves(red, half_rows):
    prow, lanes = red.shape
    rep_rows = prow - half_rows
    ch = half_rows // N_CHUNK

    def body(in_ref, out_ref, send_sems, recv_sems, local_sems):
        x, y, c = lax.axis_index("x"), lax.axis_index("y"), lax.axis_index("c")
        src = lambda q: in_ref.at[pl.ds(q * ch, ch), :]
        dst = lambda q: out_ref.at[pl.ds(c * half_rows + q * ch, ch), :]
        local = [pltpu.make_async_copy(src(q), dst(q), local_sems.at[q]) for q in range(N_CHUNK)]
        local.append(pltpu.make_async_copy(in_ref.at[pl.ds(half_rows, rep_rows), :],
                                           out_ref.at[pl.ds(2 * half_rows, rep_rows), :], local_sems.at[N_CHUNK]))
        remote = [pltpu.make_async_remote_copy(src_ref=src(q), dst_ref=dst(q), send_sem=send_sems.at[q],
                                               recv_sem=recv_sems.at[q], device_id=(x, y, 1 - c), device_id_type=MESH)
                  for q in range(N_CHUNK)]
        for cp in remote + local:
            cp.start()
        for cp in remote + local:
            cp.wait()

    return pl.pallas_call(
        body, name="share_halves", in_specs=[ANY], out_specs=ANY,
        out_shape=jax.ShapeDtypeStruct((2 * half_rows + rep_rows, lanes), red.dtype),
        scratch_shapes=[pltpu.SemaphoreType.DMA((N_CHUNK,)), pltpu.SemaphoreType.DMA((N_CHUNK,)),
                        pltpu.SemaphoreType.DMA((N_CHUNK + 1,))],
    )(red)


def _adamw(wp, gp, mp, vp, *, tm=1024):
    rows, lanes = wp.shape
    tm = _tile(rows, tm, SUBLANES)
    c1 = 1.0 / (1.0 - ADAM_B1 ** ADAM_STEP)
    c2 = 1.0 / (1.0 - ADAM_B2 ** ADAM_STEP)

    def body(w_ref, g_ref, m_ref, v_ref, d_ref, nm_ref, nv_ref):
        g = g_ref[...]
        m = ADAM_B1 * m_ref[...] + (1.0 - ADAM_B1) * g
        v = ADAM_B2 * v_ref[...] + (1.0 - ADAM_B2) * (g * g)
        m_hat = m * c1
        v_hat = v * c2
        d_ref[...] = -ADAM_LR * (m_hat / (jnp.sqrt(v_hat) + ADAM_EPS) + ADAM_WD * w_ref[...])
        nm_ref[...] = m
        nv_ref[...] = v

    spec = pl.BlockSpec((tm, lanes), lambda i: (i, 0))
    shp = jax.ShapeDtypeStruct((rows, lanes), F32)
    return pl.pallas_call(
        body, name="adamw", grid=(rows // tm,), in_specs=[spec] * 4, out_specs=[spec] * 3,
        out_shape=[shp, shp, shp], compiler_params=_cparams(("parallel",)),
    )(wp, gp, mp, vp)


AG_ROW_MULT = 2 * N_CHUNK * 16
GRAD_ROW_MULT = 2048
REP_ROW_MULT = 512


def _shard_to_front(a, axis):
    n = a.shape[axis]
    a = a.reshape(a.shape[:axis] + (N_CHIP, n // N_CHIP) + a.shape[axis + 1:])
    return jnp.moveaxis(a, axis, 0)


def _shards_to_full(a, axis):
    a = jnp.moveaxis(a, 0, axis)
    return a.reshape(a.shape[:axis] + (a.shape[axis] * a.shape[axis + 1],) + a.shape[axis + 2:])


def kernel(x, p, a_w_in, a_b_f, a_w_out, b_w_in, b_conv_w, b_conv_b, b_w_a, b_b_a, b_w_i, b_b_i, b_lam, b_w_out, f_w_up, f_conv_w, f_conv_b, f_w_down, ln1_g, ln1_b, ln2_g, ln2_b, ple_w, ple_gate_w, ple_gate_b, loss_target, m_a_w_in, m_a_b_f, m_a_w_out, m_b_w_in, m_b_conv_w, m_b_conv_b, m_b_w_a, m_b_b_a, m_b_w_i, m_b_b_i, m_b_lam, m_b_w_out, m_f_w_up, m_f_conv_w, m_f_conv_b, m_f_w_down, m_ln1_g, m_ln1_b, m_ln2_g, m_ln2_b, m_ple_w, m_ple_gate_w, m_ple_gate_b, v_a_w_in, v_a_b_f, v_a_w_out, v_b_w_in, v_b_conv_w, v_b_conv_b, v_b_w_a, v_b_b_a, v_b_w_i, v_b_b_i, v_b_lam, v_b_w_out, v_f_w_up, v_f_conv_w, v_f_conv_b, v_f_w_down, v_ln1_g, v_ln1_b, v_ln2_g, v_ln2_b, v_ple_w, v_ple_gate_w, v_ple_gate_b):
    args = dict(locals())
    swap = lambda n, a: jnp.swapaxes(a, -1, -2) if n in TRANSPOSED else a
    axis_of = lambda n: (3 - SHARD_AXIS[n]) if n in TRANSPOSED else SHARD_AXIS[n]
    local_w = {n: swap(n, args[n]) for n in WEIGHTS}
    local_m = {n: swap(n, args['m_' + n]) for n in WEIGHTS}
    local_v = {n: swap(n, args['v_' + n]) for n in WEIGHTS}

    as_pairs = lambda a: lax.bitcast_convert_type(a, BF16)
    from_pairs = lambda a: lax.bitcast_convert_type(a, F32)
    send = [local_w[n] for n in GATHER_BF16] + [as_pairs(local_w[n]) for n in GATHER_F32]
    gathered = _all_gather_shards(_pack(send, BF16, AG_ROW_MULT))
    shapes = [local_w[n].shape for n in GATHER_BF16] + [local_w[n].shape + (2,) for n in GATHER_F32]
    parts = _unpack(gathered, shapes)
    full_w = {}
    for n, part in zip(GATHER_BF16 + GATHER_F32, parts):
        if n in GATHER_F32:
            part = from_pairs(part)
        full_w[n] = _shards_to_full(part, axis_of(n))
    for n in REPLICATED:
        full_w[n] = local_w[n]

    loss_row, grad_x, g = _local_step(x[0], p[:, 0], loss_target[0], full_w)

    shard_rows = _round_up(sum(_round_up(local_w[n].size, LANES) for n in SHARDED) // LANES, GRAD_ROW_MULT)
    half_rows = shard_rows // 2

    def by_chip(n):
        return jnp.concatenate([_shard_to_front(gl, axis_of(n) - 1).reshape(N_CHIP, -1) for gl in g[n]], axis=1)

    sharded_g = _pack([by_chip(n) for n in SHARDED], BF16, GRAD_ROW_MULT, lead=1)
    rep_g = _pack([jnp.concatenate([gl.reshape(-1) for gl in g[n]]) for n in REPLICATED] + [loss_row], F32, REP_ROW_MULT)
    rep_rows = rep_g.shape[0]
    rep_top = lax.reduce_precision(rep_g, 8, 7)
    rep_hi = rep_top.astype(BF16)
    rep_lo = (rep_g - rep_top).astype(BF16)
    pieces = jnp.concatenate([sharded_g.reshape(N_DEV, half_rows, LANES),
                              jnp.broadcast_to(jnp.concatenate([rep_hi, rep_lo]), (N_DEV, 2 * rep_rows, LANES))], axis=1)
    reduced = _sum_slots(_exchange_pieces(pieces))
    reduced = jnp.concatenate([reduced[:half_rows],
                               reduced[half_rows:half_rows + rep_rows] + reduced[half_rows + rep_rows:]])
    g_packed = _share_halves(reduced, half_rows)

    rep_shapes = [local_w[n].shape for n in REPLICATED] + [loss_row.shape]

    def packed(d):
        return jnp.concatenate([_pack([d[n] for n in SHARDED], F32, GRAD_ROW_MULT),
                                _pack([d[n] for n in REPLICATED] + [jnp.zeros_like(loss_row)], F32, REP_ROW_MULT)])

    delta_p, new_m_p, new_v_p = _adamw(packed(local_w), g_packed, packed(local_m), packed(local_v))

    def unpacked(buf):
        d = dict(zip(SHARDED, _unpack(buf[:shard_rows], [local_w[n].shape for n in SHARDED])))
        d.update(zip(REPLICATED, _unpack(buf[shard_rows:], rep_shapes)))
        return [swap(n, d[n]) for n in WEIGHTS]

    loss = _unpack(g_packed[shard_rows:], rep_shapes)[-1][0, 0]
    return (loss, grad_x[None], *unpacked(g_packed), *unpacked(delta_p), *unpacked(new_m_p), *unpacked(new_v_p))
```

```python
import functools
import math

import jax
import jax.numpy as jnp
from jax import lax
from jax.experimental import pallas as pl
from jax.experimental.pallas import tpu as pltpu

F32 = jnp.float32
BF16 = jnp.bfloat16

D_MODEL = 1024
DEPTH = 4
N_HEADS = 16
HEAD_DIM = 64
N_BLOCKS_B = 8
BLOCK_B = 128
CONV_B = 4
LRU_C = 8.0
D_FF = 2816
CONV_F = 3
D_PLE = 256
LN_EPS = 1e-5
ALPHA = (2.0 * DEPTH) ** 0.25
ATTN_SCALE = 1.0 / math.sqrt(HEAD_DIM)

ADAM_LR = 0.001
ADAM_B1 = 0.9
ADAM_B2 = 0.999
ADAM_EPS = 1e-08
ADAM_WD = 0.01
ADAM_STEP = 10

LANES = 128
SUBLANES = 8
VMEM_LIMIT = 52 * 1024 * 1024
NEG_BIG = -1e30
N_DEV = 8
N_CHIP = 4

WEIGHTS = ['a_w_in', 'a_b_f', 'a_w_out', 'b_w_in', 'b_conv_w', 'b_conv_b', 'b_w_a', 'b_b_a', 'b_w_i', 'b_b_i',
           'b_lam', 'b_w_out', 'f_w_up', 'f_conv_w', 'f_conv_b', 'f_w_down', 'ln1_g', 'ln1_b', 'ln2_g', 'ln2_b',
           'ple_w', 'ple_gate_w', 'ple_gate_b']
SHARD_AXIS = {'a_w_in': 2, 'a_w_out': 1, 'b_w_in': 2, 'b_conv_w': 2, 'b_conv_b': 1, 'b_lam': 1, 'b_w_out': 1,
              'f_w_up': 2, 'f_conv_w': 2, 'f_w_down': 1, 'ple_w': 2, 'ple_gate_w': 1}
TRANSPOSED = ('a_w_in',)
NATIVE = ['a_w_out', 'b_w_in', 'b_w_out', 'f_w_up', 'f_w_down', 'ple_w', 'ple_gate_w']
SHARDED = [n for n in WEIGHTS if n in SHARD_AXIS]
REPLICATED = [n for n in WEIGHTS if n not in SHARD_AXIS]
GATHER_BF16 = ['a_w_in', 'a_w_out', 'b_w_in', 'b_w_out', 'f_w_up', 'f_w_down', 'ple_w', 'ple_gate_w']
GATHER_F32 = ['b_conv_w', 'b_conv_b', 'b_lam', 'f_conv_w']


def _cparams(sem, vmem=VMEM_LIMIT):
    return pltpu.CompilerParams(dimension_semantics=sem, vmem_limit_bytes=vmem)


def _tile(n, cap, q=LANES):
    best = None
    for t in range(q, min(n, cap) + 1, q):
        if n % t == 0:
            best = t
    return best if best is not None else n


def _sigmoid(x):
    return 1.0 / (1.0 + jnp.exp(-x))


_GELU_C = math.sqrt(2.0 / math.pi)


def _gelu_and_grad(x):
    x2 = x * x
    t = jnp.tanh(_GELU_C * (x + 0.044715 * x * x2))
    cdf = 0.5 * (1.0 + t)
    g = x * cdf
    dg = cdf + x * 0.5 * (1.0 - t * t) * _GELU_C * (1.0 + 3.0 * 0.044715 * x2)
    return g, dg


def _gelu(x):
    t = jnp.tanh(_GELU_C * (x + 0.044715 * x * x * x))
    return x * (0.5 * (1.0 + t))


def _log1p(u):
    w = 1.0 + u
    d = w - 1.0
    return jnp.where(d == 0.0, u, jnp.log(w) * (u / jnp.where(d == 0.0, 1.0, d)))


def _softplus(y):
    return jnp.maximum(y, 0.0) + _log1p(jnp.exp(-jnp.abs(y)))


def _log_sigmoid(z):
    return -_softplus(-z)


def _neg_expm1(x):
    poly = x * (1.0 + x * (1.0 / 2 + x * (1.0 / 6 + x * (1.0 / 24 + x * (1.0 / 120 + x * (1.0 / 720 + x * (1.0 / 5040)))))))
    return -jnp.where(x > -0.25, poly, jnp.exp(x) - 1.0)


def _split3(x):
    hi = x.astype(BF16)
    r1 = x - hi.astype(F32)
    mid = r1.astype(BF16)
    lo = (r1 - mid.astype(F32)).astype(BF16)
    return hi, mid, lo


def _shift_down(x, halo, k):
    rolled = pltpu.roll(x, k, axis=0)
    hal = pltpu.roll(halo, k, axis=0)
    r8 = lax.broadcasted_iota(jnp.int32, halo.shape, 0)
    head = jnp.where(r8 < k, hal, rolled[:SUBLANES])
    if x.shape[0] == SUBLANES:
        return head
    return jnp.concatenate([head, rolled[SUBLANES:]], axis=0)


def _shift_up(x, nxt, k):
    n = x.shape[0]
    rolled = pltpu.roll(x, n - k, axis=0)
    nx = pltpu.roll(nxt, SUBLANES - k, axis=0)
    r8 = lax.broadcasted_iota(jnp.int32, nxt.shape, 0)
    tail = jnp.where(r8 >= SUBLANES - k, nx, rolled[n - SUBLANES:])
    if n == SUBLANES:
        return tail
    return jnp.concatenate([rolled[:n - SUBLANES], tail], axis=0)


def _split_spec(arr_ndim, part_cols, br, bc, idx):
    if arr_ndim == 3:
        nbh = part_cols // bc
        return pl.BlockSpec((None, br, bc), lambda i, j, k: (lax.div(idx(i, j, k)[1], nbh), idx(i, j, k)[0],
                                                             lax.rem(idx(i, j, k)[1], nbh)))
    return pl.BlockSpec((br, bc), lambda i, j, k: idx(i, j, k))


def _dims(arr):
    if arr.ndim == 3:
        return arr.shape[1], arr.shape[0] * arr.shape[2], arr.shape[2]
    return arr.shape[0], arr.shape[1], arr.shape[1]


def _mm(a, b, *, ta=False, tb=False, out_dtype=F32, out_split=1, add=None, add_scale=1.0,
        tm_cap=1024, tn_cap=1024, tk_cap=1408, name):
    ar, ac, apart = _dims(a)
    br_, bc_, bpart = _dims(b)
    m, kdim = (ac, ar) if ta else (ar, ac)
    kdim_b, n = (bc_, br_) if tb else (br_, bc_)
    assert kdim == kdim_b, (name, a.shape, b.shape)
    tm = _tile(apart, tm_cap) if ta else _tile(m, tm_cap, SUBLANES)
    tn = _tile(n, tn_cap, SUBLANES) if tb else _tile(math.gcd(bpart, n // out_split), tn_cap)
    if ta:
        tk = _tile(kdim, 1024, 2 * SUBLANES)
    elif tb:
        tk = _tile(math.gcd(apart, bpart), tk_cap)
    else:
        tk = _tile(apart, tk_cap)
    assert m % tm == 0 and n % tn == 0 and kdim % tk == 0, (name, m, n, kdim, tm, tn, tk)
    nk = kdim // tk
    a_spec = (_split_spec(a.ndim, apart, tk, tm, lambda i, j, k: (k, i)) if ta
              else _split_spec(a.ndim, apart, tm, tk, lambda i, j, k: (i, k)))
    b_spec = (_split_spec(b.ndim, bpart, tn, tk, lambda i, j, k: (j, k)) if tb
              else _split_spec(b.ndim, bpart, tk, tn, lambda i, j, k: (k, j)))
    if out_split > 1:
        out_shape = jax.ShapeDtypeStruct((out_split, m, n // out_split), out_dtype)
        o_spec = _split_spec(3, n // out_split, tm, tn, lambda i, j, k: (i, j))
    else:
        out_shape = jax.ShapeDtypeStruct((m, n), out_dtype)
        o_spec = pl.BlockSpec((tm, tn), lambda i, j, k: (i, j))
    dn = (((0 if ta else 1,), (1 if tb else 0,)), ((), ()))
    in_specs = [a_spec, b_spec]
    args = [a, b]
    if add is not None:
        in_specs.append(pl.BlockSpec((tm, tn), lambda i, j, k: (i, j)))
        args.append(add)
    use_acc = nk > 1
    has_add = add is not None

    def body(*refs):
        a_ref, b_ref = refs[0], refs[1]
        add_ref = refs[2] if has_add else None
        o_ref = refs[3] if has_add else refs[2]
        part = lax.dot_general(a_ref[...], b_ref[...], dn, preferred_element_type=F32)

        def finish(acc):
            if has_add:
                acc = acc + add_scale * add_ref[...]
            o_ref[...] = acc.astype(out_dtype)

        if not use_acc:
            finish(part)
        else:
            acc_ref = refs[-1]
            k = pl.program_id(2)

            @pl.when(k == 0)
            def _():
                acc_ref[...] = part

            @pl.when(k > 0)
            def _():
                acc_ref[...] += part

            @pl.when(k == nk - 1)
            def _():
                finish(acc_ref[...])

    return pl.pallas_call(
        body, name=name, grid=(m // tm, n // tn, nk), in_specs=in_specs, out_specs=o_spec, out_shape=out_shape,
        scratch_shapes=[pltpu.VMEM((tm, tn), F32)] if use_acc else [],
        compiler_params=_cparams(("parallel", "parallel", "arbitrary")),
    )(*args)


def _ln_fwd(x, m, g, b, *, name, tm=512):
    t, d = x.shape

    def body(x_ref, m_ref, g_ref, b_ref, y_ref, yb_ref, xhat_ref, rstd_ref):
        z = ALPHA * x_ref[...] + m_ref[...]
        mu = jnp.mean(z, axis=-1, keepdims=True)
        zc = z - mu
        var = jnp.mean(zc * zc, axis=-1, keepdims=True)
        rstd = lax.rsqrt(var + LN_EPS)
        xhat = zc * rstd
        y = xhat * g_ref[...] + b_ref[...]
        y_ref[...] = y
        yb_ref[...] = y.astype(BF16)
        xhat_ref[...] = xhat
        rstd_ref[...] = rstd

    row = pl.BlockSpec((tm, d), lambda i: (i, 0))
    vec = pl.BlockSpec((1, d), lambda i: (0, 0))
    return pl.pallas_call(
        body, name=name, grid=(t // tm,), in_specs=[row, row, vec, vec],
        out_specs=[row, row, row, pl.BlockSpec((tm, 1), lambda i: (i, 0))],
        out_shape=[jax.ShapeDtypeStruct((t, d), F32), jax.ShapeDtypeStruct((t, d), BF16),
                   jax.ShapeDtypeStruct((t, d), F32), jax.ShapeDtypeStruct((t, 1), F32)],
        compiler_params=_cparams(("parallel",)),
    )(x, m, g, b)


def _ln_bwd(dy, xhat, rstd, g, *, name, tm=512):
    t, d = dy.shape

    def body(dy_ref, xhat_ref, rstd_ref, g_ref, dz_ref, dzb_ref, dg_ref, db_ref):
        @pl.when(pl.program_id(0) == 0)
        def _():
            dg_ref[...] = jnp.zeros_like(dg_ref)
            db_ref[...] = jnp.zeros_like(db_ref)

        dyv = dy_ref[...]
        xh = xhat_ref[...]
        dg_ref[...] += jnp.sum(dyv * xh, axis=0, keepdims=True)
        db_ref[...] += jnp.sum(dyv, axis=0, keepdims=True)
        dxh = dyv * g_ref[...]
        m1 = jnp.mean(dxh, axis=-1, keepdims=True)
        m2 = jnp.mean(dxh * xh, axis=-1, keepdims=True)
        dz = rstd_ref[...] * (dxh - m1 - xh * m2)
        dz_ref[...] = dz
        dzb_ref[...] = dz.astype(BF16)

    row = pl.BlockSpec((tm, d), lambda i: (i, 0))
    vec = pl.BlockSpec((1, d), lambda i: (0, 0))
    return pl.pallas_call(
        body, name=name, grid=(t // tm,), in_specs=[row, row, pl.BlockSpec((tm, 1), lambda i: (i, 0)), vec],
        out_specs=[row, row, vec, vec],
        out_shape=[jax.ShapeDtypeStruct((t, d), F32), jax.ShapeDtypeStruct((t, d), BF16),
                   jax.ShapeDtypeStruct((1, d), F32), jax.ShapeDtypeStruct((1, d), F32)],
        compiler_params=_cparams(("arbitrary",)),
    )(dy, xhat, rstd, g)


def _ple_fwd(x2, gl, pe, gate_b, *, name, tm=512):
    t, d = x2.shape

    def body(x_ref, gl_ref, pe_ref, b_ref, y_ref, yb_ref):
        y = x_ref[...] + _sigmoid(gl_ref[...] + b_ref[...]) * pe_ref[...]
        y_ref[...] = y
        yb_ref[...] = y.astype(BF16)

    row = pl.BlockSpec((tm, d), lambda i: (i, 0))
    vec = pl.BlockSpec((1, d), lambda i: (0, 0))
    return pl.pallas_call(
        body, name=name, grid=(t // tm,), in_specs=[row, row, row, vec], out_specs=[row, row],
        out_shape=[jax.ShapeDtypeStruct((t, d), F32), jax.ShapeDtypeStruct((t, d), BF16)],
        compiler_params=_cparams(("parallel",)),
    )(x2, gl, pe, gate_b)


def _ple_bwd(dx3, gl, pe, gate_b, *, name, tm=512):
    t, d = dx3.shape

    def body(dx_ref, gl_ref, pe_ref, b_ref, dgl_ref, dpe_ref, db_ref):
        @pl.when(pl.program_id(0) == 0)
        def _():
            db_ref[...] = jnp.zeros_like(db_ref)

        dx = dx_ref[...]
        gt = _sigmoid(gl_ref[...] + b_ref[...])
        dgl = dx * pe_ref[...] * gt * (1.0 - gt)
        db_ref[...] += jnp.sum(dgl, axis=0, keepdims=True)
        dgl_ref[...] = dgl.astype(BF16)
        dpe_ref[...] = (dx * gt).astype(BF16)

    row = pl.BlockSpec((tm, d), lambda i: (i, 0))
    vec = pl.BlockSpec((1, d), lambda i: (0, 0))
    return pl.pallas_call(
        body, name=name, grid=(t // tm,), in_specs=[row, row, row, vec], out_specs=[row, row, vec],
        out_shape=[jax.ShapeDtypeStruct((t, d), BF16), jax.ShapeDtypeStruct((t, d), BF16),
                   jax.ShapeDtypeStruct((1, d), F32)],
        compiler_params=_cparams(("arbitrary",)),
    )(dx3, gl, pe, gate_b)


def _loss_bwd(y, tgt, *, name, tm=512):
    t, d = y.shape

    def body(y_ref, t_ref, dy_ref, l_ref):
        @pl.when(pl.program_id(0) == 0)
        def _():
            l_ref[...] = jnp.zeros_like(l_ref)

        err = y_ref[...] - t_ref[...]
        dy_ref[...] = err * (1.0 / d)
        part = jnp.sum(jnp.sum(err * err, axis=0, keepdims=True), axis=1, keepdims=True) * (0.5 / d)
        l_ref[...] += jnp.broadcast_to(part, l_ref.shape)

    row = pl.BlockSpec((tm, d), lambda i: (i, 0))
    return pl.pallas_call(
        body, name=name, grid=(t // tm,), in_specs=[row, row],
        out_specs=[row, pl.BlockSpec((1, LANES), lambda i: (0, 0))],
        out_shape=[jax.ShapeDtypeStruct((t, d), F32), jax.ShapeDtypeStruct((1, LANES), F32)],
        compiler_params=_cparams(("arbitrary",)),
    )(y, tgt)


def _conv_causal(x, halo, w_ref, b, kw):
    acc = x * w_ref[kw - 1:kw, :] + b
    for k in range(kw - 1):
        acc = acc + _shift_down(x, halo, kw - 1 - k) * w_ref[k:k + 1, :]
    return acc


def _ffn_act_fwd(hdn, conv_w, conv_b, *, name, tm=512):
    _, t, f = hdn.shape
    tc = _tile(f, 1408)
    hb = tm // SUBLANES

    def body(h_ref, halo_ref, w_ref, b_ref, a_ref):
        first = pl.program_id(1) == 0
        parts = []
        for s in range(2):
            halo = jnp.where(first, 0.0, halo_ref[s])
            parts.append(_conv_causal(h_ref[s], halo, w_ref.at[s], b_ref[s], CONV_F))
        a_ref[...] = (_gelu(parts[1]) * parts[0]).astype(BF16)

    return pl.pallas_call(
        body, name=name, grid=(f // tc, t // tm),
        in_specs=[pl.BlockSpec((2, tm, tc), lambda j, i: (0, i, j)),
                  pl.BlockSpec((2, SUBLANES, tc), lambda j, i: (0, jnp.maximum(i * hb - 1, 0), j)),
                  pl.BlockSpec((2, CONV_F, tc), lambda j, i: (0, 0, j)),
                  pl.BlockSpec((2, 1, tc), lambda j, i: (0, 0, j))],
        out_specs=pl.BlockSpec((tm, tc), lambda j, i: (i, j)),
        out_shape=jax.ShapeDtypeStruct((t, f), BF16),
        compiler_params=_cparams(("parallel", "arbitrary")),
    )(hdn, hdn, conv_w, conv_b)


def _ffn_act_bwd(da, hdn, conv_w, conv_b, *, name, tm=256):
    _, t, f = hdn.shape
    tc = _tile(f, 1408)
    hb = tm // SUBLANES
    nr = t // tm

    def body(da_ref, h_ref, halo_ref, w_ref, b_ref, dh_ref, dw_ref, db_ref, carry_ref):
        i = pl.program_id(1)
        r = nr - 1 - i

        @pl.when(i == 0)
        def _():
            dw_ref[...] = jnp.zeros_like(dw_ref)
            db_ref[...] = jnp.zeros_like(db_ref)
            carry_ref[...] = jnp.zeros_like(carry_ref)

        hs, hcs = [], []
        for s in range(2):
            halo = jnp.where(r == 0, 0.0, halo_ref[s])
            hs.append((h_ref[s], halo))
            hcs.append(_conv_causal(h_ref[s], halo, w_ref.at[s], b_ref[s], CONV_F))
        val, gate = hcs
        g, dg = _gelu_and_grad(gate)
        dav = da_ref[...]
        dhc = [dav * g, dav * val * dg]
        for s in range(2):
            d = dhc[s]
            x, halo = hs[s]
            nxt = carry_ref[s]
            db_ref[s] += jnp.sum(d, axis=0, keepdims=True)
            out = d * w_ref[s, CONV_F - 1:CONV_F, :]
            dw_ref[s, CONV_F - 1:CONV_F, :] += jnp.sum(d * x, axis=0, keepdims=True)
            for k in range(CONV_F - 1):
                sh = CONV_F - 1 - k
                out = out + _shift_up(d, nxt, sh) * w_ref[s, k:k + 1, :]
                dw_ref[s, k:k + 1, :] += jnp.sum(d * _shift_down(x, halo, sh), axis=0, keepdims=True)
            dh_ref[s] = out.astype(BF16)
            carry_ref[s] = d[:SUBLANES]

    return pl.pallas_call(
        body, name=name, grid=(f // tc, nr),
        in_specs=[pl.BlockSpec((tm, tc), lambda j, i: (nr - 1 - i, j)),
                  pl.BlockSpec((2, tm, tc), lambda j, i: (0, nr - 1 - i, j)),
                  pl.BlockSpec((2, SUBLANES, tc), lambda j, i: (0, jnp.maximum((nr - 1 - i) * hb - 1, 0), j)),
                  pl.BlockSpec((2, CONV_F, tc), lambda j, i: (0, 0, j)),
                  pl.BlockSpec((2, 1, tc), lambda j, i: (0, 0, j))],
        out_specs=[pl.BlockSpec((2, tm, tc), lambda j, i: (0, nr - 1 - i, j)),
                   pl.BlockSpec((2, CONV_F, tc), lambda j, i: (0, 0, j)),
                   pl.BlockSpec((2, 1, tc), lambda j, i: (0, 0, j))],
        out_shape=[jax.ShapeDtypeStruct((2, t, f), BF16), jax.ShapeDtypeStruct((2, CONV_F, f), F32),
                   jax.ShapeDtypeStruct((2, 1, f), F32)],
        scratch_shapes=[pltpu.VMEM((2, SUBLANES, tc), F32)],
        compiler_params=_cparams(("arbitrary", "arbitrary")),
    )(da, hdn, hdn, conv_w, conv_b)


def _fgate_fwd(xb, w_f, b_f, *, name, tm=256):
    t, d = xb.shape

    def body(x_ref, w_ref, b_ref, z_ref, c_ref, carry_ref):
        @pl.when(pl.program_id(0) == 0)
        def _():
            carry_ref[...] = jnp.zeros_like(carry_ref)

        z = jnp.dot(x_ref[...], w_ref[...], preferred_element_type=F32) + b_ref[...]
        z_ref[...] = z
        ls = _log_sigmoid(z)
        rr = lax.broadcasted_iota(jnp.int32, (tm, tm), 0)
        cc = lax.broadcasted_iota(jnp.int32, (tm, tm), 1)
        tri = (cc <= rr).astype(BF16)
        cum = carry_ref[...]
        for piece in _split3(ls):
            cum = cum + jnp.dot(tri, piece, preferred_element_type=F32)
        c_ref[...] = cum
        carry_ref[...] = cum[tm - 1:tm, :]

    return pl.pallas_call(
        body, name=name, grid=(t // tm,),
        in_specs=[pl.BlockSpec((tm, d), lambda i: (i, 0)), pl.BlockSpec((d, LANES), lambda i: (0, 0)),
                  pl.BlockSpec((1, LANES), lambda i: (0, 0))],
        out_specs=[pl.BlockSpec((tm, LANES), lambda i: (i, 0)), pl.BlockSpec((tm, LANES), lambda i: (i, 0))],
        out_shape=[jax.ShapeDtypeStruct((t, LANES), F32), jax.ShapeDtypeStruct((t, LANES), F32)],
        scratch_shapes=[pltpu.VMEM((1, LANES), F32)],
        compiler_params=_cparams(("arbitrary",)),
    )(xb, w_f, b_f)


def _fgate_bwd(dc, z, *, name, tm=256):
    t = dc.shape[0]
    nr = t // tm

    def body(dc_ref, z_ref, dz_ref, db_ref, carry_ref):
        @pl.when(pl.program_id(0) == 0)
        def _():
            carry_ref[...] = jnp.zeros_like(carry_ref)
            db_ref[...] = jnp.zeros_like(db_ref)

        rr = lax.broadcasted_iota(jnp.int32, (tm, tm), 0)
        cc = lax.broadcasted_iota(jnp.int32, (tm, tm), 1)
        tri = (cc >= rr).astype(BF16)
        cum = carry_ref[...]
        for piece in _split3(dc_ref[...]):
            cum = cum + jnp.dot(tri, piece, preferred_element_type=F32)
        carry_ref[...] = cum[0:1, :]
        dz = cum * _sigmoid(-z_ref[...])
        db_ref[...] += jnp.sum(dz, axis=0, keepdims=True)
        dz_ref[...] = dz.astype(BF16)

    rev = pl.BlockSpec((tm, LANES), lambda i: (nr - 1 - i, 0))
    return pl.pallas_call(
        body, name=name, grid=(nr,), in_specs=[rev, rev],
        out_specs=[rev, pl.BlockSpec((1, LANES), lambda i: (0, 0))],
        out_shape=[jax.ShapeDtypeStruct((t, LANES), BF16), jax.ShapeDtypeStruct((1, LANES), F32)],
        scratch_shapes=[pltpu.VMEM((1, LANES), F32)],
        compiler_params=_cparams(("arbitrary",)),
    )(dc, z)


def _head_lane_select(blk, head):
    lane = lax.broadcasted_iota(jnp.int32, (1, LANES), 1)
    return jnp.sum(jnp.where(lane == head, blk, 0.0), axis=1, keepdims=True)


def _flash_fwd(qkv, c, crow, *, name, tq=512):
    _, t, d = qkv.shape
    nq = t // tq
    n_pairs = d // LANES

    def body(q_ref, k_ref, v_ref, c_ref, cr0_ref, cr1_ref, o_ref, lse_ref):
        hp = pl.program_id(0)
        i = pl.program_id(1)
        lane = lax.broadcasted_iota(jnp.int32, (1, LANES), 1)
        rr = lax.broadcasted_iota(jnp.int32, (tq, tq), 0)
        cc = lax.broadcasted_iota(jnp.int32, (tq, tq), 1)
        causal = cc <= rr
        q = q_ref[...]
        cblk = c_ref[...]
        outs, lses = [], []
        for hh, cr_ref in enumerate((cr0_ref, cr1_ref)):
            sel = (lane < HEAD_DIM) if hh == 0 else (lane >= HEAD_DIM)
            qh = jnp.where(sel, q, jnp.zeros_like(q)) * jnp.asarray(ATTN_SCALE, BF16)
            ci = _head_lane_select(cblk, 2 * hp + hh)

            def kv_step(j, carry, masked, qh=qh, ci=ci, cr_ref=cr_ref):
                m, l, acc = carry
                off = pl.multiple_of(j * tq, tq)
                kj = k_ref[pl.ds(off, tq), :]
                vj = v_ref[pl.ds(off, tq), :]
                s = lax.dot_general(qh, kj, (((1,), (1,)), ((), ())), preferred_element_type=F32)
                s = s + (ci - cr_ref[:, pl.ds(off, tq)])
                if masked:
                    s = jnp.where(causal, s, NEG_BIG)
                m_new = jnp.maximum(m, jnp.max(s, axis=1, keepdims=True))
                p = jnp.exp(s - m_new)
                corr = jnp.exp(m - m_new)
                l = l * corr + jnp.sum(p, axis=1, keepdims=True)
                acc = acc * corr + jnp.dot(p.astype(BF16), vj, preferred_element_type=F32)
                return m_new, l, acc

            init = (jnp.full((tq, 1), NEG_BIG, F32), jnp.zeros((tq, 1), F32), jnp.zeros((tq, LANES), F32))
            carry = lax.fori_loop(0, i, functools.partial(kv_step, masked=False), init)
            m, l, acc = kv_step(i, carry, True)
            outs.append(acc / l)
            lses.append(m + jnp.log(l))
        o_ref[...] = jnp.where(lane < HEAD_DIM, outs[0], outs[1]).astype(BF16)
        lse_ref[...] = jnp.where(lane == 0, lses[0], jnp.where(lane == 1, lses[1], 0.0))

    return pl.pallas_call(
        body, name=name, grid=(n_pairs, nq),
        in_specs=[pl.BlockSpec((None, tq, LANES), lambda h, i: (0, i, h)),
                  pl.BlockSpec((None, t, LANES), lambda h, i: (1, 0, h)),
                  pl.BlockSpec((None, t, LANES), lambda h, i: (2, 0, h)),
                  pl.BlockSpec((tq, LANES), lambda h, i: (i, 0)),
                  pl.BlockSpec((None, 1, t), lambda h, i: (2 * h, 0, 0)),
                  pl.BlockSpec((None, 1, t), lambda h, i: (2 * h + 1, 0, 0))],
        out_specs=[pl.BlockSpec((tq, LANES), lambda h, i: (i, h)),
                   pl.BlockSpec((None, tq, LANES), lambda h, i: (h, i, 0))],
        out_shape=[jax.ShapeDtypeStruct((t, d), BF16), jax.ShapeDtypeStruct((n_pairs, t, LANES), F32)],
        compiler_params=_cparams(("parallel", "arbitrary")),
    )(qkv, qkv, qkv, c, crow, crow)


def _attn_delta(o, do, *, name, tm=512):
    t, d = o.shape

    def body(o_ref, do_ref, out_ref):
        prod = o_ref[...].astype(F32) * do_ref[...].astype(F32)
        col = lax.broadcasted_iota(jnp.int32, (d, LANES), 0)
        head = lax.broadcasted_iota(jnp.int32, (d, LANES), 1)
        sel = (lax.div(col, HEAD_DIM) == head).astype(BF16)
        acc = jnp.zeros((tm, LANES), F32)
        for piece in _split3(prod):
            acc = acc + jnp.dot(piece, sel, preferred_element_type=F32)
        out_ref[...] = acc

    row = pl.BlockSpec((tm, d), lambda i: (i, 0))
    return pl.pallas_call(
        body, name=name, grid=(t // tm,), in_specs=[row, row],
        out_specs=pl.BlockSpec((tm, LANES), lambda i: (i, 0)),
        out_shape=jax.ShapeDtypeStruct((t, LANES), F32),
        compiler_params=_cparams(("parallel",)),
    )(o, do)


def _flash_bwd(qkv, do, c, crow, lse_row, delta_row, *, name, tq=512):
    _, t, d = qkv.shape
    nq = t // tq
    n_pairs = d // LANES

    def body(q_ref, do_ref, k_ref, v_ref, c_ref, cr0, cr1, ls0, ls1, dl0, dl1,
             dq_ref, dkv_ref, dc_ref, dcq_ref, dq_acc):
        hp = pl.program_id(0)
        j = pl.program_id(1)
        lane = lax.broadcasted_iota(jnp.int32, (1, LANES), 1)
        kr = lax.broadcasted_iota(jnp.int32, (tq, tq), 0)
        qc = lax.broadcasted_iota(jnp.int32, (tq, tq), 1)
        causal_t = kr <= qc

        @pl.when(j == 0)
        def _():
            dq_acc[...] = jnp.zeros_like(dq_acc)
            dcq_ref[...] = jnp.zeros_like(dcq_ref)

        k = k_ref[...]
        v = v_ref[...]
        cblk = c_ref[...]
        dks, dvs, dcs = [], [], []
        for hh, (cr_ref, ls_ref, dl_ref) in enumerate(((cr0, ls0, dl0), (cr1, ls1, dl1))):
            sel = (lane < HEAD_DIM) if hh == 0 else (lane >= HEAD_DIM)
            kh = jnp.where(sel, k, jnp.zeros_like(k)) * jnp.asarray(ATTN_SCALE, BF16)
            vh = jnp.where(sel, v, jnp.zeros_like(v))
            cj = _head_lane_select(cblk, 2 * hp + hh)

            def q_step(i, carry, masked, hh=hh, kh=kh, vh=vh, cj=cj, cr_ref=cr_ref, ls_ref=ls_ref, dl_ref=dl_ref):
                dk, dv, dcj = carry
                off = pl.multiple_of(i * tq, tq)
                qi = q_ref[pl.ds(off, tq), :]
                doi = do_ref[pl.ds(off, tq), :]
                st = lax.dot_general(kh, qi, (((1,), (1,)), ((), ())), preferred_element_type=F32)
                st = st + (cr_ref[:, pl.ds(off, tq)] - cj)
                if masked:
                    st = jnp.where(causal_t, st, NEG_BIG)
                pt = jnp.exp(st - ls_ref[:, pl.ds(off, tq)])
                dv = dv + jnp.dot(pt.astype(BF16), doi, preferred_element_type=F32)
                dpt = lax.dot_general(vh, doi, (((1,), (1,)), ((), ())), preferred_element_type=F32)
                dst = pt * (dpt - dl_ref[:, pl.ds(off, tq)])
                dcj = dcj + jnp.sum(dst, axis=1, keepdims=True)
                dcq_ref[hh:hh + 1, pl.ds(off, tq)] += jnp.sum(dst, axis=0, keepdims=True)
                dsb = dst.astype(BF16)
                dk = dk + jnp.dot(dsb, qi, preferred_element_type=F32)
                dq_acc[pl.ds(off, tq), :] += lax.dot_general(dsb, kh, (((0,), (0,)), ((), ())),
                                                             preferred_element_type=F32)
                return dk, dv, dcj

            init = (jnp.zeros((tq, LANES), F32), jnp.zeros((tq, LANES), F32), jnp.zeros((tq, 1), F32))
            carry = q_step(j, init, True)
            dk, dv, dcj = lax.fori_loop(j + 1, nq, functools.partial(q_step, masked=False), carry)
            dks.append(dk * ATTN_SCALE)
            dvs.append(dv)
            dcs.append(-dcj)
        dkv_ref[0] = jnp.where(lane < HEAD_DIM, dks[0], dks[1]).astype(BF16)
        dkv_ref[1] = jnp.where(lane < HEAD_DIM, dvs[0], dvs[1]).astype(BF16)
        dc_ref[...] = jnp.where(lane == 0, dcs[0], jnp.where(lane == 1, dcs[1], 0.0))

        @pl.when(j == nq - 1)
        def _():
            dq_ref[...] = dq_acc[...].astype(BF16)

    full = lambda part: pl.BlockSpec((None, t, LANES), lambda h, j: (part, 0, h))
    rowspec = lambda hh: pl.BlockSpec((None, 1, t), lambda h, j: (2 * h + hh, 0, 0))
    return pl.pallas_call(
        body, name=name, grid=(n_pairs, nq),
        in_specs=[full(0),
                  pl.BlockSpec((t, LANES), lambda h, j: (0, h)),
                  pl.BlockSpec((None, tq, LANES), lambda h, j: (1, j, h)),
                  pl.BlockSpec((None, tq, LANES), lambda h, j: (2, j, h)),
                  pl.BlockSpec((tq, LANES), lambda h, j: (j, 0)),
                  rowspec(0), rowspec(1), rowspec(0), rowspec(1), rowspec(0), rowspec(1)],
        out_specs=[pl.BlockSpec((t, LANES), lambda h, j: (0, h)),
                   pl.BlockSpec((2, tq, LANES), lambda h, j: (0, j, h)),
                   pl.BlockSpec((None, tq, LANES), lambda h, j: (h, j, 0)),
                   pl.BlockSpec((None, SUBLANES, t), lambda h, j: (h, 0, 0))],
        out_shape=[jax.ShapeDtypeStruct((t, d), BF16), jax.ShapeDtypeStruct((2, t, d), BF16),
                   jax.ShapeDtypeStruct((n_pairs, t, LANES), F32),
                   jax.ShapeDtypeStruct((n_pairs, SUBLANES, t), F32)],
        scratch_shapes=[pltpu.VMEM((t, LANES), F32)],
        compiler_params=_cparams(("parallel", "arbitrary")),
    )(qkv, do, qkv, qkv, c, crow, crow, lse_row, lse_row, delta_row, delta_row)


def _rows_from_lanes(x, n):
    return x[:, :n].T.reshape(n, 1, x.shape[0])


def _rows_from_pairs(x8):
    hp, t, _ = x8.shape
    return x8[:, :, :2].transpose(0, 2, 1).reshape(2 * hp, 1, t)


def _lanes_from_pairs(x8):
    hp, t, _ = x8.shape
    y = x8[:, :, :2].transpose(1, 0, 2).reshape(t, 2 * hp)
    return jnp.pad(y, ((0, 0), (0, LANES - 2 * hp)))


def _lanes_from_pair_rows(x8):
    hp, _, t = x8.shape
    y = x8[:, :2, :].reshape(2 * hp, t).T
    return jnp.pad(y, ((0, 0), (0, LANES - 2 * hp)))


HM = 2 * HEAD_DIM
LANE_C, LANE_ONE, LANE_LSE = 64, 67, 70
Q_SUB = 256


def _three_parts(col, sign=1.0):
    col = sign * col
    hi = col.astype(BF16).astype(F32)
    r1 = col - hi
    mid = r1.astype(BF16).astype(F32)
    lo = (r1 - mid).astype(BF16).astype(F32)
    return hi, mid, lo


def _fill_lanes(base, lane, first, parts):
    out = base
    for n, part in enumerate(parts):
        out = jnp.where(lane == first + n, part, out)
    return out


def _hm_weight_t(w_t):
    rows, k = w_t.shape
    h = rows // HEAD_DIM
    return jnp.pad(w_t.reshape(h, HEAD_DIM, k), ((0, 0), (0, HM - HEAD_DIM), (0, 0))).reshape(h * HM, k)


def _hm_unpad_t(g_t):
    rows, k = g_t.shape
    h = rows // HM
    return g_t.reshape(h, HM, k)[:, :HEAD_DIM].reshape(h * HEAD_DIM, k)


def _attn_prep(qkv, c, *, name, tm=256):
    t, w = qkv.shape
    nh = w // (3 * HM)

    def body(x_ref, c_ref, o_ref):
        lane = lax.broadcasted_iota(jnp.int32, (1, HM), 1)
        cblk = c_ref[...]
        for h in range(nh):
            ch = cblk[:, h:h + 1]
            pos = _three_parts(ch)
            neg = _three_parts(ch, -1.0)
            for part in range(3):
                col = (part * nh + h) * HM
                x = x_ref[:, col:col + HM].astype(F32)
                if part == 0:
                    y = _fill_lanes(_fill_lanes(x * ATTN_SCALE, lane, LANE_C, pos), lane, LANE_ONE, (1.0, 1.0, 1.0))
                elif part == 1:
                    y = _fill_lanes(_fill_lanes(x, lane, LANE_C, (1.0, 1.0, 1.0)), lane, LANE_ONE, neg)
                    y = _fill_lanes(y, lane, LANE_LSE, (1.0, 1.0, 1.0))
                else:
                    y = _fill_lanes(x, lane, LANE_C, (1.0, 1.0, 1.0))
                o_ref[:, col:col + HM] = y.astype(BF16)

    return pl.pallas_call(
        body, name=name, grid=(t // tm,),
        in_specs=[pl.BlockSpec((tm, w), lambda i: (i, 0)), pl.BlockSpec((tm, LANES), lambda i: (i, 0))],
        out_specs=pl.BlockSpec((tm, w), lambda i: (i, 0)),
        out_shape=jax.ShapeDtypeStruct((t, w), BF16),
        compiler_params=_cparams(("parallel",)),
    )(qkv, c)


def _flash_fwd_hm(qkva, *, name, tq=1024):
    t, w = qkva.shape
    nh = w // (3 * HM)
    nq = t // tq
    nsub = tq // Q_SUB

    def body(q_ref, k_ref, v_ref, o_ref, qb_ref, s_scr, p_scr, m_scr, acc_scr):
        i = pl.program_id(1)
        lane = lax.broadcasted_iota(jnp.int32, (1, HM), 1)
        rr = lax.broadcasted_iota(jnp.int32, (Q_SUB, tq), 0)
        cc = lax.broadcasted_iota(jnp.int32, (Q_SUB, tq), 1)
        sub = lambda r: slice(r * Q_SUB, (r + 1) * Q_SUB)
        nt = (((1,), (1,)), ((), ()))

        def scores(j):
            kj = k_ref[pl.ds(pl.multiple_of(j * tq, tq), tq), :]
            for r in range(nsub):
                s_scr[sub(r), :] = lax.dot_general(q_ref[sub(r), :], kj, nt, preferred_element_type=F32)

        def step(jp, masked):
            vj = v_ref[pl.ds(pl.multiple_of(jp * tq, tq), tq), :]
            for r in range(nsub):
                pv = jnp.dot(p_scr[sub(r), :], vj, preferred_element_type=F32)
                if masked:
                    s_scr[sub(r), :] = jnp.where(cc <= rr + r * Q_SUB, s_scr[sub(r), :], NEG_BIG)
                m_old = m_scr[sub(r), :]
                m_new = jnp.maximum(m_old, jnp.max(s_scr[sub(r), :], axis=1, keepdims=True))
                m_scr[sub(r), :] = m_new
                p_scr[sub(r), :] = jnp.exp(s_scr[sub(r), :] - m_new).astype(BF16)
                acc_scr[sub(r), :] = (acc_scr[sub(r), :] + pv) * jnp.exp(m_old - m_new)

        def kv_step(j, carry):
            step(jnp.maximum(j - 1, 0), False)
            scores(j + 1)
            return carry

        p_scr[...] = jnp.zeros_like(p_scr)
        m_scr[...] = jnp.full_like(m_scr, NEG_BIG)
        acc_scr[...] = jnp.zeros_like(acc_scr)
        scores(0)
        lax.fori_loop(0, i, kv_step, 0)
        step(jnp.maximum(i - 1, 0), True)
        vi = v_ref[pl.ds(pl.multiple_of(i * tq, tq), tq), :]
        for r in range(nsub):
            acc = acc_scr[sub(r), :] + jnp.dot(p_scr[sub(r), :], vi, preferred_element_type=F32)
            l = jnp.sum(jnp.where(lane == LANE_C, acc, 0.0), axis=1, keepdims=True)
            o_ref[sub(r), :] = jnp.where(lane < HEAD_DIM, acc / l, 0.0).astype(BF16)
            lse = m_scr[sub(r), :] + jnp.log(l)
            qb = _fill_lanes(q_ref[sub(r), :].astype(F32), lane, LANE_LSE, _three_parts(lse, -1.0))
            qb_ref[sub(r), :] = qb.astype(BF16)

    blk = lambda part: pl.BlockSpec((t, HM), lambda h, i: (0, part * nh + h))
    tile = pl.BlockSpec((tq, HM), lambda h, i: (i, h))
    return pl.pallas_call(
        body, name=name, grid=(nh, nq), in_specs=[tile, blk(1), blk(2)], out_specs=[tile, tile],
        out_shape=[jax.ShapeDtypeStruct((t, nh * HM), BF16), jax.ShapeDtypeStruct((t, nh * HM), BF16)],
        scratch_shapes=[pltpu.VMEM((tq, tq), F32), pltpu.VMEM((tq, tq), BF16), pltpu.VMEM((tq, 1), F32),
                        pltpu.VMEM((tq, HM), F32)],
        compiler_params=_cparams(("parallel", "arbitrary")),
    )(qkva, qkva, qkva)


def _attn_prep_bwd(o, do, *, name, tm=512):
    t, w = o.shape
    nh = w // HM

    def body(o_ref, do_ref, out_ref):
        lane = lax.broadcasted_iota(jnp.int32, (1, HM), 1)
        for h in range(nh):
            cols = slice(h * HM, (h + 1) * HM)
            dov = do_ref[:, cols].astype(F32)
            delta = jnp.sum(o_ref[:, cols].astype(F32) * dov, axis=1, keepdims=True)
            out_ref[:, cols] = _fill_lanes(dov, lane, LANE_C, _three_parts(delta, -1.0)).astype(BF16)

    row = pl.BlockSpec((tm, w), lambda i: (i, 0))
    return pl.pallas_call(
        body, name=name, grid=(t // tm,), in_specs=[row, row], out_specs=row,
        out_shape=jax.ShapeDtypeStruct((t, w), BF16), compiler_params=_cparams(("parallel",)),
    )(o, do)


def _flash_bwd_hm(qb, doa, qkva, *, name, tq=512):
    t, w = qb.shape
    nh = w // HM
    nq = t // tq
    nsub = tq // Q_SUB
    grp = slice(LANE_C, LANE_C + SUBLANES)

    def body(q_ref, do_ref, k_ref, v_ref, dq_ref, dk_ref, dv_ref, dcq_ref, dcj_ref,
             dq_acc, st_scr, dpt_scr, pt_scr, ds_scr, dk_scr, dv_scr):
        j = pl.program_id(1)
        kr = lax.broadcasted_iota(jnp.int32, (tq, Q_SUB), 0)
        qc = lax.broadcasted_iota(jnp.int32, (tq, Q_SUB), 1)
        sub = lambda r: slice(r * Q_SUB, (r + 1) * Q_SUB)
        nt = (((1,), (1,)), ((), ()))
        tn = (((0,), (0,)), ((), ()))

        @pl.when(j == 0)
        def _():
            dq_acc[...] = jnp.zeros_like(dq_acc)

        def rows_of(i, r):
            return pl.ds(pl.multiple_of(i * tq + r * Q_SUB, Q_SUB), Q_SUB)

        def scores(i):
            for r in range(nsub):
                st_scr[:, sub(r)] = lax.dot_general(k_ref[...], q_ref[rows_of(i, r), :], nt, preferred_element_type=F32)
                dpt_scr[:, sub(r)] = lax.dot_general(v_ref[...], do_ref[rows_of(i, r), :], nt,
                                                     preferred_element_type=F32)

        def step(ip, masked):
            for r in range(nsub):
                dv_scr[...] += jnp.dot(pt_scr[:, sub(r)], do_ref[rows_of(ip, r), :], preferred_element_type=F32)
                dk_scr[...] += jnp.dot(ds_scr[:, sub(r)], q_ref[rows_of(ip, r), :], preferred_element_type=F32)
                dq_acc[rows_of(ip, r), :] += lax.dot_general(ds_scr[:, sub(r)], k_ref[...], tn,
                                                             preferred_element_type=F32)
            for r in range(nsub):
                st = st_scr[:, sub(r)]
                if masked:
                    st = jnp.where(kr <= qc + r * Q_SUB, st, NEG_BIG)
                pt = jnp.exp(st)
                pt_scr[:, sub(r)] = pt.astype(BF16)
                ds_scr[:, sub(r)] = (pt * dpt_scr[:, sub(r)]).astype(BF16)

        def q_step(i, carry):
            step(i - 1, False)
            scores(jnp.minimum(i + 1, nq - 1))
            return carry

        pt_scr[...] = jnp.zeros_like(pt_scr)
        ds_scr[...] = jnp.zeros_like(ds_scr)
        dk_scr[...] = jnp.zeros_like(dk_scr)
        dv_scr[...] = jnp.zeros_like(dv_scr)
        scores(j)
        step(j, True)
        scores(jnp.minimum(j + 1, nq - 1))
        lax.fori_loop(j + 1, nq, q_step, 0)
        for r in range(nsub):
            dv_scr[...] += jnp.dot(pt_scr[:, sub(r)], do_ref[rows_of(nq - 1, r), :], preferred_element_type=F32)
            dk_scr[...] += jnp.dot(ds_scr[:, sub(r)], q_ref[rows_of(nq - 1, r), :], preferred_element_type=F32)
            dq_acc[rows_of(nq - 1, r), :] += lax.dot_general(ds_scr[:, sub(r)], k_ref[...], tn,
                                                             preferred_element_type=F32)
        dk = dk_scr[...]
        dk_ref[...] = dk.astype(BF16)
        dv_ref[...] = dv_scr[...].astype(BF16)
        dcj_ref[...] = dk.T[grp, :]

        @pl.when(j == nq - 1)
        def _():
            dq_ref[...] = (dq_acc[...] * ATTN_SCALE).astype(BF16)
            for cidx in range(nq):
                rows = slice(cidx * tq, (cidx + 1) * tq)
                dcq_ref[:, rows] = dq_acc[rows, :].T[grp, :]

    full = pl.BlockSpec((t, HM), lambda h, j: (0, h))
    ktile = lambda part: pl.BlockSpec((tq, HM), lambda h, j: (j, part * nh + h))
    tile = pl.BlockSpec((tq, HM), lambda h, j: (j, h))
    hm_shape = jax.ShapeDtypeStruct((t, w), BF16)
    row_shape = jax.ShapeDtypeStruct((nh, SUBLANES, t), F32)
    return pl.pallas_call(
        body, name=name, grid=(nh, nq),
        in_specs=[pl.BlockSpec((t, HM), lambda h, j: (0, h), pipeline_mode=pl.Buffered(1)),
                  pl.BlockSpec((t, HM), lambda h, j: (0, h), pipeline_mode=pl.Buffered(1)), ktile(1), ktile(2)],
        out_specs=[full, tile, tile, pl.BlockSpec((None, SUBLANES, t), lambda h, j: (h, 0, 0)),
                   pl.BlockSpec((None, SUBLANES, tq), lambda h, j: (h, 0, j))],
        out_shape=[hm_shape, hm_shape, hm_shape, row_shape, row_shape],
        scratch_shapes=[pltpu.VMEM((t, HM), F32), pltpu.VMEM((tq, tq), F32), pltpu.VMEM((tq, tq), F32),
                        pltpu.VMEM((tq, tq), BF16), pltpu.VMEM((tq, tq), BF16), pltpu.VMEM((tq, HM), F32),
                        pltpu.VMEM((tq, HM), F32)],
        compiler_params=_cparams(("parallel", "arbitrary")),
    )(qb, doa, qkva, qkva)


def _block_diag_dot(xb, w_ref, transpose_w=False):
    outs = []
    for n in range(N_BLOCKS_B):
        xn = xb[:, n * BLOCK_B:(n + 1) * BLOCK_B]
        dn = (((1,), (1 if transpose_w else 0,)), ((), ()))
        outs.append(lax.dot_general(xn, w_ref[n], dn, preferred_element_type=F32))
    return jnp.concatenate(outs, axis=1)


def _rglru_fwd(proj, conv_w, conv_b, w_a, b_a, w_i, b_i, lam, *, name, tm=256):
    _, t, d = proj.shape
    hb = tm // SUBLANES
    ng = tm // SUBLANES

    def body(p_ref, halo_ref, cw_ref, cb_ref, wa_ref, ba_ref, wi_ref, bi_ref, lam_ref,
             xb_ref, r_ref, ig_ref, a_ref, h_ref, y_ref, u_scr, hc_scr):
        first = pl.program_id(0) == 0

        @pl.when(first)
        def _():
            hc_scr[...] = jnp.zeros_like(hc_scr)

        halo = jnp.where(first, 0.0, halo_ref[0])
        xb = _conv_causal(p_ref[0], halo, cw_ref, cb_ref[...], CONV_B)
        xb_ref[...] = xb
        xbb = xb.astype(BF16)
        r = _sigmoid(_block_diag_dot(xbb, wa_ref) + ba_ref[...])
        ig = _sigmoid(_block_diag_dot(xbb, wi_ref) + bi_ref[...])
        r_ref[...] = r
        ig_ref[...] = ig
        log_a = (-LRU_C) * r * _softplus(-lam_ref[...])
        a_ref[...] = jnp.exp(log_a)
        u_scr[...] = jnp.sqrt(_neg_expm1(2.0 * log_a)) * (ig * xb)

        ones8 = jnp.ones((SUBLANES, d), F32)
        zeros8 = jnp.zeros((SUBLANES, d), F32)

        def group(gi, hprev):
            off = pl.multiple_of(gi * SUBLANES, SUBLANES)
            a8 = a_ref[pl.ds(off, SUBLANES), :]
            u8 = u_scr[pl.ds(off, SUBLANES), :]
            for s in (1, 2, 4):
                u8 = a8 * _shift_down(u8, zeros8, s) + u8
                a8 = a8 * _shift_down(a8, ones8, s)
            h8 = a8 * hprev + u8
            h_ref[pl.ds(off, SUBLANES), :] = h8
            return h8[SUBLANES - 1:SUBLANES, :]

        hc_scr[...] = lax.fori_loop(0, ng, group, hc_scr[...])
        y_ref[...] = (h_ref[...] * _gelu(p_ref[1])).astype(BF16)

    row = pl.BlockSpec((tm, d), lambda i: (i, 0))
    vec = pl.BlockSpec((1, d), lambda i: (0, 0))
    wblk = pl.BlockSpec((N_BLOCKS_B, BLOCK_B, BLOCK_B), lambda i: (0, 0, 0))
    f32_td = jax.ShapeDtypeStruct((t, d), F32)
    return pl.pallas_call(
        body, name=name, grid=(t // tm,),
        in_specs=[pl.BlockSpec((2, tm, d), lambda i: (0, i, 0)),
                  pl.BlockSpec((1, SUBLANES, d), lambda i: (0, jnp.maximum(i * hb - 1, 0), 0)),
                  pl.BlockSpec((CONV_B, d), lambda i: (0, 0)), vec, wblk, vec, wblk, vec, vec],
        out_specs=[row, row, row, row, row, row],
        out_shape=[f32_td, f32_td, f32_td, f32_td, f32_td, jax.ShapeDtypeStruct((t, d), BF16)],
        scratch_shapes=[pltpu.VMEM((tm, d), F32), pltpu.VMEM((1, d), F32)],
        compiler_params=_cparams(("arbitrary",)),
    )(proj, proj, conv_w, conv_b, w_a, b_a, w_i, b_i, lam)


def _rglru_bwd(dy, proj, xb, r, ig, a, h, conv_w, w_a, w_i, lam, *, name, tm=256):
    _, t, d = proj.shape
    hb = tm // SUBLANES
    ng = tm // SUBLANES
    nr = t // tm

    def body(dy_ref, p_ref, phalo_ref, xb_ref, r_ref, ig_ref, a_ref, h_ref, hhalo_ref,
             cw_ref, wa_ref, wi_ref, lam_ref,
             dp_ref, dcw_ref, dcb_ref, dba_ref, dbi_ref, dlam_ref, dwa_ref, dwi_ref,
             g_scr, carry_g, carry_dxb):
        i = pl.program_id(0)
        rpos = nr - 1 - i

        @pl.when(i == 0)
        def _():
            for ref in (dcw_ref, dcb_ref, dba_ref, dbi_ref, dlam_ref, dwa_ref, dwi_ref, carry_g, carry_dxb):
                ref[...] = jnp.zeros_like(ref)

        gb = p_ref[1]
        gel, dgel = _gelu_and_grad(gb)
        dyv = dy_ref[...]
        hv = h_ref[...]
        dp_ref[1] = (dyv * hv * dgel).astype(BF16)
        g_scr[...] = dyv * gel
        av = a_ref[...]

        ones8 = jnp.ones((SUBLANES, d), F32)
        zeros8 = jnp.zeros((SUBLANES, d), F32)

        def group(gi, cin):
            off = pl.multiple_of((ng - 1 - gi) * SUBLANES, SUBLANES)
            g8 = g_scr[pl.ds(off, SUBLANES), :]
            a8 = a_ref[pl.ds(off, SUBLANES), :]
            b8 = _shift_up(a8, ones8, 1)
            row8 = lax.broadcasted_iota(jnp.int32, (SUBLANES, d), 0)
            g8 = g8 + jnp.where(row8 == SUBLANES - 1, cin, 0.0)
            b8 = jnp.where(row8 == SUBLANES - 1, 0.0, b8)
            for s in (1, 2, 4):
                g8 = g8 + b8 * _shift_up(g8, zeros8, s)
                b8 = b8 * _shift_up(b8, zeros8, s)
            g_scr[pl.ds(off, SUBLANES), :] = g8
            return a8[0:1, :] * g8[0:1, :]

        carry_g[...] = lax.fori_loop(0, ng, group, carry_g[...])

        du = g_scr[...]
        hhalo = jnp.where(rpos == 0, 0.0, hhalo_ref[...])
        hprev = _shift_down(hv, hhalo, 1)
        da = du * hprev
        rv = r_ref[...]
        igv = ig_ref[...]
        xbv = xb_ref[...]
        sp = _softplus(-lam_ref[...])
        log_a = (-LRU_C) * rv * sp
        mult = jnp.sqrt(_neg_expm1(2.0 * log_a))
        dmult = du * (igv * xbv)
        dig = du * mult * xbv
        dxb = du * mult * igv
        dlog_a = da * av - dmult * (av * av) / mult
        dr = dlog_a * ((-LRU_C) * sp)
        dsp = jnp.sum(dlog_a * ((-LRU_C) * rv), axis=0, keepdims=True)
        dlam_ref[...] += dsp * (-_sigmoid(-lam_ref[...]))
        dra = dr * rv * (1.0 - rv)
        dia = dig * igv * (1.0 - igv)
        dba_ref[...] += jnp.sum(dra, axis=0, keepdims=True)
        dbi_ref[...] += jnp.sum(dia, axis=0, keepdims=True)
        drab = dra.astype(BF16)
        diab = dia.astype(BF16)
        xbb = xbv.astype(BF16)
        dxb = dxb + _block_diag_dot(drab, wa_ref, True) + _block_diag_dot(diab, wi_ref, True)
        tn = (((0,), (0,)), ((), ()))
        for n in range(N_BLOCKS_B):
            sl = slice(n * BLOCK_B, (n + 1) * BLOCK_B)
            dwa_ref[n] += lax.dot_general(xbb[:, sl], drab[:, sl], tn, preferred_element_type=F32)
            dwi_ref[n] += lax.dot_general(xbb[:, sl], diab[:, sl], tn, preferred_element_type=F32)

        xpre = p_ref[0]
        phalo = jnp.where(rpos == 0, 0.0, phalo_ref[0])
        nxt = carry_dxb[...]
        dcb_ref[...] += jnp.sum(dxb, axis=0, keepdims=True)
        out = dxb * cw_ref[CONV_B - 1:CONV_B, :]
        dcw_ref[CONV_B - 1:CONV_B, :] += jnp.sum(dxb * xpre, axis=0, keepdims=True)
        for k in range(CONV_B - 1):
            sh = CONV_B - 1 - k
            out = out + _shift_up(dxb, nxt, sh) * cw_ref[k:k + 1, :]
            dcw_ref[k:k + 1, :] += jnp.sum(dxb * _shift_down(xpre, phalo, sh), axis=0, keepdims=True)
        dp_ref[0] = out.astype(BF16)
        carry_dxb[...] = dxb[:SUBLANES]

    rev = pl.BlockSpec((tm, d), lambda i: (nr - 1 - i, 0))
    halo8 = pl.BlockSpec((SUBLANES, d), lambda i: (jnp.maximum((nr - 1 - i) * hb - 1, 0), 0))
    vec = pl.BlockSpec((1, d), lambda i: (0, 0))
    wblk = pl.BlockSpec((N_BLOCKS_B, BLOCK_B, BLOCK_B), lambda i: (0, 0, 0))
    vec_shape = jax.ShapeDtypeStruct((1, d), F32)
    w_shape = jax.ShapeDtypeStruct((N_BLOCKS_B, BLOCK_B, BLOCK_B), F32)
    return pl.pallas_call(
        body, name=name, grid=(nr,),
        in_specs=[rev, pl.BlockSpec((2, tm, d), lambda i: (0, nr - 1 - i, 0)),
                  pl.BlockSpec((1, SUBLANES, d), lambda i: (0, jnp.maximum((nr - 1 - i) * hb - 1, 0), 0)),
                  rev, rev, rev, rev, rev, halo8,
                  pl.BlockSpec((CONV_B, d), lambda i: (0, 0)), wblk, wblk, vec],
        out_specs=[pl.BlockSpec((2, tm, d), lambda i: (0, nr - 1 - i, 0)),
                   pl.BlockSpec((CONV_B, d), lambda i: (0, 0)), vec, vec, vec, vec, wblk, wblk],
        out_shape=[jax.ShapeDtypeStruct((2, t, d), BF16), jax.ShapeDtypeStruct((CONV_B, d), F32),
                   vec_shape, vec_shape, vec_shape, vec_shape, w_shape, w_shape],
        scratch_shapes=[pltpu.VMEM((tm, d), F32), pltpu.VMEM((1, d), F32), pltpu.VMEM((SUBLANES, d), F32)],
        compiler_params=_cparams(("arbitrary",)),
    )(dy, proj, proj, xb, r, ig, a, h, h, conv_w, w_a, w_i, lam)


def _split_cols(w, s):
    k, c = w.shape[-2:]
    return jnp.moveaxis(w.reshape(w.shape[:-2] + (k, s, c // s)), -2, -3)


def _merge_cols(w):
    s, k, c = w.shape
    return jnp.moveaxis(w, 0, 1).reshape(k, s * c)


def _local_step(x, p, tgt, w):
    t = x.shape[0]
    bf = lambda v: v.astype(BF16)
    saved = []
    xcur = x
    xcur_b = bf(x)
    for i in range(DEPTH):
        j = i // 2
        L = f"l{i}_"
        sv = {'x_in_b': xcur_b}
        if i % 2 == 0:
            w_in_t = bf(w['a_w_in'][j])
            sv['w_qkv_t'] = jnp.concatenate(
                [_hm_weight_t(w_in_t[part * D_MODEL:(part + 1) * D_MODEL]) for part in range(3)])
            sv['w_f'] = jnp.pad(w_in_t[3 * D_MODEL:].T, ((0, 0), (0, LANES - N_HEADS)))
            sv['w_out'] = _hm_weight_t(bf(w['a_w_out'][j]))
            b_f = jnp.pad(w['a_b_f'][j], (0, LANES - N_HEADS)).reshape(1, LANES)
            qkv = _mm(xcur_b, sv['w_qkv_t'], tb=True, out_dtype=BF16, name=L + "qkv")
            z, c = _fgate_fwd(xcur_b, sv['w_f'], b_f, name=L + "fgate")
            qkva = _attn_prep(qkv, c, name=L + "attn_prep")
            o, qb = _flash_fwd_hm(qkva, name=L + "flash_fwd")
            mix = _mm(o, sv['w_out'], name=L + "attn_out")
            sv.update(qkva=qkva, z=z, o=o, qb=qb)
        else:
            sv['w_in'] = bf(w['b_w_in'][j])
            sv['w_out'] = bf(w['b_w_out'][j])
            sv['conv_w'] = w['b_conv_w'][j]
            sv['w_a'] = bf(w['b_w_a'][j])
            sv['w_i'] = bf(w['b_w_i'][j])
            sv['lam'] = w['b_lam'][j].reshape(1, D_MODEL)
            proj = _mm(xcur_b, sv['w_in'], out_split=2, name=L + "rg_in")
            xb, r, ig, a, h, y = _rglru_fwd(
                proj, sv['conv_w'], w['b_conv_b'][j].reshape(1, D_MODEL), sv['w_a'],
                w['b_b_a'][j].reshape(1, D_MODEL), sv['w_i'], w['b_b_i'][j].reshape(1, D_MODEL), sv['lam'],
                name=L + "rglru_fwd")
            mix = _mm(y, sv['w_out'], name=L + "rg_out")
            sv.update(proj=proj, xb=xb, r=r, ig=ig, a=a, h=h, y=y)
        sv['ln1_g'] = w['ln1_g'][i].reshape(1, D_MODEL)
        x1, x1b, xhat1, rstd1 = _ln_fwd(xcur, mix, sv['ln1_g'], w['ln1_b'][i].reshape(1, D_MODEL), name=L + "ln1")
        sv['w_up'] = bf(w['f_w_up'][i])
        sv['w_down'] = bf(w['f_w_down'][i])
        sv['fconv_w'] = _split_cols(w['f_conv_w'][i], 2)
        sv['fconv_b'] = w['f_conv_b'][i].reshape(2, 1, D_FF)
        hdn = _mm(x1b, sv['w_up'], out_split=2, tn_cap=1408, name=L + "ffn_up")
        act = _ffn_act_fwd(hdn, sv['fconv_w'], sv['fconv_b'], name=L + "ffn_act")
        ff = _mm(act, sv['w_down'], tk_cap=2816, name=L + "ffn_down")
        sv['ln2_g'] = w['ln2_g'][i].reshape(1, D_MODEL)
        x2, x2b, xhat2, rstd2 = _ln_fwd(x1, ff, sv['ln2_g'], w['ln2_b'][i].reshape(1, D_MODEL), name=L + "ln2")
        sv['gate_w'] = bf(w['ple_gate_w'][i])
        sv['ple_w'] = bf(w['ple_w'][i])
        sv['gate_b'] = w['ple_gate_b'][i].reshape(1, D_MODEL)
        sv['p_b'] = bf(p[i])
        gl = _mm(x2b, sv['gate_w'], name=L + "ple_gate")
        pe = _mm(sv['p_b'], sv['ple_w'], name=L + "ple_emb")
        x3, x3b = _ple_fwd(x2, gl, pe, sv['gate_b'], name=L + "ple")
        sv.update(xhat1=xhat1, rstd1=rstd1, x1b=x1b, hdn=hdn, act=act, xhat2=xhat2, rstd2=rstd2, x2b=x2b,
                  gl=gl, pe=pe)
        saved.append(sv)
        xcur, xcur_b = x3, x3b

    dx, loss_row = _loss_bwd(xcur, tgt, name="loss")

    g = {n: [None] * w[n].shape[0] for n in WEIGHTS}
    for i in reversed(range(DEPTH)):
        j = i // 2
        L = f"l{i}b_"
        sv = saved[i]
        dgl, dpe, d_gate_b = _ple_bwd(dx, sv['gl'], sv['pe'], sv['gate_b'], name=L + "ple")
        g['ple_gate_b'][i] = d_gate_b[0]
        g['ple_w'][i] = _mm(sv['p_b'], dpe, ta=True, name=L + "ple_emb_dw")
        g['ple_gate_w'][i] = _mm(sv['x2b'], dgl, ta=True, name=L + "ple_gate_dw")
        dx2 = _mm(dgl, sv['gate_w'], tb=True, add=dx, name=L + "ple_gate_dx")
        dz2, dz2b, dg2, db2 = _ln_bwd(dx2, sv['xhat2'], sv['rstd2'], sv['ln2_g'], name=L + "ln2")
        g['ln2_g'][i], g['ln2_b'][i] = dg2[0], db2[0]
        g['f_w_down'][i] = _mm(sv['act'], dz2b, ta=True, tm_cap=1408, name=L + "ffn_down_dw")
        da = _mm(dz2b, sv['w_down'], tb=True, tn_cap=1408, name=L + "ffn_down_dx")
        dhdn, d_fcw, d_fcb = _ffn_act_bwd(da, sv['hdn'], sv['fconv_w'], sv['fconv_b'], name=L + "ffn_act")
        g['f_conv_w'][i] = _merge_cols(d_fcw)
        g['f_conv_b'][i] = d_fcb.reshape(2 * D_FF)
        g['f_w_up'][i] = _mm(sv['x1b'], dhdn, ta=True, tn_cap=1408, name=L + "ffn_up_dw")
        dx1 = _mm(dhdn, sv['w_up'], tb=True, add=dz2, add_scale=ALPHA, name=L + "ffn_up_dx")
        dz1, dz1b, dg1, db1 = _ln_bwd(dx1, sv['xhat1'], sv['rstd1'], sv['ln1_g'], name=L + "ln1")
        g['ln1_g'][i], g['ln1_b'][i] = dg1[0], db1[0]
        if i % 2 == 0:
            g['a_w_out'][j] = _hm_unpad_t(_mm(sv['o'], dz1b, ta=True, name=L + "attn_out_dw"))
            do = _mm(dz1b, sv['w_out'], tb=True, out_dtype=BF16, name=L + "attn_out_dx")
            doa = _attn_prep_bwd(sv['o'], do, name=L + "attn_prep")
            dqkv = _flash_bwd_hm(sv['qb'], doa, sv['qkva'], name=L + "flash_bwd")
            dcq, dcj = dqkv[3], dqkv[4]
            dc = jnp.pad((dcq[:, 0, :] - dcj[:, LANE_ONE - LANE_C, :]).T, ((0, 0), (0, LANES - N_HEADS)))
            dzf, d_b_f = _fgate_bwd(dc, sv['z'], name=L + "fgate")
            g['a_b_f'][j] = d_b_f[0, :N_HEADS]
            xb_in = sv['x_in_b']
            d_w = [_hm_unpad_t(_mm(dqkv[part], xb_in, ta=True, name=L + "qkv"[part] + "_dw")) for part in range(3)]
            d_wf = _mm(xb_in, dzf, ta=True, name=L + "f_dw")
            g['a_w_in'][j] = jnp.concatenate(d_w + [d_wf[:, :N_HEADS].T], axis=0)
            dxa, scale = dz1, ALPHA
            for part in range(3):
                w_part = sv['w_qkv_t'][part * N_HEADS * HM:(part + 1) * N_HEADS * HM]
                dxa = _mm(dqkv[part], w_part, add=dxa, add_scale=scale, name=L + "qkv"[part] + "_dx")
                scale = 1.0
            dx = _mm(dzf, sv['w_f'], tb=True, add=dxa, name=L + "f_dx")
        else:
            g['b_w_out'][j] = _mm(sv['y'], dz1b, ta=True, name=L + "rg_out_dw")
            dy = _mm(dz1b, sv['w_out'], tb=True, name=L + "rg_out_dx")
            dproj, d_cw, d_cb, d_ba, d_bi, d_lam, d_wa, d_wi = _rglru_bwd(
                dy, sv['proj'], sv['xb'], sv['r'], sv['ig'], sv['a'], sv['h'], sv['conv_w'], sv['w_a'], sv['w_i'],
                sv['lam'], name=L + "rglru_bwd")
            g['b_conv_w'][j], g['b_conv_b'][j] = d_cw, d_cb[0]
            g['b_b_a'][j] = d_ba.reshape(N_BLOCKS_B, BLOCK_B)
            g['b_b_i'][j] = d_bi.reshape(N_BLOCKS_B, BLOCK_B)
            g['b_lam'][j] = d_lam[0]
            g['b_w_a'][j], g['b_w_i'][j] = d_wa, d_wi
            g['b_w_in'][j] = _mm(sv['x_in_b'], dproj, ta=True, name=L + "rg_in_dw")
            dx = _mm(dproj, sv['w_in'], tb=True, add=dz1, add_scale=ALPHA, name=L + "rg_in_dx")
    return loss_row, dx, g


def _round_up(n, q):
    return -(-n // q) * q


def _pack(arrs, dtype, row_multiple, lead=0):
    pieces = []
    for a in arrs:
        flat = a.reshape(a.shape[:lead] + (-1,)).astype(dtype)
        n = flat.shape[-1]
        pieces.append(jnp.pad(flat, [(0, 0)] * lead + [(0, _round_up(n, LANES) - n)]))
    flat = jnp.concatenate(pieces, axis=-1)
    rows = _round_up(flat.shape[-1] // LANES, row_multiple)
    flat = jnp.pad(flat, [(0, 0)] * lead + [(0, rows * LANES - flat.shape[-1])])
    return flat.reshape(flat.shape[:lead] + (rows, LANES))


def _unpack(buf, shapes):
    lead = buf.shape[:-2]
    flat = buf.reshape(lead + (-1,))
    out, off = [], 0
    for shp in shapes:
        n = math.prod(shp)
        out.append(flat[..., off:off + n].reshape(lead + tuple(shp)))
        off += _round_up(n, LANES)
    return out


MESH = pl.DeviceIdType.MESH
ANY = pl.BlockSpec(memory_space=pl.ANY)
N_CHUNK = 8


def _all_gather_shards(buf):
    rows, lanes = buf.shape
    half = rows // 2
    ch = half // N_CHUNK
    n_ici = 3 * N_CHUNK

    def body(in_ref, out_ref, send_sems, recv_sems):
        x, y, c = lax.axis_index("x"), lax.axis_index("y"), lax.axis_index("c")
        sibling = (x, y, 1 - c)
        chips = [(1 - x, y), (x, 1 - y), (1 - x, 1 - y)]

        def piece(cx, cy, hc, q):
            return out_ref.at[2 * cx + cy, pl.ds(hc * half + q * ch, ch), :]

        def copy(k, src, dst, to):
            return pltpu.make_async_remote_copy(src_ref=src, dst_ref=dst, send_sem=send_sems.at[k],
                                                recv_sem=recv_sems.at[k], device_id=to, device_id_type=MESH)

        my_chunk = lambda q: in_ref.at[pl.ds(c * half + q * ch, ch), :]
        first, passed = [], []
        for k, chip in enumerate(chips):
            for q in range(N_CHUNK):
                first.append(copy(k * N_CHUNK + q, my_chunk(q), piece(x, y, c, q), (*chip, c)))
                passed.append(copy(n_ici + k * N_CHUNK + q, piece(*chip, c, q), piece(*chip, c, q), sibling))
        for cp in first:
            cp.start()
        for k, chip in enumerate(chips):
            for q in range(N_CHUNK):
                n = k * N_CHUNK + q
                copy(n, my_chunk(q), piece(*chip, c, q), (*chip, c)).wait_recv()
                passed[n].start()
        for k, chip in enumerate(chips):
            for q in range(N_CHUNK):
                copy(n_ici + k * N_CHUNK + q, my_chunk(q), piece(*chip, 1 - c, q), sibling).wait_recv()
        for cp in first + passed:
            cp.wait_send()

    others = pl.pallas_call(
        body, name="gather_weights", in_specs=[ANY], out_specs=ANY,
        out_shape=jax.ShapeDtypeStruct((N_CHIP, rows, lanes), buf.dtype),
        scratch_shapes=[pltpu.SemaphoreType.DMA((2 * n_ici,)), pltpu.SemaphoreType.DMA((2 * n_ici,))],
    )(buf)
    return lax.dynamic_update_slice(others, buf[None], (2 * lax.axis_index("x") + lax.axis_index("y"), 0, 0))


def _exchange_pieces(pieces):
    n, prow, lanes = pieces.shape

    def body(in_ref, out_ref, send_sems, recv_sems, local_sem):
        x, y, c = lax.axis_index("x"), lax.axis_index("y"), lax.axis_index("c")
        me = 4 * x + 2 * y + c
        mine = pltpu.make_async_copy(in_ref.at[me], out_ref.at[me], local_sem)
        mine.start()
        copies = []
        for k in range(1, N_DEV):
            px, py, pc = x ^ (k >> 2), y ^ ((k >> 1) & 1), c ^ (k & 1)
            peer = 4 * px + 2 * py + pc
            cp = pltpu.make_async_remote_copy(src_ref=in_ref.at[peer], dst_ref=out_ref.at[me],
                                              send_sem=send_sems.at[k - 1], recv_sem=recv_sems.at[k - 1],
                                              device_id=(px, py, pc), device_id_type=MESH)
            cp.start()
            copies.append(cp)
        for cp in copies:
            cp.wait()
        mine.wait()

    return pl.pallas_call(
        body, name="exchange_grads", in_specs=[ANY], out_specs=ANY,
        out_shape=jax.ShapeDtypeStruct((n, prow, lanes), pieces.dtype),
        scratch_shapes=[pltpu.SemaphoreType.DMA((N_DEV - 1,)), pltpu.SemaphoreType.DMA((N_DEV - 1,)),
                        pltpu.SemaphoreType.DMA],
    )(pieces)


def _exchange_many(arrs):
    na = len(arrs)

    def body(*refs):
        ins, outs = refs[:na], refs[na:2 * na]
        send_sems, recv_sems = refs[2 * na:]
        x, y, c = lax.axis_index("x"), lax.axis_index("y"), lax.axis_index("c")
        me = 4 * x + 2 * y + c
        copies = []
        for k in range(1, N_DEV):
            px, py, pc = x ^ (k >> 2), y ^ ((k >> 1) & 1), c ^ (k & 1)
            peer = 4 * px + 2 * py + pc
            for a in range(na):
                n = (k - 1) * na + a
                copies.append(pltpu.make_async_remote_copy(
                    src_ref=ins[a].at[peer], dst_ref=outs[a].at[me], send_sem=send_sems.at[n],
                    recv_sem=recv_sems.at[n], device_id=(px, py, pc), device_id_type=MESH))
        for cp in copies:
            cp.start()
        for cp in copies:
            cp.wait()

    n_remote = (N_DEV - 1) * na
    outs = pl.pallas_call(
        body, name="exchange_grads", in_specs=[ANY] * na, out_specs=[ANY] * na,
        out_shape=[jax.ShapeDtypeStruct(a.shape, a.dtype) for a in arrs],
        scratch_shapes=[pltpu.SemaphoreType.DMA((n_remote,)), pltpu.SemaphoreType.DMA((n_remote,))],
    )(*arrs)
    me = 4 * lax.axis_index("x") + 2 * lax.axis_index("y") + lax.axis_index("c")
    return [lax.dynamic_update_slice(o, lax.dynamic_index_in_dim(a, me, 0, keepdims=True), (me, 0, 0))
            for o, a in zip(outs, arrs)]


def _share_many(reds, halves):
    na = len(reds)

    def body(*refs):
        ins, outs = refs[:na], refs[na:2 * na]
        send_sems, recv_sems = refs[2 * na:]
        x, y, c = lax.axis_index("x"), lax.axis_index("y"), lax.axis_index("c")
        copies = []
        for a in range(na):
            h = halves[a]
            copies.append(pltpu.make_async_remote_copy(
                src_ref=ins[a].at[pl.ds(0, h), :], dst_ref=outs[a].at[pl.ds(c * h, h), :], send_sem=send_sems.at[a],
                recv_sem=recv_sems.at[a], device_id=(x, y, 1 - c), device_id_type=MESH))
        for cp in copies:
            cp.start()
        for cp in copies:
            cp.wait()

    outs = pl.pallas_call(
        body, name="share_halves", in_specs=[ANY] * na, out_specs=[ANY] * na,
        out_shape=[jax.ShapeDtypeStruct((r.shape[0] + h, r.shape[1]), r.dtype) for r, h in zip(reds, halves)],
        scratch_shapes=[pltpu.SemaphoreType.DMA((na,)), pltpu.SemaphoreType.DMA((na,))],
    )(*reds)
    c = lax.axis_index("c")
    full = []
    for o, r, h in zip(outs, reds, halves):
        o = lax.dynamic_update_slice(o, r[:h], (c * h, 0))
        if r.shape[0] > h:
            o = lax.dynamic_update_slice(o, r[h:], (2 * h, 0))
        full.append(o)
    return full


SUM_BLOCK_ELEMS = 256 * 1024
ADAM_BLOCK_ELEMS = 256 * 1024


def _sum_slots(slots, *, name="sum_grads"):
    n, prow, lanes = slots.shape
    tm = _tile(prow, max(2 * SUBLANES, SUM_BLOCK_ELEMS // lanes), 2 * SUBLANES)

    def body(s_ref, o_ref):
        acc = s_ref[0].astype(F32)
        for s in range(1, n):
            acc = acc + s_ref[s].astype(F32)
        o_ref[...] = acc

    return pl.pallas_call(
        body, name=name, grid=(prow // tm,),
        in_specs=[pl.BlockSpec((n, tm, lanes), lambda i: (0, i, 0))],
        out_specs=pl.BlockSpec((tm, lanes), lambda i: (i, 0)),
        out_shape=jax.ShapeDtypeStruct((prow, lanes), F32),
        compiler_params=_cparams(("parallel",)),
    )(slots)


def _share_halves(red, half_rows):
    prow, lanes = red.shape
    rep_rows = prow - half_rows
    ch = half_rows // N_CHUNK

    def body(in_ref, out_ref, send_sems, recv_sems, local_sems):
        x, y, c = lax.axis_index("x"), lax.axis_index("y"), lax.axis_index("c")
        src = lambda q: in_ref.at[pl.ds(q * ch, ch), :]
        dst = lambda q: out_ref.at[pl.ds(c * half_rows + q * ch, ch), :]
        local = [pltpu.make_async_copy(src(q), dst(q), local_sems.at[q]) for q in range(N_CHUNK)]
        local.append(pltpu.make_async_copy(in_ref.at[pl.ds(half_rows, rep_rows), :],
                                           out_ref.at[pl.ds(2 * half_rows, rep_rows), :], local_sems.at[N_CHUNK]))
        remote = [pltpu.make_async_remote_copy(src_ref=src(q), dst_ref=dst(q), send_sem=send_sems.at[q],
                                               recv_sem=recv_sems.at[q], device_id=(x, y, 1 - c), device_id_type=MESH)
                  for q in range(N_CHUNK)]
        for cp in remote + local:
            cp.start()
        for cp in remote + local:
            cp.wait()

    return pl.pallas_call(
        body, name="share_halves", in_specs=[ANY], out_specs=ANY,
        out_shape=jax.ShapeDtypeStruct((2 * half_rows + rep_rows, lanes), red.dtype),
        scratch_shapes=[pltpu.SemaphoreType.DMA((N_CHUNK,)), pltpu.SemaphoreType.DMA((N_CHUNK,)),
                        pltpu.SemaphoreType.DMA((N_CHUNK + 1,))],
    )(red)


def _adamw(wp, gp, mp, vp, *, name="adamw"):
    rows, lanes = wp.shape
    tm = _tile(rows, max(SUBLANES, ADAM_BLOCK_ELEMS // lanes), SUBLANES)
    c1 = 1.0 / (1.0 - ADAM_B1 ** ADAM_STEP)
    c2 = 1.0 / (1.0 - ADAM_B2 ** ADAM_STEP)

    def body(w_ref, g_ref, m_ref, v_ref, d_ref, nm_ref, nv_ref):
        g = g_ref[...]
        m = ADAM_B1 * m_ref[...] + (1.0 - ADAM_B1) * g
        v = ADAM_B2 * v_ref[...] + (1.0 - ADAM_B2) * (g * g)
        m_hat = m * c1
        v_hat = v * c2
        d_ref[...] = -ADAM_LR * (m_hat / (jnp.sqrt(v_hat) + ADAM_EPS) + ADAM_WD * w_ref[...])
        nm_ref[...] = m
        nv_ref[...] = v

    spec = pl.BlockSpec((tm, lanes), lambda i: (i, 0))
    shp = jax.ShapeDtypeStruct((rows, lanes), F32)
    return pl.pallas_call(
        body, name=name, grid=(rows // tm,), in_specs=[spec] * 4, out_specs=[spec] * 3,
        out_shape=[shp, shp, shp], compiler_params=_cparams(("parallel",)),
    )(wp, gp, mp, vp)


AG_ROW_MULT = 2 * N_CHUNK * 16
GRAD_ROW_MULT = 2048
REP_ROW_MULT = 512


def _shard_to_front(a, axis):
    n = a.shape[axis]
    a = a.reshape(a.shape[:axis] + (N_CHIP, n // N_CHIP) + a.shape[axis + 1:])
    return jnp.moveaxis(a, axis, 0)


def _shards_to_full(a, axis):
    a = jnp.moveaxis(a, 0, axis)
    return a.reshape(a.shape[:axis] + (a.shape[axis] * a.shape[axis + 1],) + a.shape[axis + 2:])


def kernel(x, p, a_w_in, a_b_f, a_w_out, b_w_in, b_conv_w, b_conv_b, b_w_a, b_b_a, b_w_i, b_b_i, b_lam, b_w_out, f_w_up, f_conv_w, f_conv_b, f_w_down, ln1_g, ln1_b, ln2_g, ln2_b, ple_w, ple_gate_w, ple_gate_b, loss_target, m_a_w_in, m_a_b_f, m_a_w_out, m_b_w_in, m_b_conv_w, m_b_conv_b, m_b_w_a, m_b_b_a, m_b_w_i, m_b_b_i, m_b_lam, m_b_w_out, m_f_w_up, m_f_conv_w, m_f_conv_b, m_f_w_down, m_ln1_g, m_ln1_b, m_ln2_g, m_ln2_b, m_ple_w, m_ple_gate_w, m_ple_gate_b, v_a_w_in, v_a_b_f, v_a_w_out, v_b_w_in, v_b_conv_w, v_b_conv_b, v_b_w_a, v_b_b_a, v_b_w_i, v_b_b_i, v_b_lam, v_b_w_out, v_f_w_up, v_f_conv_w, v_f_conv_b, v_f_w_down, v_ln1_g, v_ln1_b, v_ln2_g, v_ln2_b, v_ple_w, v_ple_gate_w, v_ple_gate_b):
    args = dict(locals())
    swap = lambda n, a: jnp.swapaxes(a, -1, -2) if n in TRANSPOSED else a
    axis_of = lambda n: (3 - SHARD_AXIS[n]) if n in TRANSPOSED else SHARD_AXIS[n]
    local_w = {n: swap(n, args[n]) for n in WEIGHTS}
    local_m = {n: swap(n, args['m_' + n]) for n in WEIGHTS}
    local_v = {n: swap(n, args['v_' + n]) for n in WEIGHTS}

    as_pairs = lambda a: lax.bitcast_convert_type(a, BF16)
    from_pairs = lambda a: lax.bitcast_convert_type(a, F32)
    send = [local_w[n] for n in GATHER_BF16] + [as_pairs(local_w[n]) for n in GATHER_F32]
    gathered = _all_gather_shards(_pack(send, BF16, AG_ROW_MULT))
    shapes = [local_w[n].shape for n in GATHER_BF16] + [local_w[n].shape + (2,) for n in GATHER_F32]
    parts = _unpack(gathered, shapes)
    full_w = {}
    for n, part in zip(GATHER_BF16 + GATHER_F32, parts):
        if n in GATHER_F32:
            part = from_pairs(part)
        full_w[n] = _shards_to_full(part, axis_of(n))
    for n in REPLICATED:
        full_w[n] = local_w[n]

    loss_row, grad_x, g = _local_step(x[0], p[:, 0], loss_target[0], full_w)

    grads = {n: jnp.stack(g[n]) for n in SHARDED}
    small = [n for n in SHARDED if n not in NATIVE]
    shard_rows = _round_up(sum(_round_up(local_w[n].size, LANES) for n in small) // LANES, GRAD_ROW_MULT)
    half_rows = shard_rows // 2

    def native_pieces(n):
        a = _shard_to_front(grads[n], axis_of(n))
        return a.reshape(N_DEV, -1, a.shape[-1]).astype(BF16)

    sharded_g = _pack([_shard_to_front(grads[n], axis_of(n)) for n in small], BF16, GRAD_ROW_MULT, lead=1)
    rep_g = _pack([jnp.stack(g[n]) for n in REPLICATED] + [loss_row], F32, REP_ROW_MULT)
    rep_rows = rep_g.shape[0]
    rep_top = lax.reduce_precision(rep_g, 8, 7)
    rep_hi = rep_top.astype(BF16)
    rep_lo = (rep_g - rep_top).astype(BF16)
    packed_pieces = jnp.concatenate(
        [sharded_g.reshape(N_DEV, half_rows, LANES),
         jnp.broadcast_to(jnp.concatenate([rep_hi, rep_lo]), (N_DEV, 2 * rep_rows, LANES))], axis=1)
    slots = _exchange_many([native_pieces(n) for n in NATIVE] + [packed_pieces])
    reduced = [_sum_slots(s, name="sum_" + n) for s, n in zip(slots, NATIVE + ["packed"])]
    reduced[-1] = jnp.concatenate([reduced[-1][:half_rows], reduced[-1][half_rows:half_rows + rep_rows]
                                   + reduced[-1][half_rows + rep_rows:]])
    full_g = _share_many(reduced, [r.shape[0] for r in reduced[:-1]] + [half_rows])

    rep_shapes = [local_w[n].shape for n in REPLICATED] + [loss_row.shape]

    def packed(d):
        return jnp.concatenate([_pack([d[n] for n in small], F32, GRAD_ROW_MULT),
                                _pack([d[n] for n in REPLICATED] + [jnp.zeros_like(loss_row)], F32, REP_ROW_MULT)])

    def unpacked(buf):
        d = dict(zip(small, _unpack(buf[:shard_rows], [local_w[n].shape for n in small])))
        d.update(zip(REPLICATED, _unpack(buf[shard_rows:], rep_shapes)))
        return d

    outs = [unpacked(b) for b in (full_g[-1],) + tuple(_adamw(packed(local_w), full_g[-1], packed(local_m),
                                                              packed(local_v), name="adamw_packed"))]
    for n, gn in zip(NATIVE, full_g):
        two_d = lambda a: a.reshape(gn.shape)
        res = _adamw(two_d(local_w[n]), gn, two_d(local_m[n]), two_d(local_v[n]), name="adamw_" + n)
        for d, a in zip(outs, (gn,) + tuple(res)):
            d[n] = a.reshape(local_w[n].shape)
    loss = _unpack(full_g[-1][shard_rows:], rep_shapes)[-1][0, 0]
    return (loss, grad_x[None], *[swap(n, d[n]) for d in outs for n in WEIGHTS])
```

```python
import functools
import math

import jax
import jax.numpy as jnp
from jax import lax
from jax.experimental import pallas as pl
from jax.experimental.pallas import tpu as pltpu

F32 = jnp.float32
BF16 = jnp.bfloat16

D_MODEL = 1024
DEPTH = 4
N_HEADS = 16
HEAD_DIM = 64
N_BLOCKS_B = 8
BLOCK_B = 128
CONV_B = 4
LRU_C = 8.0
D_FF = 2816
CONV_F = 3
D_PLE = 256
LN_EPS = 1e-5
ALPHA = (2.0 * DEPTH) ** 0.25
ATTN_SCALE = 1.0 / math.sqrt(HEAD_DIM)

ADAM_LR = 0.001
ADAM_B1 = 0.9
ADAM_B2 = 0.999
ADAM_EPS = 1e-08
ADAM_WD = 0.01
ADAM_STEP = 10

LANES = 128
SUBLANES = 8
VMEM_LIMIT = 52 * 1024 * 1024
NEG_BIG = -1e30
N_DEV = 8
N_CHIP = 4

WEIGHTS = ['a_w_in', 'a_b_f', 'a_w_out', 'b_w_in', 'b_conv_w', 'b_conv_b', 'b_w_a', 'b_b_a', 'b_w_i', 'b_b_i',
           'b_lam', 'b_w_out', 'f_w_up', 'f_conv_w', 'f_conv_b', 'f_w_down', 'ln1_g', 'ln1_b', 'ln2_g', 'ln2_b',
           'ple_w', 'ple_gate_w', 'ple_gate_b']
SHARD_AXIS = {'a_w_in': 2, 'a_w_out': 1, 'b_w_in': 2, 'b_conv_w': 2, 'b_conv_b': 1, 'b_lam': 1, 'b_w_out': 1,
              'f_w_up': 2, 'f_conv_w': 2, 'f_w_down': 1, 'ple_w': 2, 'ple_gate_w': 1}
TRANSPOSED = ('a_w_in',)
NATIVE = ['a_w_out', 'b_w_in', 'b_w_out', 'f_w_up', 'f_w_down', 'ple_w', 'ple_gate_w']
SHARDED = [n for n in WEIGHTS if n in SHARD_AXIS]
REPLICATED = [n for n in WEIGHTS if n not in SHARD_AXIS]
GATHER_BF16 = ['a_w_in', 'a_w_out', 'b_w_in', 'b_w_out', 'f_w_up', 'f_w_down', 'ple_w', 'ple_gate_w']
GATHER_F32 = ['b_conv_w', 'b_conv_b', 'b_lam', 'f_conv_w']


def _cparams(sem, vmem=VMEM_LIMIT):
    return pltpu.CompilerParams(dimension_semantics=sem, vmem_limit_bytes=vmem)


def _tile(n, cap, q=LANES):
    best = None
    for t in range(q, min(n, cap) + 1, q):
        if n % t == 0:
            best = t
    return best if best is not None else n


def _sigmoid(x):
    return 1.0 / (1.0 + jnp.exp(-x))


_GELU_C = math.sqrt(2.0 / math.pi)


def _gelu_and_grad(x):
    x2 = x * x
    t = jnp.tanh(_GELU_C * (x + 0.044715 * x * x2))
    cdf = 0.5 * (1.0 + t)
    g = x * cdf
    dg = cdf + x * 0.5 * (1.0 - t * t) * _GELU_C * (1.0 + 3.0 * 0.044715 * x2)
    return g, dg


def _gelu(x):
    t = jnp.tanh(_GELU_C * (x + 0.044715 * x * x * x))
    return x * (0.5 * (1.0 + t))


def _log1p(u):
    w = 1.0 + u
    d = w - 1.0
    return jnp.where(d == 0.0, u, jnp.log(w) * (u / jnp.where(d == 0.0, 1.0, d)))


def _softplus(y):
    return jnp.maximum(y, 0.0) + _log1p(jnp.exp(-jnp.abs(y)))


def _log_sigmoid(z):
    return -_softplus(-z)


def _neg_expm1(x):
    poly = x * (1.0 + x * (1.0 / 2 + x * (1.0 / 6 + x * (1.0 / 24 + x * (1.0 / 120 + x * (1.0 / 720 + x * (1.0 / 5040)))))))
    return -jnp.where(x > -0.25, poly, jnp.exp(x) - 1.0)


def _split3(x):
    hi = x.astype(BF16)
    r1 = x - hi.astype(F32)
    mid = r1.astype(BF16)
    lo = (r1 - mid.astype(F32)).astype(BF16)
    return hi, mid, lo


def _shift_down(x, halo, k):
    rolled = pltpu.roll(x, k, axis=0)
    hal = pltpu.roll(halo, k, axis=0)
    r8 = lax.broadcasted_iota(jnp.int32, halo.shape, 0)
    head = jnp.where(r8 < k, hal, rolled[:SUBLANES])
    if x.shape[0] == SUBLANES:
        return head
    return jnp.concatenate([head, rolled[SUBLANES:]], axis=0)


def _shift_up(x, nxt, k):
    n = x.shape[0]
    rolled = pltpu.roll(x, n - k, axis=0)
    nx = pltpu.roll(nxt, SUBLANES - k, axis=0)
    r8 = lax.broadcasted_iota(jnp.int32, nxt.shape, 0)
    tail = jnp.where(r8 >= SUBLANES - k, nx, rolled[n - SUBLANES:])
    if n == SUBLANES:
        return tail
    return jnp.concatenate([rolled[:n - SUBLANES], tail], axis=0)


def _split_spec(arr_ndim, part_cols, br, bc, idx):
    if arr_ndim == 3:
        nbh = part_cols // bc
        return pl.BlockSpec((None, br, bc), lambda i, j, k: (lax.div(idx(i, j, k)[1], nbh), idx(i, j, k)[0],
                                                             lax.rem(idx(i, j, k)[1], nbh)))
    return pl.BlockSpec((br, bc), lambda i, j, k: idx(i, j, k))


def _dims(arr):
    if arr.ndim == 3:
        return arr.shape[1], arr.shape[0] * arr.shape[2], arr.shape[2]
    return arr.shape[0], arr.shape[1], arr.shape[1]


def _mm(a, b, *, ta=False, tb=False, out_dtype=F32, out_split=1, add=None, add_scale=1.0,
        tm_cap=1024, tn_cap=1024, tk_cap=1408, name):
    ar, ac, apart = _dims(a)
    br_, bc_, bpart = _dims(b)
    m, kdim = (ac, ar) if ta else (ar, ac)
    kdim_b, n = (bc_, br_) if tb else (br_, bc_)
    assert kdim == kdim_b, (name, a.shape, b.shape)
    tm = _tile(apart, tm_cap) if ta else _tile(m, tm_cap, SUBLANES)
    tn = _tile(n, tn_cap, SUBLANES) if tb else _tile(math.gcd(bpart, n // out_split), tn_cap)
    if ta:
        tk = _tile(kdim, 1024, 2 * SUBLANES)
    elif tb:
        tk = _tile(math.gcd(apart, bpart), tk_cap)
    else:
        tk = _tile(apart, tk_cap)
    assert m % tm == 0 and n % tn == 0 and kdim % tk == 0, (name, m, n, kdim, tm, tn, tk)
    nk = kdim // tk
    a_spec = (_split_spec(a.ndim, apart, tk, tm, lambda i, j, k: (k, i)) if ta
              else _split_spec(a.ndim, apart, tm, tk, lambda i, j, k: (i, k)))
    b_spec = (_split_spec(b.ndim, bpart, tn, tk, lambda i, j, k: (j, k)) if tb
              else _split_spec(b.ndim, bpart, tk, tn, lambda i, j, k: (k, j)))
    if out_split > 1:
        out_shape = jax.ShapeDtypeStruct((out_split, m, n // out_split), out_dtype)
        o_spec = _split_spec(3, n // out_split, tm, tn, lambda i, j, k: (i, j))
    else:
        out_shape = jax.ShapeDtypeStruct((m, n), out_dtype)
        o_spec = pl.BlockSpec((tm, tn), lambda i, j, k: (i, j))
    dn = (((0 if ta else 1,), (1 if tb else 0,)), ((), ()))
    in_specs = [a_spec, b_spec]
    args = [a, b]
    if add is not None:
        in_specs.append(pl.BlockSpec((tm, tn), lambda i, j, k: (i, j)))
        args.append(add)
    use_acc = nk > 1
    has_add = add is not None

    def body(*refs):
        a_ref, b_ref = refs[0], refs[1]
        add_ref = refs[2] if has_add else None
        o_ref = refs[3] if has_add else refs[2]
        part = lax.dot_general(a_ref[...], b_ref[...], dn, preferred_element_type=F32)

        def finish(acc):
            if has_add:
                acc = acc + add_scale * add_ref[...]
            o_ref[...] = acc.astype(out_dtype)

        if not use_acc:
            finish(part)
        else:
            acc_ref = refs[-1]
            k = pl.program_id(2)

            @pl.when(k == 0)
            def _():
                acc_ref[...] = part

            @pl.when(k > 0)
            def _():
                acc_ref[...] += part

            @pl.when(k == nk - 1)
            def _():
                finish(acc_ref[...])

    return pl.pallas_call(
        body, name=name, grid=(m // tm, n // tn, nk), in_specs=in_specs, out_specs=o_spec, out_shape=out_shape,
        scratch_shapes=[pltpu.VMEM((tm, tn), F32)] if use_acc else [],
        compiler_params=_cparams(("parallel", "parallel", "arbitrary")),
    )(*args)


def _ln_fwd(x, m, g, b, *, name, tm=512):
    t, d = x.shape

    def body(x_ref, m_ref, g_ref, b_ref, y_ref, yb_ref, xhat_ref, rstd_ref):
        z = ALPHA * x_ref[...] + m_ref[...]
        mu = jnp.mean(z, axis=-1, keepdims=True)
        zc = z - mu
        var = jnp.mean(zc * zc, axis=-1, keepdims=True)
        rstd = lax.rsqrt(var + LN_EPS)
        xhat = zc * rstd
        y = xhat * g_ref[...] + b_ref[...]
        y_ref[...] = y
        yb_ref[...] = y.astype(BF16)
        xhat_ref[...] = xhat
        rstd_ref[...] = rstd

    row = pl.BlockSpec((tm, d), lambda i: (i, 0))
    vec = pl.BlockSpec((1, d), lambda i: (0, 0))
    return pl.pallas_call(
        body, name=name, grid=(t // tm,), in_specs=[row, row, vec, vec],
        out_specs=[row, row, row, pl.BlockSpec((tm, 1), lambda i: (i, 0))],
        out_shape=[jax.ShapeDtypeStruct((t, d), F32), jax.ShapeDtypeStruct((t, d), BF16),
                   jax.ShapeDtypeStruct((t, d), F32), jax.ShapeDtypeStruct((t, 1), F32)],
        compiler_params=_cparams(("parallel",)),
    )(x, m, g, b)


def _ln_bwd(dy, xhat, rstd, g, *, name, tm=512):
    t, d = dy.shape

    def body(dy_ref, xhat_ref, rstd_ref, g_ref, dz_ref, dzb_ref, dg_ref, db_ref):
        @pl.when(pl.program_id(0) == 0)
        def _():
            dg_ref[...] = jnp.zeros_like(dg_ref)
            db_ref[...] = jnp.zeros_like(db_ref)

        dyv = dy_ref[...]
        xh = xhat_ref[...]
        dg_ref[...] += jnp.sum(dyv * xh, axis=0, keepdims=True)
        db_ref[...] += jnp.sum(dyv, axis=0, keepdims=True)
        dxh = dyv * g_ref[...]
        m1 = jnp.mean(dxh, axis=-1, keepdims=True)
        m2 = jnp.mean(dxh * xh, axis=-1, keepdims=True)
        dz = rstd_ref[...] * (dxh - m1 - xh * m2)
        dz_ref[...] = dz
        dzb_ref[...] = dz.astype(BF16)

    row = pl.BlockSpec((tm, d), lambda i: (i, 0))
    vec = pl.BlockSpec((1, d), lambda i: (0, 0))
    return pl.pallas_call(
        body, name=name, grid=(t // tm,), in_specs=[row, row, pl.BlockSpec((tm, 1), lambda i: (i, 0)), vec],
        out_specs=[row, row, vec, vec],
        out_shape=[jax.ShapeDtypeStruct((t, d), F32), jax.ShapeDtypeStruct((t, d), BF16),
                   jax.ShapeDtypeStruct((1, d), F32), jax.ShapeDtypeStruct((1, d), F32)],
        compiler_params=_cparams(("arbitrary",)),
    )(dy, xhat, rstd, g)


def _ple_fwd(x2, gl, pe, gate_b, *, name, tm=512):
    t, d = x2.shape

    def body(x_ref, gl_ref, pe_ref, b_ref, y_ref, yb_ref):
        y = x_ref[...] + _sigmoid(gl_ref[...] + b_ref[...]) * pe_ref[...]
        y_ref[...] = y
        yb_ref[...] = y.astype(BF16)

    row = pl.BlockSpec((tm, d), lambda i: (i, 0))
    vec = pl.BlockSpec((1, d), lambda i: (0, 0))
    return pl.pallas_call(
        body, name=name, grid=(t // tm,), in_specs=[row, row, row, vec], out_specs=[row, row],
        out_shape=[jax.ShapeDtypeStruct((t, d), F32), jax.ShapeDtypeStruct((t, d), BF16)],
        compiler_params=_cparams(("parallel",)),
    )(x2, gl, pe, gate_b)


def _ple_bwd(dx3, gl, pe, gate_b, *, name, tm=512):
    t, d = dx3.shape

    def body(dx_ref, gl_ref, pe_ref, b_ref, dgl_ref, dpe_ref, db_ref):
        @pl.when(pl.program_id(0) == 0)
        def _():
            db_ref[...] = jnp.zeros_like(db_ref)

        dx = dx_ref[...]
        gt = _sigmoid(gl_ref[...] + b_ref[...])
        dgl = dx * pe_ref[...] * gt * (1.0 - gt)
        db_ref[...] += jnp.sum(dgl, axis=0, keepdims=True)
        dgl_ref[...] = dgl.astype(BF16)
        dpe_ref[...] = (dx * gt).astype(BF16)

    row = pl.BlockSpec((tm, d), lambda i: (i, 0))
    vec = pl.BlockSpec((1, d), lambda i: (0, 0))
    return pl.pallas_call(
        body, name=name, grid=(t // tm,), in_specs=[row, row, row, vec], out_specs=[row, row, vec],
        out_shape=[jax.ShapeDtypeStruct((t, d), BF16), jax.ShapeDtypeStruct((t, d), BF16),
                   jax.ShapeDtypeStruct((1, d), F32)],
        compiler_params=_cparams(("arbitrary",)),
    )(dx3, gl, pe, gate_b)


def _loss_bwd(y, tgt, *, name, tm=512):
    t, d = y.shape

    def body(y_ref, t_ref, dy_ref, l_ref):
        @pl.when(pl.program_id(0) == 0)
        def _():
            l_ref[...] = jnp.zeros_like(l_ref)

        err = y_ref[...] - t_ref[...]
        dy_ref[...] = err * (1.0 / d)
        part = jnp.sum(jnp.sum(err * err, axis=0, keepdims=True), axis=1, keepdims=True) * (0.5 / d)
        l_ref[...] += jnp.broadcast_to(part, l_ref.shape)

    row = pl.BlockSpec((tm, d), lambda i: (i, 0))
    return pl.pallas_call(
        body, name=name, grid=(t // tm,), in_specs=[row, row],
        out_specs=[row, pl.BlockSpec((1, LANES), lambda i: (0, 0))],
        out_shape=[jax.ShapeDtypeStruct((t, d), F32), jax.ShapeDtypeStruct((1, LANES), F32)],
        compiler_params=_cparams(("arbitrary",)),
    )(y, tgt)


def _conv_causal(x, halo, w_ref, b, kw):
    acc = x * w_ref[kw - 1:kw, :] + b
    for k in range(kw - 1):
        acc = acc + _shift_down(x, halo, kw - 1 - k) * w_ref[k:k + 1, :]
    return acc


def _ffn_act_fwd(hdn, conv_w, conv_b, *, name, tm=512):
    _, t, f = hdn.shape
    tc = _tile(f, 1408)
    hb = tm // SUBLANES

    def body(h_ref, halo_ref, w_ref, b_ref, a_ref):
        first = pl.program_id(1) == 0
        parts = []
        for s in range(2):
            halo = jnp.where(first, 0.0, halo_ref[s])
            parts.append(_conv_causal(h_ref[s], halo, w_ref.at[s], b_ref[s], CONV_F))
        a_ref[...] = (_gelu(parts[1]) * parts[0]).astype(BF16)

    return pl.pallas_call(
        body, name=name, grid=(f // tc, t // tm),
        in_specs=[pl.BlockSpec((2, tm, tc), lambda j, i: (0, i, j)),
                  pl.BlockSpec((2, SUBLANES, tc), lambda j, i: (0, jnp.maximum(i * hb - 1, 0), j)),
                  pl.BlockSpec((2, CONV_F, tc), lambda j, i: (0, 0, j)),
                  pl.BlockSpec((2, 1, tc), lambda j, i: (0, 0, j))],
        out_specs=pl.BlockSpec((tm, tc), lambda j, i: (i, j)),
        out_shape=jax.ShapeDtypeStruct((t, f), BF16),
        compiler_params=_cparams(("parallel", "arbitrary")),
    )(hdn, hdn, conv_w, conv_b)


def _ffn_act_bwd(da, hdn, conv_w, conv_b, *, name, tm=256):
    _, t, f = hdn.shape
    tc = _tile(f, 1408)
    hb = tm // SUBLANES
    nr = t // tm

    strip = 2 * SUBLANES
    ns = tm // strip
    pieces = [(c0, min(2 * LANES, tc - c0)) for c0 in range(0, tc, 2 * LANES)]

    def body(da_ref, h_ref, halo_ref, w_ref, b_ref, dh_ref, dw_ref, db_ref, carry_ref, dw_acc, db_acc):
        i = pl.program_id(1)
        r = nr - 1 - i

        @pl.when(i == 0)
        def _():
            carry_ref[...] = jnp.zeros_like(carry_ref)
            dw_acc[...] = jnp.zeros_like(dw_acc)
            db_acc[...] = jnp.zeros_like(db_acc)

        def do_strip(si, carry):
            s = ns - 1 - si
            r0 = pl.multiple_of(s * strip, strip)
            above = pl.multiple_of(jnp.maximum(s * strip - SUBLANES, 0), SUBLANES)
            for c0, cw in pieces:
                cols = slice(c0, c0 + cw)
                xs, hcs = [], []
                for part in range(2):
                    x = h_ref[part, pl.ds(r0, strip), cols]
                    halo = jnp.where(r == 0, 0.0, halo_ref[part, :, cols])
                    prev = jnp.where(s == 0, halo, h_ref[part, pl.ds(above, SUBLANES), cols])
                    sh = [_shift_down(x, prev, k) for k in range(1, CONV_F)]
                    hc = x * w_ref[part, CONV_F - 1:CONV_F, cols] + b_ref[part, :, cols]
                    for k in range(1, CONV_F):
                        hc = hc + sh[k - 1] * w_ref[part, CONV_F - 1 - k:CONV_F - k, cols]
                    xs.append([x] + sh)
                    hcs.append(hc)
                g, dg = _gelu_and_grad(hcs[1])
                dav = da_ref[pl.ds(r0, strip), cols]
                dhc = [dav * g, dav * hcs[0] * dg]
                for part in range(2):
                    d = dhc[part]
                    nxt = carry_ref[part, :, cols]
                    db_acc[part, :, cols] += d
                    out = d * w_ref[part, CONV_F - 1:CONV_F, cols]
                    for k in range(CONV_F):
                        dw_acc[part, CONV_F - 1 - k, :, cols] += d * xs[part][k]
                        if k:
                            out = out + _shift_up(d, nxt, k) * w_ref[part, CONV_F - 1 - k:CONV_F - k, cols]
                    dh_ref[part, pl.ds(r0, strip), cols] = out.astype(BF16)
                    carry_ref[part, :, cols] = d[:SUBLANES]
            return carry

        lax.fori_loop(0, ns, do_strip, 0)

        @pl.when(i == nr - 1)
        def _():
            dw_ref[...] = jnp.sum(dw_acc[...], axis=2)
            db_ref[...] = jnp.sum(db_acc[...], axis=1, keepdims=True)

    return pl.pallas_call(
        body, name=name, grid=(f // tc, nr),
        in_specs=[pl.BlockSpec((tm, tc), lambda j, i: (nr - 1 - i, j)),
                  pl.BlockSpec((2, tm, tc), lambda j, i: (0, nr - 1 - i, j)),
                  pl.BlockSpec((2, SUBLANES, tc), lambda j, i: (0, jnp.maximum((nr - 1 - i) * hb - 1, 0), j)),
                  pl.BlockSpec((2, CONV_F, tc), lambda j, i: (0, 0, j)),
                  pl.BlockSpec((2, 1, tc), lambda j, i: (0, 0, j))],
        out_specs=[pl.BlockSpec((2, tm, tc), lambda j, i: (0, nr - 1 - i, j)),
                   pl.BlockSpec((2, CONV_F, tc), lambda j, i: (0, 0, j)),
                   pl.BlockSpec((2, 1, tc), lambda j, i: (0, 0, j))],
        out_shape=[jax.ShapeDtypeStruct((2, t, f), BF16), jax.ShapeDtypeStruct((2, CONV_F, f), F32),
                   jax.ShapeDtypeStruct((2, 1, f), F32)],
        scratch_shapes=[pltpu.VMEM((2, SUBLANES, tc), F32), pltpu.VMEM((2, CONV_F, 2 * SUBLANES, tc), F32),
                        pltpu.VMEM((2, 2 * SUBLANES, tc), F32)],
        compiler_params=_cparams(("arbitrary", "arbitrary")),
    )(da, hdn, hdn, conv_w, conv_b)


def _fgate_fwd(xb, w_f, b_f, *, name, tm=256):
    t, d = xb.shape

    def body(x_ref, w_ref, b_ref, z_ref, c_ref, carry_ref):
        @pl.when(pl.program_id(0) == 0)
        def _():
            carry_ref[...] = jnp.zeros_like(carry_ref)

        z = jnp.dot(x_ref[...], w_ref[...], preferred_element_type=F32) + b_ref[...]
        z_ref[...] = z
        ls = _log_sigmoid(z)
        rr = lax.broadcasted_iota(jnp.int32, (tm, tm), 0)
        cc = lax.broadcasted_iota(jnp.int32, (tm, tm), 1)
        tri = (cc <= rr).astype(BF16)
        cum = carry_ref[...]
        for piece in _split3(ls):
            cum = cum + jnp.dot(tri, piece, preferred_element_type=F32)
        c_ref[...] = cum
        carry_ref[...] = cum[tm - 1:tm, :]

    return pl.pallas_call(
        body, name=name, grid=(t // tm,),
        in_specs=[pl.BlockSpec((tm, d), lambda i: (i, 0)), pl.BlockSpec((d, LANES), lambda i: (0, 0)),
                  pl.BlockSpec((1, LANES), lambda i: (0, 0))],
        out_specs=[pl.BlockSpec((tm, LANES), lambda i: (i, 0)), pl.BlockSpec((tm, LANES), lambda i: (i, 0))],
        out_shape=[jax.ShapeDtypeStruct((t, LANES), F32), jax.ShapeDtypeStruct((t, LANES), F32)],
        scratch_shapes=[pltpu.VMEM((1, LANES), F32)],
        compiler_params=_cparams(("arbitrary",)),
    )(xb, w_f, b_f)


def _fgate_bwd(dc, z, *, name, tm=256):
    t = dc.shape[0]
    nr = t // tm

    def body(dc_ref, z_ref, dz_ref, db_ref, carry_ref):
        @pl.when(pl.program_id(0) == 0)
        def _():
            carry_ref[...] = jnp.zeros_like(carry_ref)
            db_ref[...] = jnp.zeros_like(db_ref)

        rr = lax.broadcasted_iota(jnp.int32, (tm, tm), 0)
        cc = lax.broadcasted_iota(jnp.int32, (tm, tm), 1)
        tri = (cc >= rr).astype(BF16)
        cum = carry_ref[...]
        for piece in _split3(dc_ref[...]):
            cum = cum + jnp.dot(tri, piece, preferred_element_type=F32)
        carry_ref[...] = cum[0:1, :]
        dz = cum * _sigmoid(-z_ref[...])
        db_ref[...] += jnp.sum(dz, axis=0, keepdims=True)
        dz_ref[...] = dz.astype(BF16)

    rev = pl.BlockSpec((tm, LANES), lambda i: (nr - 1 - i, 0))
    return pl.pallas_call(
        body, name=name, grid=(nr,), in_specs=[rev, rev],
        out_specs=[rev, pl.BlockSpec((1, LANES), lambda i: (0, 0))],
        out_shape=[jax.ShapeDtypeStruct((t, LANES), BF16), jax.ShapeDtypeStruct((1, LANES), F32)],
        scratch_shapes=[pltpu.VMEM((1, LANES), F32)],
        compiler_params=_cparams(("arbitrary",)),
    )(dc, z)


def _head_lane_select(blk, head):
    lane = lax.broadcasted_iota(jnp.int32, (1, LANES), 1)
    return jnp.sum(jnp.where(lane == head, blk, 0.0), axis=1, keepdims=True)


def _flash_fwd(qkv, c, crow, *, name, tq=512):
    _, t, d = qkv.shape
    nq = t // tq
    n_pairs = d // LANES

    def body(q_ref, k_ref, v_ref, c_ref, cr0_ref, cr1_ref, o_ref, lse_ref):
        hp = pl.program_id(0)
        i = pl.program_id(1)
        lane = lax.broadcasted_iota(jnp.int32, (1, LANES), 1)
        rr = lax.broadcasted_iota(jnp.int32, (tq, tq), 0)
        cc = lax.broadcasted_iota(jnp.int32, (tq, tq), 1)
        causal = cc <= rr
        q = q_ref[...]
        cblk = c_ref[...]
        outs, lses = [], []
        for hh, cr_ref in enumerate((cr0_ref, cr1_ref)):
            sel = (lane < HEAD_DIM) if hh == 0 else (lane >= HEAD_DIM)
            qh = jnp.where(sel, q, jnp.zeros_like(q)) * jnp.asarray(ATTN_SCALE, BF16)
            ci = _head_lane_select(cblk, 2 * hp + hh)

            def kv_step(j, carry, masked, qh=qh, ci=ci, cr_ref=cr_ref):
                m, l, acc = carry
                off = pl.multiple_of(j * tq, tq)
                kj = k_ref[pl.ds(off, tq), :]
                vj = v_ref[pl.ds(off, tq), :]
                s = lax.dot_general(qh, kj, (((1,), (1,)), ((), ())), preferred_element_type=F32)
                s = s + (ci - cr_ref[:, pl.ds(off, tq)])
                if masked:
                    s = jnp.where(causal, s, NEG_BIG)
                m_new = jnp.maximum(m, jnp.max(s, axis=1, keepdims=True))
                p = jnp.exp(s - m_new)
                corr = jnp.exp(m - m_new)
                l = l * corr + jnp.sum(p, axis=1, keepdims=True)
                acc = acc * corr + jnp.dot(p.astype(BF16), vj, preferred_element_type=F32)
                return m_new, l, acc

            init = (jnp.full((tq, 1), NEG_BIG, F32), jnp.zeros((tq, 1), F32), jnp.zeros((tq, LANES), F32))
            carry = lax.fori_loop(0, i, functools.partial(kv_step, masked=False), init)
            m, l, acc = kv_step(i, carry, True)
            outs.append(acc / l)
            lses.append(m + jnp.log(l))
        o_ref[...] = jnp.where(lane < HEAD_DIM, outs[0], outs[1]).astype(BF16)
        lse_ref[...] = jnp.where(lane == 0, lses[0], jnp.where(lane == 1, lses[1], 0.0))

    return pl.pallas_call(
        body, name=name, grid=(n_pairs, nq),
        in_specs=[pl.BlockSpec((None, tq, LANES), lambda h, i: (0, i, h)),
                  pl.BlockSpec((None, t, LANES), lambda h, i: (1, 0, h)),
                  pl.BlockSpec((None, t, LANES), lambda h, i: (2, 0, h)),
                  pl.BlockSpec((tq, LANES), lambda h, i: (i, 0)),
                  pl.BlockSpec((None, 1, t), lambda h, i: (2 * h, 0, 0)),
                  pl.BlockSpec((None, 1, t), lambda h, i: (2 * h + 1, 0, 0))],
        out_specs=[pl.BlockSpec((tq, LANES), lambda h, i: (i, h)),
                   pl.BlockSpec((None, tq, LANES), lambda h, i: (h, i, 0))],
        out_shape=[jax.ShapeDtypeStruct((t, d), BF16), jax.ShapeDtypeStruct((n_pairs, t, LANES), F32)],
        compiler_params=_cparams(("parallel", "arbitrary")),
    )(qkv, qkv, qkv, c, crow, crow)


def _attn_delta(o, do, *, name, tm=512):
    t, d = o.shape

    def body(o_ref, do_ref, out_ref):
        prod = o_ref[...].astype(F32) * do_ref[...].astype(F32)
        col = lax.broadcasted_iota(jnp.int32, (d, LANES), 0)
        head = lax.broadcasted_iota(jnp.int32, (d, LANES), 1)
        sel = (lax.div(col, HEAD_DIM) == head).astype(BF16)
        acc = jnp.zeros((tm, LANES), F32)
        for piece in _split3(prod):
            acc = acc + jnp.dot(piece, sel, preferred_element_type=F32)
        out_ref[...] = acc

    row = pl.BlockSpec((tm, d), lambda i: (i, 0))
    return pl.pallas_call(
        body, name=name, grid=(t // tm,), in_specs=[row, row],
        out_specs=pl.BlockSpec((tm, LANES), lambda i: (i, 0)),
        out_shape=jax.ShapeDtypeStruct((t, LANES), F32),
        compiler_params=_cparams(("parallel",)),
    )(o, do)


def _flash_bwd(qkv, do, c, crow, lse_row, delta_row, *, name, tq=512):
    _, t, d = qkv.shape
    nq = t // tq
    n_pairs = d // LANES

    def body(q_ref, do_ref, k_ref, v_ref, c_ref, cr0, cr1, ls0, ls1, dl0, dl1,
             dq_ref, dkv_ref, dc_ref, dcq_ref, dq_acc):
        hp = pl.program_id(0)
        j = pl.program_id(1)
        lane = lax.broadcasted_iota(jnp.int32, (1, LANES), 1)
        kr = lax.broadcasted_iota(jnp.int32, (tq, tq), 0)
        qc = lax.broadcasted_iota(jnp.int32, (tq, tq), 1)
        causal_t = kr <= qc

        @pl.when(j == 0)
        def _():
            dq_acc[...] = jnp.zeros_like(dq_acc)
            dcq_ref[...] = jnp.zeros_like(dcq_ref)

        k = k_ref[...]
        v = v_ref[...]
        cblk = c_ref[...]
        dks, dvs, dcs = [], [], []
        for hh, (cr_ref, ls_ref, dl_ref) in enumerate(((cr0, ls0, dl0), (cr1, ls1, dl1))):
            sel = (lane < HEAD_DIM) if hh == 0 else (lane >= HEAD_DIM)
            kh = jnp.where(sel, k, jnp.zeros_like(k)) * jnp.asarray(ATTN_SCALE, BF16)
            vh = jnp.where(sel, v, jnp.zeros_like(v))
            cj = _head_lane_select(cblk, 2 * hp + hh)

            def q_step(i, carry, masked, hh=hh, kh=kh, vh=vh, cj=cj, cr_ref=cr_ref, ls_ref=ls_ref, dl_ref=dl_ref):
                dk, dv, dcj = carry
                off = pl.multiple_of(i * tq, tq)
                qi = q_ref[pl.ds(off, tq), :]
                doi = do_ref[pl.ds(off, tq), :]
                st = lax.dot_general(kh, qi, (((1,), (1,)), ((), ())), preferred_element_type=F32)
                st = st + (cr_ref[:, pl.ds(off, tq)] - cj)
                if masked:
                    st = jnp.where(causal_t, st, NEG_BIG)
                pt = jnp.exp(st - ls_ref[:, pl.ds(off, tq)])
                dv = dv + jnp.dot(pt.astype(BF16), doi, preferred_element_type=F32)
                dpt = lax.dot_general(vh, doi, (((1,), (1,)), ((), ())), preferred_element_type=F32)
                dst = pt * (dpt - dl_ref[:, pl.ds(off, tq)])
                dcj = dcj + jnp.sum(dst, axis=1, keepdims=True)
                dcq_ref[hh:hh + 1, pl.ds(off, tq)] += jnp.sum(dst, axis=0, keepdims=True)
                dsb = dst.astype(BF16)
                dk = dk + jnp.dot(dsb, qi, preferred_element_type=F32)
                dq_acc[pl.ds(off, tq), :] += lax.dot_general(dsb, kh, (((0,), (0,)), ((), ())),
                                                             preferred_element_type=F32)
                return dk, dv, dcj

            init = (jnp.zeros((tq, LANES), F32), jnp.zeros((tq, LANES), F32), jnp.zeros((tq, 1), F32))
            carry = q_step(j, init, True)
            dk, dv, dcj = lax.fori_loop(j + 1, nq, functools.partial(q_step, masked=False), carry)
            dks.append(dk * ATTN_SCALE)
            dvs.append(dv)
            dcs.append(-dcj)
        dkv_ref[0] = jnp.where(lane < HEAD_DIM, dks[0], dks[1]).astype(BF16)
        dkv_ref[1] = jnp.where(lane < HEAD_DIM, dvs[0], dvs[1]).astype(BF16)
        dc_ref[...] = jnp.where(lane == 0, dcs[0], jnp.where(lane == 1, dcs[1], 0.0))

        @pl.when(j == nq - 1)
        def _():
            dq_ref[...] = dq_acc[...].astype(BF16)

    full = lambda part: pl.BlockSpec((None, t, LANES), lambda h, j: (part, 0, h))
    rowspec = lambda hh: pl.BlockSpec((None, 1, t), lambda h, j: (2 * h + hh, 0, 0))
    return pl.pallas_call(
        body, name=name, grid=(n_pairs, nq),
        in_specs=[full(0),
                  pl.BlockSpec((t, LANES), lambda h, j: (0, h)),
                  pl.BlockSpec((None, tq, LANES), lambda h, j: (1, j, h)),
                  pl.BlockSpec((None, tq, LANES), lambda h, j: (2, j, h)),
                  pl.BlockSpec((tq, LANES), lambda h, j: (j, 0)),
                  rowspec(0), rowspec(1), rowspec(0), rowspec(1), rowspec(0), rowspec(1)],
        out_specs=[pl.BlockSpec((t, LANES), lambda h, j: (0, h)),
                   pl.BlockSpec((2, tq, LANES), lambda h, j: (0, j, h)),
                   pl.BlockSpec((None, tq, LANES), lambda h, j: (h, j, 0)),
                   pl.BlockSpec((None, SUBLANES, t), lambda h, j: (h, 0, 0))],
        out_shape=[jax.ShapeDtypeStruct((t, d), BF16), jax.ShapeDtypeStruct((2, t, d), BF16),
                   jax.ShapeDtypeStruct((n_pairs, t, LANES), F32),
                   jax.ShapeDtypeStruct((n_pairs, SUBLANES, t), F32)],
        scratch_shapes=[pltpu.VMEM((t, LANES), F32)],
        compiler_params=_cparams(("parallel", "arbitrary")),
    )(qkv, do, qkv, qkv, c, crow, crow, lse_row, lse_row, delta_row, delta_row)


def _rows_from_lanes(x, n):
    return x[:, :n].T.reshape(n, 1, x.shape[0])


def _rows_from_pairs(x8):
    hp, t, _ = x8.shape
    return x8[:, :, :2].transpose(0, 2, 1).reshape(2 * hp, 1, t)


def _lanes_from_pairs(x8):
    hp, t, _ = x8.shape
    y = x8[:, :, :2].transpose(1, 0, 2).reshape(t, 2 * hp)
    return jnp.pad(y, ((0, 0), (0, LANES - 2 * hp)))


def _lanes_from_pair_rows(x8):
    hp, _, t = x8.shape
    y = x8[:, :2, :].reshape(2 * hp, t).T
    return jnp.pad(y, ((0, 0), (0, LANES - 2 * hp)))


HM = 2 * HEAD_DIM
LANE_C, LANE_ONE, LANE_LSE = 64, 67, 70
Q_SUB = 256


def _three_parts(col, sign=1.0):
    col = sign * col
    hi = col.astype(BF16).astype(F32)
    r1 = col - hi
    mid = r1.astype(BF16).astype(F32)
    lo = (r1 - mid).astype(BF16).astype(F32)
    return hi, mid, lo


def _fill_lanes(base, lane, first, parts):
    out = base
    for n, part in enumerate(parts):
        out = jnp.where(lane == first + n, part, out)
    return out


def _hm_weight_t(w_t):
    rows, k = w_t.shape
    h = rows // HEAD_DIM
    return jnp.pad(w_t.reshape(h, HEAD_DIM, k), ((0, 0), (0, HM - HEAD_DIM), (0, 0))).reshape(h * HM, k)


def _hm_unpad_t(g_t):
    rows, k = g_t.shape
    h = rows // HM
    return g_t.reshape(h, HM, k)[:, :HEAD_DIM].reshape(h * HEAD_DIM, k)


def _attn_prep(qkv, c, *, name, tm=256):
    t, w = qkv.shape
    nh = w // (3 * HM)

    def body(x_ref, c_ref, o_ref):
        lane = lax.broadcasted_iota(jnp.int32, (1, HM), 1)
        cblk = c_ref[...]
        for h in range(nh):
            ch = cblk[:, h:h + 1]
            pos = _three_parts(ch)
            neg = _three_parts(ch, -1.0)
            for part in range(3):
                col = (part * nh + h) * HM
                x = x_ref[:, col:col + HM].astype(F32)
                if part == 0:
                    y = _fill_lanes(_fill_lanes(x * ATTN_SCALE, lane, LANE_C, pos), lane, LANE_ONE, (1.0, 1.0, 1.0))
                elif part == 1:
                    y = _fill_lanes(_fill_lanes(x, lane, LANE_C, (1.0, 1.0, 1.0)), lane, LANE_ONE, neg)
                    y = _fill_lanes(y, lane, LANE_LSE, (1.0, 1.0, 1.0))
                else:
                    y = _fill_lanes(x, lane, LANE_C, (1.0, 1.0, 1.0))
                o_ref[:, col:col + HM] = y.astype(BF16)

    return pl.pallas_call(
        body, name=name, grid=(t // tm,),
        in_specs=[pl.BlockSpec((tm, w), lambda i: (i, 0)), pl.BlockSpec((tm, LANES), lambda i: (i, 0))],
        out_specs=pl.BlockSpec((tm, w), lambda i: (i, 0)),
        out_shape=jax.ShapeDtypeStruct((t, w), BF16),
        compiler_params=_cparams(("parallel",)),
    )(qkv, c)


def _flash_fwd_hm(qkva, *, name, tq=1024):
    t, w = qkva.shape
    nh = w // (3 * HM)
    nq = t // tq
    nsub = tq // Q_SUB

    def body(q_ref, k_ref, v_ref, o_ref, qb_ref, s_scr, p_scr, m_scr, acc_scr):
        i = pl.program_id(1)
        lane = lax.broadcasted_iota(jnp.int32, (1, HM), 1)
        rr = lax.broadcasted_iota(jnp.int32, (Q_SUB, tq), 0)
        cc = lax.broadcasted_iota(jnp.int32, (Q_SUB, tq), 1)
        sub = lambda r: slice(r * Q_SUB, (r + 1) * Q_SUB)
        nt = (((1,), (1,)), ((), ()))

        def scores(j):
            kj = k_ref[pl.ds(pl.multiple_of(j * tq, tq), tq), :]
            for r in range(nsub):
                s_scr[sub(r), :] = lax.dot_general(q_ref[sub(r), :], kj, nt, preferred_element_type=F32)

        def step(jp, masked, with_pv=True):
            vj = v_ref[pl.ds(pl.multiple_of(jp * tq, tq), tq), :]
            for r in range(nsub):
                if masked:
                    s_scr[sub(r), :] = jnp.where(cc <= rr + r * Q_SUB, s_scr[sub(r), :], NEG_BIG)
                m_old = m_scr[sub(r), :]
                m_new = jnp.maximum(m_old, jnp.max(s_scr[sub(r), :], axis=1, keepdims=True))
                m_scr[sub(r), :] = m_new
                if with_pv:
                    pv = jnp.dot(p_scr[sub(r), :], vj, preferred_element_type=F32)
                    acc_scr[sub(r), :] = (acc_scr[sub(r), :] + pv) * jnp.exp(m_old - m_new)
                p_scr[sub(r), :] = jnp.exp(s_scr[sub(r), :] - m_new).astype(BF16)

        def kv_step(j, carry):
            step(j - 1, False)
            scores(j + 1)
            return carry

        p_scr[...] = jnp.zeros_like(p_scr)
        m_scr[...] = jnp.full_like(m_scr, NEG_BIG)
        acc_scr[...] = jnp.zeros_like(acc_scr)
        scores(0)

        @pl.when(i > 0)
        def _():
            step(0, False, with_pv=False)
            scores(1)

        lax.fori_loop(1, i, kv_step, 0)
        step(jnp.maximum(i - 1, 0), True)
        vi = v_ref[pl.ds(pl.multiple_of(i * tq, tq), tq), :]
        for r in range(nsub):
            acc = acc_scr[sub(r), :] + jnp.dot(p_scr[sub(r), :], vi, preferred_element_type=F32)
            l = jnp.sum(jnp.where(lane == LANE_C, acc, 0.0), axis=1, keepdims=True)
            o_ref[sub(r), :] = jnp.where(lane < HEAD_DIM, acc / l, 0.0).astype(BF16)
            lse = m_scr[sub(r), :] + jnp.log(l)
            qb = _fill_lanes(q_ref[sub(r), :].astype(F32), lane, LANE_LSE, _three_parts(lse, -1.0))
            qb_ref[sub(r), :] = qb.astype(BF16)

    blk = lambda part: pl.BlockSpec((t, HM), lambda h, i: (0, part * nh + h))
    tile = pl.BlockSpec((tq, HM), lambda h, i: (i, h))
    return pl.pallas_call(
        body, name=name, grid=(nh, nq), in_specs=[tile, blk(1), blk(2)], out_specs=[tile, tile],
        out_shape=[jax.ShapeDtypeStruct((t, nh * HM), BF16), jax.ShapeDtypeStruct((t, nh * HM), BF16)],
        scratch_shapes=[pltpu.VMEM((tq, tq), F32), pltpu.VMEM((tq, tq), BF16), pltpu.VMEM((tq, 1), F32),
                        pltpu.VMEM((tq, HM), F32)],
        compiler_params=_cparams(("parallel", "arbitrary")),
    )(qkva, qkva, qkva)


def _attn_prep_bwd(o, do, *, name, tm=512):
    t, w = o.shape
    nh = w // HM

    def body(o_ref, do_ref, out_ref):
        lane = lax.broadcasted_iota(jnp.int32, (1, HM), 1)
        for h in range(nh):
            cols = slice(h * HM, (h + 1) * HM)
            dov = do_ref[:, cols].astype(F32)
            delta = jnp.sum(o_ref[:, cols].astype(F32) * dov, axis=1, keepdims=True)
            out_ref[:, cols] = _fill_lanes(dov, lane, LANE_C, _three_parts(delta, -1.0)).astype(BF16)

    row = pl.BlockSpec((tm, w), lambda i: (i, 0))
    return pl.pallas_call(
        body, name=name, grid=(t // tm,), in_specs=[row, row], out_specs=row,
        out_shape=jax.ShapeDtypeStruct((t, w), BF16), compiler_params=_cparams(("parallel",)),
    )(o, do)


def _flash_bwd_hm(qb, doa, qkva, *, name, tq=512, tqc=512):
    t, w = qb.shape
    nh = w // HM
    nq = t // tq
    nqc = t // tqc
    nsub = tqc // Q_SUB
    grp = slice(LANE_C, LANE_C + SUBLANES)

    def body(q_ref, do_ref, k_ref, v_ref, dq_ref, dk_ref, dv_ref, dcq_ref, dcj_ref,
             dq_acc, st_scr, dpt_scr, pt_scr, ds_scr, dk_scr, dv_scr):
        j = pl.program_id(1)
        kv_minus_q = (lax.broadcasted_iota(jnp.int32, (tq, Q_SUB), 0)
                      - lax.broadcasted_iota(jnp.int32, (tq, Q_SUB), 1))
        sub = lambda r: slice(r * Q_SUB, (r + 1) * Q_SUB)
        nt = (((1,), (1,)), ((), ()))
        tn = (((0,), (0,)), ((), ()))
        first = lax.div(j * tq, tqc)

        @pl.when(j == 0)
        def _():
            dq_acc[...] = jnp.zeros_like(dq_acc)

        def rows_of(i, r):
            return pl.ds(pl.multiple_of(i * tqc + r * Q_SUB, Q_SUB), Q_SUB)

        def scores(i):
            for r in range(nsub):
                st_scr[:, sub(r)] = lax.dot_general(k_ref[...], q_ref[rows_of(i, r), :], nt, preferred_element_type=F32)
                dpt_scr[:, sub(r)] = lax.dot_general(v_ref[...], do_ref[rows_of(i, r), :], nt,
                                                     preferred_element_type=F32)

        def products(ip):
            for r in range(nsub):
                dv_scr[...] += jnp.dot(pt_scr[:, sub(r)], do_ref[rows_of(ip, r), :], preferred_element_type=F32)
                dk_scr[...] += jnp.dot(ds_scr[:, sub(r)], q_ref[rows_of(ip, r), :], preferred_element_type=F32)
                dq_acc[rows_of(ip, r), :] += lax.dot_general(ds_scr[:, sub(r)], k_ref[...], tn,
                                                             preferred_element_type=F32)

        def probabilities(masked):
            for r in range(nsub):
                st = st_scr[:, sub(r)]
                if masked:
                    st = jnp.where(kv_minus_q <= first * tqc + r * Q_SUB - j * tq, st, NEG_BIG)
                pt = jnp.exp(st)
                pt_scr[:, sub(r)] = pt.astype(BF16)
                ds_scr[:, sub(r)] = (pt * dpt_scr[:, sub(r)]).astype(BF16)

        def q_step(i, carry):
            products(i - 1)
            probabilities(False)
            scores(jnp.minimum(i + 1, nqc - 1))
            return carry

        dk_scr[...] = jnp.zeros_like(dk_scr)
        dv_scr[...] = jnp.zeros_like(dv_scr)
        scores(first)
        probabilities(True)
        scores(jnp.minimum(first + 1, nqc - 1))
        lax.fori_loop(first + 1, nqc, q_step, 0)
        products(nqc - 1)
        dk = dk_scr[...]
        dk_ref[...] = dk.astype(BF16)
        dv_ref[...] = dv_scr[...].astype(BF16)
        dcj_ref[...] = dk.T[grp, :]

        @pl.when(j == nq - 1)
        def _():
            dq_ref[...] = (dq_acc[...] * ATTN_SCALE).astype(BF16)
            for cidx in range(nq):
                rows = slice(cidx * tq, (cidx + 1) * tq)
                dcq_ref[:, rows] = dq_acc[rows, :].T[grp, :]

    full = pl.BlockSpec((t, HM), lambda h, j: (0, h))
    ktile = lambda part: pl.BlockSpec((tq, HM), lambda h, j: (j, part * nh + h))
    tile = pl.BlockSpec((tq, HM), lambda h, j: (j, h))
    hm_shape = jax.ShapeDtypeStruct((t, w), BF16)
    row_shape = jax.ShapeDtypeStruct((nh, SUBLANES, t), F32)
    return pl.pallas_call(
        body, name=name, grid=(nh, nq),
        in_specs=[pl.BlockSpec((t, HM), lambda h, j: (0, h), pipeline_mode=pl.Buffered(1)),
                  pl.BlockSpec((t, HM), lambda h, j: (0, h), pipeline_mode=pl.Buffered(1)), ktile(1), ktile(2)],
        out_specs=[full, tile, tile, pl.BlockSpec((None, SUBLANES, t), lambda h, j: (h, 0, 0)),
                   pl.BlockSpec((None, SUBLANES, tq), lambda h, j: (h, 0, j))],
        out_shape=[hm_shape, hm_shape, hm_shape, row_shape, row_shape],
        scratch_shapes=[pltpu.VMEM((t, HM), F32), pltpu.VMEM((tq, tqc), F32), pltpu.VMEM((tq, tqc), F32),
                        pltpu.VMEM((tq, tqc), BF16), pltpu.VMEM((tq, tqc), BF16), pltpu.VMEM((tq, HM), F32),
                        pltpu.VMEM((tq, HM), F32)],
        compiler_params=_cparams(("parallel", "arbitrary")),
    )(qb, doa, qkva, qkva)


def _block_diag_dot(xb, w_ref, transpose_w=False):
    outs = []
    for n in range(N_BLOCKS_B):
        xn = xb[:, n * BLOCK_B:(n + 1) * BLOCK_B]
        dn = (((1,), (1 if transpose_w else 0,)), ((), ()))
        outs.append(lax.dot_general(xn, w_ref[n], dn, preferred_element_type=F32))
    return jnp.concatenate(outs, axis=1)


def _rglru_fwd(proj, conv_w, conv_b, w_a, b_a, w_i, b_i, lam, *, name, tm=256):
    _, t, d = proj.shape
    hb = tm // SUBLANES
    ng = tm // SUBLANES

    def body(p_ref, halo_ref, cw_ref, cb_ref, wa_ref, ba_ref, wi_ref, bi_ref, lam_ref,
             xb_ref, r_ref, ig_ref, a_ref, h_ref, y_ref, u_scr, hc_scr):
        first = pl.program_id(0) == 0

        @pl.when(first)
        def _():
            hc_scr[...] = jnp.zeros_like(hc_scr)

        halo = jnp.where(first, 0.0, halo_ref[0])
        xb = _conv_causal(p_ref[0], halo, cw_ref, cb_ref[...], CONV_B)
        xb_ref[...] = xb
        xbb = xb.astype(BF16)
        r = _sigmoid(_block_diag_dot(xbb, wa_ref) + ba_ref[...])
        ig = _sigmoid(_block_diag_dot(xbb, wi_ref) + bi_ref[...])
        r_ref[...] = r
        ig_ref[...] = ig
        log_a = (-LRU_C) * r * _softplus(-lam_ref[...])
        a_ref[...] = jnp.exp(log_a)
        u_scr[...] = jnp.sqrt(_neg_expm1(2.0 * log_a)) * (ig * xb)

        ones8 = jnp.ones((SUBLANES, d), F32)
        zeros8 = jnp.zeros((SUBLANES, d), F32)

        def group(gi, hprev):
            off = pl.multiple_of(gi * SUBLANES, SUBLANES)
            a8 = a_ref[pl.ds(off, SUBLANES), :]
            u8 = u_scr[pl.ds(off, SUBLANES), :]
            for s in (1, 2, 4):
                u8 = a8 * _shift_down(u8, zeros8, s) + u8
                a8 = a8 * _shift_down(a8, ones8, s)
            h8 = a8 * hprev + u8
            h_ref[pl.ds(off, SUBLANES), :] = h8
            return h8[SUBLANES - 1:SUBLANES, :]

        hc_scr[...] = lax.fori_loop(0, ng, group, hc_scr[...])
        y_ref[...] = (h_ref[...] * _gelu(p_ref[1])).astype(BF16)

    row = pl.BlockSpec((tm, d), lambda i: (i, 0))
    vec = pl.BlockSpec((1, d), lambda i: (0, 0))
    wblk = pl.BlockSpec((N_BLOCKS_B, BLOCK_B, BLOCK_B), lambda i: (0, 0, 0))
    f32_td = jax.ShapeDtypeStruct((t, d), F32)
    return pl.pallas_call(
        body, name=name, grid=(t // tm,),
        in_specs=[pl.BlockSpec((2, tm, d), lambda i: (0, i, 0)),
                  pl.BlockSpec((1, SUBLANES, d), lambda i: (0, jnp.maximum(i * hb - 1, 0), 0)),
                  pl.BlockSpec((CONV_B, d), lambda i: (0, 0)), vec, wblk, vec, wblk, vec, vec],
        out_specs=[row, row, row, row, row, row],
        out_shape=[f32_td, f32_td, f32_td, f32_td, f32_td, jax.ShapeDtypeStruct((t, d), BF16)],
        scratch_shapes=[pltpu.VMEM((tm, d), F32), pltpu.VMEM((1, d), F32)],
        compiler_params=_cparams(("arbitrary",)),
    )(proj, proj, conv_w, conv_b, w_a, b_a, w_i, b_i, lam)


def _rglru_bwd(dy, proj, xb, r, ig, a, h, conv_w, w_a, w_i, lam, *, name, tm=256):
    _, t, d = proj.shape
    hb = tm // SUBLANES
    ng = tm // SUBLANES
    nr = t // tm

    def body(dy_ref, p_ref, phalo_ref, xb_ref, r_ref, ig_ref, a_ref, h_ref, hhalo_ref,
             cw_ref, wa_ref, wi_ref, lam_ref,
             dp_ref, dcw_ref, dcb_ref, dba_ref, dbi_ref, dlam_ref, dwa_ref, dwi_ref,
             g_scr, carry_g, carry_dxb):
        i = pl.program_id(0)
        rpos = nr - 1 - i

        @pl.when(i == 0)
        def _():
            for ref in (dcw_ref, dcb_ref, dba_ref, dbi_ref, dlam_ref, dwa_ref, dwi_ref, carry_g, carry_dxb):
                ref[...] = jnp.zeros_like(ref)

        gb = p_ref[1]
        gel, dgel = _gelu_and_grad(gb)
        dyv = dy_ref[...]
        hv = h_ref[...]
        dp_ref[1] = (dyv * hv * dgel).astype(BF16)
        g_scr[...] = dyv * gel
        av = a_ref[...]

        ones8 = jnp.ones((SUBLANES, d), F32)
        zeros8 = jnp.zeros((SUBLANES, d), F32)

        def group(gi, cin):
            off = pl.multiple_of((ng - 1 - gi) * SUBLANES, SUBLANES)
            g8 = g_scr[pl.ds(off, SUBLANES), :]
            a8 = a_ref[pl.ds(off, SUBLANES), :]
            b8 = _shift_up(a8, ones8, 1)
            row8 = lax.broadcasted_iota(jnp.int32, (SUBLANES, d), 0)
            g8 = g8 + jnp.where(row8 == SUBLANES - 1, cin, 0.0)
            b8 = jnp.where(row8 == SUBLANES - 1, 0.0, b8)
            for s in (1, 2, 4):
                g8 = g8 + b8 * _shift_up(g8, zeros8, s)
                b8 = b8 * _shift_up(b8, zeros8, s)
            g_scr[pl.ds(off, SUBLANES), :] = g8
            return a8[0:1, :] * g8[0:1, :]

        carry_g[...] = lax.fori_loop(0, ng, group, carry_g[...])

        du = g_scr[...]
        hhalo = jnp.where(rpos == 0, 0.0, hhalo_ref[...])
        hprev = _shift_down(hv, hhalo, 1)
        da = du * hprev
        rv = r_ref[...]
        igv = ig_ref[...]
        xbv = xb_ref[...]
        sp = _softplus(-lam_ref[...])
        log_a = (-LRU_C) * rv * sp
        mult = jnp.sqrt(_neg_expm1(2.0 * log_a))
        dmult = du * (igv * xbv)
        dig = du * mult * xbv
        dxb = du * mult * igv
        dlog_a = da * av - dmult * (av * av) / mult
        dr = dlog_a * ((-LRU_C) * sp)
        dsp = jnp.sum(dlog_a * ((-LRU_C) * rv), axis=0, keepdims=True)
        dlam_ref[...] += dsp * (-_sigmoid(-lam_ref[...]))
        dra = dr * rv * (1.0 - rv)
        dia = dig * igv * (1.0 - igv)
        dba_ref[...] += jnp.sum(dra, axis=0, keepdims=True)
        dbi_ref[...] += jnp.sum(dia, axis=0, keepdims=True)
        drab = dra.astype(BF16)
        diab = dia.astype(BF16)
        xbb = xbv.astype(BF16)
        dxb = dxb + _block_diag_dot(drab, wa_ref, True) + _block_diag_dot(diab, wi_ref, True)
        tn = (((0,), (0,)), ((), ()))
        for n in range(N_BLOCKS_B):
            sl = slice(n * BLOCK_B, (n + 1) * BLOCK_B)
            dwa_ref[n] += lax.dot_general(xbb[:, sl], drab[:, sl], tn, preferred_element_type=F32)
            dwi_ref[n] += lax.dot_general(xbb[:, sl], diab[:, sl], tn, preferred_element_type=F32)

        xpre = p_ref[0]
        phalo = jnp.where(rpos == 0, 0.0, phalo_ref[0])
        nxt = carry_dxb[...]
        dcb_ref[...] += jnp.sum(dxb, axis=0, keepdims=True)
        out = dxb * cw_ref[CONV_B - 1:CONV_B, :]
        dcw_ref[CONV_B - 1:CONV_B, :] += jnp.sum(dxb * xpre, axis=0, keepdims=True)
        for k in range(CONV_B - 1):
            sh = CONV_B - 1 - k
            out = out + _shift_up(dxb, nxt, sh) * cw_ref[k:k + 1, :]
            dcw_ref[k:k + 1, :] += jnp.sum(dxb * _shift_down(xpre, phalo, sh), axis=0, keepdims=True)
        dp_ref[0] = out.astype(BF16)
        carry_dxb[...] = dxb[:SUBLANES]

    rev = pl.BlockSpec((tm, d), lambda i: (nr - 1 - i, 0))
    halo8 = pl.BlockSpec((SUBLANES, d), lambda i: (jnp.maximum((nr - 1 - i) * hb - 1, 0), 0))
    vec = pl.BlockSpec((1, d), lambda i: (0, 0))
    wblk = pl.BlockSpec((N_BLOCKS_B, BLOCK_B, BLOCK_B), lambda i: (0, 0, 0))
    vec_shape = jax.ShapeDtypeStruct((1, d), F32)
    w_shape = jax.ShapeDtypeStruct((N_BLOCKS_B, BLOCK_B, BLOCK_B), F32)
    return pl.pallas_call(
        body, name=name, grid=(nr,),
        in_specs=[rev, pl.BlockSpec((2, tm, d), lambda i: (0, nr - 1 - i, 0)),
                  pl.BlockSpec((1, SUBLANES, d), lambda i: (0, jnp.maximum((nr - 1 - i) * hb - 1, 0), 0)),
                  rev, rev, rev, rev, rev, halo8,
                  pl.BlockSpec((CONV_B, d), lambda i: (0, 0)), wblk, wblk, vec],
        out_specs=[pl.BlockSpec((2, tm, d), lambda i: (0, nr - 1 - i, 0)),
                   pl.BlockSpec((CONV_B, d), lambda i: (0, 0)), vec, vec, vec, vec, wblk, wblk],
        out_shape=[jax.ShapeDtypeStruct((2, t, d), BF16), jax.ShapeDtypeStruct((CONV_B, d), F32),
                   vec_shape, vec_shape, vec_shape, vec_shape, w_shape, w_shape],
        scratch_shapes=[pltpu.VMEM((tm, d), F32), pltpu.VMEM((1, d), F32), pltpu.VMEM((SUBLANES, d), F32)],
        compiler_params=_cparams(("arbitrary",)),
    )(dy, proj, proj, xb, r, ig, a, h, h, conv_w, w_a, w_i, lam)


def _split_cols(w, s):
    k, c = w.shape[-2:]
    return jnp.moveaxis(w.reshape(w.shape[:-2] + (k, s, c // s)), -2, -3)


def _merge_cols(w):
    s, k, c = w.shape
    return jnp.moveaxis(w, 0, 1).reshape(k, s * c)


def _local_step(x, p, tgt, w):
    t = x.shape[0]
    bf = lambda v: v.astype(BF16)
    saved = []
    xcur = x
    xcur_b = bf(x)
    for i in range(DEPTH):
        j = i // 2
        L = f"l{i}_"
        sv = {'x_in_b': xcur_b}
        if i % 2 == 0:
            w_in_t = bf(w['a_w_in'][j])
            sv['w_qkv_t'] = jnp.concatenate(
                [_hm_weight_t(w_in_t[part * D_MODEL:(part + 1) * D_MODEL]) for part in range(3)])
            sv['w_f'] = jnp.pad(w_in_t[3 * D_MODEL:].T, ((0, 0), (0, LANES - N_HEADS)))
            sv['w_out'] = _hm_weight_t(bf(w['a_w_out'][j]))
            b_f = jnp.pad(w['a_b_f'][j], (0, LANES - N_HEADS)).reshape(1, LANES)
            qkv = _mm(xcur_b, sv['w_qkv_t'], tb=True, out_dtype=BF16, name=L + "qkv")
            z, c = _fgate_fwd(xcur_b, sv['w_f'], b_f, name=L + "fgate")
            qkva = _attn_prep(qkv, c, name=L + "attn_prep")
            o, qb = _flash_fwd_hm(qkva, name=L + "flash_fwd")
            mix = _mm(o, sv['w_out'], name=L + "attn_out")
            sv.update(qkva=qkva, z=z, o=o, qb=qb)
        else:
            sv['w_in'] = bf(w['b_w_in'][j])
            sv['w_out'] = bf(w['b_w_out'][j])
            sv['conv_w'] = w['b_conv_w'][j]
            sv['w_a'] = bf(w['b_w_a'][j])
            sv['w_i'] = bf(w['b_w_i'][j])
            sv['lam'] = w['b_lam'][j].reshape(1, D_MODEL)
            proj = _mm(xcur_b, sv['w_in'], out_split=2, name=L + "rg_in")
            xb, r, ig, a, h, y = _rglru_fwd(
                proj, sv['conv_w'], w['b_conv_b'][j].reshape(1, D_MODEL), sv['w_a'],
                w['b_b_a'][j].reshape(1, D_MODEL), sv['w_i'], w['b_b_i'][j].reshape(1, D_MODEL), sv['lam'],
                name=L + "rglru_fwd")
            mix = _mm(y, sv['w_out'], name=L + "rg_out")
            sv.update(proj=proj, xb=xb, r=r, ig=ig, a=a, h=h, y=y)
        sv['ln1_g'] = w['ln1_g'][i].reshape(1, D_MODEL)
        x1, x1b, xhat1, rstd1 = _ln_fwd(xcur, mix, sv['ln1_g'], w['ln1_b'][i].reshape(1, D_MODEL), name=L + "ln1")
        sv['w_up'] = bf(w['f_w_up'][i])
        sv['w_down'] = bf(w['f_w_down'][i])
        sv['fconv_w'] = _split_cols(w['f_conv_w'][i], 2)
        sv['fconv_b'] = w['f_conv_b'][i].reshape(2, 1, D_FF)
        hdn = _mm(x1b, sv['w_up'], out_split=2, tn_cap=1408, name=L + "ffn_up")
        act = _ffn_act_fwd(hdn, sv['fconv_w'], sv['fconv_b'], name=L + "ffn_act")
        ff = _mm(act, sv['w_down'], tk_cap=2816, name=L + "ffn_down")
        sv['ln2_g'] = w['ln2_g'][i].reshape(1, D_MODEL)
        x2, x2b, xhat2, rstd2 = _ln_fwd(x1, ff, sv['ln2_g'], w['ln2_b'][i].reshape(1, D_MODEL), name=L + "ln2")
        sv['gate_w'] = bf(w['ple_gate_w'][i])
        sv['ple_w'] = bf(w['ple_w'][i])
        sv['gate_b'] = w['ple_gate_b'][i].reshape(1, D_MODEL)
        sv['p_b'] = bf(p[i])
        gl = _mm(x2b, sv['gate_w'], name=L + "ple_gate")
        pe = _mm(sv['p_b'], sv['ple_w'], name=L + "ple_emb")
        x3, x3b = _ple_fwd(x2, gl, pe, sv['gate_b'], name=L + "ple")
        sv.update(xhat1=xhat1, rstd1=rstd1, x1b=x1b, hdn=hdn, act=act, xhat2=xhat2, rstd2=rstd2, x2b=x2b,
                  gl=gl, pe=pe)
        saved.append(sv)
        xcur, xcur_b = x3, x3b

    dx, loss_row = _loss_bwd(xcur, tgt, name="loss")

    g = {n: [None] * w[n].shape[0] for n in WEIGHTS}
    for i in reversed(range(DEPTH)):
        j = i // 2
        L = f"l{i}b_"
        sv = saved[i]
        dgl, dpe, d_gate_b = _ple_bwd(dx, sv['gl'], sv['pe'], sv['gate_b'], name=L + "ple")
        g['ple_gate_b'][i] = d_gate_b[0]
        g['ple_w'][i] = _mm(sv['p_b'], dpe, ta=True, name=L + "ple_emb_dw")
        g['ple_gate_w'][i] = _mm(sv['x2b'], dgl, ta=True, name=L + "ple_gate_dw")
        dx2 = _mm(dgl, sv['gate_w'], tb=True, add=dx, name=L + "ple_gate_dx")
        dz2, dz2b, dg2, db2 = _ln_bwd(dx2, sv['xhat2'], sv['rstd2'], sv['ln2_g'], name=L + "ln2")
        g['ln2_g'][i], g['ln2_b'][i] = dg2[0], db2[0]
        g['f_w_down'][i] = _mm(sv['act'], dz2b, ta=True, tm_cap=1408, name=L + "ffn_down_dw")
        da = _mm(dz2b, sv['w_down'], tb=True, tn_cap=1408, name=L + "ffn_down_dx")
        dhdn, d_fcw, d_fcb = _ffn_act_bwd(da, sv['hdn'], sv['fconv_w'], sv['fconv_b'], name=L + "ffn_act")
        g['f_conv_w'][i] = _merge_cols(d_fcw)
        g['f_conv_b'][i] = d_fcb.reshape(2 * D_FF)
        g['f_w_up'][i] = _mm(sv['x1b'], dhdn, ta=True, tn_cap=1408, name=L + "ffn_up_dw")
        dx1 = _mm(dhdn, sv['w_up'], tb=True, add=dz2, add_scale=ALPHA, name=L + "ffn_up_dx")
        dz1, dz1b, dg1, db1 = _ln_bwd(dx1, sv['xhat1'], sv['rstd1'], sv['ln1_g'], name=L + "ln1")
        g['ln1_g'][i], g['ln1_b'][i] = dg1[0], db1[0]
        if i % 2 == 0:
            g['a_w_out'][j] = _hm_unpad_t(_mm(sv['o'], dz1b, ta=True, name=L + "attn_out_dw"))
            do = _mm(dz1b, sv['w_out'], tb=True, out_dtype=BF16, name=L + "attn_out_dx")
            doa = _attn_prep_bwd(sv['o'], do, name=L + "attn_prep")
            dqkv = _flash_bwd_hm(sv['qb'], doa, sv['qkva'], name=L + "flash_bwd")
            dcq, dcj = dqkv[3], dqkv[4]
            dc = jnp.pad((dcq[:, 0, :] - dcj[:, LANE_ONE - LANE_C, :]).T, ((0, 0), (0, LANES - N_HEADS)))
            dzf, d_b_f = _fgate_bwd(dc, sv['z'], name=L + "fgate")
            g['a_b_f'][j] = d_b_f[0, :N_HEADS]
            xb_in = sv['x_in_b']
            d_w = [_hm_unpad_t(_mm(dqkv[part], xb_in, ta=True, name=L + "qkv"[part] + "_dw")) for part in range(3)]
            d_wf = _mm(xb_in, dzf, ta=True, name=L + "f_dw")
            g['a_w_in'][j] = jnp.concatenate(d_w + [d_wf[:, :N_HEADS].T], axis=0)
            dxa, scale = dz1, ALPHA
            for part in range(3):
                w_part = sv['w_qkv_t'][part * N_HEADS * HM:(part + 1) * N_HEADS * HM]
                dxa = _mm(dqkv[part], w_part, add=dxa, add_scale=scale, name=L + "qkv"[part] + "_dx")
                scale = 1.0
            dx = _mm(dzf, sv['w_f'], tb=True, add=dxa, name=L + "f_dx")
        else:
            g['b_w_out'][j] = _mm(sv['y'], dz1b, ta=True, name=L + "rg_out_dw")
            dy = _mm(dz1b, sv['w_out'], tb=True, name=L + "rg_out_dx")
            dproj, d_cw, d_cb, d_ba, d_bi, d_lam, d_wa, d_wi = _rglru_bwd(
                dy, sv['proj'], sv['xb'], sv['r'], sv['ig'], sv['a'], sv['h'], sv['conv_w'], sv['w_a'], sv['w_i'],
                sv['lam'], name=L + "rglru_bwd")
            g['b_conv_w'][j], g['b_conv_b'][j] = d_cw, d_cb[0]
            g['b_b_a'][j] = d_ba.reshape(N_BLOCKS_B, BLOCK_B)
            g['b_b_i'][j] = d_bi.reshape(N_BLOCKS_B, BLOCK_B)
            g['b_lam'][j] = d_lam[0]
            g['b_w_a'][j], g['b_w_i'][j] = d_wa, d_wi
            g['b_w_in'][j] = _mm(sv['x_in_b'], dproj, ta=True, name=L + "rg_in_dw")
            dx = _mm(dproj, sv['w_in'], tb=True, add=dz1, add_scale=ALPHA, name=L + "rg_in_dx")
    return loss_row, dx, g


def _round_up(n, q):
    return -(-n // q) * q


def _pack(arrs, dtype, row_multiple, lead=0):
    pieces = []
    for a in arrs:
        flat = a.reshape(a.shape[:lead] + (-1,)).astype(dtype)
        n = flat.shape[-1]
        pieces.append(jnp.pad(flat, [(0, 0)] * lead + [(0, _round_up(n, LANES) - n)]))
    flat = jnp.concatenate(pieces, axis=-1)
    rows = _round_up(flat.shape[-1] // LANES, row_multiple)
    flat = jnp.pad(flat, [(0, 0)] * lead + [(0, rows * LANES - flat.shape[-1])])
    return flat.reshape(flat.shape[:lead] + (rows, LANES))


def _unpack(buf, shapes):
    lead = buf.shape[:-2]
    flat = buf.reshape(lead + (-1,))
    out, off = [], 0
    for shp in shapes:
        n = math.prod(shp)
        out.append(flat[..., off:off + n].reshape(lead + tuple(shp)))
        off += _round_up(n, LANES)
    return out


MESH = pl.DeviceIdType.MESH
ANY = pl.BlockSpec(memory_space=pl.ANY)
N_CHUNK = 8


def _all_gather_shards(buf):
    rows, lanes = buf.shape
    half = rows // 2
    ch = half // N_CHUNK
    n_ici = 3 * N_CHUNK

    def body(in_ref, out_ref, send_sems, recv_sems):
        x, y, c = lax.axis_index("x"), lax.axis_index("y"), lax.axis_index("c")
        sibling = (x, y, 1 - c)
        chips = [(1 - x, y), (x, 1 - y), (1 - x, 1 - y)]

        def piece(cx, cy, hc, q):
            return out_ref.at[2 * cx + cy, pl.ds(hc * half + q * ch, ch), :]

        def copy(k, src, dst, to):
            return pltpu.make_async_remote_copy(src_ref=src, dst_ref=dst, send_sem=send_sems.at[k],
                                                recv_sem=recv_sems.at[k], device_id=to, device_id_type=MESH)

        my_chunk = lambda q: in_ref.at[pl.ds(c * half + q * ch, ch), :]
        first, passed = [], []
        for k, chip in enumerate(chips):
            for q in range(N_CHUNK):
                first.append(copy(k * N_CHUNK + q, my_chunk(q), piece(x, y, c, q), (*chip, c)))
                passed.append(copy(n_ici + k * N_CHUNK + q, piece(*chip, c, q), piece(*chip, c, q), sibling))
        for cp in first:
            cp.start()
        for k, chip in enumerate(chips):
            for q in range(N_CHUNK):
                n = k * N_CHUNK + q
                copy(n, my_chunk(q), piece(*chip, c, q), (*chip, c)).wait_recv()
                passed[n].start()
        for k, chip in enumerate(chips):
            for q in range(N_CHUNK):
                copy(n_ici + k * N_CHUNK + q, my_chunk(q), piece(*chip, 1 - c, q), sibling).wait_recv()
        for cp in first + passed:
            cp.wait_send()

    others = pl.pallas_call(
        body, name="gather_weights", in_specs=[ANY], out_specs=ANY,
        out_shape=jax.ShapeDtypeStruct((N_CHIP, rows, lanes), buf.dtype),
        scratch_shapes=[pltpu.SemaphoreType.DMA((2 * n_ici,)), pltpu.SemaphoreType.DMA((2 * n_ici,))],
    )(buf)
    return lax.dynamic_update_slice(others, buf[None], (2 * lax.axis_index("x") + lax.axis_index("y"), 0, 0))


def _exchange_pieces(pieces):
    n, prow, lanes = pieces.shape

    def body(in_ref, out_ref, send_sems, recv_sems, local_sem):
        x, y, c = lax.axis_index("x"), lax.axis_index("y"), lax.axis_index("c")
        me = 4 * x + 2 * y + c
        mine = pltpu.make_async_copy(in_ref.at[me], out_ref.at[me], local_sem)
        mine.start()
        copies = []
        for k in range(1, N_DEV):
            px, py, pc = x ^ (k >> 2), y ^ ((k >> 1) & 1), c ^ (k & 1)
            peer = 4 * px + 2 * py + pc
            cp = pltpu.make_async_remote_copy(src_ref=in_ref.at[peer], dst_ref=out_ref.at[me],
                                              send_sem=send_sems.at[k - 1], recv_sem=recv_sems.at[k - 1],
                                              device_id=(px, py, pc), device_id_type=MESH)
            cp.start()
            copies.append(cp)
        for cp in copies:
            cp.wait()
        mine.wait()

    return pl.pallas_call(
        body, name="exchange_grads", in_specs=[ANY], out_specs=ANY,
        out_shape=jax.ShapeDtypeStruct((n, prow, lanes), pieces.dtype),
        scratch_shapes=[pltpu.SemaphoreType.DMA((N_DEV - 1,)), pltpu.SemaphoreType.DMA((N_DEV - 1,)),
                        pltpu.SemaphoreType.DMA],
    )(pieces)


def _exchange_many(arrs):
    na = len(arrs)

    def body(*refs):
        ins, outs = refs[:na], refs[na:2 * na]
        send_sems, recv_sems = refs[2 * na:]
        x, y, c = lax.axis_index("x"), lax.axis_index("y"), lax.axis_index("c")
        me = 4 * x + 2 * y + c
        copies = []
        for k in range(1, N_DEV):
            px, py, pc = x ^ (k >> 2), y ^ ((k >> 1) & 1), c ^ (k & 1)
            peer = 4 * px + 2 * py + pc
            for a in range(na):
                n = (k - 1) * na + a
                copies.append(pltpu.make_async_remote_copy(
                    src_ref=ins[a].at[peer], dst_ref=outs[a].at[me], send_sem=send_sems.at[n],
                    recv_sem=recv_sems.at[n], device_id=(px, py, pc), device_id_type=MESH))
        for cp in copies:
            cp.start()
        for cp in copies:
            cp.wait()

    n_remote = (N_DEV - 1) * na
    outs = pl.pallas_call(
        body, name="exchange_grads", in_specs=[ANY] * na, out_specs=[ANY] * na,
        out_shape=[jax.ShapeDtypeStruct(a.shape, a.dtype) for a in arrs],
        scratch_shapes=[pltpu.SemaphoreType.DMA((n_remote,)), pltpu.SemaphoreType.DMA((n_remote,))],
    )(*arrs)
    me = 4 * lax.axis_index("x") + 2 * lax.axis_index("y") + lax.axis_index("c")
    return [lax.dynamic_update_slice(o, lax.dynamic_index_in_dim(a, me, 0, keepdims=True), (me, 0, 0))
            for o, a in zip(outs, arrs)]


def _share_many(reds, halves):
    na = len(reds)

    def body(*refs):
        ins, outs = refs[:na], refs[na:2 * na]
        send_sems, recv_sems = refs[2 * na:]
        x, y, c = lax.axis_index("x"), lax.axis_index("y"), lax.axis_index("c")
        copies = []
        for a in range(na):
            h = halves[a]
            copies.append(pltpu.make_async_remote_copy(
                src_ref=ins[a].at[pl.ds(0, h), :], dst_ref=outs[a].at[pl.ds(c * h, h), :], send_sem=send_sems.at[a],
                recv_sem=recv_sems.at[a], device_id=(x, y, 1 - c), device_id_type=MESH))
        for cp in copies:
            cp.start()
        for cp in copies:
            cp.wait()

    outs = pl.pallas_call(
        body, name="share_halves", in_specs=[ANY] * na, out_specs=[ANY] * na,
        out_shape=[jax.ShapeDtypeStruct((r.shape[0] + h, r.shape[1]), r.dtype) for r, h in zip(reds, halves)],
        scratch_shapes=[pltpu.SemaphoreType.DMA((na,)), pltpu.SemaphoreType.DMA((na,))],
    )(*reds)
    c = lax.axis_index("c")
    full = []
    for o, r, h in zip(outs, reds, halves):
        o = lax.dynamic_update_slice(o, r[:h], (c * h, 0))
        if r.shape[0] > h:
            o = lax.dynamic_update_slice(o, r[h:], (2 * h, 0))
        full.append(o)
    return full


SUM_BLOCK_ELEMS = 256 * 1024
ADAM_BLOCK_ELEMS = 256 * 1024


def _sum_slots(slots, *, name="sum_grads"):
    n, prow, lanes = slots.shape
    tm = _tile(prow, max(2 * SUBLANES, SUM_BLOCK_ELEMS // lanes), 2 * SUBLANES)

    def body(s_ref, o_ref):
        acc = s_ref[0].astype(F32)
        for s in range(1, n):
            acc = acc + s_ref[s].astype(F32)
        o_ref[...] = acc

    return pl.pallas_call(
        body, name=name, grid=(prow // tm,),
        in_specs=[pl.BlockSpec((n, tm, lanes), lambda i: (0, i, 0))],
        out_specs=pl.BlockSpec((tm, lanes), lambda i: (i, 0)),
        out_shape=jax.ShapeDtypeStruct((prow, lanes), F32),
        compiler_params=_cparams(("parallel",)),
    )(slots)


def _share_halves(red, half_rows):
    prow, lanes = red.shape
    rep_rows = prow - half_rows
    ch = half_rows // N_CHUNK

    def body(in_ref, out_ref, send_sems, recv_sems, local_sems):
        x, y, c = lax.axis_index("x"), lax.axis_index("y"), lax.axis_index("c")
        src = lambda q: in_ref.at[pl.ds(q * ch, ch), :]
        dst = lambda q: out_ref.at[pl.ds(c * half_rows + q * ch, ch), :]
        local = [pltpu.make_async_copy(src(q), dst(q), local_sems.at[q]) for q in range(N_CHUNK)]
        local.append(pltpu.make_async_copy(in_ref.at[pl.ds(half_rows, rep_rows), :],
                                           out_ref.at[pl.ds(2 * half_rows, rep_rows), :], local_sems.at[N_CHUNK]))
        remote = [pltpu.make_async_remote_copy(src_ref=src(q), dst_ref=dst(q), send_sem=send_sems.at[q],
                                               recv_sem=recv_sems.at[q], device_id=(x, y, 1 - c), device_id_type=MESH)
                  for q in range(N_CHUNK)]
        for cp in remote + local:
            cp.start()
        for cp in remote + local:
            cp.wait()

    return pl.pallas_call(
        body, name="share_halves", in_specs=[ANY], out_specs=ANY,
        out_shape=jax.ShapeDtypeStruct((2 * half_rows + rep_rows, lanes), red.dtype),
        scratch_shapes=[pltpu.SemaphoreType.DMA((N_CHUNK,)), pltpu.SemaphoreType.DMA((N_CHUNK,)),
                        pltpu.SemaphoreType.DMA((N_CHUNK + 1,))],
    )(red)


def _adamw(wp, gp, mp, vp, *, name="adamw"):
    rows, lanes = wp.shape
    tm = _tile(rows, max(SUBLANES, ADAM_BLOCK_ELEMS // lanes), SUBLANES)
    c1 = 1.0 / (1.0 - ADAM_B1 ** ADAM_STEP)
    c2 = 1.0 / (1.0 - ADAM_B2 ** ADAM_STEP)

    def body(w_ref, g_ref, m_ref, v_ref, d_ref, nm_ref, nv_ref):
        g = g_ref[...]
        m = ADAM_B1 * m_ref[...] + (1.0 - ADAM_B1) * g
        v = ADAM_B2 * v_ref[...] + (1.0 - ADAM_B2) * (g * g)
        m_hat = m * c1
        v_hat = v * c2
        d_ref[...] = -ADAM_LR * (m_hat / (jnp.sqrt(v_hat) + ADAM_EPS) + ADAM_WD * w_ref[...])
        nm_ref[...] = m
        nv_ref[...] = v

    spec = pl.BlockSpec((tm, lanes), lambda i: (i, 0))
    shp = jax.ShapeDtypeStruct((rows, lanes), F32)
    return pl.pallas_call(
        body, name=name, grid=(rows // tm,), in_specs=[spec] * 4, out_specs=[spec] * 3,
        out_shape=[shp, shp, shp], compiler_params=_cparams(("parallel",)),
    )(wp, gp, mp, vp)


AG_ROW_MULT = 2 * N_CHUNK * 16
GRAD_ROW_MULT = 2048
REP_ROW_MULT = 512


def _shard_to_front(a, axis):
    n = a.shape[axis]
    a = a.reshape(a.shape[:axis] + (N_CHIP, n // N_CHIP) + a.shape[axis + 1:])
    return jnp.moveaxis(a, axis, 0)


def _shards_to_full(a, axis):
    a = jnp.moveaxis(a, 0, axis)
    return a.reshape(a.shape[:axis] + (a.shape[axis] * a.shape[axis + 1],) + a.shape[axis + 2:])


def kernel(x, p, a_w_in, a_b_f, a_w_out, b_w_in, b_conv_w, b_conv_b, b_w_a, b_b_a, b_w_i, b_b_i, b_lam, b_w_out, f_w_up, f_conv_w, f_conv_b, f_w_down, ln1_g, ln1_b, ln2_g, ln2_b, ple_w, ple_gate_w, ple_gate_b, loss_target, m_a_w_in, m_a_b_f, m_a_w_out, m_b_w_in, m_b_conv_w, m_b_conv_b, m_b_w_a, m_b_b_a, m_b_w_i, m_b_b_i, m_b_lam, m_b_w_out, m_f_w_up, m_f_conv_w, m_f_conv_b, m_f_w_down, m_ln1_g, m_ln1_b, m_ln2_g, m_ln2_b, m_ple_w, m_ple_gate_w, m_ple_gate_b, v_a_w_in, v_a_b_f, v_a_w_out, v_b_w_in, v_b_conv_w, v_b_conv_b, v_b_w_a, v_b_b_a, v_b_w_i, v_b_b_i, v_b_lam, v_b_w_out, v_f_w_up, v_f_conv_w, v_f_conv_b, v_f_w_down, v_ln1_g, v_ln1_b, v_ln2_g, v_ln2_b, v_ple_w, v_ple_gate_w, v_ple_gate_b):
    args = dict(locals())
    swap = lambda n, a: jnp.swapaxes(a, -1, -2) if n in TRANSPOSED else a
    axis_of = lambda n: (3 - SHARD_AXIS[n]) if n in TRANSPOSED else SHARD_AXIS[n]
    local_w = {n: swap(n, args[n]) for n in WEIGHTS}
    local_m = {n: swap(n, args['m_' + n]) for n in WEIGHTS}
    local_v = {n: swap(n, args['v_' + n]) for n in WEIGHTS}

    as_pairs = lambda a: lax.bitcast_convert_type(a, BF16)
    from_pairs = lambda a: lax.bitcast_convert_type(a, F32)
    send = [local_w[n] for n in GATHER_BF16] + [as_pairs(local_w[n]) for n in GATHER_F32]
    gathered = _all_gather_shards(_pack(send, BF16, AG_ROW_MULT))
    shapes = [local_w[n].shape for n in GATHER_BF16] + [local_w[n].shape + (2,) for n in GATHER_F32]
    parts = _unpack(gathered, shapes)
    full_w = {}
    for n, part in zip(GATHER_BF16 + GATHER_F32, parts):
        if n in GATHER_F32:
            part = from_pairs(part)
        full_w[n] = _shards_to_full(part, axis_of(n))
    for n in REPLICATED:
        full_w[n] = local_w[n]

    loss_row, grad_x, g = _local_step(x[0], p[:, 0], loss_target[0], full_w)

    grads = {n: jnp.stack(g[n]) for n in SHARDED}
    small = [n for n in SHARDED if n not in NATIVE]
    shard_rows = _round_up(sum(_round_up(local_w[n].size, LANES) for n in small) // LANES, GRAD_ROW_MULT)
    half_rows = shard_rows // 2

    def native_pieces(n):
        a = _shard_to_front(grads[n], axis_of(n))
        return a.reshape(N_DEV, -1, a.shape[-1]).astype(BF16)

    sharded_g = _pack([_shard_to_front(grads[n], axis_of(n)) for n in small], BF16, GRAD_ROW_MULT, lead=1)
    rep_g = _pack([jnp.stack(g[n]) for n in REPLICATED] + [loss_row], F32, REP_ROW_MULT)
    rep_rows = rep_g.shape[0]
    rep_top = lax.reduce_precision(rep_g, 8, 7)
    rep_hi = rep_top.astype(BF16)
    rep_lo = (rep_g - rep_top).astype(BF16)
    packed_pieces = jnp.concatenate(
        [sharded_g.reshape(N_DEV, half_rows, LANES),
         jnp.broadcast_to(jnp.concatenate([rep_hi, rep_lo]), (N_DEV, 2 * rep_rows, LANES))], axis=1)
    slots = _exchange_many([native_pieces(n) for n in NATIVE] + [packed_pieces])
    reduced = [_sum_slots(s, name="sum_" + n) for s, n in zip(slots, NATIVE + ["packed"])]
    reduced[-1] = jnp.concatenate([reduced[-1][:half_rows], reduced[-1][half_rows:half_rows + rep_rows]
                                   + reduced[-1][half_rows + rep_rows:]])
    full_g = _share_many(reduced, [r.shape[0] for r in reduced[:-1]] + [half_rows])

    rep_shapes = [local_w[n].shape for n in REPLICATED] + [loss_row.shape]

    def packed(d):
        return jnp.concatenate([_pack([d[n] for n in small], F32, GRAD_ROW_MULT),
                                _pack([d[n] for n in REPLICATED] + [jnp.zeros_like(loss_row)], F32, REP_ROW_MULT)])

    def unpacked(buf):
        d = dict(zip(small, _unpack(buf[:shard_rows], [local_w[n].shape for n in small])))
        d.update(zip(REPLICATED, _unpack(buf[shard_rows:], rep_shapes)))
        return d

    outs = [unpacked(b) for b in (full_g[-1],) + tuple(_adamw(packed(local_w), full_g[-1], packed(local_m),
                                                              packed(local_v), name="adamw_packed"))]
    for n, gn in zip(NATIVE, full_g):
        two_d = lambda a: a.reshape(gn.shape)
        res = _adamw(two_d(local_w[n]), gn, two_d(local_m[n]), two_d(local_v[n]), name="adamw_" + n)
        for d, a in zip(outs, (gn,) + tuple(res)):
            d[n] = a.reshape(local_w[n].shape)
    loss = _unpack(full_g[-1][shard_rows:], rep_shapes)[-1][0, 0]
    return (loss, grad_x[None], *[swap(n, d[n]) for d in outs for n in WEIGHTS])
```

```python
import math

import jax
import jax.numpy as jnp
from jax import lax
from jax.experimental import pallas as pl
from jax.experimental.pallas import tpu as pltpu

F32 = jnp.float32
BF16 = jnp.bfloat16

D_MODEL = 1024
DEPTH = 4
N_HEADS = 16
HEAD_DIM = 64
N_BLOCKS_B = 8
BLOCK_B = 128
CONV_B = 4
LRU_C = 8.0
D_FF = 2816
CONV_F = 3
D_PLE = 256
LN_EPS = 1e-5
ALPHA = (2.0 * DEPTH) ** 0.25
ATTN_SCALE = 1.0 / math.sqrt(HEAD_DIM)

ADAM_LR = 0.001
ADAM_B1 = 0.9
ADAM_B2 = 0.999
ADAM_EPS = 1e-08
ADAM_WD = 0.01
ADAM_STEP = 10

LANES = 128
SUBLANES = 8
VMEM_LIMIT = 52 * 1024 * 1024
MM_VMEM_BUDGET = 40 * 1024 * 1024
NEG_BIG = -1e30
N_DEV = 8
N_CHIP = 4

WEIGHTS = ['a_w_in', 'a_b_f', 'a_w_out', 'b_w_in', 'b_conv_w', 'b_conv_b', 'b_w_a', 'b_b_a', 'b_w_i', 'b_b_i',
           'b_lam', 'b_w_out', 'f_w_up', 'f_conv_w', 'f_conv_b', 'f_w_down', 'ln1_g', 'ln1_b', 'ln2_g', 'ln2_b',
           'ple_w', 'ple_gate_w', 'ple_gate_b']
SHARD_AXIS = {'a_w_in': 2, 'a_w_out': 1, 'b_w_in': 2, 'b_conv_w': 2, 'b_conv_b': 1, 'b_lam': 1, 'b_w_out': 1,
              'f_w_up': 2, 'f_conv_w': 2, 'f_w_down': 1, 'ple_w': 2, 'ple_gate_w': 1}
TRANSPOSED = ('a_w_in',)
NATIVE = ['a_w_out', 'b_w_in', 'b_w_out', 'f_w_up', 'f_w_down', 'ple_w', 'ple_gate_w']
SHARDED = [n for n in WEIGHTS if n in SHARD_AXIS]
REPLICATED = [n for n in WEIGHTS if n not in SHARD_AXIS]
GATHER_BF16 = ['a_w_in', 'a_w_out', 'b_w_in', 'b_w_out', 'f_w_up', 'f_w_down', 'ple_w', 'ple_gate_w']
GATHER_F32 = ['b_conv_w', 'b_conv_b', 'b_lam', 'f_conv_w']


def _cparams(sem, vmem=VMEM_LIMIT):
    return pltpu.CompilerParams(dimension_semantics=sem, vmem_limit_bytes=vmem)


def _tile(n, cap, q=LANES):
    best = None
    for t in range(q, min(n, cap) + 1, q):
        if n % t == 0:
            best = t
    return best if best is not None else n


def _sigmoid(x):
    return 1.0 / (1.0 + jnp.exp(-x))


_GELU_C = math.sqrt(2.0 / math.pi)


def _gelu_and_grad(x):
    x2 = x * x
    t = jnp.tanh(_GELU_C * (x + 0.044715 * x * x2))
    cdf = 0.5 * (1.0 + t)
    g = x * cdf
    dg = cdf + x * 0.5 * (1.0 - t * t) * _GELU_C * (1.0 + 3.0 * 0.044715 * x2)
    return g, dg


def _gelu(x):
    t = jnp.tanh(_GELU_C * (x + 0.044715 * x * x * x))
    return x * (0.5 * (1.0 + t))


def _log1p(u):
    w = 1.0 + u
    d = w - 1.0
    return jnp.where(d == 0.0, u, jnp.log(w) * (u / jnp.where(d == 0.0, 1.0, d)))


def _softplus(y):
    return jnp.maximum(y, 0.0) + _log1p(jnp.exp(-jnp.abs(y)))


def _log_sigmoid(z):
    return -_softplus(-z)


def _neg_expm1(x):
    poly = x * (1.0 + x * (1.0 / 2 + x * (1.0 / 6 + x * (1.0 / 24 + x * (1.0 / 120 + x * (1.0 / 720 + x * (1.0 / 5040)))))))
    return -jnp.where(x > -0.25, poly, jnp.exp(x) - 1.0)


def _split3(x):
    hi = x.astype(BF16)
    r1 = x - hi.astype(F32)
    mid = r1.astype(BF16)
    lo = (r1 - mid.astype(F32)).astype(BF16)
    return hi, mid, lo


def _shift_down(x, halo, k):
    rolled = pltpu.roll(x, k, axis=0)
    hal = pltpu.roll(halo, k, axis=0)
    r8 = lax.broadcasted_iota(jnp.int32, halo.shape, 0)
    head = jnp.where(r8 < k, hal, rolled[:SUBLANES])
    if x.shape[0] == SUBLANES:
        return head
    return jnp.concatenate([head, rolled[SUBLANES:]], axis=0)


def _shift_up(x, nxt, k):
    n = x.shape[0]
    rolled = pltpu.roll(x, n - k, axis=0)
    nx = pltpu.roll(nxt, SUBLANES - k, axis=0)
    r8 = lax.broadcasted_iota(jnp.int32, nxt.shape, 0)
    tail = jnp.where(r8 >= SUBLANES - k, nx, rolled[n - SUBLANES:])
    if n == SUBLANES:
        return tail
    return jnp.concatenate([rolled[:n - SUBLANES], tail], axis=0)


def _split_spec(arr_ndim, part_cols, br, bc, idx):
    if arr_ndim == 3:
        nbh = part_cols // bc
        return pl.BlockSpec((None, br, bc), lambda i, j, k: (lax.div(idx(i, j, k)[1], nbh), idx(i, j, k)[0],
                                                             lax.rem(idx(i, j, k)[1], nbh)))
    return pl.BlockSpec((br, bc), lambda i, j, k: idx(i, j, k))


def _dims(arr):
    if arr.ndim == 3:
        return arr.shape[1], arr.shape[0] * arr.shape[2], arr.shape[2]
    return arr.shape[0], arr.shape[1], arr.shape[1]


def _mm(a, b, *, ta=False, tb=False, out_dtype=F32, out_split=1, add=None, add_scale=1.0,
        tm_cap=1024, tn_cap=1024, tk_cap=1408, name):
    ar, ac, apart = _dims(a)
    br_, bc_, bpart = _dims(b)
    m, kdim = (ac, ar) if ta else (ar, ac)
    kdim_b, n = (bc_, br_) if tb else (br_, bc_)
    assert kdim == kdim_b, (name, a.shape, b.shape)
    tm = _tile(apart, tm_cap) if ta else _tile(m, tm_cap, SUBLANES)
    tn = _tile(n, tn_cap, SUBLANES) if tb else _tile(math.gcd(bpart, n // out_split), tn_cap)
    if ta:
        tk = _tile(kdim, 1024, 2 * SUBLANES)
    elif tb:
        tk = _tile(math.gcd(apart, bpart), tk_cap)
    else:
        tk = _tile(apart, tk_cap)

    def vmem_bytes(tm_, tk_):
        per_step = 2 * (tm_ * tk_ + tk_ * tn) + tm_ * tn * (jnp.dtype(out_dtype).itemsize + (4 if add is not None else 0))
        return 2 * per_step + (4 * tm_ * tn if kdim // tk_ > 1 else 0)

    if ta and kdim % (2 * tk) == 0 and vmem_bytes(tm, 2 * tk) <= MM_VMEM_BUDGET:
        tk *= 2
    elif not ta and m % (2 * tm) == 0 and vmem_bytes(2 * tm, tk) <= MM_VMEM_BUDGET:
        tm *= 2
    assert m % tm == 0 and n % tn == 0 and kdim % tk == 0, (name, m, n, kdim, tm, tn, tk)
    nk = kdim // tk
    a_spec = (_split_spec(a.ndim, apart, tk, tm, lambda i, j, k: (k, i)) if ta
              else _split_spec(a.ndim, apart, tm, tk, lambda i, j, k: (i, k)))
    b_spec = (_split_spec(b.ndim, bpart, tn, tk, lambda i, j, k: (j, k)) if tb
              else _split_spec(b.ndim, bpart, tk, tn, lambda i, j, k: (k, j)))
    if out_split > 1:
        out_shape = jax.ShapeDtypeStruct((out_split, m, n // out_split), out_dtype)
        o_spec = _split_spec(3, n // out_split, tm, tn, lambda i, j, k: (i, j))
    else:
        out_shape = jax.ShapeDtypeStruct((m, n), out_dtype)
        o_spec = pl.BlockSpec((tm, tn), lambda i, j, k: (i, j))
    dn = (((0 if ta else 1,), (1 if tb else 0,)), ((), ()))
    in_specs = [a_spec, b_spec]
    args = [a, b]
    if add is not None:
        in_specs.append(pl.BlockSpec((tm, tn), lambda i, j, k: (i, j)))
        args.append(add)
    use_acc = nk > 1
    has_add = add is not None

    def body(*refs):
        a_ref, b_ref = refs[0], refs[1]
        add_ref = refs[2] if has_add else None
        o_ref = refs[3] if has_add else refs[2]
        part = lax.dot_general(a_ref[...], b_ref[...], dn, preferred_element_type=F32)

        def finish(acc):
            if has_add:
                acc = acc + add_scale * add_ref[...]
            o_ref[...] = acc.astype(out_dtype)

        if not use_acc:
            finish(part)
        else:
            acc_ref = refs[-1]
            k = pl.program_id(2)

            @pl.when(k == 0)
            def _():
                acc_ref[...] = part

            @pl.when(k > 0)
            def _():
                acc_ref[...] += part

            @pl.when(k == nk - 1)
            def _():
                finish(acc_ref[...])

    return pl.pallas_call(
        body, name=name, grid=(m // tm, n // tn, nk), in_specs=in_specs, out_specs=o_spec, out_shape=out_shape,
        scratch_shapes=[pltpu.VMEM((tm, tn), F32)] if use_acc else [],
        compiler_params=_cparams(("parallel", "parallel", "arbitrary")),
    )(*args)


def _ln_fwd(x, m, g, b, *, name, tm=512):
    t, d = x.shape

    def body(x_ref, m_ref, g_ref, b_ref, y_ref, yb_ref, xhat_ref, rstd_ref):
        z = ALPHA * x_ref[...] + m_ref[...]
        mu = jnp.mean(z, axis=-1, keepdims=True)
        zc = z - mu
        var = jnp.mean(zc * zc, axis=-1, keepdims=True)
        rstd = lax.rsqrt(var + LN_EPS)
        xhat = zc * rstd
        y = xhat * g_ref[...] + b_ref[...]
        y_ref[...] = y
        yb_ref[...] = y.astype(BF16)
        xhat_ref[...] = xhat
        rstd_ref[...] = rstd

    row = pl.BlockSpec((tm, d), lambda i: (i, 0))
    vec = pl.BlockSpec((1, d), lambda i: (0, 0))
    return pl.pallas_call(
        body, name=name, grid=(t // tm,), in_specs=[row, row, vec, vec],
        out_specs=[row, row, row, pl.BlockSpec((tm, 1), lambda i: (i, 0))],
        out_shape=[jax.ShapeDtypeStruct((t, d), F32), jax.ShapeDtypeStruct((t, d), BF16),
                   jax.ShapeDtypeStruct((t, d), F32), jax.ShapeDtypeStruct((t, 1), F32)],
        compiler_params=_cparams(("parallel",)),
    )(x, m, g, b)


def _ln_bwd(dy, xhat, rstd, g, *, name, tm=512):
    t, d = dy.shape

    def body(dy_ref, xhat_ref, rstd_ref, g_ref, dz_ref, dzb_ref, dg_ref, db_ref):
        @pl.when(pl.program_id(0) == 0)
        def _():
            dg_ref[...] = jnp.zeros_like(dg_ref)
            db_ref[...] = jnp.zeros_like(db_ref)

        dyv = dy_ref[...]
        xh = xhat_ref[...]
        dg_ref[...] += jnp.sum(dyv * xh, axis=0, keepdims=True)
        db_ref[...] += jnp.sum(dyv, axis=0, keepdims=True)
        dxh = dyv * g_ref[...]
        m1 = jnp.mean(dxh, axis=-1, keepdims=True)
        m2 = jnp.mean(dxh * xh, axis=-1, keepdims=True)
        dz = rstd_ref[...] * (dxh - m1 - xh * m2)
        dz_ref[...] = dz
        dzb_ref[...] = dz.astype(BF16)

    row = pl.BlockSpec((tm, d), lambda i: (i, 0))
    vec = pl.BlockSpec((1, d), lambda i: (0, 0))
    return pl.pallas_call(
        body, name=name, grid=(t // tm,), in_specs=[row, row, pl.BlockSpec((tm, 1), lambda i: (i, 0)), vec],
        out_specs=[row, row, vec, vec],
        out_shape=[jax.ShapeDtypeStruct((t, d), F32), jax.ShapeDtypeStruct((t, d), BF16),
                   jax.ShapeDtypeStruct((1, d), F32), jax.ShapeDtypeStruct((1, d), F32)],
        compiler_params=_cparams(("arbitrary",)),
    )(dy, xhat, rstd, g)


def _ple_fwd(x2, gl, pe, gate_b, *, name, tm=512):
    t, d = x2.shape

    def body(x_ref, gl_ref, pe_ref, b_ref, y_ref, yb_ref):
        y = x_ref[...] + _sigmoid(gl_ref[...] + b_ref[...]) * pe_ref[...]
        y_ref[...] = y
        yb_ref[...] = y.astype(BF16)

    row = pl.BlockSpec((tm, d), lambda i: (i, 0))
    vec = pl.BlockSpec((1, d), lambda i: (0, 0))
    return pl.pallas_call(
        body, name=name, grid=(t // tm,), in_specs=[row, row, row, vec], out_specs=[row, row],
        out_shape=[jax.ShapeDtypeStruct((t, d), F32), jax.ShapeDtypeStruct((t, d), BF16)],
        compiler_params=_cparams(("parallel",)),
    )(x2, gl, pe, gate_b)


def _ple_bwd(dx3, gl, pe, gate_b, *, name, tm=512):
    t, d = dx3.shape

    def body(dx_ref, gl_ref, pe_ref, b_ref, dgl_ref, dpe_ref, db_ref):
        @pl.when(pl.program_id(0) == 0)
        def _():
            db_ref[...] = jnp.zeros_like(db_ref)

        dx = dx_ref[...]
        gt = _sigmoid(gl_ref[...] + b_ref[...])
        dgl = dx * pe_ref[...] * gt * (1.0 - gt)
        db_ref[...] += jnp.sum(dgl, axis=0, keepdims=True)
        dgl_ref[...] = dgl.astype(BF16)
        dpe_ref[...] = (dx * gt).astype(BF16)

    row = pl.BlockSpec((tm, d), lambda i: (i, 0))
    vec = pl.BlockSpec((1, d), lambda i: (0, 0))
    return pl.pallas_call(
        body, name=name, grid=(t // tm,), in_specs=[row, row, row, vec], out_specs=[row, row, vec],
        out_shape=[jax.ShapeDtypeStruct((t, d), BF16), jax.ShapeDtypeStruct((t, d), BF16),
                   jax.ShapeDtypeStruct((1, d), F32)],
        compiler_params=_cparams(("arbitrary",)),
    )(dx3, gl, pe, gate_b)


def _loss_bwd(y, tgt, *, name, tm=512):
    t, d = y.shape

    def body(y_ref, t_ref, dy_ref, l_ref):
        @pl.when(pl.program_id(0) == 0)
        def _():
            l_ref[...] = jnp.zeros_like(l_ref)

        err = y_ref[...] - t_ref[...]
        dy_ref[...] = err * (1.0 / d)
        part = jnp.sum(jnp.sum(err * err, axis=0, keepdims=True), axis=1, keepdims=True) * (0.5 / d)
        l_ref[...] += jnp.broadcast_to(part, l_ref.shape)

    row = pl.BlockSpec((tm, d), lambda i: (i, 0))
    return pl.pallas_call(
        body, name=name, grid=(t // tm,), in_specs=[row, row],
        out_specs=[row, pl.BlockSpec((1, LANES), lambda i: (0, 0))],
        out_shape=[jax.ShapeDtypeStruct((t, d), F32), jax.ShapeDtypeStruct((1, LANES), F32)],
        compiler_params=_cparams(("arbitrary",)),
    )(y, tgt)


def _conv_causal(x, halo, w_ref, b, kw):
    acc = x * w_ref[kw - 1:kw, :] + b
    for k in range(kw - 1):
        acc = acc + _shift_down(x, halo, kw - 1 - k) * w_ref[k:k + 1, :]
    return acc


def _ffn_act_fwd(hdn, conv_w, conv_b, *, name, tm=512):
    _, t, f = hdn.shape
    tc = _tile(f, 1408)
    hb = tm // SUBLANES

    def body(h_ref, halo_ref, w_ref, b_ref, a_ref):
        first = pl.program_id(1) == 0
        parts = []
        for s in range(2):
            halo = jnp.where(first, 0.0, halo_ref[s])
            parts.append(_conv_causal(h_ref[s], halo, w_ref.at[s], b_ref[s], CONV_F))
        a_ref[...] = (_gelu(parts[1]) * parts[0]).astype(BF16)

    return pl.pallas_call(
        body, name=name, grid=(f // tc, t // tm),
        in_specs=[pl.BlockSpec((2, tm, tc), lambda j, i: (0, i, j)),
                  pl.BlockSpec((2, SUBLANES, tc), lambda j, i: (0, jnp.maximum(i * hb - 1, 0), j)),
                  pl.BlockSpec((2, CONV_F, tc), lambda j, i: (0, 0, j)),
                  pl.BlockSpec((2, 1, tc), lambda j, i: (0, 0, j))],
        out_specs=pl.BlockSpec((tm, tc), lambda j, i: (i, j)),
        out_shape=jax.ShapeDtypeStruct((t, f), BF16),
        compiler_params=_cparams(("parallel", "arbitrary")),
    )(hdn, hdn, conv_w, conv_b)


def _ffn_act_bwd(da, hdn, conv_w, conv_b, *, name, tm=256):
    _, t, f = hdn.shape
    tc = _tile(f, 1408)
    hb = tm // SUBLANES
    nr = t // tm

    strip = 2 * SUBLANES
    ns = tm // strip
    pieces = [(c0, min(2 * LANES, tc - c0)) for c0 in range(0, tc, 2 * LANES)]

    def body(da_ref, h_ref, halo_ref, w_ref, b_ref, dh_ref, dw_ref, db_ref, carry_ref, dw_acc, db_acc):
        i = pl.program_id(1)
        r = nr - 1 - i

        @pl.when(i == 0)
        def _():
            carry_ref[...] = jnp.zeros_like(carry_ref)
            dw_acc[...] = jnp.zeros_like(dw_acc)
            db_acc[...] = jnp.zeros_like(db_acc)

        def do_strip(si, carry):
            s = ns - 1 - si
            r0 = pl.multiple_of(s * strip, strip)
            above = pl.multiple_of(jnp.maximum(s * strip - SUBLANES, 0), SUBLANES)
            for c0, cw in pieces:
                cols = slice(c0, c0 + cw)
                xs, hcs = [], []
                for part in range(2):
                    x = h_ref[part, pl.ds(r0, strip), cols]
                    halo = jnp.where(r == 0, 0.0, halo_ref[part, :, cols])
                    prev = jnp.where(s == 0, halo, h_ref[part, pl.ds(above, SUBLANES), cols])
                    sh = [_shift_down(x, prev, k) for k in range(1, CONV_F)]
                    hc = x * w_ref[part, CONV_F - 1:CONV_F, cols] + b_ref[part, :, cols]
                    for k in range(1, CONV_F):
                        hc = hc + sh[k - 1] * w_ref[part, CONV_F - 1 - k:CONV_F - k, cols]
                    xs.append([x] + sh)
                    hcs.append(hc)
                g, dg = _gelu_and_grad(hcs[1])
                dav = da_ref[pl.ds(r0, strip), cols]
                dhc = [dav * g, dav * hcs[0] * dg]
                for part in range(2):
                    d = dhc[part]
                    nxt = carry_ref[part, :, cols]
                    db_acc[part, :, cols] += d
                    out = d * w_ref[part, CONV_F - 1:CONV_F, cols]
                    for k in range(CONV_F):
                        dw_acc[part, CONV_F - 1 - k, :, cols] += d * xs[part][k]
                        if k:
                            out = out + _shift_up(d, nxt, k) * w_ref[part, CONV_F - 1 - k:CONV_F - k, cols]
                    dh_ref[part, pl.ds(r0, strip), cols] = out.astype(BF16)
                    carry_ref[part, :, cols] = d[:SUBLANES]
            return carry

        lax.fori_loop(0, ns, do_strip, 0)

        @pl.when(i == nr - 1)
        def _():
            dw_ref[...] = jnp.sum(dw_acc[...], axis=2)
            db_ref[...] = jnp.sum(db_acc[...], axis=1, keepdims=True)

    return pl.pallas_call(
        body, name=name, grid=(f // tc, nr),
        in_specs=[pl.BlockSpec((tm, tc), lambda j, i: (nr - 1 - i, j)),
                  pl.BlockSpec((2, tm, tc), lambda j, i: (0, nr - 1 - i, j)),
                  pl.BlockSpec((2, SUBLANES, tc), lambda j, i: (0, jnp.maximum((nr - 1 - i) * hb - 1, 0), j)),
                  pl.BlockSpec((2, CONV_F, tc), lambda j, i: (0, 0, j)),
                  pl.BlockSpec((2, 1, tc), lambda j, i: (0, 0, j))],
        out_specs=[pl.BlockSpec((2, tm, tc), lambda j, i: (0, nr - 1 - i, j)),
                   pl.BlockSpec((2, CONV_F, tc), lambda j, i: (0, 0, j)),
                   pl.BlockSpec((2, 1, tc), lambda j, i: (0, 0, j))],
        out_shape=[jax.ShapeDtypeStruct((2, t, f), BF16), jax.ShapeDtypeStruct((2, CONV_F, f), F32),
                   jax.ShapeDtypeStruct((2, 1, f), F32)],
        scratch_shapes=[pltpu.VMEM((2, SUBLANES, tc), F32), pltpu.VMEM((2, CONV_F, 2 * SUBLANES, tc), F32),
                        pltpu.VMEM((2, 2 * SUBLANES, tc), F32)],
        compiler_params=_cparams(("arbitrary", "arbitrary")),
    )(da, hdn, hdn, conv_w, conv_b)


def _fgate_fwd(xb, w_f, b_f, *, name, tm=256):
    t, d = xb.shape

    def body(x_ref, w_ref, b_ref, z_ref, c_ref, carry_ref):
        @pl.when(pl.program_id(0) == 0)
        def _():
            carry_ref[...] = jnp.zeros_like(carry_ref)

        z = jnp.dot(x_ref[...], w_ref[...], preferred_element_type=F32) + b_ref[...]
        z_ref[...] = z
        ls = _log_sigmoid(z)
        rr = lax.broadcasted_iota(jnp.int32, (tm, tm), 0)
        cc = lax.broadcasted_iota(jnp.int32, (tm, tm), 1)
        tri = (cc <= rr).astype(BF16)
        cum = carry_ref[...]
        for piece in _split3(ls):
            cum = cum + jnp.dot(tri, piece, preferred_element_type=F32)
        c_ref[...] = cum
        carry_ref[...] = cum[tm - 1:tm, :]

    return pl.pallas_call(
        body, name=name, grid=(t // tm,),
        in_specs=[pl.BlockSpec((tm, d), lambda i: (i, 0)), pl.BlockSpec((d, LANES), lambda i: (0, 0)),
                  pl.BlockSpec((1, LANES), lambda i: (0, 0))],
        out_specs=[pl.BlockSpec((tm, LANES), lambda i: (i, 0)), pl.BlockSpec((tm, LANES), lambda i: (i, 0))],
        out_shape=[jax.ShapeDtypeStruct((t, LANES), F32), jax.ShapeDtypeStruct((t, LANES), F32)],
        scratch_shapes=[pltpu.VMEM((1, LANES), F32)],
        compiler_params=_cparams(("arbitrary",)),
    )(xb, w_f, b_f)


def _fgate_bwd(dc, z, *, name, tm=256):
    t = dc.shape[0]
    nr = t // tm

    def body(dc_ref, z_ref, dz_ref, db_ref, carry_ref):
        @pl.when(pl.program_id(0) == 0)
        def _():
            carry_ref[...] = jnp.zeros_like(carry_ref)
            db_ref[...] = jnp.zeros_like(db_ref)

        rr = lax.broadcasted_iota(jnp.int32, (tm, tm), 0)
        cc = lax.broadcasted_iota(jnp.int32, (tm, tm), 1)
        tri = (cc >= rr).astype(BF16)
        cum = carry_ref[...]
        for piece in _split3(dc_ref[...]):
            cum = cum + jnp.dot(tri, piece, preferred_element_type=F32)
        carry_ref[...] = cum[0:1, :]
        dz = cum * _sigmoid(-z_ref[...])
        db_ref[...] += jnp.sum(dz, axis=0, keepdims=True)
        dz_ref[...] = dz.astype(BF16)

    rev = pl.BlockSpec((tm, LANES), lambda i: (nr - 1 - i, 0))
    return pl.pallas_call(
        body, name=name, grid=(nr,), in_specs=[rev, rev],
        out_specs=[rev, pl.BlockSpec((1, LANES), lambda i: (0, 0))],
        out_shape=[jax.ShapeDtypeStruct((t, LANES), BF16), jax.ShapeDtypeStruct((1, LANES), F32)],
        scratch_shapes=[pltpu.VMEM((1, LANES), F32)],
        compiler_params=_cparams(("arbitrary",)),
    )(dc, z)


HM = 2 * HEAD_DIM
LANE_C, LANE_ONE, LANE_LSE = 64, 67, 70
Q_SUB = 256


def _three_parts(col, sign=1.0):
    col = sign * col
    hi = col.astype(BF16).astype(F32)
    r1 = col - hi
    mid = r1.astype(BF16).astype(F32)
    lo = (r1 - mid).astype(BF16).astype(F32)
    return hi, mid, lo


def _fill_lanes(base, lane, first, parts):
    out = base
    for n, part in enumerate(parts):
        out = jnp.where(lane == first + n, part, out)
    return out


def _hm_weight_t(w_t):
    rows, k = w_t.shape
    h = rows // HEAD_DIM
    return jnp.pad(w_t.reshape(h, HEAD_DIM, k), ((0, 0), (0, HM - HEAD_DIM), (0, 0))).reshape(h * HM, k)


def _hm_unpad_t(g_t):
    rows, k = g_t.shape
    h = rows // HM
    return g_t.reshape(h, HM, k)[:, :HEAD_DIM].reshape(h * HEAD_DIM, k)


def _attn_prep(qkv, c, *, name, tm=256):
    t, w = qkv.shape
    nh = w // (3 * HM)

    def body(x_ref, c_ref, o_ref):
        lane = lax.broadcasted_iota(jnp.int32, (1, HM), 1)
        cblk = c_ref[...]
        for h in range(nh):
            ch = cblk[:, h:h + 1]
            pos = _three_parts(ch)
            neg = _three_parts(ch, -1.0)
            for part in range(3):
                col = (part * nh + h) * HM
                x = x_ref[:, col:col + HM].astype(F32)
                if part == 0:
                    y = _fill_lanes(_fill_lanes(x * ATTN_SCALE, lane, LANE_C, pos), lane, LANE_ONE, (1.0, 1.0, 1.0))
                elif part == 1:
                    y = _fill_lanes(_fill_lanes(x, lane, LANE_C, (1.0, 1.0, 1.0)), lane, LANE_ONE, neg)
                    y = _fill_lanes(y, lane, LANE_LSE, (1.0, 1.0, 1.0))
                else:
                    y = _fill_lanes(x, lane, LANE_C, (1.0, 1.0, 1.0))
                o_ref[:, col:col + HM] = y.astype(BF16)

    return pl.pallas_call(
        body, name=name, grid=(t // tm,),
        in_specs=[pl.BlockSpec((tm, w), lambda i: (i, 0)), pl.BlockSpec((tm, LANES), lambda i: (i, 0))],
        out_specs=pl.BlockSpec((tm, w), lambda i: (i, 0)),
        out_shape=jax.ShapeDtypeStruct((t, w), BF16),
        compiler_params=_cparams(("parallel",)),
    )(qkv, c)


def _flash_fwd_hm(qkva, *, name, tq=1024):
    t, w = qkva.shape
    nh = w // (3 * HM)
    nq = t // tq
    nsub = tq // Q_SUB

    def body(q_ref, k_ref, v_ref, o_ref, qb_ref, s_scr, p_scr, m_scr, acc_scr):
        i = pl.program_id(1)
        lane = lax.broadcasted_iota(jnp.int32, (1, HM), 1)
        rr = lax.broadcasted_iota(jnp.int32, (Q_SUB, tq), 0)
        cc = lax.broadcasted_iota(jnp.int32, (Q_SUB, tq), 1)
        sub = lambda r: slice(r * Q_SUB, (r + 1) * Q_SUB)
        nt = (((1,), (1,)), ((), ()))

        def scores(j):
            kj = k_ref[pl.ds(pl.multiple_of(j * tq, tq), tq), :]
            for r in range(nsub):
                s_scr[sub(r), :] = lax.dot_general(q_ref[sub(r), :], kj, nt, preferred_element_type=F32)

        def step(jp, masked, with_pv=True):
            vj = v_ref[pl.ds(pl.multiple_of(jp * tq, tq), tq), :]
            for r in range(nsub):
                if masked:
                    s_scr[sub(r), :] = jnp.where(cc <= rr + r * Q_SUB, s_scr[sub(r), :], NEG_BIG)
                m_old = m_scr[sub(r), :]
                m_new = jnp.maximum(m_old, jnp.max(s_scr[sub(r), :], axis=1, keepdims=True))
                m_scr[sub(r), :] = m_new
                if with_pv:
                    pv = jnp.dot(p_scr[sub(r), :], vj, preferred_element_type=F32)
                    acc_scr[sub(r), :] = (acc_scr[sub(r), :] + pv) * jnp.exp(m_old - m_new)
                p_scr[sub(r), :] = jnp.exp(s_scr[sub(r), :] - m_new).astype(BF16)

        def kv_step(j, carry):
            step(j - 1, False)
            scores(j + 1)
            return carry

        p_scr[...] = jnp.zeros_like(p_scr)
        m_scr[...] = jnp.full_like(m_scr, NEG_BIG)
        acc_scr[...] = jnp.zeros_like(acc_scr)
        scores(0)

        @pl.when(i > 0)
        def _():
            step(0, False, with_pv=False)
            scores(1)

        lax.fori_loop(1, i, kv_step, 0)
        step(jnp.maximum(i - 1, 0), True)
        vi = v_ref[pl.ds(pl.multiple_of(i * tq, tq), tq), :]
        for r in range(nsub):
            acc = acc_scr[sub(r), :] + jnp.dot(p_scr[sub(r), :], vi, preferred_element_type=F32)
            l = jnp.sum(jnp.where(lane == LANE_C, acc, 0.0), axis=1, keepdims=True)
            o_ref[sub(r), :] = jnp.where(lane < HEAD_DIM, acc / l, 0.0).astype(BF16)
            lse = m_scr[sub(r), :] + jnp.log(l)
            qb = _fill_lanes(q_ref[sub(r), :].astype(F32), lane, LANE_LSE, _three_parts(lse, -1.0))
            qb_ref[sub(r), :] = qb.astype(BF16)

    blk = lambda part: pl.BlockSpec((t, HM), lambda h, i: (0, part * nh + h))
    tile = pl.BlockSpec((tq, HM), lambda h, i: (i, h))
    return pl.pallas_call(
        body, name=name, grid=(nh, nq), in_specs=[tile, blk(1), blk(2)], out_specs=[tile, tile],
        out_shape=[jax.ShapeDtypeStruct((t, nh * HM), BF16), jax.ShapeDtypeStruct((t, nh * HM), BF16)],
        scratch_shapes=[pltpu.VMEM((tq, tq), F32), pltpu.VMEM((tq, tq), BF16), pltpu.VMEM((tq, 1), F32),
                        pltpu.VMEM((tq, HM), F32)],
        compiler_params=_cparams(("parallel", "arbitrary")),
    )(qkva, qkva, qkva)


def _attn_prep_bwd(o, do, *, name, tm=512):
    t, w = o.shape
    nh = w // HM

    def body(o_ref, do_ref, out_ref):
        lane = lax.broadcasted_iota(jnp.int32, (1, HM), 1)
        for h in range(nh):
            cols = slice(h * HM, (h + 1) * HM)
            dov = do_ref[:, cols].astype(F32)
            delta = jnp.sum(o_ref[:, cols].astype(F32) * dov, axis=1, keepdims=True)
            out_ref[:, cols] = _fill_lanes(dov, lane, LANE_C, _three_parts(delta, -1.0)).astype(BF16)

    row = pl.BlockSpec((tm, w), lambda i: (i, 0))
    return pl.pallas_call(
        body, name=name, grid=(t // tm,), in_specs=[row, row], out_specs=row,
        out_shape=jax.ShapeDtypeStruct((t, w), BF16), compiler_params=_cparams(("parallel",)),
    )(o, do)


def _flash_bwd_hm(qb, doa, qkva, *, name, tq=512, tqc=512):
    t, w = qb.shape
    nh = w // HM
    nq = t // tq
    nqc = t // tqc
    nsub = tqc // Q_SUB
    grp = slice(LANE_C, LANE_C + SUBLANES)

    def body(q_ref, do_ref, k_ref, v_ref, dq_ref, dk_ref, dv_ref, dcq_ref, dcj_ref,
             dq_acc, st_scr, dpt_scr, pt_scr, ds_scr, dk_scr, dv_scr):
        j = pl.program_id(1)
        kv_minus_q = (lax.broadcasted_iota(jnp.int32, (tq, Q_SUB), 0)
                      - lax.broadcasted_iota(jnp.int32, (tq, Q_SUB), 1))
        sub = lambda r: slice(r * Q_SUB, (r + 1) * Q_SUB)
        nt = (((1,), (1,)), ((), ()))
        tn = (((0,), (0,)), ((), ()))
        first = lax.div(j * tq, tqc)

        @pl.when(j == 0)
        def _():
            dq_acc[...] = jnp.zeros_like(dq_acc)

        def rows_of(i, r):
            return pl.ds(pl.multiple_of(i * tqc + r * Q_SUB, Q_SUB), Q_SUB)

        def scores(i):
            for r in range(nsub):
                st_scr[:, sub(r)] = lax.dot_general(k_ref[...], q_ref[rows_of(i, r), :], nt, preferred_element_type=F32)
                dpt_scr[:, sub(r)] = lax.dot_general(v_ref[...], do_ref[rows_of(i, r), :], nt,
                                                     preferred_element_type=F32)

        def products(ip):
            for r in range(nsub):
                dv_scr[...] += jnp.dot(pt_scr[:, sub(r)], do_ref[rows_of(ip, r), :], preferred_element_type=F32)
                dk_scr[...] += jnp.dot(ds_scr[:, sub(r)], q_ref[rows_of(ip, r), :], preferred_element_type=F32)
                dq_acc[rows_of(ip, r), :] += lax.dot_general(ds_scr[:, sub(r)], k_ref[...], tn,
                                                             preferred_element_type=F32)

        def probabilities(masked):
            for r in range(nsub):
                st = st_scr[:, sub(r)]
                if masked:
                    st = jnp.where(kv_minus_q <= first * tqc + r * Q_SUB - j * tq, st, NEG_BIG)
                pt = jnp.exp(st)
                pt_scr[:, sub(r)] = pt.astype(BF16)
                ds_scr[:, sub(r)] = (pt * dpt_scr[:, sub(r)]).astype(BF16)

        def q_step(i, carry):
            products(i - 1)
            probabilities(False)
            scores(jnp.minimum(i + 1, nqc - 1))
            return carry

        dk_scr[...] = jnp.zeros_like(dk_scr)
        dv_scr[...] = jnp.zeros_like(dv_scr)
        scores(first)
        probabilities(True)
        scores(jnp.minimum(first + 1, nqc - 1))
        lax.fori_loop(first + 1, nqc, q_step, 0)
        products(nqc - 1)
        dk = dk_scr[...]
        dk_ref[...] = dk.astype(BF16)
        dv_ref[...] = dv_scr[...].astype(BF16)
        dcj_ref[...] = dk.T[grp, :]

        @pl.when(j == nq - 1)
        def _():
            dq_ref[...] = (dq_acc[...] * ATTN_SCALE).astype(BF16)
            for cidx in range(nq):
                rows = slice(cidx * tq, (cidx + 1) * tq)
                dcq_ref[:, rows] = dq_acc[rows, :].T[grp, :]

    full = pl.BlockSpec((t, HM), lambda h, j: (0, h))
    ktile = lambda part: pl.BlockSpec((tq, HM), lambda h, j: (j, part * nh + h))
    tile = pl.BlockSpec((tq, HM), lambda h, j: (j, h))
    hm_shape = jax.ShapeDtypeStruct((t, w), BF16)
    row_shape = jax.ShapeDtypeStruct((nh, SUBLANES, t), F32)
    return pl.pallas_call(
        body, name=name, grid=(nh, nq),
        in_specs=[pl.BlockSpec((t, HM), lambda h, j: (0, h), pipeline_mode=pl.Buffered(1)),
                  pl.BlockSpec((t, HM), lambda h, j: (0, h), pipeline_mode=pl.Buffered(1)), ktile(1), ktile(2)],
        out_specs=[full, tile, tile, pl.BlockSpec((None, SUBLANES, t), lambda h, j: (h, 0, 0)),
                   pl.BlockSpec((None, SUBLANES, tq), lambda h, j: (h, 0, j))],
        out_shape=[hm_shape, hm_shape, hm_shape, row_shape, row_shape],
        scratch_shapes=[pltpu.VMEM((t, HM), F32), pltpu.VMEM((tq, tqc), F32), pltpu.VMEM((tq, tqc), F32),
                        pltpu.VMEM((tq, tqc), BF16), pltpu.VMEM((tq, tqc), BF16), pltpu.VMEM((tq, HM), F32),
                        pltpu.VMEM((tq, HM), F32)],
        compiler_params=_cparams(("parallel", "arbitrary")),
    )(qb, doa, qkva, qkva)


def _block_diag_dot(xb, w_ref, transpose_w=False):
    outs = []
    for n in range(N_BLOCKS_B):
        xn = xb[:, n * BLOCK_B:(n + 1) * BLOCK_B]
        dn = (((1,), (1 if transpose_w else 0,)), ((), ()))
        outs.append(lax.dot_general(xn, w_ref[n], dn, preferred_element_type=F32))
    return jnp.concatenate(outs, axis=1)


def _rglru_fwd(proj, conv_w, conv_b, w_a, b_a, w_i, b_i, lam, *, name, tm=256):
    _, t, d = proj.shape
    hb = tm // SUBLANES
    ng = tm // SUBLANES

    def body(p_ref, halo_ref, cw_ref, cb_ref, wa_ref, ba_ref, wi_ref, bi_ref, lam_ref,
             xb_ref, r_ref, ig_ref, a_ref, h_ref, y_ref, u_scr, hc_scr):
        first = pl.program_id(0) == 0

        @pl.when(first)
        def _():
            hc_scr[...] = jnp.zeros_like(hc_scr)

        halo = jnp.where(first, 0.0, halo_ref[0])
        xb = _conv_causal(p_ref[0], halo, cw_ref, cb_ref[...], CONV_B)
        xb_ref[...] = xb
        xbb = xb.astype(BF16)
        r = _sigmoid(_block_diag_dot(xbb, wa_ref) + ba_ref[...])
        ig = _sigmoid(_block_diag_dot(xbb, wi_ref) + bi_ref[...])
        r_ref[...] = r
        ig_ref[...] = ig
        log_a = (-LRU_C) * r * _softplus(-lam_ref[...])
        a_ref[...] = jnp.exp(log_a)
        u_scr[...] = jnp.sqrt(_neg_expm1(2.0 * log_a)) * (ig * xb)

        ones8 = jnp.ones((SUBLANES, d), F32)
        zeros8 = jnp.zeros((SUBLANES, d), F32)

        def group(gi, hprev):
            off = pl.multiple_of(gi * SUBLANES, SUBLANES)
            a8 = a_ref[pl.ds(off, SUBLANES), :]
            u8 = u_scr[pl.ds(off, SUBLANES), :]
            for s in (1, 2, 4):
                u8 = a8 * _shift_down(u8, zeros8, s) + u8
                a8 = a8 * _shift_down(a8, ones8, s)
            h8 = a8 * hprev + u8
            h_ref[pl.ds(off, SUBLANES), :] = h8
            return h8[SUBLANES - 1:SUBLANES, :]

        hc_scr[...] = lax.fori_loop(0, ng, group, hc_scr[...])
        y_ref[...] = (h_ref[...] * _gelu(p_ref[1])).astype(BF16)

    row = pl.BlockSpec((tm, d), lambda i: (i, 0))
    vec = pl.BlockSpec((1, d), lambda i: (0, 0))
    wblk = pl.BlockSpec((N_BLOCKS_B, BLOCK_B, BLOCK_B), lambda i: (0, 0, 0))
    f32_td = jax.ShapeDtypeStruct((t, d), F32)
    return pl.pallas_call(
        body, name=name, grid=(t // tm,),
        in_specs=[pl.BlockSpec((2, tm, d), lambda i: (0, i, 0)),
                  pl.BlockSpec((1, SUBLANES, d), lambda i: (0, jnp.maximum(i * hb - 1, 0), 0)),
                  pl.BlockSpec((CONV_B, d), lambda i: (0, 0)), vec, wblk, vec, wblk, vec, vec],
        out_specs=[row, row, row, row, row, row],
        out_shape=[f32_td, f32_td, f32_td, f32_td, f32_td, jax.ShapeDtypeStruct((t, d), BF16)],
        scratch_shapes=[pltpu.VMEM((tm, d), F32), pltpu.VMEM((1, d), F32)],
        compiler_params=_cparams(("arbitrary",)),
    )(proj, proj, conv_w, conv_b, w_a, b_a, w_i, b_i, lam)


def _rglru_bwd(dy, proj, xb, r, ig, a, h, conv_w, w_a, w_i, lam, *, name, tm=256):
    _, t, d = proj.shape
    hb = tm // SUBLANES
    ng = tm // SUBLANES
    nr = t // tm

    def body(dy_ref, p_ref, phalo_ref, xb_ref, r_ref, ig_ref, a_ref, h_ref, hhalo_ref,
             cw_ref, wa_ref, wi_ref, lam_ref,
             dp_ref, dcw_ref, dcb_ref, dba_ref, dbi_ref, dlam_ref, dwa_ref, dwi_ref,
             g_scr, carry_g, carry_dxb):
        i = pl.program_id(0)
        rpos = nr - 1 - i

        @pl.when(i == 0)
        def _():
            for ref in (dcw_ref, dcb_ref, dba_ref, dbi_ref, dlam_ref, dwa_ref, dwi_ref, carry_g, carry_dxb):
                ref[...] = jnp.zeros_like(ref)

        gb = p_ref[1]
        gel, dgel = _gelu_and_grad(gb)
        dyv = dy_ref[...]
        hv = h_ref[...]
        dp_ref[1] = (dyv * hv * dgel).astype(BF16)
        g_scr[...] = dyv * gel
        av = a_ref[...]

        ones8 = jnp.ones((SUBLANES, d), F32)
        zeros8 = jnp.zeros((SUBLANES, d), F32)

        def group(gi, cin):
            off = pl.multiple_of((ng - 1 - gi) * SUBLANES, SUBLANES)
            g8 = g_scr[pl.ds(off, SUBLANES), :]
            a8 = a_ref[pl.ds(off, SUBLANES), :]
            b8 = _shift_up(a8, ones8, 1)
            row8 = lax.broadcasted_iota(jnp.int32, (SUBLANES, d), 0)
            g8 = g8 + jnp.where(row8 == SUBLANES - 1, cin, 0.0)
            b8 = jnp.where(row8 == SUBLANES - 1, 0.0, b8)
            for s in (1, 2, 4):
                g8 = g8 + b8 * _shift_up(g8, zeros8, s)
                b8 = b8 * _shift_up(b8, zeros8, s)
            g_scr[pl.ds(off, SUBLANES), :] = g8
            return a8[0:1, :] * g8[0:1, :]

        carry_g[...] = lax.fori_loop(0, ng, group, carry_g[...])

        du = g_scr[...]
        hhalo = jnp.where(rpos == 0, 0.0, hhalo_ref[...])
        hprev = _shift_down(hv, hhalo, 1)
        da = du * hprev
        rv = r_ref[...]
        igv = ig_ref[...]
        xbv = xb_ref[...]
        sp = _softplus(-lam_ref[...])
        log_a = (-LRU_C) * rv * sp
        mult = jnp.sqrt(_neg_expm1(2.0 * log_a))
        dmult = du * (igv * xbv)
        dig = du * mult * xbv
        dxb = du * mult * igv
        dlog_a = da * av - dmult * (av * av) / mult
        dr = dlog_a * ((-LRU_C) * sp)
        dsp = jnp.sum(dlog_a * ((-LRU_C) * rv), axis=0, keepdims=True)
        dlam_ref[...] += dsp * (-_sigmoid(-lam_ref[...]))
        dra = dr * rv * (1.0 - rv)
        dia = dig * igv * (1.0 - igv)
        dba_ref[...] += jnp.sum(dra, axis=0, keepdims=True)
        dbi_ref[...] += jnp.sum(dia, axis=0, keepdims=True)
        drab = dra.astype(BF16)
        diab = dia.astype(BF16)
        xbb = xbv.astype(BF16)
        dxb = dxb + _block_diag_dot(drab, wa_ref, True) + _block_diag_dot(diab, wi_ref, True)
        tn = (((0,), (0,)), ((), ()))
        for n in range(N_BLOCKS_B):
            sl = slice(n * BLOCK_B, (n + 1) * BLOCK_B)
            dwa_ref[n] += lax.dot_general(xbb[:, sl], drab[:, sl], tn, preferred_element_type=F32)
            dwi_ref[n] += lax.dot_general(xbb[:, sl], diab[:, sl], tn, preferred_element_type=F32)

        xpre = p_ref[0]
        phalo = jnp.where(rpos == 0, 0.0, phalo_ref[0])
        nxt = carry_dxb[...]
        dcb_ref[...] += jnp.sum(dxb, axis=0, keepdims=True)
        out = dxb * cw_ref[CONV_B - 1:CONV_B, :]
        dcw_ref[CONV_B - 1:CONV_B, :] += jnp.sum(dxb * xpre, axis=0, keepdims=True)
        for k in range(CONV_B - 1):
            sh = CONV_B - 1 - k
            out = out + _shift_up(dxb, nxt, sh) * cw_ref[k:k + 1, :]
            dcw_ref[k:k + 1, :] += jnp.sum(dxb * _shift_down(xpre, phalo, sh), axis=0, keepdims=True)
        dp_ref[0] = out.astype(BF16)
        carry_dxb[...] = dxb[:SUBLANES]

    rev = pl.BlockSpec((tm, d), lambda i: (nr - 1 - i, 0))
    halo8 = pl.BlockSpec((SUBLANES, d), lambda i: (jnp.maximum((nr - 1 - i) * hb - 1, 0), 0))
    vec = pl.BlockSpec((1, d), lambda i: (0, 0))
    wblk = pl.BlockSpec((N_BLOCKS_B, BLOCK_B, BLOCK_B), lambda i: (0, 0, 0))
    vec_shape = jax.ShapeDtypeStruct((1, d), F32)
    w_shape = jax.ShapeDtypeStruct((N_BLOCKS_B, BLOCK_B, BLOCK_B), F32)
    return pl.pallas_call(
        body, name=name, grid=(nr,),
        in_specs=[rev, pl.BlockSpec((2, tm, d), lambda i: (0, nr - 1 - i, 0)),
                  pl.BlockSpec((1, SUBLANES, d), lambda i: (0, jnp.maximum((nr - 1 - i) * hb - 1, 0), 0)),
                  rev, rev, rev, rev, rev, halo8,
                  pl.BlockSpec((CONV_B, d), lambda i: (0, 0)), wblk, wblk, vec],
        out_specs=[pl.BlockSpec((2, tm, d), lambda i: (0, nr - 1 - i, 0)),
                   pl.BlockSpec((CONV_B, d), lambda i: (0, 0)), vec, vec, vec, vec, wblk, wblk],
        out_shape=[jax.ShapeDtypeStruct((2, t, d), BF16), jax.ShapeDtypeStruct((CONV_B, d), F32),
                   vec_shape, vec_shape, vec_shape, vec_shape, w_shape, w_shape],
        scratch_shapes=[pltpu.VMEM((tm, d), F32), pltpu.VMEM((1, d), F32), pltpu.VMEM((SUBLANES, d), F32)],
        compiler_params=_cparams(("arbitrary",)),
    )(dy, proj, proj, xb, r, ig, a, h, h, conv_w, w_a, w_i, lam)


def _split_cols(w, s):
    k, c = w.shape[-2:]
    return jnp.moveaxis(w.reshape(w.shape[:-2] + (k, s, c // s)), -2, -3)


def _merge_cols(w):
    s, k, c = w.shape
    return jnp.moveaxis(w, 0, 1).reshape(k, s * c)


def _local_step(x, p, tgt, w):
    t = x.shape[0]
    bf = lambda v: v.astype(BF16)
    saved = []
    xcur = x
    xcur_b = bf(x)
    for i in range(DEPTH):
        j = i // 2
        L = f"l{i}_"
        sv = {'x_in_b': xcur_b}
        if i % 2 == 0:
            w_in_t = bf(w['a_w_in'][j])
            sv['w_qkv_t'] = jnp.concatenate(
                [_hm_weight_t(w_in_t[part * D_MODEL:(part + 1) * D_MODEL]) for part in range(3)])
            sv['w_f'] = jnp.pad(w_in_t[3 * D_MODEL:].T, ((0, 0), (0, LANES - N_HEADS)))
            sv['w_out'] = _hm_weight_t(bf(w['a_w_out'][j]))
            b_f = jnp.pad(w['a_b_f'][j], (0, LANES - N_HEADS)).reshape(1, LANES)
            qkv = _mm(xcur_b, sv['w_qkv_t'], tb=True, out_dtype=BF16, name=L + "qkv")
            z, c = _fgate_fwd(xcur_b, sv['w_f'], b_f, name=L + "fgate")
            qkva = _attn_prep(qkv, c, name=L + "attn_prep")
            o, qb = _flash_fwd_hm(qkva, name=L + "flash_fwd")
            mix = _mm(o, sv['w_out'], name=L + "attn_out")
            sv.update(qkva=qkva, z=z, o=o, qb=qb)
        else:
            sv['w_in'] = bf(w['b_w_in'][j])
            sv['w_out'] = bf(w['b_w_out'][j])
            sv['conv_w'] = w['b_conv_w'][j]
            sv['w_a'] = bf(w['b_w_a'][j])
            sv['w_i'] = bf(w['b_w_i'][j])
            sv['lam'] = w['b_lam'][j].reshape(1, D_MODEL)
            proj = _mm(xcur_b, sv['w_in'], out_split=2, name=L + "rg_in")
            xb, r, ig, a, h, y = _rglru_fwd(
                proj, sv['conv_w'], w['b_conv_b'][j].reshape(1, D_MODEL), sv['w_a'],
                w['b_b_a'][j].reshape(1, D_MODEL), sv['w_i'], w['b_b_i'][j].reshape(1, D_MODEL), sv['lam'],
                name=L + "rglru_fwd")
            mix = _mm(y, sv['w_out'], name=L + "rg_out")
            sv.update(proj=proj, xb=xb, r=r, ig=ig, a=a, h=h, y=y)
        sv['ln1_g'] = w['ln1_g'][i].reshape(1, D_MODEL)
        x1, x1b, xhat1, rstd1 = _ln_fwd(xcur, mix, sv['ln1_g'], w['ln1_b'][i].reshape(1, D_MODEL), name=L + "ln1")
        sv['w_up'] = bf(w['f_w_up'][i])
        sv['w_down'] = bf(w['f_w_down'][i])
        sv['fconv_w'] = _split_cols(w['f_conv_w'][i], 2)
        sv['fconv_b'] = w['f_conv_b'][i].reshape(2, 1, D_FF)
        hdn = _mm(x1b, sv['w_up'], out_split=2, tn_cap=1408, name=L + "ffn_up")
        act = _ffn_act_fwd(hdn, sv['fconv_w'], sv['fconv_b'], name=L + "ffn_act")
        ff = _mm(act, sv['w_down'], tk_cap=2816, name=L + "ffn_down")
        sv['ln2_g'] = w['ln2_g'][i].reshape(1, D_MODEL)
        x2, x2b, xhat2, rstd2 = _ln_fwd(x1, ff, sv['ln2_g'], w['ln2_b'][i].reshape(1, D_MODEL), name=L + "ln2")
        sv['gate_w'] = bf(w['ple_gate_w'][i])
        sv['ple_w'] = bf(w['ple_w'][i])
        sv['gate_b'] = w['ple_gate_b'][i].reshape(1, D_MODEL)
        sv['p_b'] = bf(p[i])
        gl = _mm(x2b, sv['gate_w'], name=L + "ple_gate")
        pe = _mm(sv['p_b'], sv['ple_w'], name=L + "ple_emb")
        x3, x3b = _ple_fwd(x2, gl, pe, sv['gate_b'], name=L + "ple")
        sv.update(xhat1=xhat1, rstd1=rstd1, x1b=x1b, hdn=hdn, act=act, xhat2=xhat2, rstd2=rstd2, x2b=x2b,
                  gl=gl, pe=pe)
        saved.append(sv)
        xcur, xcur_b = x3, x3b

    dx, loss_row = _loss_bwd(xcur, tgt, name="loss")

    g = {n: [None] * w[n].shape[0] for n in WEIGHTS}
    for i in reversed(range(DEPTH)):
        j = i // 2
        L = f"l{i}b_"
        sv = saved[i]
        dgl, dpe, d_gate_b = _ple_bwd(dx, sv['gl'], sv['pe'], sv['gate_b'], name=L + "ple")
        g['ple_gate_b'][i] = d_gate_b[0]
        g['ple_w'][i] = _mm(sv['p_b'], dpe, ta=True, name=L + "ple_emb_dw")
        g['ple_gate_w'][i] = _mm(sv['x2b'], dgl, ta=True, name=L + "ple_gate_dw")
        dx2 = _mm(dgl, sv['gate_w'], tb=True, add=dx, name=L + "ple_gate_dx")
        dz2, dz2b, dg2, db2 = _ln_bwd(dx2, sv['xhat2'], sv['rstd2'], sv['ln2_g'], name=L + "ln2")
        g['ln2_g'][i], g['ln2_b'][i] = dg2[0], db2[0]
        g['f_w_down'][i] = _mm(sv['act'], dz2b, ta=True, tm_cap=1408, name=L + "ffn_down_dw")
        da = _mm(dz2b, sv['w_down'], tb=True, tn_cap=1408, name=L + "ffn_down_dx")
        dhdn, d_fcw, d_fcb = _ffn_act_bwd(da, sv['hdn'], sv['fconv_w'], sv['fconv_b'], name=L + "ffn_act")
        g['f_conv_w'][i] = _merge_cols(d_fcw)
        g['f_conv_b'][i] = d_fcb.reshape(2 * D_FF)
        g['f_w_up'][i] = _mm(sv['x1b'], dhdn, ta=True, tn_cap=1408, name=L + "ffn_up_dw")
        dx1 = _mm(dhdn, sv['w_up'], tb=True, add=dz2, add_scale=ALPHA, tk_cap=D_FF, name=L + "ffn_up_dx")
        dz1, dz1b, dg1, db1 = _ln_bwd(dx1, sv['xhat1'], sv['rstd1'], sv['ln1_g'], name=L + "ln1")
        g['ln1_g'][i], g['ln1_b'][i] = dg1[0], db1[0]
        if i % 2 == 0:
            g['a_w_out'][j] = _hm_unpad_t(_mm(sv['o'], dz1b, ta=True, name=L + "attn_out_dw"))
            do = _mm(dz1b, sv['w_out'], tb=True, out_dtype=BF16, name=L + "attn_out_dx")
            doa = _attn_prep_bwd(sv['o'], do, name=L + "attn_prep")
            dqkv = _flash_bwd_hm(sv['qb'], doa, sv['qkva'], name=L + "flash_bwd")
            dcq, dcj = dqkv[3], dqkv[4]
            dc = jnp.pad((dcq[:, 0, :] - dcj[:, LANE_ONE - LANE_C, :]).T, ((0, 0), (0, LANES - N_HEADS)))
            dzf, d_b_f = _fgate_bwd(dc, sv['z'], name=L + "fgate")
            g['a_b_f'][j] = d_b_f[0, :N_HEADS]
            xb_in = sv['x_in_b']
            d_w = [_hm_unpad_t(_mm(dqkv[part], xb_in, ta=True, name=L + "qkv"[part] + "_dw")) for part in range(3)]
            d_wf = _mm(xb_in, dzf, ta=True, name=L + "f_dw")
            g['a_w_in'][j] = jnp.concatenate(d_w + [d_wf[:, :N_HEADS].T], axis=0)
            dxa, scale = dz1, ALPHA
            for part in range(3):
                w_part = sv['w_qkv_t'][part * N_HEADS * HM:(part + 1) * N_HEADS * HM]
                dxa = _mm(dqkv[part], w_part, add=dxa, add_scale=scale, name=L + "qkv"[part] + "_dx")
                scale = 1.0
            dx = _mm(dzf, sv['w_f'], tb=True, add=dxa, name=L + "f_dx")
        else:
            g['b_w_out'][j] = _mm(sv['y'], dz1b, ta=True, name=L + "rg_out_dw")
            dy = _mm(dz1b, sv['w_out'], tb=True, name=L + "rg_out_dx")
            dproj, d_cw, d_cb, d_ba, d_bi, d_lam, d_wa, d_wi = _rglru_bwd(
                dy, sv['proj'], sv['xb'], sv['r'], sv['ig'], sv['a'], sv['h'], sv['conv_w'], sv['w_a'], sv['w_i'],
                sv['lam'], name=L + "rglru_bwd")
            g['b_conv_w'][j], g['b_conv_b'][j] = d_cw, d_cb[0]
            g['b_b_a'][j] = d_ba.reshape(N_BLOCKS_B, BLOCK_B)
            g['b_b_i'][j] = d_bi.reshape(N_BLOCKS_B, BLOCK_B)
            g['b_lam'][j] = d_lam[0]
            g['b_w_a'][j], g['b_w_i'][j] = d_wa, d_wi
            g['b_w_in'][j] = _mm(sv['x_in_b'], dproj, ta=True, name=L + "rg_in_dw")
            dx = _mm(dproj, sv['w_in'], tb=True, add=dz1, add_scale=ALPHA, name=L + "rg_in_dx")
    return loss_row, dx, g


def _round_up(n, q):
    return -(-n // q) * q


def _pack(arrs, dtype, row_multiple, lead=0):
    pieces = []
    for a in arrs:
        flat = a.reshape(a.shape[:lead] + (-1,)).astype(dtype)
        n = flat.shape[-1]
        pieces.append(jnp.pad(flat, [(0, 0)] * lead + [(0, _round_up(n, LANES) - n)]))
    flat = jnp.concatenate(pieces, axis=-1)
    rows = _round_up(flat.shape[-1] // LANES, row_multiple)
    flat = jnp.pad(flat, [(0, 0)] * lead + [(0, rows * LANES - flat.shape[-1])])
    return flat.reshape(flat.shape[:lead] + (rows, LANES))


def _unpack(buf, shapes):
    lead = buf.shape[:-2]
    flat = buf.reshape(lead + (-1,))
    out, off = [], 0
    for shp in shapes:
        n = math.prod(shp)
        out.append(flat[..., off:off + n].reshape(lead + tuple(shp)))
        off += _round_up(n, LANES)
    return out


MESH = pl.DeviceIdType.MESH
ANY = pl.BlockSpec(memory_space=pl.ANY)
N_CHUNK = 8


def _all_gather_shards(buf):
    rows, lanes = buf.shape
    half = rows // 2
    ch = half // N_CHUNK
    n_ici = 3 * N_CHUNK

    def body(in_ref, out_ref, send_sems, recv_sems):
        x, y, c = lax.axis_index("x"), lax.axis_index("y"), lax.axis_index("c")
        sibling = (x, y, 1 - c)
        chips = [(1 - x, y), (x, 1 - y), (1 - x, 1 - y)]

        def piece(cx, cy, hc, q):
            return out_ref.at[2 * cx + cy, pl.ds(hc * half + q * ch, ch), :]

        def copy(k, src, dst, to):
            return pltpu.make_async_remote_copy(src_ref=src, dst_ref=dst, send_sem=send_sems.at[k],
                                                recv_sem=recv_sems.at[k], device_id=to, device_id_type=MESH)

        my_chunk = lambda q: in_ref.at[pl.ds(c * half + q * ch, ch), :]
        first, passed = [], []
        for k, chip in enumerate(chips):
            for q in range(N_CHUNK):
                first.append(copy(k * N_CHUNK + q, my_chunk(q), piece(x, y, c, q), (*chip, c)))
                passed.append(copy(n_ici + k * N_CHUNK + q, piece(*chip, c, q), piece(*chip, c, q), sibling))
        for cp in first:
            cp.start()
        for k, chip in enumerate(chips):
            for q in range(N_CHUNK):
                n = k * N_CHUNK + q
                copy(n, my_chunk(q), piece(*chip, c, q), (*chip, c)).wait_recv()
                passed[n].start()
        for k, chip in enumerate(chips):
            for q in range(N_CHUNK):
                copy(n_ici + k * N_CHUNK + q, my_chunk(q), piece(*chip, 1 - c, q), sibling).wait_recv()
        for cp in first + passed:
            cp.wait_send()

    others = pl.pallas_call(
        body, name="gather_weights", in_specs=[ANY], out_specs=ANY,
        out_shape=jax.ShapeDtypeStruct((N_CHIP, rows, lanes), buf.dtype),
        scratch_shapes=[pltpu.SemaphoreType.DMA((2 * n_ici,)), pltpu.SemaphoreType.DMA((2 * n_ici,))],
    )(buf)
    return lax.dynamic_update_slice(others, buf[None], (2 * lax.axis_index("x") + lax.axis_index("y"), 0, 0))


def _exchange_many(arrs, *, per_chip, name):
    na = len(arrs)
    flips = (2, 4, 6) if per_chip else tuple(range(1, N_DEV))
    index_of = (lambda x, y, c: 2 * x + y) if per_chip else (lambda x, y, c: 4 * x + 2 * y + c)

    def body(*refs):
        ins, outs = refs[:na], refs[na:2 * na]
        send_sems, recv_sems = refs[2 * na:]
        x, y, c = lax.axis_index("x"), lax.axis_index("y"), lax.axis_index("c")
        me = index_of(x, y, c)
        copies = []
        for nk, k in enumerate(flips):
            px, py, pc = x ^ (k >> 2), y ^ ((k >> 1) & 1), c ^ (k & 1)
            peer = index_of(px, py, pc)
            for a in range(na):
                n = nk * na + a
                copies.append(pltpu.make_async_remote_copy(
                    src_ref=ins[a].at[peer], dst_ref=outs[a].at[me], send_sem=send_sems.at[n],
                    recv_sem=recv_sems.at[n], device_id=(px, py, pc), device_id_type=MESH))
        for cp in copies:
            cp.start()
        for cp in copies:
            cp.wait()

    n_remote = len(flips) * na
    outs = pl.pallas_call(
        body, name=name, in_specs=[ANY] * na, out_specs=[ANY] * na,
        out_shape=[jax.ShapeDtypeStruct(a.shape, a.dtype) for a in arrs],
        scratch_shapes=[pltpu.SemaphoreType.DMA((n_remote,)), pltpu.SemaphoreType.DMA((n_remote,))],
    )(*arrs)
    me = index_of(lax.axis_index("x"), lax.axis_index("y"), lax.axis_index("c"))
    return [lax.dynamic_update_slice(o, lax.dynamic_index_in_dim(a, me, 0, keepdims=True), (me, 0, 0))
            for o, a in zip(outs, arrs)]


def _swap_with_sibling(arrs):
    na = len(arrs)

    def body(*refs):
        ins, outs = refs[:na], refs[na:2 * na]
        send_sems, recv_sems = refs[2 * na:]
        x, y, c = lax.axis_index("x"), lax.axis_index("y"), lax.axis_index("c")
        copies = [pltpu.make_async_remote_copy(
            src_ref=ins[a].at[:, 1 - c], dst_ref=outs[a], send_sem=send_sems.at[a], recv_sem=recv_sems.at[a],
            device_id=(x, y, 1 - c), device_id_type=MESH) for a in range(na)]
        for cp in copies:
            cp.start()
        for cp in copies:
            cp.wait()

    return pl.pallas_call(
        body, name="swap_pieces", in_specs=[ANY] * na, out_specs=[ANY] * na,
        out_shape=[jax.ShapeDtypeStruct((a.shape[0],) + a.shape[2:], a.dtype) for a in arrs],
        scratch_shapes=[pltpu.SemaphoreType.DMA((na,)), pltpu.SemaphoreType.DMA((na,))],
    )(*arrs)


def _add_pair(a, b, *, name):
    n, r, c = a.shape
    tm = _tile(r, max(2 * SUBLANES, SUM_BLOCK_ELEMS // c), 2 * SUBLANES)

    def body(a_ref, b_ref, o_ref):
        o_ref[...] = (a_ref[...].astype(F32) + b_ref[...].astype(F32)).astype(BF16)

    spec = pl.BlockSpec((n, tm, c), lambda i: (0, i, 0))
    return pl.pallas_call(
        body, name=name, grid=(r // tm,), in_specs=[spec, spec], out_specs=spec,
        out_shape=jax.ShapeDtypeStruct(a.shape, BF16), compiler_params=_cparams(("parallel",)),
    )(a, b)


def _share_many(reds, halves):
    na = len(reds)

    def body(*refs):
        ins, outs = refs[:na], refs[na:2 * na]
        send_sems, recv_sems = refs[2 * na:]
        x, y, c = lax.axis_index("x"), lax.axis_index("y"), lax.axis_index("c")
        copies = []
        for a in range(na):
            h = halves[a]
            copies.append(pltpu.make_async_remote_copy(
                src_ref=ins[a].at[pl.ds(0, h), :], dst_ref=outs[a].at[pl.ds(c * h, h), :], send_sem=send_sems.at[a],
                recv_sem=recv_sems.at[a], device_id=(x, y, 1 - c), device_id_type=MESH))
        for cp in copies:
            cp.start()
        for cp in copies:
            cp.wait()

    outs = pl.pallas_call(
        body, name="share_halves", in_specs=[ANY] * na, out_specs=[ANY] * na,
        out_shape=[jax.ShapeDtypeStruct((r.shape[0] + h, r.shape[1]), r.dtype) for r, h in zip(reds, halves)],
        scratch_shapes=[pltpu.SemaphoreType.DMA((na,)), pltpu.SemaphoreType.DMA((na,))],
    )(*reds)
    c = lax.axis_index("c")
    full = []
    for o, r, h in zip(outs, reds, halves):
        o = lax.dynamic_update_slice(o, r[:h], (c * h, 0))
        if r.shape[0] > h:
            o = lax.dynamic_update_slice(o, r[h:], (2 * h, 0))
        full.append(o)
    return full


SUM_BLOCK_ELEMS = 256 * 1024
ADAM_BLOCK_ELEMS = 256 * 1024


def _sum_slots(slots, *, name="sum_grads"):
    n, prow, lanes = slots.shape
    tm = _tile(prow, max(2 * SUBLANES, SUM_BLOCK_ELEMS // lanes), 2 * SUBLANES)

    def body(s_ref, o_ref):
        acc = s_ref[0].astype(F32)
        for s in range(1, n):
            acc = acc + s_ref[s].astype(F32)
        o_ref[...] = acc

    return pl.pallas_call(
        body, name=name, grid=(prow // tm,),
        in_specs=[pl.BlockSpec((n, tm, lanes), lambda i: (0, i, 0))],
        out_specs=pl.BlockSpec((tm, lanes), lambda i: (i, 0)),
        out_shape=jax.ShapeDtypeStruct((prow, lanes), F32),
        compiler_params=_cparams(("parallel",)),
    )(slots)


def _adamw(wp, gp, mp, vp, *, name="adamw"):
    rows, lanes = wp.shape
    tm = _tile(rows, max(SUBLANES, ADAM_BLOCK_ELEMS // lanes), SUBLANES)
    c1 = 1.0 / (1.0 - ADAM_B1 ** ADAM_STEP)
    c2 = 1.0 / (1.0 - ADAM_B2 ** ADAM_STEP)

    def body(w_ref, g_ref, m_ref, v_ref, d_ref, nm_ref, nv_ref):
        g = g_ref[...]
        m = ADAM_B1 * m_ref[...] + (1.0 - ADAM_B1) * g
        v = ADAM_B2 * v_ref[...] + (1.0 - ADAM_B2) * (g * g)
        m_hat = m * c1
        v_hat = v * c2
        d_ref[...] = -ADAM_LR * (m_hat / (jnp.sqrt(v_hat) + ADAM_EPS) + ADAM_WD * w_ref[...])
        nm_ref[...] = m
        nv_ref[...] = v

    spec = pl.BlockSpec((tm, lanes), lambda i: (i, 0))
    shp = jax.ShapeDtypeStruct((rows, lanes), F32)
    return pl.pallas_call(
        body, name=name, grid=(rows // tm,), in_specs=[spec] * 4, out_specs=[spec] * 3,
        out_shape=[shp, shp, shp], compiler_params=_cparams(("parallel",)),
    )(wp, gp, mp, vp)


AG_ROW_MULT = 2 * N_CHUNK * 16
GRAD_ROW_MULT = 2048
REP_ROW_MULT = 512


def _shard_to_front(a, axis):
    n = a.shape[axis]
    a = a.reshape(a.shape[:axis] + (N_CHIP, n // N_CHIP) + a.shape[axis + 1:])
    return jnp.moveaxis(a, axis, 0)


def _shards_to_full(a, axis):
    a = jnp.moveaxis(a, 0, axis)
    return a.reshape(a.shape[:axis] + (a.shape[axis] * a.shape[axis + 1],) + a.shape[axis + 2:])


def kernel(x, p, a_w_in, a_b_f, a_w_out, b_w_in, b_conv_w, b_conv_b, b_w_a, b_b_a, b_w_i, b_b_i, b_lam, b_w_out, f_w_up, f_conv_w, f_conv_b, f_w_down, ln1_g, ln1_b, ln2_g, ln2_b, ple_w, ple_gate_w, ple_gate_b, loss_target, m_a_w_in, m_a_b_f, m_a_w_out, m_b_w_in, m_b_conv_w, m_b_conv_b, m_b_w_a, m_b_b_a, m_b_w_i, m_b_b_i, m_b_lam, m_b_w_out, m_f_w_up, m_f_conv_w, m_f_conv_b, m_f_w_down, m_ln1_g, m_ln1_b, m_ln2_g, m_ln2_b, m_ple_w, m_ple_gate_w, m_ple_gate_b, v_a_w_in, v_a_b_f, v_a_w_out, v_b_w_in, v_b_conv_w, v_b_conv_b, v_b_w_a, v_b_b_a, v_b_w_i, v_b_b_i, v_b_lam, v_b_w_out, v_f_w_up, v_f_conv_w, v_f_conv_b, v_f_w_down, v_ln1_g, v_ln1_b, v_ln2_g, v_ln2_b, v_ple_w, v_ple_gate_w, v_ple_gate_b):
    args = dict(locals())
    swap = lambda n, a: jnp.swapaxes(a, -1, -2) if n in TRANSPOSED else a
    axis_of = lambda n: (3 - SHARD_AXIS[n]) if n in TRANSPOSED else SHARD_AXIS[n]
    local_w = {n: swap(n, args[n]) for n in WEIGHTS}
    local_m = {n: swap(n, args['m_' + n]) for n in WEIGHTS}
    local_v = {n: swap(n, args['v_' + n]) for n in WEIGHTS}

    as_pairs = lambda a: lax.bitcast_convert_type(a, BF16)
    from_pairs = lambda a: lax.bitcast_convert_type(a, F32)
    send = [local_w[n] for n in GATHER_BF16] + [as_pairs(local_w[n]) for n in GATHER_F32]
    gathered = _all_gather_shards(_pack(send, BF16, AG_ROW_MULT))
    shapes = [local_w[n].shape for n in GATHER_BF16] + [local_w[n].shape + (2,) for n in GATHER_F32]
    parts = _unpack(gathered, shapes)
    full_w = {}
    for n, part in zip(GATHER_BF16 + GATHER_F32, parts):
        if n in GATHER_F32:
            part = from_pairs(part)
        full_w[n] = _shards_to_full(part, axis_of(n))
    for n in REPLICATED:
        full_w[n] = local_w[n]

    loss_row, grad_x, g = _local_step(x[0], p[:, 0], loss_target[0], full_w)

    grads = {n: jnp.stack(g[n]) for n in SHARDED}
    small = [n for n in SHARDED if n not in NATIVE]
    shard_rows = _round_up(sum(_round_up(local_w[n].size, LANES) for n in small) // LANES, GRAD_ROW_MULT)
    half_rows = shard_rows // 2

    def native_pieces(n):
        a = _shard_to_front(grads[n], axis_of(n))
        return a.reshape(N_DEV, -1, a.shape[-1]).astype(BF16)

    sharded_g = _pack([_shard_to_front(grads[n], axis_of(n)) for n in small], BF16, GRAD_ROW_MULT, lead=1)
    rep_g = _pack([jnp.stack(g[n]) for n in REPLICATED] + [loss_row], F32, REP_ROW_MULT)
    rep_rows = rep_g.shape[0]
    rep_top = lax.reduce_precision(rep_g, 8, 7)
    rep_hi = rep_top.astype(BF16)
    rep_lo = (rep_g - rep_top).astype(BF16)
    packed_pieces = jnp.concatenate(
        [sharded_g.reshape(N_DEV, half_rows, LANES),
         jnp.broadcast_to(jnp.concatenate([rep_hi, rep_lo]), (N_DEV, 2 * rep_rows, LANES))], axis=1)
    c = lax.axis_index("c")
    mine = [native_pieces(n) for n in NATIVE]
    theirs = _swap_with_sibling([p.reshape((N_CHIP, 2) + p.shape[1:]) for p in mine])
    chip_sums = [_add_pair(lax.dynamic_index_in_dim(p.reshape((N_CHIP, 2) + p.shape[1:]), c, 1, keepdims=False), t,
                           name="pair_" + n) for p, t, n in zip(mine, theirs, NATIVE)]
    slots = _exchange_many(chip_sums, per_chip=True, name="exchange_grads")
    slots += _exchange_many([packed_pieces], per_chip=False, name="exchange_packed")
    reduced = [_sum_slots(s, name="sum_" + n) for s, n in zip(slots, NATIVE + ["packed"])]
    reduced[-1] = jnp.concatenate([reduced[-1][:half_rows], reduced[-1][half_rows:half_rows + rep_rows]
                                   + reduced[-1][half_rows + rep_rows:]])
    full_g = _share_many(reduced, [r.shape[0] for r in reduced[:-1]] + [half_rows])

    rep_shapes = [local_w[n].shape for n in REPLICATED] + [loss_row.shape]

    def packed(d):
        return jnp.concatenate([_pack([d[n] for n in small], F32, GRAD_ROW_MULT),
                                _pack([d[n] for n in REPLICATED] + [jnp.zeros_like(loss_row)], F32, REP_ROW_MULT)])

    def unpacked(buf):
        d = dict(zip(small, _unpack(buf[:shard_rows], [local_w[n].shape for n in small])))
        d.update(zip(REPLICATED, _unpack(buf[shard_rows:], rep_shapes)))
        return d

    outs = [unpacked(b) for b in (full_g[-1],) + tuple(_adamw(packed(local_w), full_g[-1], packed(local_m),
                                                              packed(local_v), name="adamw_packed"))]
    for n, gn in zip(NATIVE, full_g):
        two_d = lambda a: a.reshape(gn.shape)
        res = _adamw(two_d(local_w[n]), gn, two_d(local_m[n]), two_d(local_v[n]), name="adamw_" + n)
        for d, a in zip(outs, (gn,) + tuple(res)):
            d[n] = a.reshape(local_w[n].shape)
    loss = _unpack(full_g[-1][shard_rows:], rep_shapes)[-1][0, 0]
    return (loss, grad_x[None], *[swap(n, d[n]) for d in outs for n in WEIGHTS])
```

```python
import math

import jax
import jax.numpy as jnp
from jax import lax
from jax.experimental import pallas as pl
from jax.experimental.pallas import tpu as pltpu

F32 = jnp.float32
BF16 = jnp.bfloat16

D_MODEL = 1024
DEPTH = 4
N_HEADS = 16
HEAD_DIM = 64
N_BLOCKS_B = 8
BLOCK_B = 128
CONV_B = 4
LRU_C = 8.0
D_FF = 2816
CONV_F = 3
D_PLE = 256
LN_EPS = 1e-5
ALPHA = (2.0 * DEPTH) ** 0.25
ATTN_SCALE = 1.0 / math.sqrt(HEAD_DIM)

ADAM_LR = 0.001
ADAM_B1 = 0.9
ADAM_B2 = 0.999
ADAM_EPS = 1e-08
ADAM_WD = 0.01
ADAM_STEP = 10

LANES = 128
SUBLANES = 8
VMEM_LIMIT = 52 * 1024 * 1024
MM_VMEM_BUDGET = 40 * 1024 * 1024
NEG_BIG = -1e30
N_DEV = 8
N_CHIP = 4

WEIGHTS = ['a_w_in', 'a_b_f', 'a_w_out', 'b_w_in', 'b_conv_w', 'b_conv_b', 'b_w_a', 'b_b_a', 'b_w_i', 'b_b_i',
           'b_lam', 'b_w_out', 'f_w_up', 'f_conv_w', 'f_conv_b', 'f_w_down', 'ln1_g', 'ln1_b', 'ln2_g', 'ln2_b',
           'ple_w', 'ple_gate_w', 'ple_gate_b']
SHARD_AXIS = {'a_w_in': 2, 'a_w_out': 1, 'b_w_in': 2, 'b_conv_w': 2, 'b_conv_b': 1, 'b_lam': 1, 'b_w_out': 1,
              'f_w_up': 2, 'f_conv_w': 2, 'f_w_down': 1, 'ple_w': 2, 'ple_gate_w': 1}
TRANSPOSED = ('a_w_in',)
NATIVE = ['a_w_out', 'b_w_in', 'b_w_out', 'f_w_up', 'f_w_down', 'ple_w', 'ple_gate_w']
SHARDED = [n for n in WEIGHTS if n in SHARD_AXIS]
REPLICATED = [n for n in WEIGHTS if n not in SHARD_AXIS]
GATHER_BF16 = ['a_w_in', 'a_w_out', 'b_w_in', 'b_w_out', 'f_w_up', 'f_w_down', 'ple_w', 'ple_gate_w']
GATHER_F32 = ['b_conv_w', 'b_conv_b', 'b_lam', 'f_conv_w']


def _cparams(sem, vmem=VMEM_LIMIT):
    return pltpu.CompilerParams(dimension_semantics=sem, vmem_limit_bytes=vmem)


def _tile(n, cap, q=LANES):
    best = None
    for t in range(q, min(n, cap) + 1, q):
        if n % t == 0:
            best = t
    return best if best is not None else n


def _sigmoid(x):
    return 1.0 / (1.0 + jnp.exp(-x))


_GELU_C = math.sqrt(2.0 / math.pi)


def _gelu_and_grad(x):
    x2 = x * x
    t = jnp.tanh(_GELU_C * (x + 0.044715 * x * x2))
    cdf = 0.5 * (1.0 + t)
    g = x * cdf
    dg = cdf + x * 0.5 * (1.0 - t * t) * _GELU_C * (1.0 + 3.0 * 0.044715 * x2)
    return g, dg


def _gelu(x):
    t = jnp.tanh(_GELU_C * (x + 0.044715 * x * x * x))
    return x * (0.5 * (1.0 + t))


def _log1p(u):
    w = 1.0 + u
    d = w - 1.0
    return jnp.where(d == 0.0, u, jnp.log(w) * (u / jnp.where(d == 0.0, 1.0, d)))


def _softplus(y):
    return jnp.maximum(y, 0.0) + _log1p(jnp.exp(-jnp.abs(y)))


def _log_sigmoid(z):
    return -_softplus(-z)


def _neg_expm1(x):
    poly = x * (1.0 + x * (1.0 / 2 + x * (1.0 / 6 + x * (1.0 / 24 + x * (1.0 / 120 + x * (1.0 / 720 + x * (1.0 / 5040)))))))
    return -jnp.where(x > -0.25, poly, jnp.exp(x) - 1.0)


def _split3(x):
    hi = x.astype(BF16)
    r1 = x - hi.astype(F32)
    mid = r1.astype(BF16)
    lo = (r1 - mid.astype(F32)).astype(BF16)
    return hi, mid, lo


def _shift_down(x, halo, k):
    rolled = pltpu.roll(x, k, axis=0)
    hal = pltpu.roll(halo, k, axis=0)
    r8 = lax.broadcasted_iota(jnp.int32, halo.shape, 0)
    head = jnp.where(r8 < k, hal, rolled[:SUBLANES])
    if x.shape[0] == SUBLANES:
        return head
    return jnp.concatenate([head, rolled[SUBLANES:]], axis=0)


def _shift_up(x, nxt, k):
    n = x.shape[0]
    rolled = pltpu.roll(x, n - k, axis=0)
    nx = pltpu.roll(nxt, SUBLANES - k, axis=0)
    r8 = lax.broadcasted_iota(jnp.int32, nxt.shape, 0)
    tail = jnp.where(r8 >= SUBLANES - k, nx, rolled[n - SUBLANES:])
    if n == SUBLANES:
        return tail
    return jnp.concatenate([rolled[:n - SUBLANES], tail], axis=0)


def _split_spec(arr_ndim, part_cols, br, bc, idx):
    if arr_ndim == 3:
        nbh = part_cols // bc
        return pl.BlockSpec((None, br, bc), lambda i, j, k: (lax.div(idx(i, j, k)[1], nbh), idx(i, j, k)[0],
                                                             lax.rem(idx(i, j, k)[1], nbh)))
    return pl.BlockSpec((br, bc), lambda i, j, k: idx(i, j, k))


def _dims(arr):
    if arr.ndim == 3:
        return arr.shape[1], arr.shape[0] * arr.shape[2], arr.shape[2]
    return arr.shape[0], arr.shape[1], arr.shape[1]


def _mm(a, b, *, ta=False, tb=False, out_dtype=F32, out_split=1, add=None, add_scale=1.0,
        tm_cap=1024, tn_cap=1024, tk_cap=1408, name):
    ar, ac, apart = _dims(a)
    br_, bc_, bpart = _dims(b)
    m, kdim = (ac, ar) if ta else (ar, ac)
    kdim_b, n = (bc_, br_) if tb else (br_, bc_)
    assert kdim == kdim_b, (name, a.shape, b.shape)
    tm = _tile(apart, tm_cap) if ta else _tile(m, tm_cap, SUBLANES)
    tn = _tile(n, tn_cap, SUBLANES) if tb else _tile(math.gcd(bpart, n // out_split), tn_cap)
    if ta:
        tk = _tile(kdim, 1024, 2 * SUBLANES)
    elif tb:
        tk = _tile(math.gcd(apart, bpart), tk_cap)
    else:
        tk = _tile(apart, tk_cap)

    def vmem_bytes(tm_, tk_):
        per_step = 2 * (tm_ * tk_ + tk_ * tn) + tm_ * tn * (jnp.dtype(out_dtype).itemsize + (4 if add is not None else 0))
        return 2 * per_step + (4 * tm_ * tn if kdim // tk_ > 1 else 0)

    if ta and kdim % (2 * tk) == 0 and vmem_bytes(tm, 2 * tk) <= MM_VMEM_BUDGET:
        tk *= 2
    elif not ta and m % (2 * tm) == 0 and vmem_bytes(2 * tm, tk) <= MM_VMEM_BUDGET:
        tm *= 2
    assert m % tm == 0 and n % tn == 0 and kdim % tk == 0, (name, m, n, kdim, tm, tn, tk)
    nk = kdim // tk
    a_spec = (_split_spec(a.ndim, apart, tk, tm, lambda i, j, k: (k, i)) if ta
              else _split_spec(a.ndim, apart, tm, tk, lambda i, j, k: (i, k)))
    b_spec = (_split_spec(b.ndim, bpart, tn, tk, lambda i, j, k: (j, k)) if tb
              else _split_spec(b.ndim, bpart, tk, tn, lambda i, j, k: (k, j)))
    if out_split > 1:
        out_shape = jax.ShapeDtypeStruct((out_split, m, n // out_split), out_dtype)
        o_spec = _split_spec(3, n // out_split, tm, tn, lambda i, j, k: (i, j))
    else:
        out_shape = jax.ShapeDtypeStruct((m, n), out_dtype)
        o_spec = pl.BlockSpec((tm, tn), lambda i, j, k: (i, j))
    dn = (((0 if ta else 1,), (1 if tb else 0,)), ((), ()))
    in_specs = [a_spec, b_spec]
    args = [a, b]
    if add is not None:
        in_specs.append(pl.BlockSpec((tm, tn), lambda i, j, k: (i, j)))
        args.append(add)
    use_acc = nk > 1
    has_add = add is not None

    def body(*refs):
        a_ref, b_ref = refs[0], refs[1]
        add_ref = refs[2] if has_add else None
        o_ref = refs[3] if has_add else refs[2]
        part = lax.dot_general(a_ref[...], b_ref[...], dn, preferred_element_type=F32)

        def finish(acc):
            if has_add:
                acc = acc + add_scale * add_ref[...]
            o_ref[...] = acc.astype(out_dtype)

        if not use_acc:
            finish(part)
        else:
            acc_ref = refs[-1]
            k = pl.program_id(2)

            @pl.when(k == 0)
            def _():
                acc_ref[...] = part

            @pl.when(k > 0)
            def _():
                acc_ref[...] += part

            @pl.when(k == nk - 1)
            def _():
                finish(acc_ref[...])

    return pl.pallas_call(
        body, name=name, grid=(m // tm, n // tn, nk), in_specs=in_specs, out_specs=o_spec, out_shape=out_shape,
        scratch_shapes=[pltpu.VMEM((tm, tn), F32)] if use_acc else [],
        compiler_params=_cparams(("parallel", "parallel", "arbitrary")),
    )(*args)


def _ln_fwd(x, m, g, b, *, name, tm=512):
    t, d = x.shape

    def body(x_ref, m_ref, g_ref, b_ref, y_ref, yb_ref, xhat_ref, rstd_ref):
        z = ALPHA * x_ref[...] + m_ref[...]
        mu = jnp.mean(z, axis=-1, keepdims=True)
        zc = z - mu
        var = jnp.mean(zc * zc, axis=-1, keepdims=True)
        rstd = lax.rsqrt(var + LN_EPS)
        xhat = zc * rstd
        y = xhat * g_ref[...] + b_ref[...]
        y_ref[...] = y
        yb_ref[...] = y.astype(BF16)
        xhat_ref[...] = xhat
        rstd_ref[...] = rstd

    row = pl.BlockSpec((tm, d), lambda i: (i, 0))
    vec = pl.BlockSpec((1, d), lambda i: (0, 0))
    return pl.pallas_call(
        body, name=name, grid=(t // tm,), in_specs=[row, row, vec, vec],
        out_specs=[row, row, row, pl.BlockSpec((tm, 1), lambda i: (i, 0))],
        out_shape=[jax.ShapeDtypeStruct((t, d), F32), jax.ShapeDtypeStruct((t, d), BF16),
                   jax.ShapeDtypeStruct((t, d), F32), jax.ShapeDtypeStruct((t, 1), F32)],
        compiler_params=_cparams(("parallel",)),
    )(x, m, g, b)


def _ln_bwd(dy, xhat, rstd, g, *, name, tm=512):
    t, d = dy.shape

    def body(dy_ref, xhat_ref, rstd_ref, g_ref, dz_ref, dzb_ref, dg_ref, db_ref):
        @pl.when(pl.program_id(0) == 0)
        def _():
            dg_ref[...] = jnp.zeros_like(dg_ref)
            db_ref[...] = jnp.zeros_like(db_ref)

        dyv = dy_ref[...]
        xh = xhat_ref[...]
        dg_ref[...] += jnp.sum(dyv * xh, axis=0, keepdims=True)
        db_ref[...] += jnp.sum(dyv, axis=0, keepdims=True)
        dxh = dyv * g_ref[...]
        m1 = jnp.mean(dxh, axis=-1, keepdims=True)
        m2 = jnp.mean(dxh * xh, axis=-1, keepdims=True)
        dz = rstd_ref[...] * (dxh - m1 - xh * m2)
        dz_ref[...] = dz
        dzb_ref[...] = dz.astype(BF16)

    row = pl.BlockSpec((tm, d), lambda i: (i, 0))
    vec = pl.BlockSpec((1, d), lambda i: (0, 0))
    return pl.pallas_call(
        body, name=name, grid=(t // tm,), in_specs=[row, row, pl.BlockSpec((tm, 1), lambda i: (i, 0)), vec],
        out_specs=[row, row, vec, vec],
        out_shape=[jax.ShapeDtypeStruct((t, d), F32), jax.ShapeDtypeStruct((t, d), BF16),
                   jax.ShapeDtypeStruct((1, d), F32), jax.ShapeDtypeStruct((1, d), F32)],
        compiler_params=_cparams(("arbitrary",)),
    )(dy, xhat, rstd, g)


def _ple_fwd(x2, gl, pe, gate_b, *, name, tm=512):
    t, d = x2.shape

    def body(x_ref, gl_ref, pe_ref, b_ref, y_ref, yb_ref):
        y = x_ref[...] + _sigmoid(gl_ref[...] + b_ref[...]) * pe_ref[...]
        y_ref[...] = y
        yb_ref[...] = y.astype(BF16)

    row = pl.BlockSpec((tm, d), lambda i: (i, 0))
    vec = pl.BlockSpec((1, d), lambda i: (0, 0))
    return pl.pallas_call(
        body, name=name, grid=(t // tm,), in_specs=[row, row, row, vec], out_specs=[row, row],
        out_shape=[jax.ShapeDtypeStruct((t, d), F32), jax.ShapeDtypeStruct((t, d), BF16)],
        compiler_params=_cparams(("parallel",)),
    )(x2, gl, pe, gate_b)


def _ple_bwd(dx3, gl, pe, gate_b, *, name, tm=512):
    t, d = dx3.shape

    def body(dx_ref, gl_ref, pe_ref, b_ref, dgl_ref, dpe_ref, db_ref):
        @pl.when(pl.program_id(0) == 0)
        def _():
            db_ref[...] = jnp.zeros_like(db_ref)

        dx = dx_ref[...]
        gt = _sigmoid(gl_ref[...] + b_ref[...])
        dgl = dx * pe_ref[...] * gt * (1.0 - gt)
        db_ref[...] += jnp.sum(dgl, axis=0, keepdims=True)
        dgl_ref[...] = dgl.astype(BF16)
        dpe_ref[...] = (dx * gt).astype(BF16)

    row = pl.BlockSpec((tm, d), lambda i: (i, 0))
    vec = pl.BlockSpec((1, d), lambda i: (0, 0))
    return pl.pallas_call(
        body, name=name, grid=(t // tm,), in_specs=[row, row, row, vec], out_specs=[row, row, vec],
        out_shape=[jax.ShapeDtypeStruct((t, d), BF16), jax.ShapeDtypeStruct((t, d), BF16),
                   jax.ShapeDtypeStruct((1, d), F32)],
        compiler_params=_cparams(("arbitrary",)),
    )(dx3, gl, pe, gate_b)


def _loss_bwd(y, tgt, *, name, tm=512):
    t, d = y.shape

    def body(y_ref, t_ref, dy_ref, l_ref):
        @pl.when(pl.program_id(0) == 0)
        def _():
            l_ref[...] = jnp.zeros_like(l_ref)

        err = y_ref[...] - t_ref[...]
        dy_ref[...] = err * (1.0 / d)
        part = jnp.sum(jnp.sum(err * err, axis=0, keepdims=True), axis=1, keepdims=True) * (0.5 / d)
        l_ref[...] += jnp.broadcast_to(part, l_ref.shape)

    row = pl.BlockSpec((tm, d), lambda i: (i, 0))
    return pl.pallas_call(
        body, name=name, grid=(t // tm,), in_specs=[row, row],
        out_specs=[row, pl.BlockSpec((1, LANES), lambda i: (0, 0))],
        out_shape=[jax.ShapeDtypeStruct((t, d), F32), jax.ShapeDtypeStruct((1, LANES), F32)],
        compiler_params=_cparams(("arbitrary",)),
    )(y, tgt)


def _conv_causal(x, halo, w_ref, b, kw):
    acc = x * w_ref[kw - 1:kw, :] + b
    for k in range(kw - 1):
        acc = acc + _shift_down(x, halo, kw - 1 - k) * w_ref[k:k + 1, :]
    return acc


def _ffn_act_fwd(hdn, conv_w, conv_b, *, name, tm=512):
    _, t, f = hdn.shape
    tc = _tile(f, 1408)
    hb = tm // SUBLANES

    def body(h_ref, halo_ref, w_ref, b_ref, a_ref):
        first = pl.program_id(1) == 0
        parts = []
        for s in range(2):
            halo = jnp.where(first, 0.0, halo_ref[s])
            parts.append(_conv_causal(h_ref[s], halo, w_ref.at[s], b_ref[s], CONV_F))
        a_ref[...] = (_gelu(parts[1]) * parts[0]).astype(BF16)

    return pl.pallas_call(
        body, name=name, grid=(f // tc, t // tm),
        in_specs=[pl.BlockSpec((2, tm, tc), lambda j, i: (0, i, j)),
                  pl.BlockSpec((2, SUBLANES, tc), lambda j, i: (0, jnp.maximum(i * hb - 1, 0), j)),
                  pl.BlockSpec((2, CONV_F, tc), lambda j, i: (0, 0, j)),
                  pl.BlockSpec((2, 1, tc), lambda j, i: (0, 0, j))],
        out_specs=pl.BlockSpec((tm, tc), lambda j, i: (i, j)),
        out_shape=jax.ShapeDtypeStruct((t, f), BF16),
        compiler_params=_cparams(("parallel", "arbitrary")),
    )(hdn, hdn, conv_w, conv_b)


def _ffn_act_bwd(da, hdn, conv_w, conv_b, *, name, tm=256):
    _, t, f = hdn.shape
    tc = _tile(f, 1408)
    hb = tm // SUBLANES
    nr = t // tm

    strip = 2 * SUBLANES
    ns = tm // strip
    pieces = [(c0, min(2 * LANES, tc - c0)) for c0 in range(0, tc, 2 * LANES)]

    def body(da_ref, h_ref, halo_ref, w_ref, b_ref, dh_ref, dw_ref, db_ref, carry_ref, dw_acc, db_acc):
        i = pl.program_id(1)
        r = nr - 1 - i

        @pl.when(i == 0)
        def _():
            carry_ref[...] = jnp.zeros_like(carry_ref)
            dw_acc[...] = jnp.zeros_like(dw_acc)
            db_acc[...] = jnp.zeros_like(db_acc)

        def do_strip(si, carry):
            s = ns - 1 - si
            r0 = pl.multiple_of(s * strip, strip)
            above = pl.multiple_of(jnp.maximum(s * strip - SUBLANES, 0), SUBLANES)
            for c0, cw in pieces:
                cols = slice(c0, c0 + cw)
                xs, hcs = [], []
                for part in range(2):
                    x = h_ref[part, pl.ds(r0, strip), cols]
                    halo = jnp.where(r == 0, 0.0, halo_ref[part, :, cols])
                    prev = jnp.where(s == 0, halo, h_ref[part, pl.ds(above, SUBLANES), cols])
                    sh = [_shift_down(x, prev, k) for k in range(1, CONV_F)]
                    hc = x * w_ref[part, CONV_F - 1:CONV_F, cols] + b_ref[part, :, cols]
                    for k in range(1, CONV_F):
                        hc = hc + sh[k - 1] * w_ref[part, CONV_F - 1 - k:CONV_F - k, cols]
                    xs.append([x] + sh)
                    hcs.append(hc)
                g, dg = _gelu_and_grad(hcs[1])
                dav = da_ref[pl.ds(r0, strip), cols]
                dhc = [dav * g, dav * hcs[0] * dg]
                for part in range(2):
                    d = dhc[part]
                    nxt = carry_ref[part, :, cols]
                    db_acc[part, :, cols] += d
                    out = d * w_ref[part, CONV_F - 1:CONV_F, cols]
                    for k in range(CONV_F):
                        dw_acc[part, CONV_F - 1 - k, :, cols] += d * xs[part][k]
                        if k:
                            out = out + _shift_up(d, nxt, k) * w_ref[part, CONV_F - 1 - k:CONV_F - k, cols]
                    dh_ref[part, pl.ds(r0, strip), cols] = out.astype(BF16)
                    carry_ref[part, :, cols] = d[:SUBLANES]
            return carry

        lax.fori_loop(0, ns, do_strip, 0)

        @pl.when(i == nr - 1)
        def _():
            dw_ref[...] = jnp.sum(dw_acc[...], axis=2)
            db_ref[...] = jnp.sum(db_acc[...], axis=1, keepdims=True)

    return pl.pallas_call(
        body, name=name, grid=(f // tc, nr),
        in_specs=[pl.BlockSpec((tm, tc), lambda j, i: (nr - 1 - i, j)),
                  pl.BlockSpec((2, tm, tc), lambda j, i: (0, nr - 1 - i, j)),
                  pl.BlockSpec((2, SUBLANES, tc), lambda j, i: (0, jnp.maximum((nr - 1 - i) * hb - 1, 0), j)),
                  pl.BlockSpec((2, CONV_F, tc), lambda j, i: (0, 0, j)),
                  pl.BlockSpec((2, 1, tc), lambda j, i: (0, 0, j))],
        out_specs=[pl.BlockSpec((2, tm, tc), lambda j, i: (0, nr - 1 - i, j)),
                   pl.BlockSpec((2, CONV_F, tc), lambda j, i: (0, 0, j)),
                   pl.BlockSpec((2, 1, tc), lambda j, i: (0, 0, j))],
        out_shape=[jax.ShapeDtypeStruct((2, t, f), BF16), jax.ShapeDtypeStruct((2, CONV_F, f), F32),
                   jax.ShapeDtypeStruct((2, 1, f), F32)],
        scratch_shapes=[pltpu.VMEM((2, SUBLANES, tc), F32), pltpu.VMEM((2, CONV_F, 2 * SUBLANES, tc), F32),
                        pltpu.VMEM((2, 2 * SUBLANES, tc), F32)],
        compiler_params=_cparams(("arbitrary", "arbitrary")),
    )(da, hdn, hdn, conv_w, conv_b)


def _fgate_fwd(xb, w_f, b_f, *, name, tm=256):
    t, d = xb.shape

    def body(x_ref, w_ref, b_ref, z_ref, c_ref, carry_ref):
        @pl.when(pl.program_id(0) == 0)
        def _():
            carry_ref[...] = jnp.zeros_like(carry_ref)

        z = jnp.dot(x_ref[...], w_ref[...], preferred_element_type=F32) + b_ref[...]
        z_ref[...] = z
        ls = _log_sigmoid(z)
        rr = lax.broadcasted_iota(jnp.int32, (tm, tm), 0)
        cc = lax.broadcasted_iota(jnp.int32, (tm, tm), 1)
        tri = (cc <= rr).astype(BF16)
        cum = carry_ref[...]
        for piece in _split3(ls):
            cum = cum + jnp.dot(tri, piece, preferred_element_type=F32)
        c_ref[...] = cum
        carry_ref[...] = cum[tm - 1:tm, :]

    return pl.pallas_call(
        body, name=name, grid=(t // tm,),
        in_specs=[pl.BlockSpec((tm, d), lambda i: (i, 0)), pl.BlockSpec((d, LANES), lambda i: (0, 0)),
                  pl.BlockSpec((1, LANES), lambda i: (0, 0))],
        out_specs=[pl.BlockSpec((tm, LANES), lambda i: (i, 0)), pl.BlockSpec((tm, LANES), lambda i: (i, 0))],
        out_shape=[jax.ShapeDtypeStruct((t, LANES), F32), jax.ShapeDtypeStruct((t, LANES), F32)],
        scratch_shapes=[pltpu.VMEM((1, LANES), F32)],
        compiler_params=_cparams(("arbitrary",)),
    )(xb, w_f, b_f)


def _fgate_bwd(dc, z, *, name, tm=256):
    t = dc.shape[0]
    nr = t // tm

    def body(dc_ref, z_ref, dz_ref, db_ref, carry_ref):
        @pl.when(pl.program_id(0) == 0)
        def _():
            carry_ref[...] = jnp.zeros_like(carry_ref)
            db_ref[...] = jnp.zeros_like(db_ref)

        rr = lax.broadcasted_iota(jnp.int32, (tm, tm), 0)
        cc = lax.broadcasted_iota(jnp.int32, (tm, tm), 1)
        tri = (cc >= rr).astype(BF16)
        cum = carry_ref[...]
        for piece in _split3(dc_ref[...]):
            cum = cum + jnp.dot(tri, piece, preferred_element_type=F32)
        carry_ref[...] = cum[0:1, :]
        dz = cum * _sigmoid(-z_ref[...])
        db_ref[...] += jnp.sum(dz, axis=0, keepdims=True)
        dz_ref[...] = dz.astype(BF16)

    rev = pl.BlockSpec((tm, LANES), lambda i: (nr - 1 - i, 0))
    return pl.pallas_call(
        body, name=name, grid=(nr,), in_specs=[rev, rev],
        out_specs=[rev, pl.BlockSpec((1, LANES), lambda i: (0, 0))],
        out_shape=[jax.ShapeDtypeStruct((t, LANES), BF16), jax.ShapeDtypeStruct((1, LANES), F32)],
        scratch_shapes=[pltpu.VMEM((1, LANES), F32)],
        compiler_params=_cparams(("arbitrary",)),
    )(dc, z)


HM = 2 * HEAD_DIM
LANE_C, LANE_ONE, LANE_LSE = 64, 67, 70
Q_SUB = 256


def _three_parts(col, sign=1.0):
    col = sign * col
    hi = col.astype(BF16).astype(F32)
    r1 = col - hi
    mid = r1.astype(BF16).astype(F32)
    lo = (r1 - mid).astype(BF16).astype(F32)
    return hi, mid, lo


def _fill_lanes(base, lane, first, parts):
    out = base
    for n, part in enumerate(parts):
        out = jnp.where(lane == first + n, part, out)
    return out


def _hm_weight_t(w_t):
    rows, k = w_t.shape
    h = rows // HEAD_DIM
    return jnp.pad(w_t.reshape(h, HEAD_DIM, k), ((0, 0), (0, HM - HEAD_DIM), (0, 0))).reshape(h * HM, k)


def _hm_unpad_t(g_t):
    rows, k = g_t.shape
    h = rows // HM
    return g_t.reshape(h, HM, k)[:, :HEAD_DIM].reshape(h * HEAD_DIM, k)


def _attn_prep(qkv, c, *, name, tm=256):
    t, w_in = qkv.shape
    nh = w_in // (3 * HEAD_DIM)
    w = 3 * nh * HM

    def body(x_ref, c_ref, o_ref):
        lane = lax.broadcasted_iota(jnp.int32, (1, HM), 1)
        cblk = c_ref[...]
        for h in range(nh):
            ch = cblk[:, h:h + 1]
            pos = _three_parts(ch)
            neg = _three_parts(ch, -1.0)
            for part in range(3):
                col = (part * nh + h) * HM
                pair = (part * nh + h - h % 2) * HEAD_DIM
                x = x_ref[:, pair:pair + HM].astype(F32)
                if h % 2:
                    x = pltpu.roll(x, HEAD_DIM, axis=1)
                x = jnp.where(lane < HEAD_DIM, x, 0.0)
                if part == 0:
                    y = _fill_lanes(_fill_lanes(x * ATTN_SCALE, lane, LANE_C, pos), lane, LANE_ONE, (1.0, 1.0, 1.0))
                elif part == 1:
                    y = _fill_lanes(_fill_lanes(x, lane, LANE_C, (1.0, 1.0, 1.0)), lane, LANE_ONE, neg)
                    y = _fill_lanes(y, lane, LANE_LSE, (1.0, 1.0, 1.0))
                else:
                    y = _fill_lanes(x, lane, LANE_C, (1.0, 1.0, 1.0))
                o_ref[:, col:col + HM] = y.astype(BF16)

    return pl.pallas_call(
        body, name=name, grid=(t // tm,),
        in_specs=[pl.BlockSpec((tm, w_in), lambda i: (i, 0)), pl.BlockSpec((tm, LANES), lambda i: (i, 0))],
        out_specs=pl.BlockSpec((tm, w), lambda i: (i, 0)),
        out_shape=jax.ShapeDtypeStruct((t, w), BF16),
        compiler_params=_cparams(("parallel",)),
    )(qkv, c)


def _attn_compact(parts, *, name, tm=512):
    t, w = parts[0].shape
    nh = w // HM

    def body(a_ref, b_ref, c_ref, o_ref):
        lane = lax.broadcasted_iota(jnp.int32, (1, HM), 1)
        for part, ref in enumerate((a_ref, b_ref, c_ref)):
            for h in range(0, nh, 2):
                even = ref[:, h * HM:(h + 1) * HM]
                odd = pltpu.roll(ref[:, (h + 1) * HM:(h + 2) * HM].astype(F32), HEAD_DIM, axis=1).astype(BF16)
                col = (part * nh + h) * HEAD_DIM
                o_ref[:, col:col + HM] = jnp.where(lane < HEAD_DIM, even, odd)

    row = pl.BlockSpec((tm, w), lambda i: (i, 0))
    return pl.pallas_call(
        body, name=name, grid=(t // tm,), in_specs=[row, row, row],
        out_specs=pl.BlockSpec((tm, 3 * nh * HEAD_DIM), lambda i: (i, 0)),
        out_shape=jax.ShapeDtypeStruct((t, 3 * nh * HEAD_DIM), BF16),
        compiler_params=_cparams(("parallel",)),
    )(*parts)


def _flash_fwd_hm(qkva, *, name, tq=1024):
    t, w = qkva.shape
    nh = w // (3 * HM)
    nq = t // tq
    nsub = tq // Q_SUB

    def body(q_ref, k_ref, v_ref, o_ref, qb_ref, s_scr, p_scr, m_scr, acc_scr):
        i = pl.program_id(1)
        lane = lax.broadcasted_iota(jnp.int32, (1, HM), 1)
        rr = lax.broadcasted_iota(jnp.int32, (Q_SUB, tq), 0)
        cc = lax.broadcasted_iota(jnp.int32, (Q_SUB, tq), 1)
        sub = lambda r: slice(r * Q_SUB, (r + 1) * Q_SUB)
        nt = (((1,), (1,)), ((), ()))

        def scores(j):
            kj = k_ref[pl.ds(pl.multiple_of(j * tq, tq), tq), :]
            for r in range(nsub):
                s_scr[sub(r), :] = lax.dot_general(q_ref[sub(r), :], kj, nt, preferred_element_type=F32)

        def step(jp, masked, with_pv=True):
            vj = v_ref[pl.ds(pl.multiple_of(jp * tq, tq), tq), :]
            for r in range(nsub):
                if masked:
                    s_scr[sub(r), :] = jnp.where(cc <= rr + r * Q_SUB, s_scr[sub(r), :], NEG_BIG)
                m_old = m_scr[sub(r), :]
                m_new = jnp.maximum(m_old, jnp.max(s_scr[sub(r), :], axis=1, keepdims=True))
                m_scr[sub(r), :] = m_new
                if with_pv:
                    pv = jnp.dot(p_scr[sub(r), :], vj, preferred_element_type=F32)
                    acc_scr[sub(r), :] = (acc_scr[sub(r), :] + pv) * jnp.exp(m_old - m_new)
                p_scr[sub(r), :] = jnp.exp(s_scr[sub(r), :] - m_new).astype(BF16)

        def kv_step(j, carry):
            step(j - 1, False)
            scores(j + 1)
            return carry

        p_scr[...] = jnp.zeros_like(p_scr)
        m_scr[...] = jnp.full_like(m_scr, NEG_BIG)
        acc_scr[...] = jnp.zeros_like(acc_scr)
        scores(0)

        @pl.when(i > 0)
        def _():
            step(0, False, with_pv=False)
            scores(1)

        lax.fori_loop(1, i, kv_step, 0)
        step(jnp.maximum(i - 1, 0), True)
        vi = v_ref[pl.ds(pl.multiple_of(i * tq, tq), tq), :]
        for r in range(nsub):
            acc = acc_scr[sub(r), :] + jnp.dot(p_scr[sub(r), :], vi, preferred_element_type=F32)
            l = jnp.sum(jnp.where(lane == LANE_C, acc, 0.0), axis=1, keepdims=True)
            o_ref[sub(r), :] = jnp.where(lane < HEAD_DIM, acc / l, 0.0).astype(BF16)
            lse = m_scr[sub(r), :] + jnp.log(l)
            qb = _fill_lanes(q_ref[sub(r), :].astype(F32), lane, LANE_LSE, _three_parts(lse, -1.0))
            qb_ref[sub(r), :] = qb.astype(BF16)

    blk = lambda part: pl.BlockSpec((t, HM), lambda h, i: (0, part * nh + h))
    tile = pl.BlockSpec((tq, HM), lambda h, i: (i, h))
    return pl.pallas_call(
        body, name=name, grid=(nh, nq), in_specs=[tile, blk(1), blk(2)], out_specs=[tile, tile],
        out_shape=[jax.ShapeDtypeStruct((t, nh * HM), BF16), jax.ShapeDtypeStruct((t, nh * HM), BF16)],
        scratch_shapes=[pltpu.VMEM((tq, tq), F32), pltpu.VMEM((tq, tq), BF16), pltpu.VMEM((tq, 1), F32),
                        pltpu.VMEM((tq, HM), F32)],
        compiler_params=_cparams(("parallel", "arbitrary")),
    )(qkva, qkva, qkva)


def _attn_prep_bwd(o, do, *, name, tm=512):
    t, w = o.shape
    nh = w // HM

    def body(o_ref, do_ref, out_ref):
        lane = lax.broadcasted_iota(jnp.int32, (1, HM), 1)
        for h in range(nh):
            cols = slice(h * HM, (h + 1) * HM)
            dov = do_ref[:, cols].astype(F32)
            delta = jnp.sum(o_ref[:, cols].astype(F32) * dov, axis=1, keepdims=True)
            out_ref[:, cols] = _fill_lanes(dov, lane, LANE_C, _three_parts(delta, -1.0)).astype(BF16)

    row = pl.BlockSpec((tm, w), lambda i: (i, 0))
    return pl.pallas_call(
        body, name=name, grid=(t // tm,), in_specs=[row, row], out_specs=row,
        out_shape=jax.ShapeDtypeStruct((t, w), BF16), compiler_params=_cparams(("parallel",)),
    )(o, do)


def _flash_bwd_hm(qb, doa, qkva, *, name, tq=512, tqc=512):
    t, w = qb.shape
    nh = w // HM
    nq = t // tq
    nqc = t // tqc
    nsub = tqc // Q_SUB
    grp = slice(LANE_C, LANE_C + SUBLANES)

    def body(q_ref, do_ref, k_ref, v_ref, dq_ref, dk_ref, dv_ref, dcq_ref, dcj_ref,
             dq_acc, st_scr, dpt_scr, pt_scr, ds_scr, dk_scr, dv_scr):
        j = pl.program_id(1)
        kv_minus_q = (lax.broadcasted_iota(jnp.int32, (tq, Q_SUB), 0)
                      - lax.broadcasted_iota(jnp.int32, (tq, Q_SUB), 1))
        sub = lambda r: slice(r * Q_SUB, (r + 1) * Q_SUB)
        nt = (((1,), (1,)), ((), ()))
        tn = (((0,), (0,)), ((), ()))
        first = lax.div(j * tq, tqc)

        @pl.when(j == 0)
        def _():
            dq_acc[...] = jnp.zeros_like(dq_acc)

        def rows_of(i, r):
            return pl.ds(pl.multiple_of(i * tqc + r * Q_SUB, Q_SUB), Q_SUB)

        def scores(i):
            for r in range(nsub):
                st_scr[:, sub(r)] = lax.dot_general(k_ref[...], q_ref[rows_of(i, r), :], nt, preferred_element_type=F32)
                dpt_scr[:, sub(r)] = lax.dot_general(v_ref[...], do_ref[rows_of(i, r), :], nt,
                                                     preferred_element_type=F32)

        def products(ip):
            for r in range(nsub):
                dv_scr[...] += jnp.dot(pt_scr[:, sub(r)], do_ref[rows_of(ip, r), :], preferred_element_type=F32)
                dk_scr[...] += jnp.dot(ds_scr[:, sub(r)], q_ref[rows_of(ip, r), :], preferred_element_type=F32)
                dq_acc[rows_of(ip, r), :] += lax.dot_general(ds_scr[:, sub(r)], k_ref[...], tn,
                                                             preferred_element_type=F32)

        def probabilities(masked):
            for r in range(nsub):
                st = st_scr[:, sub(r)]
                if masked:
                    st = jnp.where(kv_minus_q <= first * tqc + r * Q_SUB - j * tq, st, NEG_BIG)
                pt = jnp.exp(st)
                pt_scr[:, sub(r)] = pt.astype(BF16)
                ds_scr[:, sub(r)] = (pt * dpt_scr[:, sub(r)]).astype(BF16)

        def q_step(i, carry):
            products(i - 1)
            probabilities(False)
            scores(jnp.minimum(i + 1, nqc - 1))
            return carry

        dk_scr[...] = jnp.zeros_like(dk_scr)
        dv_scr[...] = jnp.zeros_like(dv_scr)
        scores(first)
        probabilities(True)
        scores(jnp.minimum(first + 1, nqc - 1))
        lax.fori_loop(first + 1, nqc, q_step, 0)
        products(nqc - 1)
        dk = dk_scr[...]
        dk_ref[...] = dk.astype(BF16)
        dv_ref[...] = dv_scr[...].astype(BF16)
        dcj_ref[...] = dk.T[grp, :]

        @pl.when(j == nq - 1)
        def _():
            dq_ref[...] = (dq_acc[...] * ATTN_SCALE).astype(BF16)
            for cidx in range(nq):
                rows = slice(cidx * tq, (cidx + 1) * tq)
                dcq_ref[:, rows] = dq_acc[rows, :].T[grp, :]

    full = pl.BlockSpec((t, HM), lambda h, j: (0, h))
    ktile = lambda part: pl.BlockSpec((tq, HM), lambda h, j: (j, part * nh + h))
    tile = pl.BlockSpec((tq, HM), lambda h, j: (j, h))
    hm_shape = jax.ShapeDtypeStruct((t, w), BF16)
    row_shape = jax.ShapeDtypeStruct((nh, SUBLANES, t), F32)
    return pl.pallas_call(
        body, name=name, grid=(nh, nq),
        in_specs=[pl.BlockSpec((t, HM), lambda h, j: (0, h), pipeline_mode=pl.Buffered(1)),
                  pl.BlockSpec((t, HM), lambda h, j: (0, h), pipeline_mode=pl.Buffered(1)), ktile(1), ktile(2)],
        out_specs=[full, tile, tile, pl.BlockSpec((None, SUBLANES, t), lambda h, j: (h, 0, 0)),
                   pl.BlockSpec((None, SUBLANES, tq), lambda h, j: (h, 0, j))],
        out_shape=[hm_shape, hm_shape, hm_shape, row_shape, row_shape],
        scratch_shapes=[pltpu.VMEM((t, HM), F32), pltpu.VMEM((tq, tqc), F32), pltpu.VMEM((tq, tqc), F32),
                        pltpu.VMEM((tq, tqc), BF16), pltpu.VMEM((tq, tqc), BF16), pltpu.VMEM((tq, HM), F32),
                        pltpu.VMEM((tq, HM), F32)],
        compiler_params=_cparams(("parallel", "arbitrary")),
    )(qb, doa, qkva, qkva)


def _block_diag_dot(xb, w_ref, transpose_w=False):
    outs = []
    for n in range(N_BLOCKS_B):
        xn = xb[:, n * BLOCK_B:(n + 1) * BLOCK_B]
        dn = (((1,), (1 if transpose_w else 0,)), ((), ()))
        outs.append(lax.dot_general(xn, w_ref[n], dn, preferred_element_type=F32))
    return jnp.concatenate(outs, axis=1)


def _rglru_fwd(proj, conv_w, conv_b, w_a, b_a, w_i, b_i, lam, *, name, tm=256):
    _, t, d = proj.shape
    hb = tm // SUBLANES
    ng = tm // SUBLANES

    def body(p_ref, halo_ref, cw_ref, cb_ref, wa_ref, ba_ref, wi_ref, bi_ref, lam_ref,
             xb_ref, r_ref, ig_ref, a_ref, h_ref, y_ref, u_scr, hc_scr):
        first = pl.program_id(0) == 0

        @pl.when(first)
        def _():
            hc_scr[...] = jnp.zeros_like(hc_scr)

        halo = jnp.where(first, 0.0, halo_ref[0])
        xb = _conv_causal(p_ref[0], halo, cw_ref, cb_ref[...], CONV_B)
        xb_ref[...] = xb
        xbb = xb.astype(BF16)
        r = _sigmoid(_block_diag_dot(xbb, wa_ref) + ba_ref[...])
        ig = _sigmoid(_block_diag_dot(xbb, wi_ref) + bi_ref[...])
        r_ref[...] = r
        ig_ref[...] = ig
        log_a = (-LRU_C) * r * _softplus(-lam_ref[...])
        a_ref[...] = jnp.exp(log_a)
        u_scr[...] = jnp.sqrt(_neg_expm1(2.0 * log_a)) * (ig * xb)

        ones8 = jnp.ones((SUBLANES, d), F32)
        zeros8 = jnp.zeros((SUBLANES, d), F32)

        def group(gi, hprev):
            off = pl.multiple_of(gi * SUBLANES, SUBLANES)
            a8 = a_ref[pl.ds(off, SUBLANES), :]
            u8 = u_scr[pl.ds(off, SUBLANES), :]
            for s in (1, 2, 4):
                u8 = a8 * _shift_down(u8, zeros8, s) + u8
                a8 = a8 * _shift_down(a8, ones8, s)
            h8 = a8 * hprev + u8
            h_ref[pl.ds(off, SUBLANES), :] = h8
            return h8[SUBLANES - 1:SUBLANES, :]

        hc_scr[...] = lax.fori_loop(0, ng, group, hc_scr[...])
        y_ref[...] = (h_ref[...] * _gelu(p_ref[1])).astype(BF16)

    row = pl.BlockSpec((tm, d), lambda i: (i, 0))
    vec = pl.BlockSpec((1, d), lambda i: (0, 0))
    wblk = pl.BlockSpec((N_BLOCKS_B, BLOCK_B, BLOCK_B), lambda i: (0, 0, 0))
    f32_td = jax.ShapeDtypeStruct((t, d), F32)
    return pl.pallas_call(
        body, name=name, grid=(t // tm,),
        in_specs=[pl.BlockSpec((2, tm, d), lambda i: (0, i, 0)),
                  pl.BlockSpec((1, SUBLANES, d), lambda i: (0, jnp.maximum(i * hb - 1, 0), 0)),
                  pl.BlockSpec((CONV_B, d), lambda i: (0, 0)), vec, wblk, vec, wblk, vec, vec],
        out_specs=[row, row, row, row, row, row],
        out_shape=[f32_td, f32_td, f32_td, f32_td, f32_td, jax.ShapeDtypeStruct((t, d), BF16)],
        scratch_shapes=[pltpu.VMEM((tm, d), F32), pltpu.VMEM((1, d), F32)],
        compiler_params=_cparams(("arbitrary",)),
    )(proj, proj, conv_w, conv_b, w_a, b_a, w_i, b_i, lam)


def _rglru_bwd(dy, proj, xb, r, ig, a, h, conv_w, w_a, w_i, lam, *, name, tm=256):
    _, t, d = proj.shape
    hb = tm // SUBLANES
    ng = tm // SUBLANES
    nr = t // tm

    def body(dy_ref, p_ref, phalo_ref, xb_ref, r_ref, ig_ref, a_ref, h_ref, hhalo_ref,
             cw_ref, wa_ref, wi_ref, lam_ref,
             dp_ref, dcw_ref, dcb_ref, dba_ref, dbi_ref, dlam_ref, dwa_ref, dwi_ref,
             g_scr, carry_g, carry_dxb):
        i = pl.program_id(0)
        rpos = nr - 1 - i

        @pl.when(i == 0)
        def _():
            for ref in (dcw_ref, dcb_ref, dba_ref, dbi_ref, dlam_ref, dwa_ref, dwi_ref, carry_g, carry_dxb):
                ref[...] = jnp.zeros_like(ref)

        gb = p_ref[1]
        gel, dgel = _gelu_and_grad(gb)
        dyv = dy_ref[...]
        hv = h_ref[...]
        dp_ref[1] = (dyv * hv * dgel).astype(BF16)
        g_scr[...] = dyv * gel
        av = a_ref[...]

        ones8 = jnp.ones((SUBLANES, d), F32)
        zeros8 = jnp.zeros((SUBLANES, d), F32)

        def group(gi, cin):
            off = pl.multiple_of((ng - 1 - gi) * SUBLANES, SUBLANES)
            g8 = g_scr[pl.ds(off, SUBLANES), :]
            a8 = a_ref[pl.ds(off, SUBLANES), :]
            b8 = _shift_up(a8, ones8, 1)
            row8 = lax.broadcasted_iota(jnp.int32, (SUBLANES, d), 0)
            g8 = g8 + jnp.where(row8 == SUBLANES - 1, cin, 0.0)
            b8 = jnp.where(row8 == SUBLANES - 1, 0.0, b8)
            for s in (1, 2, 4):
                g8 = g8 + b8 * _shift_up(g8, zeros8, s)
                b8 = b8 * _shift_up(b8, zeros8, s)
            g_scr[pl.ds(off, SUBLANES), :] = g8
            return a8[0:1, :] * g8[0:1, :]

        carry_g[...] = lax.fori_loop(0, ng, group, carry_g[...])

        du = g_scr[...]
        hhalo = jnp.where(rpos == 0, 0.0, hhalo_ref[...])
        hprev = _shift_down(hv, hhalo, 1)
        da = du * hprev
        rv = r_ref[...]
        igv = ig_ref[...]
        xbv = xb_ref[...]
        sp = _softplus(-lam_ref[...])
        log_a = (-LRU_C) * rv * sp
        mult = jnp.sqrt(_neg_expm1(2.0 * log_a))
        dmult = du * (igv * xbv)
        dig = du * mult * xbv
        dxb = du * mult * igv
        dlog_a = da * av - dmult * (av * av) / mult
        dr = dlog_a * ((-LRU_C) * sp)
        dsp = jnp.sum(dlog_a * ((-LRU_C) * rv), axis=0, keepdims=True)
        dlam_ref[...] += dsp * (-_sigmoid(-lam_ref[...]))
        dra = dr * rv * (1.0 - rv)
        dia = dig * igv * (1.0 - igv)
        dba_ref[...] += jnp.sum(dra, axis=0, keepdims=True)
        dbi_ref[...] += jnp.sum(dia, axis=0, keepdims=True)
        drab = dra.astype(BF16)
        diab = dia.astype(BF16)
        xbb = xbv.astype(BF16)
        dxb = dxb + _block_diag_dot(drab, wa_ref, True) + _block_diag_dot(diab, wi_ref, True)
        tn = (((0,), (0,)), ((), ()))
        for n in range(N_BLOCKS_B):
            sl = slice(n * BLOCK_B, (n + 1) * BLOCK_B)
            dwa_ref[n] += lax.dot_general(xbb[:, sl], drab[:, sl], tn, preferred_element_type=F32)
            dwi_ref[n] += lax.dot_general(xbb[:, sl], diab[:, sl], tn, preferred_element_type=F32)

        xpre = p_ref[0]
        phalo = jnp.where(rpos == 0, 0.0, phalo_ref[0])
        nxt = carry_dxb[...]
        dcb_ref[...] += jnp.sum(dxb, axis=0, keepdims=True)
        out = dxb * cw_ref[CONV_B - 1:CONV_B, :]
        dcw_ref[CONV_B - 1:CONV_B, :] += jnp.sum(dxb * xpre, axis=0, keepdims=True)
        for k in range(CONV_B - 1):
            sh = CONV_B - 1 - k
            out = out + _shift_up(dxb, nxt, sh) * cw_ref[k:k + 1, :]
            dcw_ref[k:k + 1, :] += jnp.sum(dxb * _shift_down(xpre, phalo, sh), axis=0, keepdims=True)
        dp_ref[0] = out.astype(BF16)
        carry_dxb[...] = dxb[:SUBLANES]

    rev = pl.BlockSpec((tm, d), lambda i: (nr - 1 - i, 0))
    halo8 = pl.BlockSpec((SUBLANES, d), lambda i: (jnp.maximum((nr - 1 - i) * hb - 1, 0), 0))
    vec = pl.BlockSpec((1, d), lambda i: (0, 0))
    wblk = pl.BlockSpec((N_BLOCKS_B, BLOCK_B, BLOCK_B), lambda i: (0, 0, 0))
    vec_shape = jax.ShapeDtypeStruct((1, d), F32)
    w_shape = jax.ShapeDtypeStruct((N_BLOCKS_B, BLOCK_B, BLOCK_B), F32)
    return pl.pallas_call(
        body, name=name, grid=(nr,),
        in_specs=[rev, pl.BlockSpec((2, tm, d), lambda i: (0, nr - 1 - i, 0)),
                  pl.BlockSpec((1, SUBLANES, d), lambda i: (0, jnp.maximum((nr - 1 - i) * hb - 1, 0), 0)),
                  rev, rev, rev, rev, rev, halo8,
                  pl.BlockSpec((CONV_B, d), lambda i: (0, 0)), wblk, wblk, vec],
        out_specs=[pl.BlockSpec((2, tm, d), lambda i: (0, nr - 1 - i, 0)),
                   pl.BlockSpec((CONV_B, d), lambda i: (0, 0)), vec, vec, vec, vec, wblk, wblk],
        out_shape=[jax.ShapeDtypeStruct((2, t, d), BF16), jax.ShapeDtypeStruct((CONV_B, d), F32),
                   vec_shape, vec_shape, vec_shape, vec_shape, w_shape, w_shape],
        scratch_shapes=[pltpu.VMEM((tm, d), F32), pltpu.VMEM((1, d), F32), pltpu.VMEM((SUBLANES, d), F32)],
        compiler_params=_cparams(("arbitrary",)),
    )(dy, proj, proj, xb, r, ig, a, h, h, conv_w, w_a, w_i, lam)


def _split_cols(w, s):
    k, c = w.shape[-2:]
    return jnp.moveaxis(w.reshape(w.shape[:-2] + (k, s, c // s)), -2, -3)


def _merge_cols(w):
    s, k, c = w.shape
    return jnp.moveaxis(w, 0, 1).reshape(k, s * c)


def _local_step(x, p, tgt, w):
    t = x.shape[0]
    bf = lambda v: v.astype(BF16)
    saved = []
    xcur = x
    xcur_b = bf(x)
    for i in range(DEPTH):
        j = i // 2
        L = f"l{i}_"
        sv = {'x_in_b': xcur_b}
        if i % 2 == 0:
            w_in_t = bf(w['a_w_in'][j])
            sv['w_qkv_t'] = w_in_t[:3 * D_MODEL]
            sv['w_f'] = jnp.pad(w_in_t[3 * D_MODEL:].T, ((0, 0), (0, LANES - N_HEADS)))
            sv['w_out'] = _hm_weight_t(bf(w['a_w_out'][j]))
            b_f = jnp.pad(w['a_b_f'][j], (0, LANES - N_HEADS)).reshape(1, LANES)
            qkv = _mm(xcur_b, sv['w_qkv_t'], tb=True, out_dtype=BF16, name=L + "qkv")
            z, c = _fgate_fwd(xcur_b, sv['w_f'], b_f, name=L + "fgate")
            qkva = _attn_prep(qkv, c, name=L + "attn_prep")
            o, qb = _flash_fwd_hm(qkva, name=L + "flash_fwd")
            mix = _mm(o, sv['w_out'], name=L + "attn_out")
            sv.update(qkva=qkva, z=z, o=o, qb=qb)
        else:
            sv['w_in'] = bf(w['b_w_in'][j])
            sv['w_out'] = bf(w['b_w_out'][j])
            sv['conv_w'] = w['b_conv_w'][j]
            sv['w_a'] = bf(w['b_w_a'][j])
            sv['w_i'] = bf(w['b_w_i'][j])
            sv['lam'] = w['b_lam'][j].reshape(1, D_MODEL)
            proj = _mm(xcur_b, sv['w_in'], out_split=2, name=L + "rg_in")
            xb, r, ig, a, h, y = _rglru_fwd(
                proj, sv['conv_w'], w['b_conv_b'][j].reshape(1, D_MODEL), sv['w_a'],
                w['b_b_a'][j].reshape(1, D_MODEL), sv['w_i'], w['b_b_i'][j].reshape(1, D_MODEL), sv['lam'],
                name=L + "rglru_fwd")
            mix = _mm(y, sv['w_out'], name=L + "rg_out")
            sv.update(proj=proj, xb=xb, r=r, ig=ig, a=a, h=h, y=y)
        sv['ln1_g'] = w['ln1_g'][i].reshape(1, D_MODEL)
        x1, x1b, xhat1, rstd1 = _ln_fwd(xcur, mix, sv['ln1_g'], w['ln1_b'][i].reshape(1, D_MODEL), name=L + "ln1")
        sv['w_up'] = bf(w['f_w_up'][i])
        sv['w_down'] = bf(w['f_w_down'][i])
        sv['fconv_w'] = _split_cols(w['f_conv_w'][i], 2)
        sv['fconv_b'] = w['f_conv_b'][i].reshape(2, 1, D_FF)
        hdn = _mm(x1b, sv['w_up'], out_split=2, tn_cap=1408, name=L + "ffn_up")
        act = _ffn_act_fwd(hdn, sv['fconv_w'], sv['fconv_b'], name=L + "ffn_act")
        ff = _mm(act, sv['w_down'], tk_cap=2816, name=L + "ffn_down")
        sv['ln2_g'] = w['ln2_g'][i].reshape(1, D_MODEL)
        x2, x2b, xhat2, rstd2 = _ln_fwd(x1, ff, sv['ln2_g'], w['ln2_b'][i].reshape(1, D_MODEL), name=L + "ln2")
        sv['gate_w'] = bf(w['ple_gate_w'][i])
        sv['ple_w'] = bf(w['ple_w'][i])
        sv['gate_b'] = w['ple_gate_b'][i].reshape(1, D_MODEL)
        sv['p_b'] = bf(p[i])
        gl = _mm(x2b, sv['gate_w'], name=L + "ple_gate")
        pe = _mm(sv['p_b'], sv['ple_w'], name=L + "ple_emb")
        x3, x3b = _ple_fwd(x2, gl, pe, sv['gate_b'], name=L + "ple")
        sv.update(xhat1=xhat1, rstd1=rstd1, x1b=x1b, hdn=hdn, act=act, xhat2=xhat2, rstd2=rstd2, x2b=x2b,
                  gl=gl, pe=pe)
        saved.append(sv)
        xcur, xcur_b = x3, x3b

    dx, loss_row = _loss_bwd(xcur, tgt, name="loss")

    g = {n: [None] * w[n].shape[0] for n in WEIGHTS}
    for i in reversed(range(DEPTH)):
        j = i // 2
        L = f"l{i}b_"
        sv = saved[i]
        dgl, dpe, d_gate_b = _ple_bwd(dx, sv['gl'], sv['pe'], sv['gate_b'], name=L + "ple")
        g['ple_gate_b'][i] = d_gate_b[0]
        g['ple_w'][i] = _mm(sv['p_b'], dpe, ta=True, name=L + "ple_emb_dw")
        g['ple_gate_w'][i] = _mm(sv['x2b'], dgl, ta=True, name=L + "ple_gate_dw")
        dx2 = _mm(dgl, sv['gate_w'], tb=True, add=dx, name=L + "ple_gate_dx")
        dz2, dz2b, dg2, db2 = _ln_bwd(dx2, sv['xhat2'], sv['rstd2'], sv['ln2_g'], name=L + "ln2")
        g['ln2_g'][i], g['ln2_b'][i] = dg2[0], db2[0]
        g['f_w_down'][i] = _mm(sv['act'], dz2b, ta=True, tm_cap=1408, name=L + "ffn_down_dw")
        da = _mm(dz2b, sv['w_down'], tb=True, tn_cap=1408, name=L + "ffn_down_dx")
        dhdn, d_fcw, d_fcb = _ffn_act_bwd(da, sv['hdn'], sv['fconv_w'], sv['fconv_b'], name=L + "ffn_act")
        g['f_conv_w'][i] = _merge_cols(d_fcw)
        g['f_conv_b'][i] = d_fcb.reshape(2 * D_FF)
        g['f_w_up'][i] = _mm(sv['x1b'], dhdn, ta=True, tn_cap=1408, name=L + "ffn_up_dw")
        dx1 = _mm(dhdn, sv['w_up'], tb=True, add=dz2, add_scale=ALPHA, tk_cap=D_FF, name=L + "ffn_up_dx")
        dz1, dz1b, dg1, db1 = _ln_bwd(dx1, sv['xhat1'], sv['rstd1'], sv['ln1_g'], name=L + "ln1")
        g['ln1_g'][i], g['ln1_b'][i] = dg1[0], db1[0]
        if i % 2 == 0:
            g['a_w_out'][j] = _hm_unpad_t(_mm(sv['o'], dz1b, ta=True, name=L + "attn_out_dw"))
            do = _mm(dz1b, sv['w_out'], tb=True, out_dtype=BF16, name=L + "attn_out_dx")
            doa = _attn_prep_bwd(sv['o'], do, name=L + "attn_prep")
            dqkv = _flash_bwd_hm(sv['qb'], doa, sv['qkva'], name=L + "flash_bwd")
            dcq, dcj = dqkv[3], dqkv[4]
            dc = jnp.pad((dcq[:, 0, :] - dcj[:, LANE_ONE - LANE_C, :]).T, ((0, 0), (0, LANES - N_HEADS)))
            dzf, d_b_f = _fgate_bwd(dc, sv['z'], name=L + "fgate")
            g['a_b_f'][j] = d_b_f[0, :N_HEADS]
            xb_in = sv['x_in_b']
            dqkv_c = _attn_compact(dqkv[:3], name=L + "attn_compact")
            d_w = _mm(dqkv_c, xb_in, ta=True, name=L + "qkv_dw")
            d_wf = _mm(xb_in, dzf, ta=True, name=L + "f_dw")
            g['a_w_in'][j] = jnp.concatenate([d_w, d_wf[:, :N_HEADS].T], axis=0)
            dxa = _mm(dqkv_c, sv['w_qkv_t'], add=dz1, add_scale=ALPHA, name=L + "qkv_dx")
            dx = _mm(dzf, sv['w_f'], tb=True, add=dxa, name=L + "f_dx")
        else:
            g['b_w_out'][j] = _mm(sv['y'], dz1b, ta=True, name=L + "rg_out_dw")
            dy = _mm(dz1b, sv['w_out'], tb=True, name=L + "rg_out_dx")
            dproj, d_cw, d_cb, d_ba, d_bi, d_lam, d_wa, d_wi = _rglru_bwd(
                dy, sv['proj'], sv['xb'], sv['r'], sv['ig'], sv['a'], sv['h'], sv['conv_w'], sv['w_a'], sv['w_i'],
                sv['lam'], name=L + "rglru_bwd")
            g['b_conv_w'][j], g['b_conv_b'][j] = d_cw, d_cb[0]
            g['b_b_a'][j] = d_ba.reshape(N_BLOCKS_B, BLOCK_B)
            g['b_b_i'][j] = d_bi.reshape(N_BLOCKS_B, BLOCK_B)
            g['b_lam'][j] = d_lam[0]
            g['b_w_a'][j], g['b_w_i'][j] = d_wa, d_wi
            g['b_w_in'][j] = _mm(sv['x_in_b'], dproj, ta=True, name=L + "rg_in_dw")
            dx = _mm(dproj, sv['w_in'], tb=True, add=dz1, add_scale=ALPHA, name=L + "rg_in_dx")
    return loss_row, dx, g


def _round_up(n, q):
    return -(-n // q) * q


def _pack(arrs, dtype, row_multiple, lead=0):
    pieces = []
    for a in arrs:
        flat = a.reshape(a.shape[:lead] + (-1,)).astype(dtype)
        n = flat.shape[-1]
        pieces.append(jnp.pad(flat, [(0, 0)] * lead + [(0, _round_up(n, LANES) - n)]))
    flat = jnp.concatenate(pieces, axis=-1)
    rows = _round_up(flat.shape[-1] // LANES, row_multiple)
    flat = jnp.pad(flat, [(0, 0)] * lead + [(0, rows * LANES - flat.shape[-1])])
    return flat.reshape(flat.shape[:lead] + (rows, LANES))


def _unpack(buf, shapes):
    lead = buf.shape[:-2]
    flat = buf.reshape(lead + (-1,))
    out, off = [], 0
    for shp in shapes:
        n = math.prod(shp)
        out.append(flat[..., off:off + n].reshape(lead + tuple(shp)))
        off += _round_up(n, LANES)
    return out


MESH = pl.DeviceIdType.MESH
ANY = pl.BlockSpec(memory_space=pl.ANY)
N_CHUNK = 8


def _all_gather_shards(buf):
    rows, lanes = buf.shape
    half = rows // 2
    ch = half // N_CHUNK
    n_ici = 3 * N_CHUNK

    def body(in_ref, out_ref, send_sems, recv_sems):
        x, y, c = lax.axis_index("x"), lax.axis_index("y"), lax.axis_index("c")
        sibling = (x, y, 1 - c)
        chips = [(1 - x, y), (x, 1 - y), (1 - x, 1 - y)]

        def piece(cx, cy, hc, q):
            return out_ref.at[2 * cx + cy, pl.ds(hc * half + q * ch, ch), :]

        def copy(k, src, dst, to):
            return pltpu.make_async_remote_copy(src_ref=src, dst_ref=dst, send_sem=send_sems.at[k],
                                                recv_sem=recv_sems.at[k], device_id=to, device_id_type=MESH)

        my_chunk = lambda q: in_ref.at[pl.ds(c * half + q * ch, ch), :]
        first, passed = [], []
        for k, chip in enumerate(chips):
            for q in range(N_CHUNK):
                first.append(copy(k * N_CHUNK + q, my_chunk(q), piece(x, y, c, q), (*chip, c)))
                passed.append(copy(n_ici + k * N_CHUNK + q, piece(*chip, c, q), piece(*chip, c, q), sibling))
        for cp in first:
            cp.start()
        for k, chip in enumerate(chips):
            for q in range(N_CHUNK):
                n = k * N_CHUNK + q
                copy(n, my_chunk(q), piece(*chip, c, q), (*chip, c)).wait_recv()
                passed[n].start()
        for k, chip in enumerate(chips):
            for q in range(N_CHUNK):
                copy(n_ici + k * N_CHUNK + q, my_chunk(q), piece(*chip, 1 - c, q), sibling).wait_recv()
        for cp in first + passed:
            cp.wait_send()

    others = pl.pallas_call(
        body, name="gather_weights", in_specs=[ANY], out_specs=ANY,
        out_shape=jax.ShapeDtypeStruct((N_CHIP, rows, lanes), buf.dtype),
        scratch_shapes=[pltpu.SemaphoreType.DMA((2 * n_ici,)), pltpu.SemaphoreType.DMA((2 * n_ici,))],
    )(buf)
    return lax.dynamic_update_slice(others, buf[None], (2 * lax.axis_index("x") + lax.axis_index("y"), 0, 0))


def _exchange_many(arrs, *, per_chip, name):
    na = len(arrs)
    flips = (2, 4, 6) if per_chip else tuple(range(1, N_DEV))
    index_of = (lambda x, y, c: 2 * x + y) if per_chip else (lambda x, y, c: 4 * x + 2 * y + c)

    def body(*refs):
        ins, outs = refs[:na], refs[na:2 * na]
        send_sems, recv_sems = refs[2 * na:]
        x, y, c = lax.axis_index("x"), lax.axis_index("y"), lax.axis_index("c")
        me = index_of(x, y, c)
        copies = []
        for nk, k in enumerate(flips):
            px, py, pc = x ^ (k >> 2), y ^ ((k >> 1) & 1), c ^ (k & 1)
            peer = index_of(px, py, pc)
            for a in range(na):
                n = nk * na + a
                copies.append(pltpu.make_async_remote_copy(
                    src_ref=ins[a].at[peer], dst_ref=outs[a].at[me], send_sem=send_sems.at[n],
                    recv_sem=recv_sems.at[n], device_id=(px, py, pc), device_id_type=MESH))
        for cp in copies:
            cp.start()
        for cp in copies:
            cp.wait()

    n_remote = len(flips) * na
    outs = pl.pallas_call(
        body, name=name, in_specs=[ANY] * na, out_specs=[ANY] * na,
        out_shape=[jax.ShapeDtypeStruct(a.shape, a.dtype) for a in arrs],
        scratch_shapes=[pltpu.SemaphoreType.DMA((n_remote,)), pltpu.SemaphoreType.DMA((n_remote,))],
    )(*arrs)
    me = index_of(lax.axis_index("x"), lax.axis_index("y"), lax.axis_index("c"))
    return [lax.dynamic_update_slice(o, lax.dynamic_index_in_dim(a, me, 0, keepdims=True), (me, 0, 0))
            for o, a in zip(outs, arrs)]


def _swap_with_sibling(arrs):
    na = len(arrs)

    def body(*refs):
        ins, outs = refs[:na], refs[na:2 * na]
        send_sems, recv_sems = refs[2 * na:]
        x, y, c = lax.axis_index("x"), lax.axis_index("y"), lax.axis_index("c")
        copies = [pltpu.make_async_remote_copy(
            src_ref=ins[a].at[:, 1 - c], dst_ref=outs[a], send_sem=send_sems.at[a], recv_sem=recv_sems.at[a],
            device_id=(x, y, 1 - c), device_id_type=MESH) for a in range(na)]
        for cp in copies:
            cp.start()
        for cp in copies:
            cp.wait()

    return pl.pallas_call(
        body, name="swap_pieces", in_specs=[ANY] * na, out_specs=[ANY] * na,
        out_shape=[jax.ShapeDtypeStruct((a.shape[0],) + a.shape[2:], a.dtype) for a in arrs],
        scratch_shapes=[pltpu.SemaphoreType.DMA((na,)), pltpu.SemaphoreType.DMA((na,))],
    )(*arrs)


def _add_pair(a, b, *, name):
    n, r, c = a.shape
    tm = _tile(r, max(2 * SUBLANES, SUM_BLOCK_ELEMS // c), 2 * SUBLANES)

    def body(a_ref, b_ref, o_ref):
        o_ref[...] = (a_ref[...].astype(F32) + b_ref[...].astype(F32)).astype(BF16)

    spec = pl.BlockSpec((n, tm, c), lambda i: (0, i, 0))
    return pl.pallas_call(
        body, name=name, grid=(r // tm,), in_specs=[spec, spec], out_specs=spec,
        out_shape=jax.ShapeDtypeStruct(a.shape, BF16), compiler_params=_cparams(("parallel",)),
    )(a, b)


def _share_many(reds, halves):
    na = len(reds)

    def body(*refs):
        ins, outs = refs[:na], refs[na:2 * na]
        send_sems, recv_sems = refs[2 * na:]
        x, y, c = lax.axis_index("x"), lax.axis_index("y"), lax.axis_index("c")
        copies = []
        for a in range(na):
            h = halves[a]
            copies.append(pltpu.make_async_remote_copy(
                src_ref=ins[a].at[pl.ds(0, h), :], dst_ref=outs[a].at[pl.ds(c * h, h), :], send_sem=send_sems.at[a],
                recv_sem=recv_sems.at[a], device_id=(x, y, 1 - c), device_id_type=MESH))
        for cp in copies:
            cp.start()
        for cp in copies:
            cp.wait()

    outs = pl.pallas_call(
        body, name="share_halves", in_specs=[ANY] * na, out_specs=[ANY] * na,
        out_shape=[jax.ShapeDtypeStruct((r.shape[0] + h, r.shape[1]), r.dtype) for r, h in zip(reds, halves)],
        scratch_shapes=[pltpu.SemaphoreType.DMA((na,)), pltpu.SemaphoreType.DMA((na,))],
    )(*reds)
    c = lax.axis_index("c")
    full = []
    for o, r, h in zip(outs, reds, halves):
        o = lax.dynamic_update_slice(o, r[:h], (c * h, 0))
        if r.shape[0] > h:
            o = lax.dynamic_update_slice(o, r[h:], (2 * h, 0))
        full.append(o)
    return full


SUM_BLOCK_ELEMS = 256 * 1024
ADAM_BLOCK_ELEMS = 256 * 1024


def _sum_slots(slots, *, name="sum_grads"):
    n, prow, lanes = slots.shape
    tm = _tile(prow, max(2 * SUBLANES, SUM_BLOCK_ELEMS // lanes), 2 * SUBLANES)

    def body(s_ref, o_ref):
        acc = s_ref[0].astype(F32)
        for s in range(1, n):
            acc = acc + s_ref[s].astype(F32)
        o_ref[...] = acc

    return pl.pallas_call(
        body, name=name, grid=(prow // tm,),
        in_specs=[pl.BlockSpec((n, tm, lanes), lambda i: (0, i, 0))],
        out_specs=pl.BlockSpec((tm, lanes), lambda i: (i, 0)),
        out_shape=jax.ShapeDtypeStruct((prow, lanes), F32),
        compiler_params=_cparams(("parallel",)),
    )(slots)


def _adamw(wp, gp, mp, vp, *, name="adamw"):
    rows, lanes = wp.shape
    tm = _tile(rows, max(SUBLANES, ADAM_BLOCK_ELEMS // lanes), SUBLANES)
    c1 = 1.0 / (1.0 - ADAM_B1 ** ADAM_STEP)
    c2 = 1.0 / (1.0 - ADAM_B2 ** ADAM_STEP)

    def body(w_ref, g_ref, m_ref, v_ref, d_ref, nm_ref, nv_ref):
        g = g_ref[...]
        m = ADAM_B1 * m_ref[...] + (1.0 - ADAM_B1) * g
        v = ADAM_B2 * v_ref[...] + (1.0 - ADAM_B2) * (g * g)
        m_hat = m * c1
        v_hat = v * c2
        d_ref[...] = -ADAM_LR * (m_hat / (jnp.sqrt(v_hat) + ADAM_EPS) + ADAM_WD * w_ref[...])
        nm_ref[...] = m
        nv_ref[...] = v

    spec = pl.BlockSpec((tm, lanes), lambda i: (i, 0))
    shp = jax.ShapeDtypeStruct((rows, lanes), F32)
    return pl.pallas_call(
        body, name=name, grid=(rows // tm,), in_specs=[spec] * 4, out_specs=[spec] * 3,
        out_shape=[shp, shp, shp], compiler_params=_cparams(("parallel",)),
    )(wp, gp, mp, vp)


AG_ROW_MULT = 2 * N_CHUNK * 16
GRAD_ROW_MULT = 2048
REP_ROW_MULT = 512


def _shard_to_front(a, axis):
    n = a.shape[axis]
    a = a.reshape(a.shape[:axis] + (N_CHIP, n // N_CHIP) + a.shape[axis + 1:])
    return jnp.moveaxis(a, axis, 0)


def _shards_to_full(a, axis):
    a = jnp.moveaxis(a, 0, axis)
    return a.reshape(a.shape[:axis] + (a.shape[axis] * a.shape[axis + 1],) + a.shape[axis + 2:])


def kernel(x, p, a_w_in, a_b_f, a_w_out, b_w_in, b_conv_w, b_conv_b, b_w_a, b_b_a, b_w_i, b_b_i, b_lam, b_w_out, f_w_up, f_conv_w, f_conv_b, f_w_down, ln1_g, ln1_b, ln2_g, ln2_b, ple_w, ple_gate_w, ple_gate_b, loss_target, m_a_w_in, m_a_b_f, m_a_w_out, m_b_w_in, m_b_conv_w, m_b_conv_b, m_b_w_a, m_b_b_a, m_b_w_i, m_b_b_i, m_b_lam, m_b_w_out, m_f_w_up, m_f_conv_w, m_f_conv_b, m_f_w_down, m_ln1_g, m_ln1_b, m_ln2_g, m_ln2_b, m_ple_w, m_ple_gate_w, m_ple_gate_b, v_a_w_in, v_a_b_f, v_a_w_out, v_b_w_in, v_b_conv_w, v_b_conv_b, v_b_w_a, v_b_b_a, v_b_w_i, v_b_b_i, v_b_lam, v_b_w_out, v_f_w_up, v_f_conv_w, v_f_conv_b, v_f_w_down, v_ln1_g, v_ln1_b, v_ln2_g, v_ln2_b, v_ple_w, v_ple_gate_w, v_ple_gate_b):
    args = dict(locals())
    swap = lambda n, a: jnp.swapaxes(a, -1, -2) if n in TRANSPOSED else a
    axis_of = lambda n: (3 - SHARD_AXIS[n]) if n in TRANSPOSED else SHARD_AXIS[n]
    local_w = {n: swap(n, args[n]) for n in WEIGHTS}
    local_m = {n: swap(n, args['m_' + n]) for n in WEIGHTS}
    local_v = {n: swap(n, args['v_' + n]) for n in WEIGHTS}

    as_pairs = lambda a: lax.bitcast_convert_type(a, BF16)
    from_pairs = lambda a: lax.bitcast_convert_type(a, F32)
    send = [local_w[n] for n in GATHER_BF16] + [as_pairs(local_w[n]) for n in GATHER_F32]
    gathered = _all_gather_shards(_pack(send, BF16, AG_ROW_MULT))
    shapes = [local_w[n].shape for n in GATHER_BF16] + [local_w[n].shape + (2,) for n in GATHER_F32]
    parts = _unpack(gathered, shapes)
    full_w = {}
    for n, part in zip(GATHER_BF16 + GATHER_F32, parts):
        if n in GATHER_F32:
            part = from_pairs(part)
        full_w[n] = _shards_to_full(part, axis_of(n))
    for n in REPLICATED:
        full_w[n] = local_w[n]

    loss_row, grad_x, g = _local_step(x[0], p[:, 0], loss_target[0], full_w)

    grads = {n: jnp.stack(g[n]) for n in SHARDED}
    small = [n for n in SHARDED if n not in NATIVE]
    shard_rows = _round_up(sum(_round_up(local_w[n].size, LANES) for n in small) // LANES, GRAD_ROW_MULT)
    half_rows = shard_rows // 2

    def native_pieces(n):
        a = _shard_to_front(grads[n], axis_of(n))
        return a.reshape(N_DEV, -1, a.shape[-1]).astype(BF16)

    sharded_g = _pack([_shard_to_front(grads[n], axis_of(n)) for n in small], BF16, GRAD_ROW_MULT, lead=1)
    rep_g = _pack([jnp.stack(g[n]) for n in REPLICATED] + [loss_row], F32, REP_ROW_MULT)
    rep_rows = rep_g.shape[0]
    rep_top = lax.reduce_precision(rep_g, 8, 7)
    rep_hi = rep_top.astype(BF16)
    rep_lo = (rep_g - rep_top).astype(BF16)
    packed_pieces = jnp.concatenate(
        [sharded_g.reshape(N_DEV, half_rows, LANES),
         jnp.broadcast_to(jnp.concatenate([rep_hi, rep_lo]), (N_DEV, 2 * rep_rows, LANES))], axis=1)
    c = lax.axis_index("c")
    mine = [native_pieces(n) for n in NATIVE]
    theirs = _swap_with_sibling([p.reshape((N_CHIP, 2) + p.shape[1:]) for p in mine])
    chip_sums = [_add_pair(lax.dynamic_index_in_dim(p.reshape((N_CHIP, 2) + p.shape[1:]), c, 1, keepdims=False), t,
                           name="pair_" + n) for p, t, n in zip(mine, theirs, NATIVE)]
    slots = _exchange_many(chip_sums, per_chip=True, name="exchange_grads")
    slots += _exchange_many([packed_pieces], per_chip=False, name="exchange_packed")
    reduced = [_sum_slots(s, name="sum_" + n) for s, n in zip(slots, NATIVE + ["packed"])]
    reduced[-1] = jnp.concatenate([reduced[-1][:half_rows], reduced[-1][half_rows:half_rows + rep_rows]
                                   + reduced[-1][half_rows + rep_rows:]])
    full_g = _share_many(reduced, [r.shape[0] for r in reduced[:-1]] + [half_rows])

    rep_shapes = [local_w[n].shape for n in REPLICATED] + [loss_row.shape]

    def packed(d):
        return jnp.concatenate([_pack([d[n] for n in small], F32, GRAD_ROW_MULT),
                                _pack([d[n] for n in REPLICATED] + [jnp.zeros_like(loss_row)], F32, REP_ROW_MULT)])

    def unpacked(buf):
        d = dict(zip(small, _unpack(buf[:shard_rows], [local_w[n].shape for n in small])))
        d.update(zip(REPLICATED, _unpack(buf[shard_rows:], rep_shapes)))
        return d

    outs = [unpacked(b) for b in (full_g[-1],) + tuple(_adamw(packed(local_w), full_g[-1], packed(local_m),
                                                              packed(local_v), name="adamw_packed"))]
    for n, gn in zip(NATIVE, full_g):
        two_d = lambda a: a.reshape(gn.shape)
        res = _adamw(two_d(local_w[n]), gn, two_d(local_m[n]), two_d(local_v[n]), name="adamw_" + n)
        for d, a in zip(outs, (gn,) + tuple(res)):
            d[n] = a.reshape(local_w[n].shape)
    loss = _unpack(full_g[-1][shard_rows:], rep_shapes)[-1][0, 0]
    return (loss, grad_x[None], *[swap(n, d[n]) for d in outs for n in WEIGHTS])
```

```python
import math

import jax
import jax.numpy as jnp
from jax import lax
from jax.experimental import pallas as pl
from jax.experimental.pallas import tpu as pltpu

F32 = jnp.float32
BF16 = jnp.bfloat16

D_MODEL = 1024
DEPTH = 4
N_HEADS = 16
HEAD_DIM = 64
N_BLOCKS_B = 8
BLOCK_B = 128
CONV_B = 4
LRU_C = 8.0
D_FF = 2816
CONV_F = 3
D_PLE = 256
LN_EPS = 1e-5
ALPHA = (2.0 * DEPTH) ** 0.25
ATTN_SCALE = 1.0 / math.sqrt(HEAD_DIM)

ADAM_LR = 0.001
ADAM_B1 = 0.9
ADAM_B2 = 0.999
ADAM_EPS = 1e-08
ADAM_WD = 0.01
ADAM_STEP = 10

LANES = 128
SUBLANES = 8
VMEM_LIMIT = 52 * 1024 * 1024
MM_VMEM_BUDGET = 40 * 1024 * 1024
NEG_BIG = -1e30
N_DEV = 8
N_CHIP = 4

WEIGHTS = ['a_w_in', 'a_b_f', 'a_w_out', 'b_w_in', 'b_conv_w', 'b_conv_b', 'b_w_a', 'b_b_a', 'b_w_i', 'b_b_i',
           'b_lam', 'b_w_out', 'f_w_up', 'f_conv_w', 'f_conv_b', 'f_w_down', 'ln1_g', 'ln1_b', 'ln2_g', 'ln2_b',
           'ple_w', 'ple_gate_w', 'ple_gate_b']
SHARD_AXIS = {'a_w_in': 2, 'a_w_out': 1, 'b_w_in': 2, 'b_conv_w': 2, 'b_conv_b': 1, 'b_lam': 1, 'b_w_out': 1,
              'f_w_up': 2, 'f_conv_w': 2, 'f_w_down': 1, 'ple_w': 2, 'ple_gate_w': 1}
TRANSPOSED = ('a_w_in',)
NATIVE = ['a_w_out', 'b_w_in', 'b_w_out', 'f_w_up', 'f_w_down', 'ple_w', 'ple_gate_w']
SHARDED = [n for n in WEIGHTS if n in SHARD_AXIS]
REPLICATED = [n for n in WEIGHTS if n not in SHARD_AXIS]
GATHER_BF16 = ['a_w_in', 'a_w_out', 'b_w_in', 'b_w_out', 'f_w_up', 'f_w_down', 'ple_w', 'ple_gate_w']
GATHER_F32 = ['b_conv_w', 'b_conv_b', 'b_lam', 'f_conv_w']


def _cparams(sem, vmem=VMEM_LIMIT):
    return pltpu.CompilerParams(dimension_semantics=sem, vmem_limit_bytes=vmem)


def _tile(n, cap, q=LANES):
    best = None
    for t in range(q, min(n, cap) + 1, q):
        if n % t == 0:
            best = t
    return best if best is not None else n


def _sigmoid(x):
    return 1.0 / (1.0 + jnp.exp(-x))


_GELU_C = math.sqrt(2.0 / math.pi)


def _gelu_and_grad(x):
    x2 = x * x
    t = jnp.tanh(_GELU_C * (x + 0.044715 * x * x2))
    cdf = 0.5 * (1.0 + t)
    g = x * cdf
    dg = cdf + x * 0.5 * (1.0 - t * t) * _GELU_C * (1.0 + 3.0 * 0.044715 * x2)
    return g, dg


def _gelu(x):
    t = jnp.tanh(_GELU_C * (x + 0.044715 * x * x * x))
    return x * (0.5 * (1.0 + t))


def _log1p(u):
    w = 1.0 + u
    d = w - 1.0
    return jnp.where(d == 0.0, u, jnp.log(w) * (u / jnp.where(d == 0.0, 1.0, d)))


def _softplus(y):
    return jnp.maximum(y, 0.0) + _log1p(jnp.exp(-jnp.abs(y)))


def _log_sigmoid(z):
    return -_softplus(-z)


def _neg_expm1(x):
    poly = x * (1.0 + x * (1.0 / 2 + x * (1.0 / 6 + x * (1.0 / 24 + x * (1.0 / 120 + x * (1.0 / 720 + x * (1.0 / 5040)))))))
    return -jnp.where(x > -0.25, poly, jnp.exp(x) - 1.0)


def _split3(x):
    hi = x.astype(BF16)
    r1 = x - hi.astype(F32)
    mid = r1.astype(BF16)
    lo = (r1 - mid.astype(F32)).astype(BF16)
    return hi, mid, lo


def _shift_down(x, halo, k):
    rolled = pltpu.roll(x, k, axis=0)
    hal = pltpu.roll(halo, k, axis=0)
    r8 = lax.broadcasted_iota(jnp.int32, halo.shape, 0)
    head = jnp.where(r8 < k, hal, rolled[:SUBLANES])
    if x.shape[0] == SUBLANES:
        return head
    return jnp.concatenate([head, rolled[SUBLANES:]], axis=0)


def _shift_up(x, nxt, k):
    n = x.shape[0]
    rolled = pltpu.roll(x, n - k, axis=0)
    nx = pltpu.roll(nxt, SUBLANES - k, axis=0)
    r8 = lax.broadcasted_iota(jnp.int32, nxt.shape, 0)
    tail = jnp.where(r8 >= SUBLANES - k, nx, rolled[n - SUBLANES:])
    if n == SUBLANES:
        return tail
    return jnp.concatenate([rolled[:n - SUBLANES], tail], axis=0)


def _split_spec(arr_ndim, part_cols, br, bc, idx):
    if arr_ndim == 3:
        nbh = part_cols // bc
        return pl.BlockSpec((None, br, bc), lambda i, j, k: (lax.div(idx(i, j, k)[1], nbh), idx(i, j, k)[0],
                                                             lax.rem(idx(i, j, k)[1], nbh)))
    return pl.BlockSpec((br, bc), lambda i, j, k: idx(i, j, k))


def _dims(arr):
    if arr.ndim == 3:
        return arr.shape[1], arr.shape[0] * arr.shape[2], arr.shape[2]
    return arr.shape[0], arr.shape[1], arr.shape[1]


def _mm(a, b, *, ta=False, tb=False, out_dtype=F32, out_split=1, add=None, add_scale=1.0,
        tm_cap=1024, tn_cap=1024, tk_cap=1408, name):
    ar, ac, apart = _dims(a)
    br_, bc_, bpart = _dims(b)
    m, kdim = (ac, ar) if ta else (ar, ac)
    kdim_b, n = (bc_, br_) if tb else (br_, bc_)
    assert kdim == kdim_b, (name, a.shape, b.shape)
    tm = _tile(apart, tm_cap) if ta else _tile(m, tm_cap, SUBLANES)
    tn = _tile(n, tn_cap, SUBLANES) if tb else _tile(math.gcd(bpart, n // out_split), tn_cap)
    if ta:
        tk = _tile(kdim, 1024, 2 * SUBLANES)
    elif tb:
        tk = _tile(math.gcd(apart, bpart), tk_cap)
    else:
        tk = _tile(apart, tk_cap)

    def vmem_bytes(tm_, tk_):
        per_step = 2 * (tm_ * tk_ + tk_ * tn) + tm_ * tn * (jnp.dtype(out_dtype).itemsize + (4 if add is not None else 0))
        return 2 * per_step + (4 * tm_ * tn if kdim // tk_ > 1 else 0)

    if ta and kdim % (2 * tk) == 0 and vmem_bytes(tm, 2 * tk) <= MM_VMEM_BUDGET:
        tk *= 2
    elif not ta and m % (2 * tm) == 0 and vmem_bytes(2 * tm, tk) <= MM_VMEM_BUDGET:
        tm *= 2
    assert m % tm == 0 and n % tn == 0 and kdim % tk == 0, (name, m, n, kdim, tm, tn, tk)
    nk = kdim // tk
    a_spec = (_split_spec(a.ndim, apart, tk, tm, lambda i, j, k: (k, i)) if ta
              else _split_spec(a.ndim, apart, tm, tk, lambda i, j, k: (i, k)))
    b_spec = (_split_spec(b.ndim, bpart, tn, tk, lambda i, j, k: (j, k)) if tb
              else _split_spec(b.ndim, bpart, tk, tn, lambda i, j, k: (k, j)))
    if out_split > 1:
        out_shape = jax.ShapeDtypeStruct((out_split, m, n // out_split), out_dtype)
        o_spec = _split_spec(3, n // out_split, tm, tn, lambda i, j, k: (i, j))
    else:
        out_shape = jax.ShapeDtypeStruct((m, n), out_dtype)
        o_spec = pl.BlockSpec((tm, tn), lambda i, j, k: (i, j))
    dn = (((0 if ta else 1,), (1 if tb else 0,)), ((), ()))
    in_specs = [a_spec, b_spec]
    args = [a, b]
    if add is not None:
        in_specs.append(pl.BlockSpec((tm, tn), lambda i, j, k: (i, j)))
        args.append(add)
    use_acc = nk > 1
    has_add = add is not None

    def body(*refs):
        a_ref, b_ref = refs[0], refs[1]
        add_ref = refs[2] if has_add else None
        o_ref = refs[3] if has_add else refs[2]
        part = lax.dot_general(a_ref[...], b_ref[...], dn, preferred_element_type=F32)

        def finish(acc):
            if has_add:
                acc = acc + add_scale * add_ref[...]
            o_ref[...] = acc.astype(out_dtype)

        if not use_acc:
            finish(part)
        else:
            acc_ref = refs[-1]
            k = pl.program_id(2)

            @pl.when(k == 0)
            def _():
                acc_ref[...] = part

            @pl.when(k > 0)
            def _():
                acc_ref[...] += part

            @pl.when(k == nk - 1)
            def _():
                finish(acc_ref[...])

    return pl.pallas_call(
        body, name=name, grid=(m // tm, n // tn, nk), in_specs=in_specs, out_specs=o_spec, out_shape=out_shape,
        scratch_shapes=[pltpu.VMEM((tm, tn), F32)] if use_acc else [],
        compiler_params=_cparams(("parallel", "parallel", "arbitrary")),
    )(*args)


def _ln_fwd(x, m, g, b, *, name, tm=512):
    t, d = x.shape

    def body(x_ref, m_ref, g_ref, b_ref, y_ref, yb_ref, xhat_ref, rstd_ref):
        z = ALPHA * x_ref[...] + m_ref[...]
        mu = jnp.mean(z, axis=-1, keepdims=True)
        zc = z - mu
        var = jnp.mean(zc * zc, axis=-1, keepdims=True)
        rstd = lax.rsqrt(var + LN_EPS)
        xhat = zc * rstd
        y = xhat * g_ref[...] + b_ref[...]
        y_ref[...] = y
        yb_ref[...] = y.astype(BF16)
        xhat_ref[...] = xhat
        rstd_ref[...] = rstd

    row = pl.BlockSpec((tm, d), lambda i: (i, 0))
    vec = pl.BlockSpec((1, d), lambda i: (0, 0))
    return pl.pallas_call(
        body, name=name, grid=(t // tm,), in_specs=[row, row, vec, vec],
        out_specs=[row, row, row, pl.BlockSpec((tm, 1), lambda i: (i, 0))],
        out_shape=[jax.ShapeDtypeStruct((t, d), F32), jax.ShapeDtypeStruct((t, d), BF16),
                   jax.ShapeDtypeStruct((t, d), F32), jax.ShapeDtypeStruct((t, 1), F32)],
        compiler_params=_cparams(("parallel",)),
    )(x, m, g, b)


def _ln_bwd(dy, xhat, rstd, g, *, name, tm=512):
    t, d = dy.shape

    def body(dy_ref, xhat_ref, rstd_ref, g_ref, dz_ref, dzb_ref, dg_ref, db_ref):
        @pl.when(pl.program_id(0) == 0)
        def _():
            dg_ref[...] = jnp.zeros_like(dg_ref)
            db_ref[...] = jnp.zeros_like(db_ref)

        dyv = dy_ref[...]
        xh = xhat_ref[...]
        dg_ref[...] += jnp.sum(dyv * xh, axis=0, keepdims=True)
        db_ref[...] += jnp.sum(dyv, axis=0, keepdims=True)
        dxh = dyv * g_ref[...]
        m1 = jnp.mean(dxh, axis=-1, keepdims=True)
        m2 = jnp.mean(dxh * xh, axis=-1, keepdims=True)
        dz = rstd_ref[...] * (dxh - m1 - xh * m2)
        dz_ref[...] = dz
        dzb_ref[...] = dz.astype(BF16)

    row = pl.BlockSpec((tm, d), lambda i: (i, 0))
    vec = pl.BlockSpec((1, d), lambda i: (0, 0))
    return pl.pallas_call(
        body, name=name, grid=(t // tm,), in_specs=[row, row, pl.BlockSpec((tm, 1), lambda i: (i, 0)), vec],
        out_specs=[row, row, vec, vec],
        out_shape=[jax.ShapeDtypeStruct((t, d), F32), jax.ShapeDtypeStruct((t, d), BF16),
                   jax.ShapeDtypeStruct((1, d), F32), jax.ShapeDtypeStruct((1, d), F32)],
        compiler_params=_cparams(("arbitrary",)),
    )(dy, xhat, rstd, g)


def _ple_fwd(x2, gl, pe, gate_b, *, name, tm=512):
    t, d = x2.shape

    def body(x_ref, gl_ref, pe_ref, b_ref, y_ref, yb_ref):
        y = x_ref[...] + _sigmoid(gl_ref[...] + b_ref[...]) * pe_ref[...]
        y_ref[...] = y
        yb_ref[...] = y.astype(BF16)

    row = pl.BlockSpec((tm, d), lambda i: (i, 0))
    vec = pl.BlockSpec((1, d), lambda i: (0, 0))
    return pl.pallas_call(
        body, name=name, grid=(t // tm,), in_specs=[row, row, row, vec], out_specs=[row, row],
        out_shape=[jax.ShapeDtypeStruct((t, d), F32), jax.ShapeDtypeStruct((t, d), BF16)],
        compiler_params=_cparams(("parallel",)),
    )(x2, gl, pe, gate_b)


def _ple_bwd(dx3, gl, pe, gate_b, *, name, tm=512):
    t, d = dx3.shape

    def body(dx_ref, gl_ref, pe_ref, b_ref, dgl_ref, dpe_ref, db_ref):
        @pl.when(pl.program_id(0) == 0)
        def _():
            db_ref[...] = jnp.zeros_like(db_ref)

        dx = dx_ref[...]
        gt = _sigmoid(gl_ref[...] + b_ref[...])
        dgl = dx * pe_ref[...] * gt * (1.0 - gt)
        db_ref[...] += jnp.sum(dgl, axis=0, keepdims=True)
        dgl_ref[...] = dgl.astype(BF16)
        dpe_ref[...] = (dx * gt).astype(BF16)

    row = pl.BlockSpec((tm, d), lambda i: (i, 0))
    vec = pl.BlockSpec((1, d), lambda i: (0, 0))
    return pl.pallas_call(
        body, name=name, grid=(t // tm,), in_specs=[row, row, row, vec], out_specs=[row, row, vec],
        out_shape=[jax.ShapeDtypeStruct((t, d), BF16), jax.ShapeDtypeStruct((t, d), BF16),
                   jax.ShapeDtypeStruct((1, d), F32)],
        compiler_params=_cparams(("arbitrary",)),
    )(dx3, gl, pe, gate_b)


def _loss_bwd(y, tgt, *, name, tm=512):
    t, d = y.shape

    def body(y_ref, t_ref, dy_ref, l_ref):
        @pl.when(pl.program_id(0) == 0)
        def _():
            l_ref[...] = jnp.zeros_like(l_ref)

        err = y_ref[...] - t_ref[...]
        dy_ref[...] = err * (1.0 / d)
        part = jnp.sum(jnp.sum(err * err, axis=0, keepdims=True), axis=1, keepdims=True) * (0.5 / d)
        l_ref[...] += jnp.broadcast_to(part, l_ref.shape)

    row = pl.BlockSpec((tm, d), lambda i: (i, 0))
    return pl.pallas_call(
        body, name=name, grid=(t // tm,), in_specs=[row, row],
        out_specs=[row, pl.BlockSpec((1, LANES), lambda i: (0, 0))],
        out_shape=[jax.ShapeDtypeStruct((t, d), F32), jax.ShapeDtypeStruct((1, LANES), F32)],
        compiler_params=_cparams(("arbitrary",)),
    )(y, tgt)


def _conv_causal(x, halo, w_ref, b, kw):
    acc = x * w_ref[kw - 1:kw, :] + b
    for k in range(kw - 1):
        acc = acc + _shift_down(x, halo, kw - 1 - k) * w_ref[k:k + 1, :]
    return acc


def _ffn_act_fwd(hdn, conv_w, conv_b, *, name, tm=512):
    _, t, f = hdn.shape
    tc = _tile(f, 1408)
    hb = tm // SUBLANES

    def body(h_ref, halo_ref, w_ref, b_ref, a_ref):
        first = pl.program_id(1) == 0
        parts = []
        for s in range(2):
            halo = jnp.where(first, 0.0, halo_ref[s])
            parts.append(_conv_causal(h_ref[s], halo, w_ref.at[s], b_ref[s], CONV_F))
        a_ref[...] = (_gelu(parts[1]) * parts[0]).astype(BF16)

    return pl.pallas_call(
        body, name=name, grid=(f // tc, t // tm),
        in_specs=[pl.BlockSpec((2, tm, tc), lambda j, i: (0, i, j)),
                  pl.BlockSpec((2, SUBLANES, tc), lambda j, i: (0, jnp.maximum(i * hb - 1, 0), j)),
                  pl.BlockSpec((2, CONV_F, tc), lambda j, i: (0, 0, j)),
                  pl.BlockSpec((2, 1, tc), lambda j, i: (0, 0, j))],
        out_specs=pl.BlockSpec((tm, tc), lambda j, i: (i, j)),
        out_shape=jax.ShapeDtypeStruct((t, f), BF16),
        compiler_params=_cparams(("parallel", "arbitrary")),
    )(hdn, hdn, conv_w, conv_b)


def _ffn_act_bwd(da, hdn, conv_w, conv_b, *, name, tm=512):
    _, t, f = hdn.shape
    tc = _tile(f, 1408)
    hb = tm // SUBLANES
    nr = t // tm

    strip = 2 * SUBLANES
    ns = tm // strip
    pieces = [(c0, min(2 * LANES, tc - c0)) for c0 in range(0, tc, 2 * LANES)]

    def body(da_ref, h_ref, halo_ref, w_ref, b_ref, dh_ref, dw_ref, db_ref, carry_ref, dw_acc, db_acc):
        i = pl.program_id(1)
        r = nr - 1 - i

        @pl.when(i == 0)
        def _():
            carry_ref[...] = jnp.zeros_like(carry_ref)
            dw_acc[...] = jnp.zeros_like(dw_acc)
            db_acc[...] = jnp.zeros_like(db_acc)

        def do_strip(si, carry):
            s = ns - 1 - si
            r0 = pl.multiple_of(s * strip, strip)
            above = pl.multiple_of(jnp.maximum(s * strip - SUBLANES, 0), SUBLANES)
            for c0, cw in pieces:
                cols = slice(c0, c0 + cw)
                xs, hcs = [], []
                for part in range(2):
                    x = h_ref[part, pl.ds(r0, strip), cols]
                    halo = jnp.where(r == 0, 0.0, halo_ref[part, :, cols])
                    prev = jnp.where(s == 0, halo, h_ref[part, pl.ds(above, SUBLANES), cols])
                    sh = [_shift_down(x, prev, k) for k in range(1, CONV_F)]
                    hc = x * w_ref[part, CONV_F - 1:CONV_F, cols] + b_ref[part, :, cols]
                    for k in range(1, CONV_F):
                        hc = hc + sh[k - 1] * w_ref[part, CONV_F - 1 - k:CONV_F - k, cols]
                    xs.append([x] + sh)
                    hcs.append(hc)
                g, dg = _gelu_and_grad(hcs[1])
                dav = da_ref[pl.ds(r0, strip), cols]
                dhc = [dav * g, dav * hcs[0] * dg]
                for part in range(2):
                    d = dhc[part]
                    nxt = carry_ref[part, :, cols]
                    db_acc[part, :, cols] += d
                    out = d * w_ref[part, CONV_F - 1:CONV_F, cols]
                    for k in range(CONV_F):
                        dw_acc[part, CONV_F - 1 - k, :, cols] += d * xs[part][k]
                        if k:
                            out = out + _shift_up(d, nxt, k) * w_ref[part, CONV_F - 1 - k:CONV_F - k, cols]
                    dh_ref[part, pl.ds(r0, strip), cols] = out.astype(BF16)
                    carry_ref[part, :, cols] = d[:SUBLANES]
            return carry

        lax.fori_loop(0, ns, do_strip, 0)

        @pl.when(i == nr - 1)
        def _():
            dw_ref[...] = jnp.sum(dw_acc[...], axis=2)
            db_ref[...] = jnp.sum(db_acc[...], axis=1, keepdims=True)

    return pl.pallas_call(
        body, name=name, grid=(f // tc, nr),
        in_specs=[pl.BlockSpec((tm, tc), lambda j, i: (nr - 1 - i, j)),
                  pl.BlockSpec((2, tm, tc), lambda j, i: (0, nr - 1 - i, j)),
                  pl.BlockSpec((2, SUBLANES, tc), lambda j, i: (0, jnp.maximum((nr - 1 - i) * hb - 1, 0), j)),
                  pl.BlockSpec((2, CONV_F, tc), lambda j, i: (0, 0, j)),
                  pl.BlockSpec((2, 1, tc), lambda j, i: (0, 0, j))],
        out_specs=[pl.BlockSpec((2, tm, tc), lambda j, i: (0, nr - 1 - i, j)),
                   pl.BlockSpec((2, CONV_F, tc), lambda j, i: (0, 0, j)),
                   pl.BlockSpec((2, 1, tc), lambda j, i: (0, 0, j))],
        out_shape=[jax.ShapeDtypeStruct((2, t, f), BF16), jax.ShapeDtypeStruct((2, CONV_F, f), F32),
                   jax.ShapeDtypeStruct((2, 1, f), F32)],
        scratch_shapes=[pltpu.VMEM((2, SUBLANES, tc), F32), pltpu.VMEM((2, CONV_F, 2 * SUBLANES, tc), F32),
                        pltpu.VMEM((2, 2 * SUBLANES, tc), F32)],
        compiler_params=_cparams(("arbitrary", "arbitrary")),
    )(da, hdn, hdn, conv_w, conv_b)


def _fgate_fwd(xb, w_f, b_f, *, name, tm=256):
    t, d = xb.shape

    def body(x_ref, w_ref, b_ref, z_ref, c_ref, carry_ref):
        @pl.when(pl.program_id(0) == 0)
        def _():
            carry_ref[...] = jnp.zeros_like(carry_ref)

        z = jnp.dot(x_ref[...], w_ref[...], preferred_element_type=F32) + b_ref[...]
        z_ref[...] = z
        ls = _log_sigmoid(z)
        rr = lax.broadcasted_iota(jnp.int32, (tm, tm), 0)
        cc = lax.broadcasted_iota(jnp.int32, (tm, tm), 1)
        tri = (cc <= rr).astype(BF16)
        cum = carry_ref[...]
        for piece in _split3(ls):
            cum = cum + jnp.dot(tri, piece, preferred_element_type=F32)
        c_ref[...] = cum
        carry_ref[...] = cum[tm - 1:tm, :]

    return pl.pallas_call(
        body, name=name, grid=(t // tm,),
        in_specs=[pl.BlockSpec((tm, d), lambda i: (i, 0)), pl.BlockSpec((d, LANES), lambda i: (0, 0)),
                  pl.BlockSpec((1, LANES), lambda i: (0, 0))],
        out_specs=[pl.BlockSpec((tm, LANES), lambda i: (i, 0)), pl.BlockSpec((tm, LANES), lambda i: (i, 0))],
        out_shape=[jax.ShapeDtypeStruct((t, LANES), F32), jax.ShapeDtypeStruct((t, LANES), F32)],
        scratch_shapes=[pltpu.VMEM((1, LANES), F32)],
        compiler_params=_cparams(("arbitrary",)),
    )(xb, w_f, b_f)


def _fgate_bwd(dc, z, *, name, tm=256):
    t = dc.shape[0]
    nr = t // tm

    def body(dc_ref, z_ref, dz_ref, db_ref, carry_ref):
        @pl.when(pl.program_id(0) == 0)
        def _():
            carry_ref[...] = jnp.zeros_like(carry_ref)
            db_ref[...] = jnp.zeros_like(db_ref)

        rr = lax.broadcasted_iota(jnp.int32, (tm, tm), 0)
        cc = lax.broadcasted_iota(jnp.int32, (tm, tm), 1)
        tri = (cc >= rr).astype(BF16)
        cum = carry_ref[...]
        for piece in _split3(dc_ref[...]):
            cum = cum + jnp.dot(tri, piece, preferred_element_type=F32)
        carry_ref[...] = cum[0:1, :]
        dz = cum * _sigmoid(-z_ref[...])
        db_ref[...] += jnp.sum(dz, axis=0, keepdims=True)
        dz_ref[...] = dz.astype(BF16)

    rev = pl.BlockSpec((tm, LANES), lambda i: (nr - 1 - i, 0))
    return pl.pallas_call(
        body, name=name, grid=(nr,), in_specs=[rev, rev],
        out_specs=[rev, pl.BlockSpec((1, LANES), lambda i: (0, 0))],
        out_shape=[jax.ShapeDtypeStruct((t, LANES), BF16), jax.ShapeDtypeStruct((1, LANES), F32)],
        scratch_shapes=[pltpu.VMEM((1, LANES), F32)],
        compiler_params=_cparams(("arbitrary",)),
    )(dc, z)


HM = 2 * HEAD_DIM
LANE_C, LANE_ONE, LANE_LSE = 64, 67, 70
Q_SUB = 256


def _three_parts(col, sign=1.0):
    col = sign * col
    hi = col.astype(BF16).astype(F32)
    r1 = col - hi
    mid = r1.astype(BF16).astype(F32)
    lo = (r1 - mid).astype(BF16).astype(F32)
    return hi, mid, lo


def _fill_lanes(base, lane, first, parts):
    out = base
    for n, part in enumerate(parts):
        out = jnp.where(lane == first + n, part, out)
    return out


def _attn_prep(qkv, c, *, name, tm=256):
    t, w_in = qkv.shape
    nh = w_in // (3 * HEAD_DIM)
    w = 3 * nh * HM

    def body(x_ref, c_ref, o_ref):
        lane = lax.broadcasted_iota(jnp.int32, (1, HM), 1)
        cblk = c_ref[...]
        for h in range(nh):
            ch = cblk[:, h:h + 1]
            pos = _three_parts(ch)
            neg = _three_parts(ch, -1.0)
            for part in range(3):
                col = (part * nh + h) * HM
                pair = (part * nh + h - h % 2) * HEAD_DIM
                x = x_ref[:, pair:pair + HM].astype(F32)
                if h % 2:
                    x = pltpu.roll(x, HEAD_DIM, axis=1)
                x = jnp.where(lane < HEAD_DIM, x, 0.0)
                if part == 0:
                    y = _fill_lanes(_fill_lanes(x * ATTN_SCALE, lane, LANE_C, pos), lane, LANE_ONE, (1.0, 1.0, 1.0))
                elif part == 1:
                    y = _fill_lanes(_fill_lanes(x, lane, LANE_C, (1.0, 1.0, 1.0)), lane, LANE_ONE, neg)
                    y = _fill_lanes(y, lane, LANE_LSE, (1.0, 1.0, 1.0))
                else:
                    y = _fill_lanes(x, lane, LANE_C, (1.0, 1.0, 1.0))
                o_ref[:, col:col + HM] = y.astype(BF16)

    return pl.pallas_call(
        body, name=name, grid=(t // tm,),
        in_specs=[pl.BlockSpec((tm, w_in), lambda i: (i, 0)), pl.BlockSpec((tm, LANES), lambda i: (i, 0))],
        out_specs=pl.BlockSpec((tm, w), lambda i: (i, 0)),
        out_shape=jax.ShapeDtypeStruct((t, w), BF16),
        compiler_params=_cparams(("parallel",)),
    )(qkv, c)


def _attn_compact(parts, *, name, tm=512):
    t, w = parts[0].shape
    nh = w // HM
    n_parts = len(parts)

    def body(*refs):
        o_ref = refs[n_parts]
        lane = lax.broadcasted_iota(jnp.int32, (1, HM), 1)
        for part, ref in enumerate(refs[:n_parts]):
            for h in range(0, nh, 2):
                even = ref[:, h * HM:(h + 1) * HM]
                odd = pltpu.roll(ref[:, (h + 1) * HM:(h + 2) * HM].astype(F32), HEAD_DIM, axis=1).astype(BF16)
                col = (part * nh + h) * HEAD_DIM
                o_ref[:, col:col + HM] = jnp.where(lane < HEAD_DIM, even, odd)

    row = pl.BlockSpec((tm, w), lambda i: (i, 0))
    return pl.pallas_call(
        body, name=name, grid=(t // tm,), in_specs=[row] * n_parts,
        out_specs=pl.BlockSpec((tm, n_parts * nh * HEAD_DIM), lambda i: (i, 0)),
        out_shape=jax.ShapeDtypeStruct((t, n_parts * nh * HEAD_DIM), BF16),
        compiler_params=_cparams(("parallel",)),
    )(*parts)


def _flash_fwd_hm(qkva, *, name, tq=1024):
    t, w = qkva.shape
    nh = w // (3 * HM)
    nq = t // tq
    nsub = tq // Q_SUB

    def body(q_ref, k_ref, v_ref, o_ref, qb_ref, s_scr, p_scr, m_scr, acc_scr):
        i = pl.program_id(1)
        lane = lax.broadcasted_iota(jnp.int32, (1, HM), 1)
        rr = lax.broadcasted_iota(jnp.int32, (Q_SUB, tq), 0)
        cc = lax.broadcasted_iota(jnp.int32, (Q_SUB, tq), 1)
        sub = lambda r: slice(r * Q_SUB, (r + 1) * Q_SUB)
        nt = (((1,), (1,)), ((), ()))

        def scores(j):
            kj = k_ref[pl.ds(pl.multiple_of(j * tq, tq), tq), :]
            for r in range(nsub):
                s_scr[sub(r), :] = lax.dot_general(q_ref[sub(r), :], kj, nt, preferred_element_type=F32)

        def step(jp, masked, with_pv=True):
            vj = v_ref[pl.ds(pl.multiple_of(jp * tq, tq), tq), :]
            for r in range(nsub):
                if masked:
                    s_scr[sub(r), :] = jnp.where(cc <= rr + r * Q_SUB, s_scr[sub(r), :], NEG_BIG)
                m_old = m_scr[sub(r), :]
                m_new = jnp.maximum(m_old, jnp.max(s_scr[sub(r), :], axis=1, keepdims=True))
                m_scr[sub(r), :] = m_new
                if with_pv:
                    pv = jnp.dot(p_scr[sub(r), :], vj, preferred_element_type=F32)
                    acc_scr[sub(r), :] = (acc_scr[sub(r), :] + pv) * jnp.exp(m_old - m_new)
                p_scr[sub(r), :] = jnp.exp(s_scr[sub(r), :] - m_new).astype(BF16)

        def kv_step(j, carry):
            step(j - 1, False)
            scores(j + 1)
            return carry

        p_scr[...] = jnp.zeros_like(p_scr)
        m_scr[...] = jnp.full_like(m_scr, NEG_BIG)
        acc_scr[...] = jnp.zeros_like(acc_scr)
        scores(0)

        @pl.when(i > 0)
        def _():
            step(0, False, with_pv=False)
            scores(1)

        lax.fori_loop(1, i, kv_step, 0)
        step(jnp.maximum(i - 1, 0), True)
        vi = v_ref[pl.ds(pl.multiple_of(i * tq, tq), tq), :]
        for r in range(nsub):
            acc = acc_scr[sub(r), :] + jnp.dot(p_scr[sub(r), :], vi, preferred_element_type=F32)
            l = jnp.sum(jnp.where(lane == LANE_C, acc, 0.0), axis=1, keepdims=True)
            o_ref[sub(r), :] = jnp.where(lane < HEAD_DIM, acc / l, 0.0).astype(BF16)
            lse = m_scr[sub(r), :] + jnp.log(l)
            qb = _fill_lanes(q_ref[sub(r), :].astype(F32), lane, LANE_LSE, _three_parts(lse, -1.0))
            qb_ref[sub(r), :] = qb.astype(BF16)

    blk = lambda part: pl.BlockSpec((t, HM), lambda h, i: (0, part * nh + h))
    tile = pl.BlockSpec((tq, HM), lambda h, i: (i, h))
    return pl.pallas_call(
        body, name=name, grid=(nh, nq), in_specs=[tile, blk(1), blk(2)], out_specs=[tile, tile],
        out_shape=[jax.ShapeDtypeStruct((t, nh * HM), BF16), jax.ShapeDtypeStruct((t, nh * HM), BF16)],
        scratch_shapes=[pltpu.VMEM((tq, tq), F32), pltpu.VMEM((tq, tq), BF16), pltpu.VMEM((tq, 1), F32),
                        pltpu.VMEM((tq, HM), F32)],
        compiler_params=_cparams(("parallel", "arbitrary")),
    )(qkva, qkva, qkva)


def _attn_prep_bwd(o, do, *, name, tm=512):
    t, w = o.shape
    nh = w // HM

    def body(o_ref, do_ref, out_ref):
        lane = lax.broadcasted_iota(jnp.int32, (1, HM), 1)
        for h in range(nh):
            cols = slice(h * HM, (h + 1) * HM)
            pair = (h - h % 2) * HEAD_DIM
            dov = do_ref[:, pair:pair + HM].astype(F32)
            if h % 2:
                dov = pltpu.roll(dov, HEAD_DIM, axis=1)
            dov = jnp.where(lane < HEAD_DIM, dov, 0.0)
            delta = jnp.sum(o_ref[:, cols].astype(F32) * dov, axis=1, keepdims=True)
            out_ref[:, cols] = _fill_lanes(dov, lane, LANE_C, _three_parts(delta, -1.0)).astype(BF16)

    row = pl.BlockSpec((tm, w), lambda i: (i, 0))
    return pl.pallas_call(
        body, name=name, grid=(t // tm,), in_specs=[row, pl.BlockSpec((tm, nh * HEAD_DIM), lambda i: (i, 0))],
        out_specs=row, out_shape=jax.ShapeDtypeStruct((t, w), BF16), compiler_params=_cparams(("parallel",)),
    )(o, do)


def _flash_bwd_hm(qb, doa, qkva, *, name, tq=512, tqc=512):
    t, w = qb.shape
    nh = w // HM
    nq = t // tq
    nqc = t // tqc
    nsub = tqc // Q_SUB
    grp = slice(LANE_C, LANE_C + SUBLANES)

    def body(q_ref, do_ref, k_ref, v_ref, dq_ref, dk_ref, dv_ref, dcq_ref, dcj_ref,
             dq_acc, st_scr, dpt_scr, pt_scr, ds_scr, dk_scr, dv_scr):
        j = pl.program_id(1)
        kv_minus_q = (lax.broadcasted_iota(jnp.int32, (tq, Q_SUB), 0)
                      - lax.broadcasted_iota(jnp.int32, (tq, Q_SUB), 1))
        sub = lambda r: slice(r * Q_SUB, (r + 1) * Q_SUB)
        nt = (((1,), (1,)), ((), ()))
        tn = (((0,), (0,)), ((), ()))
        first = lax.div(j * tq, tqc)

        @pl.when(j == 0)
        def _():
            dq_acc[...] = jnp.zeros_like(dq_acc)

        def rows_of(i, r):
            return pl.ds(pl.multiple_of(i * tqc + r * Q_SUB, Q_SUB), Q_SUB)

        def scores(i):
            for r in range(nsub):
                st_scr[:, sub(r)] = lax.dot_general(k_ref[...], q_ref[rows_of(i, r), :], nt, preferred_element_type=F32)
                dpt_scr[:, sub(r)] = lax.dot_general(v_ref[...], do_ref[rows_of(i, r), :], nt,
                                                     preferred_element_type=F32)

        def products(ip):
            for r in range(nsub):
                dv_scr[...] += jnp.dot(pt_scr[:, sub(r)], do_ref[rows_of(ip, r), :], preferred_element_type=F32)
                dk_scr[...] += jnp.dot(ds_scr[:, sub(r)], q_ref[rows_of(ip, r), :], preferred_element_type=F32)
                dq_acc[rows_of(ip, r), :] += lax.dot_general(ds_scr[:, sub(r)], k_ref[...], tn,
                                                             preferred_element_type=F32)

        def probabilities(masked):
            for r in range(nsub):
                st = st_scr[:, sub(r)]
                if masked:
                    st = jnp.where(kv_minus_q <= first * tqc + r * Q_SUB - j * tq, st, NEG_BIG)
                pt = jnp.exp(st)
                pt_scr[:, sub(r)] = pt.astype(BF16)
                ds_scr[:, sub(r)] = (pt * dpt_scr[:, sub(r)]).astype(BF16)

        def q_step(i, carry):
            products(i - 1)
            probabilities(False)
            scores(jnp.minimum(i + 1, nqc - 1))
            return carry

        dk_scr[...] = jnp.zeros_like(dk_scr)
        dv_scr[...] = jnp.zeros_like(dv_scr)
        scores(first)
        probabilities(True)
        scores(jnp.minimum(first + 1, nqc - 1))
        lax.fori_loop(first + 1, nqc, q_step, 0)
        products(nqc - 1)
        dk = dk_scr[...]
        dk_ref[...] = dk.astype(BF16)
        dv_ref[...] = dv_scr[...].astype(BF16)
        dcj_ref[...] = dk.T[grp, :]

        @pl.when(j == nq - 1)
        def _():
            dq_ref[...] = (dq_acc[...] * ATTN_SCALE).astype(BF16)
            for cidx in range(nq):
                rows = slice(cidx * tq, (cidx + 1) * tq)
                dcq_ref[:, rows] = dq_acc[rows, :].T[grp, :]

    full = pl.BlockSpec((t, HM), lambda h, j: (0, h))
    ktile = lambda part: pl.BlockSpec((tq, HM), lambda h, j: (j, part * nh + h))
    tile = pl.BlockSpec((tq, HM), lambda h, j: (j, h))
    hm_shape = jax.ShapeDtypeStruct((t, w), BF16)
    row_shape = jax.ShapeDtypeStruct((nh, SUBLANES, t), F32)
    return pl.pallas_call(
        body, name=name, grid=(nh, nq),
        in_specs=[pl.BlockSpec((t, HM), lambda h, j: (0, h), pipeline_mode=pl.Buffered(1)),
                  pl.BlockSpec((t, HM), lambda h, j: (0, h), pipeline_mode=pl.Buffered(1)), ktile(1), ktile(2)],
        out_specs=[full, tile, tile, pl.BlockSpec((None, SUBLANES, t), lambda h, j: (h, 0, 0)),
                   pl.BlockSpec((None, SUBLANES, tq), lambda h, j: (h, 0, j))],
        out_shape=[hm_shape, hm_shape, hm_shape, row_shape, row_shape],
        scratch_shapes=[pltpu.VMEM((t, HM), F32), pltpu.VMEM((tq, tqc), F32), pltpu.VMEM((tq, tqc), F32),
                        pltpu.VMEM((tq, tqc), BF16), pltpu.VMEM((tq, tqc), BF16), pltpu.VMEM((tq, HM), F32),
                        pltpu.VMEM((tq, HM), F32)],
        compiler_params=_cparams(("parallel", "arbitrary")),
    )(qb, doa, qkva, qkva)


def _block_diag_dot(xb, w_ref, transpose_w=False):
    outs = []
    for n in range(N_BLOCKS_B):
        xn = xb[:, n * BLOCK_B:(n + 1) * BLOCK_B]
        dn = (((1,), (1 if transpose_w else 0,)), ((), ()))
        outs.append(lax.dot_general(xn, w_ref[n], dn, preferred_element_type=F32))
    return jnp.concatenate(outs, axis=1)


def _rglru_fwd(proj, conv_w, conv_b, w_a, b_a, w_i, b_i, lam, *, name, tm=256):
    _, t, d = proj.shape
    hb = tm // SUBLANES
    ng = tm // SUBLANES

    def body(p_ref, halo_ref, cw_ref, cb_ref, wa_ref, ba_ref, wi_ref, bi_ref, lam_ref,
             xb_ref, r_ref, ig_ref, a_ref, h_ref, y_ref, u_scr, hc_scr):
        first = pl.program_id(0) == 0

        @pl.when(first)
        def _():
            hc_scr[...] = jnp.zeros_like(hc_scr)

        halo = jnp.where(first, 0.0, halo_ref[0])
        xb = _conv_causal(p_ref[0], halo, cw_ref, cb_ref[...], CONV_B)
        xb_ref[...] = xb
        xbb = xb.astype(BF16)
        r = _sigmoid(_block_diag_dot(xbb, wa_ref) + ba_ref[...])
        ig = _sigmoid(_block_diag_dot(xbb, wi_ref) + bi_ref[...])
        r_ref[...] = r
        ig_ref[...] = ig
        log_a = (-LRU_C) * r * _softplus(-lam_ref[...])
        a_ref[...] = jnp.exp(log_a)
        u_scr[...] = jnp.sqrt(_neg_expm1(2.0 * log_a)) * (ig * xb)

        ones8 = jnp.ones((SUBLANES, d), F32)
        zeros8 = jnp.zeros((SUBLANES, d), F32)

        def group(gi, hprev):
            off = pl.multiple_of(gi * SUBLANES, SUBLANES)
            a8 = a_ref[pl.ds(off, SUBLANES), :]
            u8 = u_scr[pl.ds(off, SUBLANES), :]
            for s in (1, 2, 4):
                u8 = a8 * _shift_down(u8, zeros8, s) + u8
                a8 = a8 * _shift_down(a8, ones8, s)
            h8 = a8 * hprev + u8
            h_ref[pl.ds(off, SUBLANES), :] = h8
            return h8[SUBLANES - 1:SUBLANES, :]

        hc_scr[...] = lax.fori_loop(0, ng, group, hc_scr[...])
        y_ref[...] = (h_ref[...] * _gelu(p_ref[1])).astype(BF16)

    row = pl.BlockSpec((tm, d), lambda i: (i, 0))
    vec = pl.BlockSpec((1, d), lambda i: (0, 0))
    wblk = pl.BlockSpec((N_BLOCKS_B, BLOCK_B, BLOCK_B), lambda i: (0, 0, 0))
    f32_td = jax.ShapeDtypeStruct((t, d), F32)
    return pl.pallas_call(
        body, name=name, grid=(t // tm,),
        in_specs=[pl.BlockSpec((2, tm, d), lambda i: (0, i, 0)),
                  pl.BlockSpec((1, SUBLANES, d), lambda i: (0, jnp.maximum(i * hb - 1, 0), 0)),
                  pl.BlockSpec((CONV_B, d), lambda i: (0, 0)), vec, wblk, vec, wblk, vec, vec],
        out_specs=[row, row, row, row, row, row],
        out_shape=[f32_td, f32_td, f32_td, f32_td, f32_td, jax.ShapeDtypeStruct((t, d), BF16)],
        scratch_shapes=[pltpu.VMEM((tm, d), F32), pltpu.VMEM((1, d), F32)],
        compiler_params=_cparams(("arbitrary",)),
    )(proj, proj, conv_w, conv_b, w_a, b_a, w_i, b_i, lam)


def _rglru_bwd(dy, proj, xb, r, ig, a, h, conv_w, w_a, w_i, lam, *, name, tm=256):
    _, t, d = proj.shape
    hb = tm // SUBLANES
    ng = tm // SUBLANES
    nr = t // tm

    def body(dy_ref, p_ref, phalo_ref, xb_ref, r_ref, ig_ref, a_ref, h_ref, hhalo_ref,
             cw_ref, wa_ref, wi_ref, lam_ref,
             dp_ref, dcw_ref, dcb_ref, dba_ref, dbi_ref, dlam_ref, dwa_ref, dwi_ref,
             g_scr, carry_g, carry_dxb):
        i = pl.program_id(0)
        rpos = nr - 1 - i

        @pl.when(i == 0)
        def _():
            for ref in (dcw_ref, dcb_ref, dba_ref, dbi_ref, dlam_ref, dwa_ref, dwi_ref, carry_g, carry_dxb):
                ref[...] = jnp.zeros_like(ref)

        gb = p_ref[1]
        gel, dgel = _gelu_and_grad(gb)
        dyv = dy_ref[...]
        hv = h_ref[...]
        dp_ref[1] = (dyv * hv * dgel).astype(BF16)
        g_scr[...] = dyv * gel
        av = a_ref[...]

        ones8 = jnp.ones((SUBLANES, d), F32)
        zeros8 = jnp.zeros((SUBLANES, d), F32)

        def group(gi, cin):
            off = pl.multiple_of((ng - 1 - gi) * SUBLANES, SUBLANES)
            g8 = g_scr[pl.ds(off, SUBLANES), :]
            a8 = a_ref[pl.ds(off, SUBLANES), :]
            b8 = _shift_up(a8, ones8, 1)
            row8 = lax.broadcasted_iota(jnp.int32, (SUBLANES, d), 0)
            g8 = g8 + jnp.where(row8 == SUBLANES - 1, cin, 0.0)
            b8 = jnp.where(row8 == SUBLANES - 1, 0.0, b8)
            for s in (1, 2, 4):
                g8 = g8 + b8 * _shift_up(g8, zeros8, s)
                b8 = b8 * _shift_up(b8, zeros8, s)
            g_scr[pl.ds(off, SUBLANES), :] = g8
            return a8[0:1, :] * g8[0:1, :]

        carry_g[...] = lax.fori_loop(0, ng, group, carry_g[...])

        du = g_scr[...]
        hhalo = jnp.where(rpos == 0, 0.0, hhalo_ref[...])
        hprev = _shift_down(hv, hhalo, 1)
        da = du * hprev
        rv = r_ref[...]
        igv = ig_ref[...]
        xbv = xb_ref[...]
        sp = _softplus(-lam_ref[...])
        log_a = (-LRU_C) * rv * sp
        mult = jnp.sqrt(_neg_expm1(2.0 * log_a))
        dmult = du * (igv * xbv)
        dig = du * mult * xbv
        dxb = du * mult * igv
        dlog_a = da * av - dmult * (av * av) / mult
        dr = dlog_a * ((-LRU_C) * sp)
        dsp = jnp.sum(dlog_a * ((-LRU_C) * rv), axis=0, keepdims=True)
        dlam_ref[...] += dsp * (-_sigmoid(-lam_ref[...]))
        dra = dr * rv * (1.0 - rv)
        dia = dig * igv * (1.0 - igv)
        dba_ref[...] += jnp.sum(dra, axis=0, keepdims=True)
        dbi_ref[...] += jnp.sum(dia, axis=0, keepdims=True)
        drab = dra.astype(BF16)
        diab = dia.astype(BF16)
        xbb = xbv.astype(BF16)
        dxb = dxb + _block_diag_dot(drab, wa_ref, True) + _block_diag_dot(diab, wi_ref, True)
        tn = (((0,), (0,)), ((), ()))
        for n in range(N_BLOCKS_B):
            sl = slice(n * BLOCK_B, (n + 1) * BLOCK_B)
            dwa_ref[n] += lax.dot_general(xbb[:, sl], drab[:, sl], tn, preferred_element_type=F32)
            dwi_ref[n] += lax.dot_general(xbb[:, sl], diab[:, sl], tn, preferred_element_type=F32)

        xpre = p_ref[0]
        phalo = jnp.where(rpos == 0, 0.0, phalo_ref[0])
        nxt = carry_dxb[...]
        dcb_ref[...] += jnp.sum(dxb, axis=0, keepdims=True)
        out = dxb * cw_ref[CONV_B - 1:CONV_B, :]
        dcw_ref[CONV_B - 1:CONV_B, :] += jnp.sum(dxb * xpre, axis=0, keepdims=True)
        for k in range(CONV_B - 1):
            sh = CONV_B - 1 - k
            out = out + _shift_up(dxb, nxt, sh) * cw_ref[k:k + 1, :]
            dcw_ref[k:k + 1, :] += jnp.sum(dxb * _shift_down(xpre, phalo, sh), axis=0, keepdims=True)
        dp_ref[0] = out.astype(BF16)
        carry_dxb[...] = dxb[:SUBLANES]

    rev = pl.BlockSpec((tm, d), lambda i: (nr - 1 - i, 0))
    halo8 = pl.BlockSpec((SUBLANES, d), lambda i: (jnp.maximum((nr - 1 - i) * hb - 1, 0), 0))
    vec = pl.BlockSpec((1, d), lambda i: (0, 0))
    wblk = pl.BlockSpec((N_BLOCKS_B, BLOCK_B, BLOCK_B), lambda i: (0, 0, 0))
    vec_shape = jax.ShapeDtypeStruct((1, d), F32)
    w_shape = jax.ShapeDtypeStruct((N_BLOCKS_B, BLOCK_B, BLOCK_B), F32)
    return pl.pallas_call(
        body, name=name, grid=(nr,),
        in_specs=[rev, pl.BlockSpec((2, tm, d), lambda i: (0, nr - 1 - i, 0)),
                  pl.BlockSpec((1, SUBLANES, d), lambda i: (0, jnp.maximum((nr - 1 - i) * hb - 1, 0), 0)),
                  rev, rev, rev, rev, rev, halo8,
                  pl.BlockSpec((CONV_B, d), lambda i: (0, 0)), wblk, wblk, vec],
        out_specs=[pl.BlockSpec((2, tm, d), lambda i: (0, nr - 1 - i, 0)),
                   pl.BlockSpec((CONV_B, d), lambda i: (0, 0)), vec, vec, vec, vec, wblk, wblk],
        out_shape=[jax.ShapeDtypeStruct((2, t, d), BF16), jax.ShapeDtypeStruct((CONV_B, d), F32),
                   vec_shape, vec_shape, vec_shape, vec_shape, w_shape, w_shape],
        scratch_shapes=[pltpu.VMEM((tm, d), F32), pltpu.VMEM((1, d), F32), pltpu.VMEM((SUBLANES, d), F32)],
        compiler_params=_cparams(("arbitrary",)),
    )(dy, proj, proj, xb, r, ig, a, h, h, conv_w, w_a, w_i, lam)


def _split_cols(w, s):
    k, c = w.shape[-2:]
    return jnp.moveaxis(w.reshape(w.shape[:-2] + (k, s, c // s)), -2, -3)


def _merge_cols(w):
    s, k, c = w.shape
    return jnp.moveaxis(w, 0, 1).reshape(k, s * c)


def _local_step(x, p, tgt, w):
    t = x.shape[0]
    bf = lambda v: v.astype(BF16)
    saved = []
    xcur = x
    xcur_b = bf(x)
    for i in range(DEPTH):
        j = i // 2
        L = f"l{i}_"
        sv = {'x_in_b': xcur_b}
        if i % 2 == 0:
            w_in_t = bf(w['a_w_in'][j])
            sv['w_qkv_t'] = w_in_t[:3 * D_MODEL]
            sv['w_f'] = jnp.pad(w_in_t[3 * D_MODEL:].T, ((0, 0), (0, LANES - N_HEADS)))
            sv['w_out'] = bf(w['a_w_out'][j])
            b_f = jnp.pad(w['a_b_f'][j], (0, LANES - N_HEADS)).reshape(1, LANES)
            qkv = _mm(xcur_b, sv['w_qkv_t'], tb=True, out_dtype=BF16, name=L + "qkv")
            z, c = _fgate_fwd(xcur_b, sv['w_f'], b_f, name=L + "fgate")
            qkva = _attn_prep(qkv, c, name=L + "attn_prep")
            o, qb = _flash_fwd_hm(qkva, name=L + "flash_fwd")
            o_c = _attn_compact([o], name=L + "o_compact")
            mix = _mm(o_c, sv['w_out'], name=L + "attn_out")
            sv.update(qkva=qkva, z=z, o=o, o_c=o_c, qb=qb)
        else:
            sv['w_in'] = bf(w['b_w_in'][j])
            sv['w_out'] = bf(w['b_w_out'][j])
            sv['conv_w'] = w['b_conv_w'][j]
            sv['w_a'] = bf(w['b_w_a'][j])
            sv['w_i'] = bf(w['b_w_i'][j])
            sv['lam'] = w['b_lam'][j].reshape(1, D_MODEL)
            proj = _mm(xcur_b, sv['w_in'], out_split=2, name=L + "rg_in")
            xb, r, ig, a, h, y = _rglru_fwd(
                proj, sv['conv_w'], w['b_conv_b'][j].reshape(1, D_MODEL), sv['w_a'],
                w['b_b_a'][j].reshape(1, D_MODEL), sv['w_i'], w['b_b_i'][j].reshape(1, D_MODEL), sv['lam'],
                name=L + "rglru_fwd")
            mix = _mm(y, sv['w_out'], name=L + "rg_out")
            sv.update(proj=proj, xb=xb, r=r, ig=ig, a=a, h=h, y=y)
        sv['ln1_g'] = w['ln1_g'][i].reshape(1, D_MODEL)
        x1, x1b, xhat1, rstd1 = _ln_fwd(xcur, mix, sv['ln1_g'], w['ln1_b'][i].reshape(1, D_MODEL), name=L + "ln1")
        sv['w_up'] = bf(w['f_w_up'][i])
        sv['w_down'] = bf(w['f_w_down'][i])
        sv['fconv_w'] = _split_cols(w['f_conv_w'][i], 2)
        sv['fconv_b'] = w['f_conv_b'][i].reshape(2, 1, D_FF)
        hdn = _mm(x1b, sv['w_up'], out_split=2, tn_cap=1408, name=L + "ffn_up")
        act = _ffn_act_fwd(hdn, sv['fconv_w'], sv['fconv_b'], name=L + "ffn_act")
        ff = _mm(act, sv['w_down'], tk_cap=2816, name=L + "ffn_down")
        sv['ln2_g'] = w['ln2_g'][i].reshape(1, D_MODEL)
        x2, x2b, xhat2, rstd2 = _ln_fwd(x1, ff, sv['ln2_g'], w['ln2_b'][i].reshape(1, D_MODEL), name=L + "ln2")
        sv['gate_w'] = bf(w['ple_gate_w'][i])
        sv['ple_w'] = bf(w['ple_w'][i])
        sv['gate_b'] = w['ple_gate_b'][i].reshape(1, D_MODEL)
        sv['p_b'] = bf(p[i])
        gl = _mm(x2b, sv['gate_w'], name=L + "ple_gate")
        pe = _mm(sv['p_b'], sv['ple_w'], name=L + "ple_emb")
        x3, x3b = _ple_fwd(x2, gl, pe, sv['gate_b'], name=L + "ple")
        sv.update(xhat1=xhat1, rstd1=rstd1, x1b=x1b, hdn=hdn, act=act, xhat2=xhat2, rstd2=rstd2, x2b=x2b,
                  gl=gl, pe=pe)
        saved.append(sv)
        xcur, xcur_b = x3, x3b

    dx, loss_row = _loss_bwd(xcur, tgt, name="loss")

    g = {n: [None] * w[n].shape[0] for n in WEIGHTS}
    for i in reversed(range(DEPTH)):
        j = i // 2
        L = f"l{i}b_"
        sv = saved[i]
        dgl, dpe, d_gate_b = _ple_bwd(dx, sv['gl'], sv['pe'], sv['gate_b'], name=L + "ple")
        g['ple_gate_b'][i] = d_gate_b[0]
        g['ple_w'][i] = _mm(sv['p_b'], dpe, ta=True, name=L + "ple_emb_dw")
        g['ple_gate_w'][i] = _mm(sv['x2b'], dgl, ta=True, name=L + "ple_gate_dw")
        dx2 = _mm(dgl, sv['gate_w'], tb=True, add=dx, name=L + "ple_gate_dx")
        dz2, dz2b, dg2, db2 = _ln_bwd(dx2, sv['xhat2'], sv['rstd2'], sv['ln2_g'], name=L + "ln2")
        g['ln2_g'][i], g['ln2_b'][i] = dg2[0], db2[0]
        g['f_w_down'][i] = _mm(sv['act'], dz2b, ta=True, tm_cap=1408, name=L + "ffn_down_dw")
        da = _mm(dz2b, sv['w_down'], tb=True, tn_cap=1408, name=L + "ffn_down_dx")
        dhdn, d_fcw, d_fcb = _ffn_act_bwd(da, sv['hdn'], sv['fconv_w'], sv['fconv_b'], name=L + "ffn_act")
        g['f_conv_w'][i] = _merge_cols(d_fcw)
        g['f_conv_b'][i] = d_fcb.reshape(2 * D_FF)
        g['f_w_up'][i] = _mm(sv['x1b'], dhdn, ta=True, tn_cap=1408, name=L + "ffn_up_dw")
        dx1 = _mm(dhdn, sv['w_up'], tb=True, add=dz2, add_scale=ALPHA, tk_cap=D_FF, name=L + "ffn_up_dx")
        dz1, dz1b, dg1, db1 = _ln_bwd(dx1, sv['xhat1'], sv['rstd1'], sv['ln1_g'], name=L + "ln1")
        g['ln1_g'][i], g['ln1_b'][i] = dg1[0], db1[0]
        if i % 2 == 0:
            g['a_w_out'][j] = _mm(sv['o_c'], dz1b, ta=True, name=L + "attn_out_dw")
            do = _mm(dz1b, sv['w_out'], tb=True, out_dtype=BF16, name=L + "attn_out_dx")
            doa = _attn_prep_bwd(sv['o'], do, name=L + "attn_prep")
            dqkv = _flash_bwd_hm(sv['qb'], doa, sv['qkva'], name=L + "flash_bwd")
            dcq, dcj = dqkv[3], dqkv[4]
            dc = jnp.pad((dcq[:, 0, :] - dcj[:, LANE_ONE - LANE_C, :]).T, ((0, 0), (0, LANES - N_HEADS)))
            dzf, d_b_f = _fgate_bwd(dc, sv['z'], name=L + "fgate")
            g['a_b_f'][j] = d_b_f[0, :N_HEADS]
            xb_in = sv['x_in_b']
            dqkv_c = _attn_compact(dqkv[:3], name=L + "attn_compact")
            d_w = _mm(dqkv_c, xb_in, ta=True, name=L + "qkv_dw")
            d_wf = _mm(xb_in, dzf, ta=True, name=L + "f_dw")
            g['a_w_in'][j] = jnp.concatenate([d_w, d_wf[:, :N_HEADS].T], axis=0)
            dxa = _mm(dqkv_c, sv['w_qkv_t'], add=dz1, add_scale=ALPHA, name=L + "qkv_dx")
            dx = _mm(dzf, sv['w_f'], tb=True, add=dxa, name=L + "f_dx")
        else:
            g['b_w_out'][j] = _mm(sv['y'], dz1b, ta=True, name=L + "rg_out_dw")
            dy = _mm(dz1b, sv['w_out'], tb=True, name=L + "rg_out_dx")
            dproj, d_cw, d_cb, d_ba, d_bi, d_lam, d_wa, d_wi = _rglru_bwd(
                dy, sv['proj'], sv['xb'], sv['r'], sv['ig'], sv['a'], sv['h'], sv['conv_w'], sv['w_a'], sv['w_i'],
                sv['lam'], name=L + "rglru_bwd")
            g['b_conv_w'][j], g['b_conv_b'][j] = d_cw, d_cb[0]
            g['b_b_a'][j] = d_ba.reshape(N_BLOCKS_B, BLOCK_B)
            g['b_b_i'][j] = d_bi.reshape(N_BLOCKS_B, BLOCK_B)
            g['b_lam'][j] = d_lam[0]
            g['b_w_a'][j], g['b_w_i'][j] = d_wa, d_wi
            g['b_w_in'][j] = _mm(sv['x_in_b'], dproj, ta=True, name=L + "rg_in_dw")
            dx = _mm(dproj, sv['w_in'], tb=True, add=dz1, add_scale=ALPHA, name=L + "rg_in_dx")
    return loss_row, dx, g


def _round_up(n, q):
    return -(-n // q) * q


def _pack(arrs, dtype, row_multiple, lead=0):
    pieces = []
    for a in arrs:
        flat = a.reshape(a.shape[:lead] + (-1,)).astype(dtype)
        n = flat.shape[-1]
        pieces.append(jnp.pad(flat, [(0, 0)] * lead + [(0, _round_up(n, LANES) - n)]))
    flat = jnp.concatenate(pieces, axis=-1)
    rows = _round_up(flat.shape[-1] // LANES, row_multiple)
    flat = jnp.pad(flat, [(0, 0)] * lead + [(0, rows * LANES - flat.shape[-1])])
    return flat.reshape(flat.shape[:lead] + (rows, LANES))


def _unpack(buf, shapes):
    lead = buf.shape[:-2]
    flat = buf.reshape(lead + (-1,))
    out, off = [], 0
    for shp in shapes:
        n = math.prod(shp)
        out.append(flat[..., off:off + n].reshape(lead + tuple(shp)))
        off += _round_up(n, LANES)
    return out


MESH = pl.DeviceIdType.MESH
ANY = pl.BlockSpec(memory_space=pl.ANY)
N_CHUNK = 8


def _all_gather_shards(buf):
    rows, lanes = buf.shape
    half = rows // 2
    ch = half // N_CHUNK
    n_ici = 3 * N_CHUNK

    def body(in_ref, out_ref, send_sems, recv_sems):
        x, y, c = lax.axis_index("x"), lax.axis_index("y"), lax.axis_index("c")
        sibling = (x, y, 1 - c)
        chips = [(1 - x, y), (x, 1 - y), (1 - x, 1 - y)]

        def piece(cx, cy, hc, q):
            return out_ref.at[2 * cx + cy, pl.ds(hc * half + q * ch, ch), :]

        def copy(k, src, dst, to):
            return pltpu.make_async_remote_copy(src_ref=src, dst_ref=dst, send_sem=send_sems.at[k],
                                                recv_sem=recv_sems.at[k], device_id=to, device_id_type=MESH)

        my_chunk = lambda q: in_ref.at[pl.ds(c * half + q * ch, ch), :]
        first, passed = [], []
        for k, chip in enumerate(chips):
            for q in range(N_CHUNK):
                first.append(copy(k * N_CHUNK + q, my_chunk(q), piece(x, y, c, q), (*chip, c)))
                passed.append(copy(n_ici + k * N_CHUNK + q, piece(*chip, c, q), piece(*chip, c, q), sibling))
        for cp in first:
            cp.start()
        for k, chip in enumerate(chips):
            for q in range(N_CHUNK):
                n = k * N_CHUNK + q
                copy(n, my_chunk(q), piece(*chip, c, q), (*chip, c)).wait_recv()
                passed[n].start()
        for k, chip in enumerate(chips):
            for q in range(N_CHUNK):
                copy(n_ici + k * N_CHUNK + q, my_chunk(q), piece(*chip, 1 - c, q), sibling).wait_recv()
        for cp in first + passed:
            cp.wait_send()

    others = pl.pallas_call(
        body, name="gather_weights", in_specs=[ANY], out_specs=ANY,
        out_shape=jax.ShapeDtypeStruct((N_CHIP, rows, lanes), buf.dtype),
        scratch_shapes=[pltpu.SemaphoreType.DMA((2 * n_ici,)), pltpu.SemaphoreType.DMA((2 * n_ici,))],
    )(buf)
    return lax.dynamic_update_slice(others, buf[None], (2 * lax.axis_index("x") + lax.axis_index("y"), 0, 0))


def _exchange_many(arrs, *, per_chip, name):
    na = len(arrs)
    flips = (2, 4, 6) if per_chip else tuple(range(1, N_DEV))
    index_of = (lambda x, y, c: 2 * x + y) if per_chip else (lambda x, y, c: 4 * x + 2 * y + c)

    def body(*refs):
        ins, outs = refs[:na], refs[na:2 * na]
        send_sems, recv_sems = refs[2 * na:]
        x, y, c = lax.axis_index("x"), lax.axis_index("y"), lax.axis_index("c")
        me = index_of(x, y, c)
        copies = []
        for nk, k in enumerate(flips):
            px, py, pc = x ^ (k >> 2), y ^ ((k >> 1) & 1), c ^ (k & 1)
            peer = index_of(px, py, pc)
            for a in range(na):
                n = nk * na + a
                copies.append(pltpu.make_async_remote_copy(
                    src_ref=ins[a].at[peer], dst_ref=outs[a].at[me], send_sem=send_sems.at[n],
                    recv_sem=recv_sems.at[n], device_id=(px, py, pc), device_id_type=MESH))
        for cp in copies:
            cp.start()
        for cp in copies:
            cp.wait()

    n_remote = len(flips) * na
    outs = pl.pallas_call(
        body, name=name, in_specs=[ANY] * na, out_specs=[ANY] * na,
        out_shape=[jax.ShapeDtypeStruct(a.shape, a.dtype) for a in arrs],
        scratch_shapes=[pltpu.SemaphoreType.DMA((n_remote,)), pltpu.SemaphoreType.DMA((n_remote,))],
    )(*arrs)
    me = index_of(lax.axis_index("x"), lax.axis_index("y"), lax.axis_index("c"))
    return [lax.dynamic_update_slice(o, lax.dynamic_index_in_dim(a, me, 0, keepdims=True), (me, 0, 0))
            for o, a in zip(outs, arrs)]


def _swap_with_sibling(arrs):
    na = len(arrs)

    def body(*refs):
        ins, outs = refs[:na], refs[na:2 * na]
        send_sems, recv_sems = refs[2 * na:]
        x, y, c = lax.axis_index("x"), lax.axis_index("y"), lax.axis_index("c")
        copies = [pltpu.make_async_remote_copy(
            src_ref=ins[a].at[:, 1 - c], dst_ref=outs[a], send_sem=send_sems.at[a], recv_sem=recv_sems.at[a],
            device_id=(x, y, 1 - c), device_id_type=MESH) for a in range(na)]
        for cp in copies:
            cp.start()
        for cp in copies:
            cp.wait()

    return pl.pallas_call(
        body, name="swap_pieces", in_specs=[ANY] * na, out_specs=[ANY] * na,
        out_shape=[jax.ShapeDtypeStruct((a.shape[0],) + a.shape[2:], a.dtype) for a in arrs],
        scratch_shapes=[pltpu.SemaphoreType.DMA((na,)), pltpu.SemaphoreType.DMA((na,))],
    )(*arrs)


def _add_pair(a, b, *, name):
    n, r, c = a.shape
    tm = _tile(r, max(2 * SUBLANES, SUM_BLOCK_ELEMS // c), 2 * SUBLANES)

    def body(a_ref, b_ref, o_ref):
        o_ref[...] = (a_ref[...].astype(F32) + b_ref[...].astype(F32)).astype(BF16)

    spec = pl.BlockSpec((n, tm, c), lambda i: (0, i, 0))
    return pl.pallas_call(
        body, name=name, grid=(r // tm,), in_specs=[spec, spec], out_specs=spec,
        out_shape=jax.ShapeDtypeStruct(a.shape, BF16), compiler_params=_cparams(("parallel",)),
    )(a, b)


def _share_many(reds, halves):
    na = len(reds)

    def body(*refs):
        ins, outs = refs[:na], refs[na:2 * na]
        send_sems, recv_sems = refs[2 * na:]
        x, y, c = lax.axis_index("x"), lax.axis_index("y"), lax.axis_index("c")
        copies = []
        for a in range(na):
            h = halves[a]
            copies.append(pltpu.make_async_remote_copy(
                src_ref=ins[a].at[pl.ds(0, h), :], dst_ref=outs[a].at[pl.ds(c * h, h), :], send_sem=send_sems.at[a],
                recv_sem=recv_sems.at[a], device_id=(x, y, 1 - c), device_id_type=MESH))
        for cp in copies:
            cp.start()
        for cp in copies:
            cp.wait()

    outs = pl.pallas_call(
        body, name="share_halves", in_specs=[ANY] * na, out_specs=[ANY] * na,
        out_shape=[jax.ShapeDtypeStruct((r.shape[0] + h, r.shape[1]), r.dtype) for r, h in zip(reds, halves)],
        scratch_shapes=[pltpu.SemaphoreType.DMA((na,)), pltpu.SemaphoreType.DMA((na,))],
    )(*reds)
    c = lax.axis_index("c")
    full = []
    for o, r, h in zip(outs, reds, halves):
        o = lax.dynamic_update_slice(o, r[:h], (c * h, 0))
        if r.shape[0] > h:
            o = lax.dynamic_update_slice(o, r[h:], (2 * h, 0))
        full.append(o)
    return full


SUM_BLOCK_ELEMS = 256 * 1024
ADAM_BLOCK_ELEMS = 256 * 1024


def _sum_slots(slots, *, name="sum_grads"):
    n, prow, lanes = slots.shape
    tm = _tile(prow, max(2 * SUBLANES, SUM_BLOCK_ELEMS // lanes), 2 * SUBLANES)

    def body(s_ref, o_ref):
        acc = s_ref[0].astype(F32)
        for s in range(1, n):
            acc = acc + s_ref[s].astype(F32)
        o_ref[...] = acc

    return pl.pallas_call(
        body, name=name, grid=(prow // tm,),
        in_specs=[pl.BlockSpec((n, tm, lanes), lambda i: (0, i, 0))],
        out_specs=pl.BlockSpec((tm, lanes), lambda i: (i, 0)),
        out_shape=jax.ShapeDtypeStruct((prow, lanes), F32),
        compiler_params=_cparams(("parallel",)),
    )(slots)


def _adamw(wp, gp, mp, vp, *, name="adamw"):
    rows, lanes = wp.shape
    tm = _tile(rows, max(SUBLANES, ADAM_BLOCK_ELEMS // lanes), SUBLANES)
    c1 = 1.0 / (1.0 - ADAM_B1 ** ADAM_STEP)
    c2 = 1.0 / (1.0 - ADAM_B2 ** ADAM_STEP)

    def body(w_ref, g_ref, m_ref, v_ref, d_ref, nm_ref, nv_ref):
        g = g_ref[...]
        m = ADAM_B1 * m_ref[...] + (1.0 - ADAM_B1) * g
        v = ADAM_B2 * v_ref[...] + (1.0 - ADAM_B2) * (g * g)
        m_hat = m * c1
        v_hat = v * c2
        d_ref[...] = -ADAM_LR * (m_hat / (jnp.sqrt(v_hat) + ADAM_EPS) + ADAM_WD * w_ref[...])
        nm_ref[...] = m
        nv_ref[...] = v

    spec = pl.BlockSpec((tm, lanes), lambda i: (i, 0))
    shp = jax.ShapeDtypeStruct((rows, lanes), F32)
    return pl.pallas_call(
        body, name=name, grid=(rows // tm,), in_specs=[spec] * 4, out_specs=[spec] * 3,
        out_shape=[shp, shp, shp], compiler_params=_cparams(("parallel",)),
    )(wp, gp, mp, vp)


AG_ROW_MULT = 2 * N_CHUNK * 16
GRAD_ROW_MULT = 2048
REP_ROW_MULT = 512


def _shard_to_front(a, axis):
    n = a.shape[axis]
    a = a.reshape(a.shape[:axis] + (N_CHIP, n // N_CHIP) + a.shape[axis + 1:])
    return jnp.moveaxis(a, axis, 0)


def _shards_to_full(a, axis):
    a = jnp.moveaxis(a, 0, axis)
    return a.reshape(a.shape[:axis] + (a.shape[axis] * a.shape[axis + 1],) + a.shape[axis + 2:])


def kernel(x, p, a_w_in, a_b_f, a_w_out, b_w_in, b_conv_w, b_conv_b, b_w_a, b_b_a, b_w_i, b_b_i, b_lam, b_w_out, f_w_up, f_conv_w, f_conv_b, f_w_down, ln1_g, ln1_b, ln2_g, ln2_b, ple_w, ple_gate_w, ple_gate_b, loss_target, m_a_w_in, m_a_b_f, m_a_w_out, m_b_w_in, m_b_conv_w, m_b_conv_b, m_b_w_a, m_b_b_a, m_b_w_i, m_b_b_i, m_b_lam, m_b_w_out, m_f_w_up, m_f_conv_w, m_f_conv_b, m_f_w_down, m_ln1_g, m_ln1_b, m_ln2_g, m_ln2_b, m_ple_w, m_ple_gate_w, m_ple_gate_b, v_a_w_in, v_a_b_f, v_a_w_out, v_b_w_in, v_b_conv_w, v_b_conv_b, v_b_w_a, v_b_b_a, v_b_w_i, v_b_b_i, v_b_lam, v_b_w_out, v_f_w_up, v_f_conv_w, v_f_conv_b, v_f_w_down, v_ln1_g, v_ln1_b, v_ln2_g, v_ln2_b, v_ple_w, v_ple_gate_w, v_ple_gate_b):
    args = dict(locals())
    swap = lambda n, a: jnp.swapaxes(a, -1, -2) if n in TRANSPOSED else a
    axis_of = lambda n: (3 - SHARD_AXIS[n]) if n in TRANSPOSED else SHARD_AXIS[n]
    local_w = {n: swap(n, args[n]) for n in WEIGHTS}
    local_m = {n: swap(n, args['m_' + n]) for n in WEIGHTS}
    local_v = {n: swap(n, args['v_' + n]) for n in WEIGHTS}

    as_pairs = lambda a: lax.bitcast_convert_type(a, BF16)
    from_pairs = lambda a: lax.bitcast_convert_type(a, F32)
    send = [local_w[n] for n in GATHER_BF16] + [as_pairs(local_w[n]) for n in GATHER_F32]
    gathered = _all_gather_shards(_pack(send, BF16, AG_ROW_MULT))
    shapes = [local_w[n].shape for n in GATHER_BF16] + [local_w[n].shape + (2,) for n in GATHER_F32]
    parts = _unpack(gathered, shapes)
    full_w = {}
    for n, part in zip(GATHER_BF16 + GATHER_F32, parts):
        if n in GATHER_F32:
            part = from_pairs(part)
        full_w[n] = _shards_to_full(part, axis_of(n))
    for n in REPLICATED:
        full_w[n] = local_w[n]

    loss_row, grad_x, g = _local_step(x[0], p[:, 0], loss_target[0], full_w)

    grads = {n: jnp.stack(g[n]) for n in SHARDED}
    small = [n for n in SHARDED if n not in NATIVE]
    shard_rows = _round_up(sum(_round_up(local_w[n].size, LANES) for n in small) // LANES, GRAD_ROW_MULT)
    half_rows = shard_rows // 2

    def native_pieces(n):
        a = _shard_to_front(grads[n], axis_of(n))
        return a.reshape(N_DEV, -1, a.shape[-1]).astype(BF16)

    sharded_g = _pack([_shard_to_front(grads[n], axis_of(n)) for n in small], BF16, GRAD_ROW_MULT, lead=1)
    rep_g = _pack([jnp.stack(g[n]) for n in REPLICATED] + [loss_row], F32, REP_ROW_MULT)
    rep_rows = rep_g.shape[0]
    rep_top = lax.reduce_precision(rep_g, 8, 7)
    rep_hi = rep_top.astype(BF16)
    rep_lo = (rep_g - rep_top).astype(BF16)
    packed_pieces = jnp.concatenate(
        [sharded_g.reshape(N_DEV, half_rows, LANES),
         jnp.broadcast_to(jnp.concatenate([rep_hi, rep_lo]), (N_DEV, 2 * rep_rows, LANES))], axis=1)
    c = lax.axis_index("c")
    mine = [native_pieces(n) for n in NATIVE]
    theirs = _swap_with_sibling([p.reshape((N_CHIP, 2) + p.shape[1:]) for p in mine])
    chip_sums = [_add_pair(lax.dynamic_index_in_dim(p.reshape((N_CHIP, 2) + p.shape[1:]), c, 1, keepdims=False), t,
                           name="pair_" + n) for p, t, n in zip(mine, theirs, NATIVE)]
    slots = _exchange_many(chip_sums, per_chip=True, name="exchange_grads")
    slots += _exchange_many([packed_pieces], per_chip=False, name="exchange_packed")
    reduced = [_sum_slots(s, name="sum_" + n) for s, n in zip(slots, NATIVE + ["packed"])]
    reduced[-1] = jnp.concatenate([reduced[-1][:half_rows], reduced[-1][half_rows:half_rows + rep_rows]
                                   + reduced[-1][half_rows + rep_rows:]])
    full_g = _share_many(reduced, [r.shape[0] for r in reduced[:-1]] + [half_rows])

    rep_shapes = [local_w[n].shape for n in REPLICATED] + [loss_row.shape]

    def packed(d):
        return jnp.concatenate([_pack([d[n] for n in small], F32, GRAD_ROW_MULT),
                                _pack([d[n] for n in REPLICATED] + [jnp.zeros_like(loss_row)], F32, REP_ROW_MULT)])

    def unpacked(buf):
        d = dict(zip(small, _unpack(buf[:shard_rows], [local_w[n].shape for n in small])))
        d.update(zip(REPLICATED, _unpack(buf[shard_rows:], rep_shapes)))
        return d

    outs = [unpacked(b) for b in (full_g[-1],) + tuple(_adamw(packed(local_w), full_g[-1], packed(local_m),
                                                              packed(local_v), name="adamw_packed"))]
    for n, gn in zip(NATIVE, full_g):
        two_d = lambda a: a.reshape(gn.shape)
        res = _adamw(two_d(local_w[n]), gn, two_d(local_m[n]), two_d(local_v[n]), name="adamw_" + n)
        for d, a in zip(outs, (gn,) + tuple(res)):
            d[n] = a.reshape(local_w[n].shape)
    loss = _unpack(full_g[-1][shard_rows:], rep_shapes)[-1][0, 0]
    return (loss, grad_x[None], *[swap(n, d[n]) for d in outs for n in WEIGHTS])
```

```python
import math

import jax
import jax.numpy as jnp
from jax import lax
from jax.experimental import pallas as pl
from jax.experimental.pallas import tpu as pltpu

F32 = jnp.float32
BF16 = jnp.bfloat16

D_MODEL = 1024
DEPTH = 4
N_HEADS = 16
HEAD_DIM = 64
N_BLOCKS_B = 8
BLOCK_B = 128
CONV_B = 4
LRU_C = 8.0
D_FF = 2816
CONV_F = 3
D_PLE = 256
LN_EPS = 1e-5
ALPHA = (2.0 * DEPTH) ** 0.25
ATTN_SCALE = 1.0 / math.sqrt(HEAD_DIM)

ADAM_LR = 0.001
ADAM_B1 = 0.9
ADAM_B2 = 0.999
ADAM_EPS = 1e-08
ADAM_WD = 0.01
ADAM_STEP = 10

LANES = 128
SUBLANES = 8
VMEM_LIMIT = 52 * 1024 * 1024
MM_VMEM_BUDGET = 40 * 1024 * 1024
NEG_BIG = -1e30
N_DEV = 8
N_CHIP = 4

WEIGHTS = ['a_w_in', 'a_b_f', 'a_w_out', 'b_w_in', 'b_conv_w', 'b_conv_b', 'b_w_a', 'b_b_a', 'b_w_i', 'b_b_i',
           'b_lam', 'b_w_out', 'f_w_up', 'f_conv_w', 'f_conv_b', 'f_w_down', 'ln1_g', 'ln1_b', 'ln2_g', 'ln2_b',
           'ple_w', 'ple_gate_w', 'ple_gate_b']
SHARD_AXIS = {'a_w_in': 2, 'a_w_out': 1, 'b_w_in': 2, 'b_conv_w': 2, 'b_conv_b': 1, 'b_lam': 1, 'b_w_out': 1,
              'f_w_up': 2, 'f_conv_w': 2, 'f_w_down': 1, 'ple_w': 2, 'ple_gate_w': 1}
TRANSPOSED = ('a_w_in',)
NATIVE = ['a_w_out', 'b_w_in', 'b_w_out', 'f_w_up', 'f_w_down', 'ple_w', 'ple_gate_w']
SHARDED = [n for n in WEIGHTS if n in SHARD_AXIS]
REPLICATED = [n for n in WEIGHTS if n not in SHARD_AXIS]
GATHER_BF16 = ['a_w_in', 'a_w_out', 'b_w_in', 'b_w_out', 'f_w_up', 'f_w_down', 'ple_w', 'ple_gate_w']
GATHER_F32 = ['b_conv_w', 'b_conv_b', 'b_lam', 'f_conv_w']


def _cparams(sem, vmem=VMEM_LIMIT):
    return pltpu.CompilerParams(dimension_semantics=sem, vmem_limit_bytes=vmem)


def _tile(n, cap, q=LANES):
    best = None
    for t in range(q, min(n, cap) + 1, q):
        if n % t == 0:
            best = t
    return best if best is not None else n


def _sigmoid(x):
    return 1.0 / (1.0 + jnp.exp(-x))


_GELU_C = math.sqrt(2.0 / math.pi)


def _gelu_and_grad(x):
    x2 = x * x
    t = jnp.tanh(_GELU_C * (x + 0.044715 * x * x2))
    cdf = 0.5 * (1.0 + t)
    g = x * cdf
    dg = cdf + x * 0.5 * (1.0 - t * t) * _GELU_C * (1.0 + 3.0 * 0.044715 * x2)
    return g, dg


def _gelu(x):
    t = jnp.tanh(_GELU_C * (x + 0.044715 * x * x * x))
    return x * (0.5 * (1.0 + t))


def _log1p(u):
    w = 1.0 + u
    d = w - 1.0
    return jnp.where(d == 0.0, u, jnp.log(w) * (u / jnp.where(d == 0.0, 1.0, d)))


def _softplus(y):
    return jnp.maximum(y, 0.0) + _log1p(jnp.exp(-jnp.abs(y)))


def _log_sigmoid(z):
    return -_softplus(-z)


def _neg_expm1(x):
    poly = x * (1.0 + x * (1.0 / 2 + x * (1.0 / 6 + x * (1.0 / 24 + x * (1.0 / 120 + x * (1.0 / 720 + x * (1.0 / 5040)))))))
    return -jnp.where(x > -0.25, poly, jnp.exp(x) - 1.0)


def _split3(x):
    hi = x.astype(BF16)
    r1 = x - hi.astype(F32)
    mid = r1.astype(BF16)
    lo = (r1 - mid.astype(F32)).astype(BF16)
    return hi, mid, lo


def _shift_down(x, halo, k):
    rolled = pltpu.roll(x, k, axis=0)
    hal = pltpu.roll(halo, k, axis=0)
    r8 = lax.broadcasted_iota(jnp.int32, halo.shape, 0)
    head = jnp.where(r8 < k, hal, rolled[:SUBLANES])
    if x.shape[0] == SUBLANES:
        return head
    return jnp.concatenate([head, rolled[SUBLANES:]], axis=0)


def _shift_up(x, nxt, k):
    n = x.shape[0]
    rolled = pltpu.roll(x, n - k, axis=0)
    nx = pltpu.roll(nxt, SUBLANES - k, axis=0)
    r8 = lax.broadcasted_iota(jnp.int32, nxt.shape, 0)
    tail = jnp.where(r8 >= SUBLANES - k, nx, rolled[n - SUBLANES:])
    if n == SUBLANES:
        return tail
    return jnp.concatenate([rolled[:n - SUBLANES], tail], axis=0)


def _split_spec(arr_ndim, part_cols, br, bc, idx):
    if arr_ndim == 3:
        nbh = part_cols // bc
        return pl.BlockSpec((None, br, bc), lambda i, j, k: (lax.div(idx(i, j, k)[1], nbh), idx(i, j, k)[0],
                                                             lax.rem(idx(i, j, k)[1], nbh)))
    return pl.BlockSpec((br, bc), lambda i, j, k: idx(i, j, k))


def _dims(arr):
    if arr.ndim == 3:
        return arr.shape[1], arr.shape[0] * arr.shape[2], arr.shape[2]
    return arr.shape[0], arr.shape[1], arr.shape[1]


def _mm(a, b, *, ta=False, tb=False, out_dtype=F32, out_split=1, add=None, add_scale=1.0,
        tm_cap=1024, tn_cap=1024, tk_cap=1408, name):
    ar, ac, apart = _dims(a)
    br_, bc_, bpart = _dims(b)
    m, kdim = (ac, ar) if ta else (ar, ac)
    kdim_b, n = (bc_, br_) if tb else (br_, bc_)
    assert kdim == kdim_b, (name, a.shape, b.shape)
    tm = _tile(apart, tm_cap) if ta else _tile(m, tm_cap, SUBLANES)
    tn = _tile(n, tn_cap, SUBLANES) if tb else _tile(math.gcd(bpart, n // out_split), tn_cap)
    if ta:
        tk = _tile(kdim, 1024, 2 * SUBLANES)
    elif tb:
        tk = _tile(math.gcd(apart, bpart), tk_cap)
    else:
        tk = _tile(apart, tk_cap)

    def vmem_bytes(tm_, tk_):
        per_step = 2 * (tm_ * tk_ + tk_ * tn) + tm_ * tn * (jnp.dtype(out_dtype).itemsize + (4 if add is not None else 0))
        return 2 * per_step + (4 * tm_ * tn if kdim // tk_ > 1 else 0)

    if ta and kdim % (2 * tk) == 0 and vmem_bytes(tm, 2 * tk) <= MM_VMEM_BUDGET:
        tk *= 2
    elif not ta and m % (2 * tm) == 0 and vmem_bytes(2 * tm, tk) <= MM_VMEM_BUDGET:
        tm *= 2
    assert m % tm == 0 and n % tn == 0 and kdim % tk == 0, (name, m, n, kdim, tm, tn, tk)
    nk = kdim // tk
    a_spec = (_split_spec(a.ndim, apart, tk, tm, lambda i, j, k: (k, i)) if ta
              else _split_spec(a.ndim, apart, tm, tk, lambda i, j, k: (i, k)))
    b_spec = (_split_spec(b.ndim, bpart, tn, tk, lambda i, j, k: (j, k)) if tb
              else _split_spec(b.ndim, bpart, tk, tn, lambda i, j, k: (k, j)))
    if out_split > 1:
        out_shape = jax.ShapeDtypeStruct((out_split, m, n // out_split), out_dtype)
        o_spec = _split_spec(3, n // out_split, tm, tn, lambda i, j, k: (i, j))
    else:
        out_shape = jax.ShapeDtypeStruct((m, n), out_dtype)
        o_spec = pl.BlockSpec((tm, tn), lambda i, j, k: (i, j))
    dn = (((0 if ta else 1,), (1 if tb else 0,)), ((), ()))
    in_specs = [a_spec, b_spec]
    args = [a, b]
    if add is not None:
        in_specs.append(pl.BlockSpec((tm, tn), lambda i, j, k: (i, j)))
        args.append(add)
    use_acc = nk > 1
    has_add = add is not None

    def body(*refs):
        a_ref, b_ref = refs[0], refs[1]
        add_ref = refs[2] if has_add else None
        o_ref = refs[3] if has_add else refs[2]
        part = lax.dot_general(a_ref[...], b_ref[...], dn, preferred_element_type=F32)

        def finish(acc):
            if has_add:
                acc = acc + add_scale * add_ref[...]
            o_ref[...] = acc.astype(out_dtype)

        if not use_acc:
            finish(part)
        else:
            acc_ref = refs[-1]
            k = pl.program_id(2)

            @pl.when(k == 0)
            def _():
                acc_ref[...] = part

            @pl.when(k > 0)
            def _():
                acc_ref[...] += part

            @pl.when(k == nk - 1)
            def _():
                finish(acc_ref[...])

    return pl.pallas_call(
        body, name=name, grid=(m // tm, n // tn, nk), in_specs=in_specs, out_specs=o_spec, out_shape=out_shape,
        scratch_shapes=[pltpu.VMEM((tm, tn), F32)] if use_acc else [],
        compiler_params=_cparams(("parallel", "parallel", "arbitrary")),
    )(*args)


def _ln_fwd(x, m, g, b, *, name, tm=1024):
    t, d = x.shape

    def body(x_ref, m_ref, g_ref, b_ref, y_ref, yb_ref, xhat_ref, rstd_ref):
        z = ALPHA * x_ref[...] + m_ref[...]
        mu = jnp.mean(z, axis=-1, keepdims=True)
        zc = z - mu
        var = jnp.mean(zc * zc, axis=-1, keepdims=True)
        rstd = lax.rsqrt(var + LN_EPS)
        xhat = zc * rstd
        y = xhat * g_ref[...] + b_ref[...]
        y_ref[...] = y
        yb_ref[...] = y.astype(BF16)
        xhat_ref[...] = xhat
        rstd_ref[...] = rstd

    row = pl.BlockSpec((tm, d), lambda i: (i, 0))
    vec = pl.BlockSpec((1, d), lambda i: (0, 0))
    return pl.pallas_call(
        body, name=name, grid=(t // tm,), in_specs=[row, row, vec, vec],
        out_specs=[row, row, row, pl.BlockSpec((tm, 1), lambda i: (i, 0))],
        out_shape=[jax.ShapeDtypeStruct((t, d), F32), jax.ShapeDtypeStruct((t, d), BF16),
                   jax.ShapeDtypeStruct((t, d), F32), jax.ShapeDtypeStruct((t, 1), F32)],
        compiler_params=_cparams(("parallel",)),
    )(x, m, g, b)


def _ln_bwd(dy, xhat, rstd, g, *, name, tm=1024):
    t, d = dy.shape

    def body(dy_ref, xhat_ref, rstd_ref, g_ref, dz_ref, dzb_ref, dg_ref, db_ref):
        @pl.when(pl.program_id(0) == 0)
        def _():
            dg_ref[...] = jnp.zeros_like(dg_ref)
            db_ref[...] = jnp.zeros_like(db_ref)

        dyv = dy_ref[...]
        xh = xhat_ref[...]
        dg_ref[...] += jnp.sum(dyv * xh, axis=0, keepdims=True)
        db_ref[...] += jnp.sum(dyv, axis=0, keepdims=True)
        dxh = dyv * g_ref[...]
        m1 = jnp.mean(dxh, axis=-1, keepdims=True)
        m2 = jnp.mean(dxh * xh, axis=-1, keepdims=True)
        dz = rstd_ref[...] * (dxh - m1 - xh * m2)
        dz_ref[...] = dz
        dzb_ref[...] = dz.astype(BF16)

    row = pl.BlockSpec((tm, d), lambda i: (i, 0))
    vec = pl.BlockSpec((1, d), lambda i: (0, 0))
    return pl.pallas_call(
        body, name=name, grid=(t // tm,), in_specs=[row, row, pl.BlockSpec((tm, 1), lambda i: (i, 0)), vec],
        out_specs=[row, row, vec, vec],
        out_shape=[jax.ShapeDtypeStruct((t, d), F32), jax.ShapeDtypeStruct((t, d), BF16),
                   jax.ShapeDtypeStruct((1, d), F32), jax.ShapeDtypeStruct((1, d), F32)],
        compiler_params=_cparams(("arbitrary",)),
    )(dy, xhat, rstd, g)


def _ple_fwd(x2, gl, pe, gate_b, *, name, tm=1024):
    t, d = x2.shape

    def body(x_ref, gl_ref, pe_ref, b_ref, y_ref, yb_ref):
        y = x_ref[...] + _sigmoid(gl_ref[...] + b_ref[...]) * pe_ref[...]
        y_ref[...] = y
        yb_ref[...] = y.astype(BF16)

    row = pl.BlockSpec((tm, d), lambda i: (i, 0))
    vec = pl.BlockSpec((1, d), lambda i: (0, 0))
    return pl.pallas_call(
        body, name=name, grid=(t // tm,), in_specs=[row, row, row, vec], out_specs=[row, row],
        out_shape=[jax.ShapeDtypeStruct((t, d), F32), jax.ShapeDtypeStruct((t, d), BF16)],
        compiler_params=_cparams(("parallel",)),
    )(x2, gl, pe, gate_b)


def _ple_bwd(dx3, gl, pe, gate_b, *, name, tm=1024):
    t, d = dx3.shape

    def body(dx_ref, gl_ref, pe_ref, b_ref, dgl_ref, dpe_ref, db_ref):
        @pl.when(pl.program_id(0) == 0)
        def _():
            db_ref[...] = jnp.zeros_like(db_ref)

        dx = dx_ref[...]
        gt = _sigmoid(gl_ref[...] + b_ref[...])
        dgl = dx * pe_ref[...] * gt * (1.0 - gt)
        db_ref[...] += jnp.sum(dgl, axis=0, keepdims=True)
        dgl_ref[...] = dgl.astype(BF16)
        dpe_ref[...] = (dx * gt).astype(BF16)

    row = pl.BlockSpec((tm, d), lambda i: (i, 0))
    vec = pl.BlockSpec((1, d), lambda i: (0, 0))
    return pl.pallas_call(
        body, name=name, grid=(t // tm,), in_specs=[row, row, row, vec], out_specs=[row, row, vec],
        out_shape=[jax.ShapeDtypeStruct((t, d), BF16), jax.ShapeDtypeStruct((t, d), BF16),
                   jax.ShapeDtypeStruct((1, d), F32)],
        compiler_params=_cparams(("arbitrary",)),
    )(dx3, gl, pe, gate_b)


def _loss_bwd(y, tgt, *, name, tm=512):
    t, d = y.shape

    def body(y_ref, t_ref, dy_ref, l_ref):
        @pl.when(pl.program_id(0) == 0)
        def _():
            l_ref[...] = jnp.zeros_like(l_ref)

        err = y_ref[...] - t_ref[...]
        dy_ref[...] = err * (1.0 / d)
        part = jnp.sum(jnp.sum(err * err, axis=0, keepdims=True), axis=1, keepdims=True) * (0.5 / d)
        l_ref[...] += jnp.broadcast_to(part, l_ref.shape)

    row = pl.BlockSpec((tm, d), lambda i: (i, 0))
    return pl.pallas_call(
        body, name=name, grid=(t // tm,), in_specs=[row, row],
        out_specs=[row, pl.BlockSpec((1, LANES), lambda i: (0, 0))],
        out_shape=[jax.ShapeDtypeStruct((t, d), F32), jax.ShapeDtypeStruct((1, LANES), F32)],
        compiler_params=_cparams(("arbitrary",)),
    )(y, tgt)


def _conv_causal(x, halo, w_ref, b, kw):
    acc = x * w_ref[kw - 1:kw, :] + b
    for k in range(kw - 1):
        acc = acc + _shift_down(x, halo, kw - 1 - k) * w_ref[k:k + 1, :]
    return acc


def _ffn_act_fwd(hdn, conv_w, conv_b, *, name, tm=512):
    _, t, f = hdn.shape
    tc = _tile(f, 1408)
    hb = tm // SUBLANES

    def body(h_ref, halo_ref, w_ref, b_ref, a_ref):
        first = pl.program_id(1) == 0
        parts = []
        for s in range(2):
            halo = jnp.where(first, 0.0, halo_ref[s])
            parts.append(_conv_causal(h_ref[s], halo, w_ref.at[s], b_ref[s], CONV_F))
        a_ref[...] = (_gelu(parts[1]) * parts[0]).astype(BF16)

    return pl.pallas_call(
        body, name=name, grid=(f // tc, t // tm),
        in_specs=[pl.BlockSpec((2, tm, tc), lambda j, i: (0, i, j)),
                  pl.BlockSpec((2, SUBLANES, tc), lambda j, i: (0, jnp.maximum(i * hb - 1, 0), j)),
                  pl.BlockSpec((2, CONV_F, tc), lambda j, i: (0, 0, j)),
                  pl.BlockSpec((2, 1, tc), lambda j, i: (0, 0, j))],
        out_specs=pl.BlockSpec((tm, tc), lambda j, i: (i, j)),
        out_shape=jax.ShapeDtypeStruct((t, f), BF16),
        compiler_params=_cparams(("parallel", "arbitrary")),
    )(hdn, hdn, conv_w, conv_b)


def _ffn_act_bwd(da, hdn, conv_w, conv_b, *, name, tm=512):
    _, t, f = hdn.shape
    tc = _tile(f, 1408)
    hb = tm // SUBLANES
    nr = t // tm

    strip = 2 * SUBLANES
    ns = tm // strip
    pieces = [(c0, min(2 * LANES, tc - c0)) for c0 in range(0, tc, 2 * LANES)]

    def body(da_ref, h_ref, halo_ref, w_ref, b_ref, dh_ref, dw_ref, db_ref, carry_ref, dw_acc, db_acc):
        i = pl.program_id(1)
        r = nr - 1 - i

        @pl.when(i == 0)
        def _():
            carry_ref[...] = jnp.zeros_like(carry_ref)
            dw_acc[...] = jnp.zeros_like(dw_acc)
            db_acc[...] = jnp.zeros_like(db_acc)

        def do_strip(si, carry):
            s = ns - 1 - si
            r0 = pl.multiple_of(s * strip, strip)
            above = pl.multiple_of(jnp.maximum(s * strip - SUBLANES, 0), SUBLANES)
            for c0, cw in pieces:
                cols = slice(c0, c0 + cw)
                xs, hcs = [], []
                for part in range(2):
                    x = h_ref[part, pl.ds(r0, strip), cols]
                    halo = jnp.where(r == 0, 0.0, halo_ref[part, :, cols])
                    prev = jnp.where(s == 0, halo, h_ref[part, pl.ds(above, SUBLANES), cols])
                    sh = [_shift_down(x, prev, k) for k in range(1, CONV_F)]
                    hc = x * w_ref[part, CONV_F - 1:CONV_F, cols] + b_ref[part, :, cols]
                    for k in range(1, CONV_F):
                        hc = hc + sh[k - 1] * w_ref[part, CONV_F - 1 - k:CONV_F - k, cols]
                    xs.append([x] + sh)
                    hcs.append(hc)
                g, dg = _gelu_and_grad(hcs[1])
                dav = da_ref[pl.ds(r0, strip), cols]
                dhc = [dav * g, dav * hcs[0] * dg]
                for part in range(2):
                    d = dhc[part]
                    nxt = carry_ref[part, :, cols]
                    db_acc[part, :, cols] += d
                    out = d * w_ref[part, CONV_F - 1:CONV_F, cols]
                    for k in range(CONV_F):
                        dw_acc[part, CONV_F - 1 - k, :, cols] += d * xs[part][k]
                        if k:
                            out = out + _shift_up(d, nxt, k) * w_ref[part, CONV_F - 1 - k:CONV_F - k, cols]
                    dh_ref[part, pl.ds(r0, strip), cols] = out.astype(BF16)
                    carry_ref[part, :, cols] = d[:SUBLANES]
            return carry

        lax.fori_loop(0, ns, do_strip, 0)

        @pl.when(i == nr - 1)
        def _():
            dw_ref[...] = jnp.sum(dw_acc[...], axis=2)
            db_ref[...] = jnp.sum(db_acc[...], axis=1, keepdims=True)

    return pl.pallas_call(
        body, name=name, grid=(f // tc, nr),
        in_specs=[pl.BlockSpec((tm, tc), lambda j, i: (nr - 1 - i, j)),
                  pl.BlockSpec((2, tm, tc), lambda j, i: (0, nr - 1 - i, j)),
                  pl.BlockSpec((2, SUBLANES, tc), lambda j, i: (0, jnp.maximum((nr - 1 - i) * hb - 1, 0), j)),
                  pl.BlockSpec((2, CONV_F, tc), lambda j, i: (0, 0, j)),
                  pl.BlockSpec((2, 1, tc), lambda j, i: (0, 0, j))],
        out_specs=[pl.BlockSpec((2, tm, tc), lambda j, i: (0, nr - 1 - i, j)),
                   pl.BlockSpec((2, CONV_F, tc), lambda j, i: (0, 0, j)),
                   pl.BlockSpec((2, 1, tc), lambda j, i: (0, 0, j))],
        out_shape=[jax.ShapeDtypeStruct((2, t, f), BF16), jax.ShapeDtypeStruct((2, CONV_F, f), F32),
                   jax.ShapeDtypeStruct((2, 1, f), F32)],
        scratch_shapes=[pltpu.VMEM((2, SUBLANES, tc), F32), pltpu.VMEM((2, CONV_F, 2 * SUBLANES, tc), F32),
                        pltpu.VMEM((2, 2 * SUBLANES, tc), F32)],
        compiler_params=_cparams(("arbitrary", "arbitrary")),
    )(da, hdn, hdn, conv_w, conv_b)


def _fgate_fwd(xb, w_f, b_f, *, name, tm=256):
    t, d = xb.shape

    def body(x_ref, w_ref, b_ref, z_ref, c_ref, carry_ref):
        @pl.when(pl.program_id(0) == 0)
        def _():
            carry_ref[...] = jnp.zeros_like(carry_ref)

        z = jnp.dot(x_ref[...], w_ref[...], preferred_element_type=F32) + b_ref[...]
        z_ref[...] = z
        ls = _log_sigmoid(z)
        rr = lax.broadcasted_iota(jnp.int32, (tm, tm), 0)
        cc = lax.broadcasted_iota(jnp.int32, (tm, tm), 1)
        tri = (cc <= rr).astype(BF16)
        cum = carry_ref[...]
        for piece in _split3(ls):
            cum = cum + jnp.dot(tri, piece, preferred_element_type=F32)
        c_ref[...] = cum
        carry_ref[...] = cum[tm - 1:tm, :]

    return pl.pallas_call(
        body, name=name, grid=(t // tm,),
        in_specs=[pl.BlockSpec((tm, d), lambda i: (i, 0)), pl.BlockSpec((d, LANES), lambda i: (0, 0)),
                  pl.BlockSpec((1, LANES), lambda i: (0, 0))],
        out_specs=[pl.BlockSpec((tm, LANES), lambda i: (i, 0)), pl.BlockSpec((tm, LANES), lambda i: (i, 0))],
        out_shape=[jax.ShapeDtypeStruct((t, LANES), F32), jax.ShapeDtypeStruct((t, LANES), F32)],
        scratch_shapes=[pltpu.VMEM((1, LANES), F32)],
        compiler_params=_cparams(("arbitrary",)),
    )(xb, w_f, b_f)


def _fgate_bwd(dc, z, *, name, tm=256):
    t = dc.shape[0]
    nr = t // tm

    def body(dc_ref, z_ref, dz_ref, db_ref, carry_ref):
        @pl.when(pl.program_id(0) == 0)
        def _():
            carry_ref[...] = jnp.zeros_like(carry_ref)
            db_ref[...] = jnp.zeros_like(db_ref)

        rr = lax.broadcasted_iota(jnp.int32, (tm, tm), 0)
        cc = lax.broadcasted_iota(jnp.int32, (tm, tm), 1)
        tri = (cc >= rr).astype(BF16)
        cum = carry_ref[...]
        for piece in _split3(dc_ref[...]):
            cum = cum + jnp.dot(tri, piece, preferred_element_type=F32)
        carry_ref[...] = cum[0:1, :]
        dz = cum * _sigmoid(-z_ref[...])
        db_ref[...] += jnp.sum(dz, axis=0, keepdims=True)
        dz_ref[...] = dz.astype(BF16)

    rev = pl.BlockSpec((tm, LANES), lambda i: (nr - 1 - i, 0))
    return pl.pallas_call(
        body, name=name, grid=(nr,), in_specs=[rev, rev],
        out_specs=[rev, pl.BlockSpec((1, LANES), lambda i: (0, 0))],
        out_shape=[jax.ShapeDtypeStruct((t, LANES), BF16), jax.ShapeDtypeStruct((1, LANES), F32)],
        scratch_shapes=[pltpu.VMEM((1, LANES), F32)],
        compiler_params=_cparams(("arbitrary",)),
    )(dc, z)


HM = 2 * HEAD_DIM
LANE_C, LANE_ONE, LANE_LSE = 64, 67, 70
Q_SUB = 256


def _three_parts(col, sign=1.0):
    col = sign * col
    hi = col.astype(BF16).astype(F32)
    r1 = col - hi
    mid = r1.astype(BF16).astype(F32)
    lo = (r1 - mid).astype(BF16).astype(F32)
    return hi, mid, lo


def _fill_lanes(base, lane, first, parts):
    out = base
    for n, part in enumerate(parts):
        out = jnp.where(lane == first + n, part, out)
    return out


def _attn_prep(qkv, c, *, name, tm=256):
    t, w_in = qkv.shape
    nh = w_in // (3 * HEAD_DIM)
    w = 3 * nh * HM

    def body(x_ref, c_ref, o_ref):
        lane = lax.broadcasted_iota(jnp.int32, (1, HM), 1)
        cblk = c_ref[...]
        for h in range(nh):
            ch = cblk[:, h:h + 1]
            pos = _three_parts(ch)
            neg = _three_parts(ch, -1.0)
            for part in range(3):
                col = (part * nh + h) * HM
                pair = (part * nh + h - h % 2) * HEAD_DIM
                x = x_ref[:, pair:pair + HM].astype(F32)
                if h % 2:
                    x = pltpu.roll(x, HEAD_DIM, axis=1)
                x = jnp.where(lane < HEAD_DIM, x, 0.0)
                if part == 0:
                    y = _fill_lanes(_fill_lanes(x * ATTN_SCALE, lane, LANE_C, pos), lane, LANE_ONE, (1.0, 1.0, 1.0))
                elif part == 1:
                    y = _fill_lanes(_fill_lanes(x, lane, LANE_C, (1.0, 1.0, 1.0)), lane, LANE_ONE, neg)
                    y = _fill_lanes(y, lane, LANE_LSE, (1.0, 1.0, 1.0))
                else:
                    y = _fill_lanes(x, lane, LANE_C, (1.0, 1.0, 1.0))
                o_ref[:, col:col + HM] = y.astype(BF16)

    return pl.pallas_call(
        body, name=name, grid=(t // tm,),
        in_specs=[pl.BlockSpec((tm, w_in), lambda i: (i, 0)), pl.BlockSpec((tm, LANES), lambda i: (i, 0))],
        out_specs=pl.BlockSpec((tm, w), lambda i: (i, 0)),
        out_shape=jax.ShapeDtypeStruct((t, w), BF16),
        compiler_params=_cparams(("parallel",)),
    )(qkv, c)


def _attn_compact(parts, *, name, tm=512):
    t, w = parts[0].shape
    nh = w // HM
    n_parts = len(parts)

    def body(*refs):
        o_ref = refs[n_parts]
        lane = lax.broadcasted_iota(jnp.int32, (1, HM), 1)
        for part, ref in enumerate(refs[:n_parts]):
            for h in range(0, nh, 2):
                even = ref[:, h * HM:(h + 1) * HM]
                odd = pltpu.roll(ref[:, (h + 1) * HM:(h + 2) * HM].astype(F32), HEAD_DIM, axis=1).astype(BF16)
                col = (part * nh + h) * HEAD_DIM
                o_ref[:, col:col + HM] = jnp.where(lane < HEAD_DIM, even, odd)

    row = pl.BlockSpec((tm, w), lambda i: (i, 0))
    return pl.pallas_call(
        body, name=name, grid=(t // tm,), in_specs=[row] * n_parts,
        out_specs=pl.BlockSpec((tm, n_parts * nh * HEAD_DIM), lambda i: (i, 0)),
        out_shape=jax.ShapeDtypeStruct((t, n_parts * nh * HEAD_DIM), BF16),
        compiler_params=_cparams(("parallel",)),
    )(*parts)


def _flash_fwd_hm(qkva, *, name, tq=1024):
    t, w = qkva.shape
    nh = w // (3 * HM)
    nq = t // tq
    nsub = tq // Q_SUB

    def body(q_ref, k_ref, v_ref, o_ref, qb_ref, s_scr, p_scr, m_scr, acc_scr):
        i = pl.program_id(1)
        lane = lax.broadcasted_iota(jnp.int32, (1, HM), 1)
        rr = lax.broadcasted_iota(jnp.int32, (Q_SUB, tq), 0)
        cc = lax.broadcasted_iota(jnp.int32, (Q_SUB, tq), 1)
        sub = lambda r: slice(r * Q_SUB, (r + 1) * Q_SUB)
        nt = (((1,), (1,)), ((), ()))

        def scores(j):
            kj = k_ref[pl.ds(pl.multiple_of(j * tq, tq), tq), :]
            for r in range(nsub):
                s_scr[sub(r), :] = lax.dot_general(q_ref[sub(r), :], kj, nt, preferred_element_type=F32)

        def step(jp, masked, with_pv=True):
            vj = v_ref[pl.ds(pl.multiple_of(jp * tq, tq), tq), :]
            for r in range(nsub):
                if masked:
                    s_scr[sub(r), :] = jnp.where(cc <= rr + r * Q_SUB, s_scr[sub(r), :], NEG_BIG)
                m_old = m_scr[sub(r), :]
                m_new = jnp.maximum(m_old, jnp.max(s_scr[sub(r), :], axis=1, keepdims=True))
                m_scr[sub(r), :] = m_new
                if with_pv:
                    pv = jnp.dot(p_scr[sub(r), :], vj, preferred_element_type=F32)
                    acc_scr[sub(r), :] = (acc_scr[sub(r), :] + pv) * jnp.exp(m_old - m_new)
                p_scr[sub(r), :] = jnp.exp(s_scr[sub(r), :] - m_new).astype(BF16)

        def kv_step(j, carry):
            step(j - 1, False)
            scores(j + 1)
            return carry

        p_scr[...] = jnp.zeros_like(p_scr)
        m_scr[...] = jnp.full_like(m_scr, NEG_BIG)
        acc_scr[...] = jnp.zeros_like(acc_scr)
        scores(0)

        @pl.when(i > 0)
        def _():
            step(0, False, with_pv=False)
            scores(1)

        lax.fori_loop(1, i, kv_step, 0)
        step(jnp.maximum(i - 1, 0), True)
        vi = v_ref[pl.ds(pl.multiple_of(i * tq, tq), tq), :]
        for r in range(nsub):
            acc = acc_scr[sub(r), :] + jnp.dot(p_scr[sub(r), :], vi, preferred_element_type=F32)
            l = jnp.sum(jnp.where(lane == LANE_C, acc, 0.0), axis=1, keepdims=True)
            o_ref[sub(r), :] = jnp.where(lane < HEAD_DIM, acc / l, 0.0).astype(BF16)
            lse = m_scr[sub(r), :] + jnp.log(l)
            qb = _fill_lanes(q_ref[sub(r), :].astype(F32), lane, LANE_LSE, _three_parts(lse, -1.0))
            qb_ref[sub(r), :] = qb.astype(BF16)

    blk = lambda part: pl.BlockSpec((t, HM), lambda h, i: (0, part * nh + h))
    tile = pl.BlockSpec((tq, HM), lambda h, i: (i, h))
    return pl.pallas_call(
        body, name=name, grid=(nh, nq), in_specs=[tile, blk(1), blk(2)], out_specs=[tile, tile],
        out_shape=[jax.ShapeDtypeStruct((t, nh * HM), BF16), jax.ShapeDtypeStruct((t, nh * HM), BF16)],
        scratch_shapes=[pltpu.VMEM((tq, tq), F32), pltpu.VMEM((tq, tq), BF16), pltpu.VMEM((tq, 1), F32),
                        pltpu.VMEM((tq, HM), F32)],
        compiler_params=_cparams(("parallel", "arbitrary")),
    )(qkva, qkva, qkva)


def _attn_prep_bwd(o, do, *, name, tm=512):
    t, w = o.shape
    nh = w // HM

    def body(o_ref, do_ref, out_ref):
        lane = lax.broadcasted_iota(jnp.int32, (1, HM), 1)
        for h in range(nh):
            cols = slice(h * HM, (h + 1) * HM)
            pair = (h - h % 2) * HEAD_DIM
            dov = do_ref[:, pair:pair + HM].astype(F32)
            if h % 2:
                dov = pltpu.roll(dov, HEAD_DIM, axis=1)
            dov = jnp.where(lane < HEAD_DIM, dov, 0.0)
            delta = jnp.sum(o_ref[:, cols].astype(F32) * dov, axis=1, keepdims=True)
            out_ref[:, cols] = _fill_lanes(dov, lane, LANE_C, _three_parts(delta, -1.0)).astype(BF16)

    row = pl.BlockSpec((tm, w), lambda i: (i, 0))
    return pl.pallas_call(
        body, name=name, grid=(t // tm,), in_specs=[row, pl.BlockSpec((tm, nh * HEAD_DIM), lambda i: (i, 0))],
        out_specs=row, out_shape=jax.ShapeDtypeStruct((t, w), BF16), compiler_params=_cparams(("parallel",)),
    )(o, do)


def _flash_bwd_hm(qb, doa, qkva, *, name, tq=512, tqc=512):
    t, w = qb.shape
    nh = w // HM
    nq = t // tq
    nqc = t // tqc
    nsub = tqc // Q_SUB
    grp = slice(LANE_C, LANE_C + SUBLANES)

    def body(q_ref, do_ref, k_ref, v_ref, dq_ref, dk_ref, dv_ref, dcq_ref, dcj_ref,
             dq_acc, st_scr, dpt_scr, pt_scr, ds_scr, dk_scr, dv_scr):
        j = pl.program_id(1)
        kv_minus_q = (lax.broadcasted_iota(jnp.int32, (tq, Q_SUB), 0)
                      - lax.broadcasted_iota(jnp.int32, (tq, Q_SUB), 1))
        sub = lambda r: slice(r * Q_SUB, (r + 1) * Q_SUB)
        nt = (((1,), (1,)), ((), ()))
        tn = (((0,), (0,)), ((), ()))
        first = lax.div(j * tq, tqc)

        @pl.when(j == 0)
        def _():
            dq_acc[...] = jnp.zeros_like(dq_acc)

        def rows_of(i, r):
            return pl.ds(pl.multiple_of(i * tqc + r * Q_SUB, Q_SUB), Q_SUB)

        def scores(i):
            for r in range(nsub):
                st_scr[:, sub(r)] = lax.dot_general(k_ref[...], q_ref[rows_of(i, r), :], nt, preferred_element_type=F32)
                dpt_scr[:, sub(r)] = lax.dot_general(v_ref[...], do_ref[rows_of(i, r), :], nt,
                                                     preferred_element_type=F32)

        def products(ip):
            for r in range(nsub):
                dv_scr[...] += jnp.dot(pt_scr[:, sub(r)], do_ref[rows_of(ip, r), :], preferred_element_type=F32)
                dk_scr[...] += jnp.dot(ds_scr[:, sub(r)], q_ref[rows_of(ip, r), :], preferred_element_type=F32)
                dq_acc[rows_of(ip, r), :] += lax.dot_general(ds_scr[:, sub(r)], k_ref[...], tn,
                                                             preferred_element_type=F32)

        def probabilities(masked):
            for r in range(nsub):
                st = st_scr[:, sub(r)]
                if masked:
                    st = jnp.where(kv_minus_q <= first * tqc + r * Q_SUB - j * tq, st, NEG_BIG)
                pt = jnp.exp(st)
                pt_scr[:, sub(r)] = pt.astype(BF16)
                ds_scr[:, sub(r)] = (pt * dpt_scr[:, sub(r)]).astype(BF16)

        def q_step(i, carry):
            products(i - 1)
            probabilities(False)
            scores(jnp.minimum(i + 1, nqc - 1))
            return carry

        dk_scr[...] = jnp.zeros_like(dk_scr)
        dv_scr[...] = jnp.zeros_like(dv_scr)
        scores(first)
        probabilities(True)
        scores(jnp.minimum(first + 1, nqc - 1))
        lax.fori_loop(first + 1, nqc, q_step, 0)
        products(nqc - 1)
        dk = dk_scr[...]
        dk_ref[...] = dk.astype(BF16)
        dv_ref[...] = dv_scr[...].astype(BF16)
        dcj_ref[...] = dk.T[grp, :]

        @pl.when(j == nq - 1)
        def _():
            dq_ref[...] = (dq_acc[...] * ATTN_SCALE).astype(BF16)
            for cidx in range(nq):
                rows = slice(cidx * tq, (cidx + 1) * tq)
                dcq_ref[:, rows] = dq_acc[rows, :].T[grp, :]

    full = pl.BlockSpec((t, HM), lambda h, j: (0, h))
    ktile = lambda part: pl.BlockSpec((tq, HM), lambda h, j: (j, part * nh + h))
    tile = pl.BlockSpec((tq, HM), lambda h, j: (j, h))
    hm_shape = jax.ShapeDtypeStruct((t, w), BF16)
    row_shape = jax.ShapeDtypeStruct((nh, SUBLANES, t), F32)
    return pl.pallas_call(
        body, name=name, grid=(nh, nq),
        in_specs=[pl.BlockSpec((t, HM), lambda h, j: (0, h), pipeline_mode=pl.Buffered(1)),
                  pl.BlockSpec((t, HM), lambda h, j: (0, h), pipeline_mode=pl.Buffered(1)), ktile(1), ktile(2)],
        out_specs=[full, tile, tile, pl.BlockSpec((None, SUBLANES, t), lambda h, j: (h, 0, 0)),
                   pl.BlockSpec((None, SUBLANES, tq), lambda h, j: (h, 0, j))],
        out_shape=[hm_shape, hm_shape, hm_shape, row_shape, row_shape],
        scratch_shapes=[pltpu.VMEM((t, HM), F32), pltpu.VMEM((tq, tqc), F32), pltpu.VMEM((tq, tqc), F32),
                        pltpu.VMEM((tq, tqc), BF16), pltpu.VMEM((tq, tqc), BF16), pltpu.VMEM((tq, HM), F32),
                        pltpu.VMEM((tq, HM), F32)],
        compiler_params=_cparams(("parallel", "arbitrary")),
    )(qb, doa, qkva, qkva)


def _block_diag_dot(xb, w_ref, transpose_w=False):
    outs = []
    for n in range(N_BLOCKS_B):
        xn = xb[:, n * BLOCK_B:(n + 1) * BLOCK_B]
        dn = (((1,), (1 if transpose_w else 0,)), ((), ()))
        outs.append(lax.dot_general(xn, w_ref[n], dn, preferred_element_type=F32))
    return jnp.concatenate(outs, axis=1)


def _rglru_fwd(proj, conv_w, conv_b, w_a, b_a, w_i, b_i, lam, *, name, tm=256):
    _, t, d = proj.shape
    hb = tm // SUBLANES
    ng = tm // SUBLANES

    def body(p_ref, halo_ref, cw_ref, cb_ref, wa_ref, ba_ref, wi_ref, bi_ref, lam_ref,
             xb_ref, r_ref, ig_ref, a_ref, h_ref, y_ref, u_scr, hc_scr):
        first = pl.program_id(0) == 0

        @pl.when(first)
        def _():
            hc_scr[...] = jnp.zeros_like(hc_scr)

        halo = jnp.where(first, 0.0, halo_ref[0])
        xb = _conv_causal(p_ref[0], halo, cw_ref, cb_ref[...], CONV_B)
        xb_ref[...] = xb
        xbb = xb.astype(BF16)
        r = _sigmoid(_block_diag_dot(xbb, wa_ref) + ba_ref[...])
        ig = _sigmoid(_block_diag_dot(xbb, wi_ref) + bi_ref[...])
        r_ref[...] = r
        ig_ref[...] = ig
        log_a = (-LRU_C) * r * _softplus(-lam_ref[...])
        a_ref[...] = jnp.exp(log_a)
        u_scr[...] = jnp.sqrt(_neg_expm1(2.0 * log_a)) * (ig * xb)

        ones8 = jnp.ones((SUBLANES, d), F32)
        zeros8 = jnp.zeros((SUBLANES, d), F32)

        def group(gi, hprev):
            off = pl.multiple_of(gi * SUBLANES, SUBLANES)
            a8 = a_ref[pl.ds(off, SUBLANES), :]
            u8 = u_scr[pl.ds(off, SUBLANES), :]
            for s in (1, 2, 4):
                u8 = a8 * _shift_down(u8, zeros8, s) + u8
                a8 = a8 * _shift_down(a8, ones8, s)
            h8 = a8 * hprev + u8
            h_ref[pl.ds(off, SUBLANES), :] = h8
            return h8[SUBLANES - 1:SUBLANES, :]

        hc_scr[...] = lax.fori_loop(0, ng, group, hc_scr[...])
        y_ref[...] = (h_ref[...] * _gelu(p_ref[1])).astype(BF16)

    row = pl.BlockSpec((tm, d), lambda i: (i, 0))
    vec = pl.BlockSpec((1, d), lambda i: (0, 0))
    wblk = pl.BlockSpec((N_BLOCKS_B, BLOCK_B, BLOCK_B), lambda i: (0, 0, 0))
    f32_td = jax.ShapeDtypeStruct((t, d), F32)
    return pl.pallas_call(
        body, name=name, grid=(t // tm,),
        in_specs=[pl.BlockSpec((2, tm, d), lambda i: (0, i, 0)),
                  pl.BlockSpec((1, SUBLANES, d), lambda i: (0, jnp.maximum(i * hb - 1, 0), 0)),
                  pl.BlockSpec((CONV_B, d), lambda i: (0, 0)), vec, wblk, vec, wblk, vec, vec],
        out_specs=[row, row, row, row, row, row],
        out_shape=[f32_td, f32_td, f32_td, f32_td, f32_td, jax.ShapeDtypeStruct((t, d), BF16)],
        scratch_shapes=[pltpu.VMEM((tm, d), F32), pltpu.VMEM((1, d), F32)],
        compiler_params=_cparams(("arbitrary",)),
    )(proj, proj, conv_w, conv_b, w_a, b_a, w_i, b_i, lam)


def _rglru_bwd(dy, proj, xb, r, ig, a, h, conv_w, w_a, w_i, lam, *, name, tm=256):
    _, t, d = proj.shape
    hb = tm // SUBLANES
    ng = tm // SUBLANES
    nr = t // tm

    def body(dy_ref, p_ref, phalo_ref, xb_ref, r_ref, ig_ref, a_ref, h_ref, hhalo_ref,
             cw_ref, wa_ref, wi_ref, lam_ref,
             dp_ref, dcw_ref, dcb_ref, dba_ref, dbi_ref, dlam_ref, dwa_ref, dwi_ref,
             g_scr, carry_g, carry_dxb):
        i = pl.program_id(0)
        rpos = nr - 1 - i

        @pl.when(i == 0)
        def _():
            for ref in (dcw_ref, dcb_ref, dba_ref, dbi_ref, dlam_ref, dwa_ref, dwi_ref, carry_g, carry_dxb):
                ref[...] = jnp.zeros_like(ref)

        gb = p_ref[1]
        gel, dgel = _gelu_and_grad(gb)
        dyv = dy_ref[...]
        hv = h_ref[...]
        dp_ref[1] = (dyv * hv * dgel).astype(BF16)
        g_scr[...] = dyv * gel
        av = a_ref[...]

        ones8 = jnp.ones((SUBLANES, d), F32)
        zeros8 = jnp.zeros((SUBLANES, d), F32)

        def group(gi, cin):
            off = pl.multiple_of((ng - 1 - gi) * SUBLANES, SUBLANES)
            g8 = g_scr[pl.ds(off, SUBLANES), :]
            a8 = a_ref[pl.ds(off, SUBLANES), :]
            b8 = _shift_up(a8, ones8, 1)
            row8 = lax.broadcasted_iota(jnp.int32, (SUBLANES, d), 0)
            g8 = g8 + jnp.where(row8 == SUBLANES - 1, cin, 0.0)
            b8 = jnp.where(row8 == SUBLANES - 1, 0.0, b8)
            for s in (1, 2, 4):
                g8 = g8 + b8 * _shift_up(g8, zeros8, s)
                b8 = b8 * _shift_up(b8, zeros8, s)
            g_scr[pl.ds(off, SUBLANES), :] = g8
            return a8[0:1, :] * g8[0:1, :]

        carry_g[...] = lax.fori_loop(0, ng, group, carry_g[...])

        du = g_scr[...]
        hhalo = jnp.where(rpos == 0, 0.0, hhalo_ref[...])
        hprev = _shift_down(hv, hhalo, 1)
        da = du * hprev
        rv = r_ref[...]
        igv = ig_ref[...]
        xbv = xb_ref[...]
        sp = _softplus(-lam_ref[...])
        log_a = (-LRU_C) * rv * sp
        mult = jnp.sqrt(_neg_expm1(2.0 * log_a))
        dmult = du * (igv * xbv)
        dig = du * mult * xbv
        dxb = du * mult * igv
        dlog_a = da * av - dmult * (av * av) / mult
        dr = dlog_a * ((-LRU_C) * sp)
        dsp = jnp.sum(dlog_a * ((-LRU_C) * rv), axis=0, keepdims=True)
        dlam_ref[...] += dsp * (-_sigmoid(-lam_ref[...]))
        dra = dr * rv * (1.0 - rv)
        dia = dig * igv * (1.0 - igv)
        dba_ref[...] += jnp.sum(dra, axis=0, keepdims=True)
        dbi_ref[...] += jnp.sum(dia, axis=0, keepdims=True)
        drab = dra.astype(BF16)
        diab = dia.astype(BF16)
        xbb = xbv.astype(BF16)
        dxb = dxb + _block_diag_dot(drab, wa_ref, True) + _block_diag_dot(diab, wi_ref, True)
        tn = (((0,), (0,)), ((), ()))
        for n in range(N_BLOCKS_B):
            sl = slice(n * BLOCK_B, (n + 1) * BLOCK_B)
            dwa_ref[n] += lax.dot_general(xbb[:, sl], drab[:, sl], tn, preferred_element_type=F32)
            dwi_ref[n] += lax.dot_general(xbb[:, sl], diab[:, sl], tn, preferred_element_type=F32)

        xpre = p_ref[0]
        phalo = jnp.where(rpos == 0, 0.0, phalo_ref[0])
        nxt = carry_dxb[...]
        dcb_ref[...] += jnp.sum(dxb, axis=0, keepdims=True)
        out = dxb * cw_ref[CONV_B - 1:CONV_B, :]
        dcw_ref[CONV_B - 1:CONV_B, :] += jnp.sum(dxb * xpre, axis=0, keepdims=True)
        for k in range(CONV_B - 1):
            sh = CONV_B - 1 - k
            out = out + _shift_up(dxb, nxt, sh) * cw_ref[k:k + 1, :]
            dcw_ref[k:k + 1, :] += jnp.sum(dxb * _shift_down(xpre, phalo, sh), axis=0, keepdims=True)
        dp_ref[0] = out.astype(BF16)
        carry_dxb[...] = dxb[:SUBLANES]

    rev = pl.BlockSpec((tm, d), lambda i: (nr - 1 - i, 0))
    halo8 = pl.BlockSpec((SUBLANES, d), lambda i: (jnp.maximum((nr - 1 - i) * hb - 1, 0), 0))
    vec = pl.BlockSpec((1, d), lambda i: (0, 0))
    wblk = pl.BlockSpec((N_BLOCKS_B, BLOCK_B, BLOCK_B), lambda i: (0, 0, 0))
    vec_shape = jax.ShapeDtypeStruct((1, d), F32)
    w_shape = jax.ShapeDtypeStruct((N_BLOCKS_B, BLOCK_B, BLOCK_B), F32)
    return pl.pallas_call(
        body, name=name, grid=(nr,),
        in_specs=[rev, pl.BlockSpec((2, tm, d), lambda i: (0, nr - 1 - i, 0)),
                  pl.BlockSpec((1, SUBLANES, d), lambda i: (0, jnp.maximum((nr - 1 - i) * hb - 1, 0), 0)),
                  rev, rev, rev, rev, rev, halo8,
                  pl.BlockSpec((CONV_B, d), lambda i: (0, 0)), wblk, wblk, vec],
        out_specs=[pl.BlockSpec((2, tm, d), lambda i: (0, nr - 1 - i, 0)),
                   pl.BlockSpec((CONV_B, d), lambda i: (0, 0)), vec, vec, vec, vec, wblk, wblk],
        out_shape=[jax.ShapeDtypeStruct((2, t, d), BF16), jax.ShapeDtypeStruct((CONV_B, d), F32),
                   vec_shape, vec_shape, vec_shape, vec_shape, w_shape, w_shape],
        scratch_shapes=[pltpu.VMEM((tm, d), F32), pltpu.VMEM((1, d), F32), pltpu.VMEM((SUBLANES, d), F32)],
        compiler_params=_cparams(("arbitrary",)),
    )(dy, proj, proj, xb, r, ig, a, h, h, conv_w, w_a, w_i, lam)


def _split_cols(w, s):
    k, c = w.shape[-2:]
    return jnp.moveaxis(w.reshape(w.shape[:-2] + (k, s, c // s)), -2, -3)


def _merge_cols(w):
    s, k, c = w.shape
    return jnp.moveaxis(w, 0, 1).reshape(k, s * c)


def _local_step(x, p, tgt, w):
    t = x.shape[0]
    bf = lambda v: v.astype(BF16)
    saved = []
    xcur = x
    xcur_b = bf(x)
    for i in range(DEPTH):
        j = i // 2
        L = f"l{i}_"
        sv = {'x_in_b': xcur_b}
        if i % 2 == 0:
            w_in_t = bf(w['a_w_in'][j])
            sv['w_qkv_t'] = w_in_t[:3 * D_MODEL]
            sv['w_f'] = jnp.pad(w_in_t[3 * D_MODEL:].T, ((0, 0), (0, LANES - N_HEADS)))
            sv['w_out'] = bf(w['a_w_out'][j])
            b_f = jnp.pad(w['a_b_f'][j], (0, LANES - N_HEADS)).reshape(1, LANES)
            qkv = _mm(xcur_b, sv['w_qkv_t'], tb=True, out_dtype=BF16, name=L + "qkv")
            z, c = _fgate_fwd(xcur_b, sv['w_f'], b_f, name=L + "fgate")
            qkva = _attn_prep(qkv, c, name=L + "attn_prep")
            o, qb = _flash_fwd_hm(qkva, name=L + "flash_fwd")
            o_c = _attn_compact([o], name=L + "o_compact")
            mix = _mm(o_c, sv['w_out'], name=L + "attn_out")
            sv.update(qkva=qkva, z=z, o=o, o_c=o_c, qb=qb)
        else:
            sv['w_in'] = bf(w['b_w_in'][j])
            sv['w_out'] = bf(w['b_w_out'][j])
            sv['conv_w'] = w['b_conv_w'][j]
            sv['w_a'] = bf(w['b_w_a'][j])
            sv['w_i'] = bf(w['b_w_i'][j])
            sv['lam'] = w['b_lam'][j].reshape(1, D_MODEL)
            proj = _mm(xcur_b, sv['w_in'], out_split=2, name=L + "rg_in")
            xb, r, ig, a, h, y = _rglru_fwd(
                proj, sv['conv_w'], w['b_conv_b'][j].reshape(1, D_MODEL), sv['w_a'],
                w['b_b_a'][j].reshape(1, D_MODEL), sv['w_i'], w['b_b_i'][j].reshape(1, D_MODEL), sv['lam'],
                name=L + "rglru_fwd")
            mix = _mm(y, sv['w_out'], name=L + "rg_out")
            sv.update(proj=proj, xb=xb, r=r, ig=ig, a=a, h=h, y=y)
        sv['ln1_g'] = w['ln1_g'][i].reshape(1, D_MODEL)
        x1, x1b, xhat1, rstd1 = _ln_fwd(xcur, mix, sv['ln1_g'], w['ln1_b'][i].reshape(1, D_MODEL), name=L + "ln1")
        sv['w_up'] = bf(w['f_w_up'][i])
        sv['w_down'] = bf(w['f_w_down'][i])
        sv['fconv_w'] = _split_cols(w['f_conv_w'][i], 2)
        sv['fconv_b'] = w['f_conv_b'][i].reshape(2, 1, D_FF)
        hdn = _mm(x1b, sv['w_up'], out_split=2, tn_cap=1408, name=L + "ffn_up")
        act = _ffn_act_fwd(hdn, sv['fconv_w'], sv['fconv_b'], name=L + "ffn_act")
        ff = _mm(act, sv['w_down'], tk_cap=2816, name=L + "ffn_down")
        sv['ln2_g'] = w['ln2_g'][i].reshape(1, D_MODEL)
        x2, x2b, xhat2, rstd2 = _ln_fwd(x1, ff, sv['ln2_g'], w['ln2_b'][i].reshape(1, D_MODEL), name=L + "ln2")
        sv['gate_w'] = bf(w['ple_gate_w'][i])
        sv['ple_w'] = bf(w['ple_w'][i])
        sv['gate_b'] = w['ple_gate_b'][i].reshape(1, D_MODEL)
        sv['p_b'] = bf(p[i])
        gl = _mm(x2b, sv['gate_w'], name=L + "ple_gate")
        pe = _mm(sv['p_b'], sv['ple_w'], name=L + "ple_emb")
        x3, x3b = _ple_fwd(x2, gl, pe, sv['gate_b'], name=L + "ple")
        sv.update(xhat1=xhat1, rstd1=rstd1, x1b=x1b, hdn=hdn, act=act, xhat2=xhat2, rstd2=rstd2, x2b=x2b,
                  gl=gl, pe=pe)
        saved.append(sv)
        xcur, xcur_b = x3, x3b

    dx, loss_row = _loss_bwd(xcur, tgt, name="loss")

    g = {n: [None] * w[n].shape[0] for n in WEIGHTS}
    for i in reversed(range(DEPTH)):
        j = i // 2
        L = f"l{i}b_"
        sv = saved[i]
        dgl, dpe, d_gate_b = _ple_bwd(dx, sv['gl'], sv['pe'], sv['gate_b'], name=L + "ple")
        g['ple_gate_b'][i] = d_gate_b[0]
        g['ple_w'][i] = _mm(sv['p_b'], dpe, ta=True, name=L + "ple_emb_dw")
        g['ple_gate_w'][i] = _mm(sv['x2b'], dgl, ta=True, name=L + "ple_gate_dw")
        dx2 = _mm(dgl, sv['gate_w'], tb=True, add=dx, name=L + "ple_gate_dx")
        dz2, dz2b, dg2, db2 = _ln_bwd(dx2, sv['xhat2'], sv['rstd2'], sv['ln2_g'], name=L + "ln2")
        g['ln2_g'][i], g['ln2_b'][i] = dg2[0], db2[0]
        g['f_w_down'][i] = _mm(sv['act'], dz2b, ta=True, tm_cap=1408, name=L + "ffn_down_dw")
        da = _mm(dz2b, sv['w_down'], tb=True, tn_cap=1408, name=L + "ffn_down_dx")
        dhdn, d_fcw, d_fcb = _ffn_act_bwd(da, sv['hdn'], sv['fconv_w'], sv['fconv_b'], name=L + "ffn_act")
        g['f_conv_w'][i] = _merge_cols(d_fcw)
        g['f_conv_b'][i] = d_fcb.reshape(2 * D_FF)
        g['f_w_up'][i] = _mm(sv['x1b'], dhdn, ta=True, tn_cap=1408, name=L + "ffn_up_dw")
        dx1 = _mm(dhdn, sv['w_up'], tb=True, add=dz2, add_scale=ALPHA, tk_cap=D_FF, name=L + "ffn_up_dx")
        dz1, dz1b, dg1, db1 = _ln_bwd(dx1, sv['xhat1'], sv['rstd1'], sv['ln1_g'], name=L + "ln1")
        g['ln1_g'][i], g['ln1_b'][i] = dg1[0], db1[0]
        if i % 2 == 0:
            g['a_w_out'][j] = _mm(sv['o_c'], dz1b, ta=True, name=L + "attn_out_dw")
            do = _mm(dz1b, sv['w_out'], tb=True, out_dtype=BF16, name=L + "attn_out_dx")
            doa = _attn_prep_bwd(sv['o'], do, name=L + "attn_prep")
            dqkv = _flash_bwd_hm(sv['qb'], doa, sv['qkva'], name=L + "flash_bwd")
            dcq, dcj = dqkv[3], dqkv[4]
            dc = jnp.pad((dcq[:, 0, :] - dcj[:, LANE_ONE - LANE_C, :]).T, ((0, 0), (0, LANES - N_HEADS)))
            dzf, d_b_f = _fgate_bwd(dc, sv['z'], name=L + "fgate")
            g['a_b_f'][j] = d_b_f[0, :N_HEADS]
            xb_in = sv['x_in_b']
            dqkv_c = _attn_compact(dqkv[:3], name=L + "attn_compact")
            d_w = _mm(dqkv_c, xb_in, ta=True, name=L + "qkv_dw")
            d_wf = _mm(xb_in, dzf, ta=True, name=L + "f_dw")
            g['a_w_in'][j] = jnp.concatenate([d_w, d_wf[:, :N_HEADS].T], axis=0)
            dxa = _mm(dqkv_c, sv['w_qkv_t'], add=dz1, add_scale=ALPHA, name=L + "qkv_dx")
            dx = _mm(dzf, sv['w_f'], tb=True, add=dxa, name=L + "f_dx")
        else:
            g['b_w_out'][j] = _mm(sv['y'], dz1b, ta=True, name=L + "rg_out_dw")
            dy = _mm(dz1b, sv['w_out'], tb=True, name=L + "rg_out_dx")
            dproj, d_cw, d_cb, d_ba, d_bi, d_lam, d_wa, d_wi = _rglru_bwd(
                dy, sv['proj'], sv['xb'], sv['r'], sv['ig'], sv['a'], sv['h'], sv['conv_w'], sv['w_a'], sv['w_i'],
                sv['lam'], name=L + "rglru_bwd")
            g['b_conv_w'][j], g['b_conv_b'][j] = d_cw, d_cb[0]
            g['b_b_a'][j] = d_ba.reshape(N_BLOCKS_B, BLOCK_B)
            g['b_b_i'][j] = d_bi.reshape(N_BLOCKS_B, BLOCK_B)
            g['b_lam'][j] = d_lam[0]
            g['b_w_a'][j], g['b_w_i'][j] = d_wa, d_wi
            g['b_w_in'][j] = _mm(sv['x_in_b'], dproj, ta=True, name=L + "rg_in_dw")
            dx = _mm(dproj, sv['w_in'], tb=True, add=dz1, add_scale=ALPHA, name=L + "rg_in_dx")
    return loss_row, dx, g


def _round_up(n, q):
    return -(-n // q) * q


def _pack(arrs, dtype, row_multiple, lead=0):
    pieces = []
    for a in arrs:
        flat = a.reshape(a.shape[:lead] + (-1,)).astype(dtype)
        n = flat.shape[-1]
        pieces.append(jnp.pad(flat, [(0, 0)] * lead + [(0, _round_up(n, LANES) - n)]))
    flat = jnp.concatenate(pieces, axis=-1)
    rows = _round_up(flat.shape[-1] // LANES, row_multiple)
    flat = jnp.pad(flat, [(0, 0)] * lead + [(0, rows * LANES - flat.shape[-1])])
    return flat.reshape(flat.shape[:lead] + (rows, LANES))


def _unpack(buf, shapes):
    lead = buf.shape[:-2]
    flat = buf.reshape(lead + (-1,))
    out, off = [], 0
    for shp in shapes:
        n = math.prod(shp)
        out.append(flat[..., off:off + n].reshape(lead + tuple(shp)))
        off += _round_up(n, LANES)
    return out


MESH = pl.DeviceIdType.MESH
ANY = pl.BlockSpec(memory_space=pl.ANY)
N_CHUNK = 8


def _all_gather_shards(buf):
    rows, lanes = buf.shape
    half = rows // 2
    ch = half // N_CHUNK
    n_ici = 3 * N_CHUNK

    def body(in_ref, out_ref, send_sems, recv_sems):
        x, y, c = lax.axis_index("x"), lax.axis_index("y"), lax.axis_index("c")
        sibling = (x, y, 1 - c)
        chips = [(1 - x, y), (x, 1 - y), (1 - x, 1 - y)]

        def piece(cx, cy, hc, q):
            return out_ref.at[2 * cx + cy, pl.ds(hc * half + q * ch, ch), :]

        def copy(k, src, dst, to):
            return pltpu.make_async_remote_copy(src_ref=src, dst_ref=dst, send_sem=send_sems.at[k],
                                                recv_sem=recv_sems.at[k], device_id=to, device_id_type=MESH)

        my_chunk = lambda q: in_ref.at[pl.ds(c * half + q * ch, ch), :]
        first, passed = [], []
        for k, chip in enumerate(chips):
            for q in range(N_CHUNK):
                first.append(copy(k * N_CHUNK + q, my_chunk(q), piece(x, y, c, q), (*chip, c)))
                passed.append(copy(n_ici + k * N_CHUNK + q, piece(*chip, c, q), piece(*chip, c, q), sibling))
        for cp in first:
            cp.start()
        for k, chip in enumerate(chips):
            for q in range(N_CHUNK):
                n = k * N_CHUNK + q
                copy(n, my_chunk(q), piece(*chip, c, q), (*chip, c)).wait_recv()
                passed[n].start()
        for k, chip in enumerate(chips):
            for q in range(N_CHUNK):
                copy(n_ici + k * N_CHUNK + q, my_chunk(q), piece(*chip, 1 - c, q), sibling).wait_recv()
        for cp in first + passed:
            cp.wait_send()

    others = pl.pallas_call(
        body, name="gather_weights", in_specs=[ANY], out_specs=ANY,
        out_shape=jax.ShapeDtypeStruct((N_CHIP, rows, lanes), buf.dtype),
        scratch_shapes=[pltpu.SemaphoreType.DMA((2 * n_ici,)), pltpu.SemaphoreType.DMA((2 * n_ici,))],
    )(buf)
    return lax.dynamic_update_slice(others, buf[None], (2 * lax.axis_index("x") + lax.axis_index("y"), 0, 0))


def _exchange_many(arrs, *, per_chip, name):
    na = len(arrs)
    flips = (2, 4, 6) if per_chip else tuple(range(1, N_DEV))
    index_of = (lambda x, y, c: 2 * x + y) if per_chip else (lambda x, y, c: 4 * x + 2 * y + c)

    def body(*refs):
        ins, outs = refs[:na], refs[na:2 * na]
        send_sems, recv_sems = refs[2 * na:]
        x, y, c = lax.axis_index("x"), lax.axis_index("y"), lax.axis_index("c")
        me = index_of(x, y, c)
        copies = []
        for nk, k in enumerate(flips):
            px, py, pc = x ^ (k >> 2), y ^ ((k >> 1) & 1), c ^ (k & 1)
            peer = index_of(px, py, pc)
            for a in range(na):
                n = nk * na + a
                copies.append(pltpu.make_async_remote_copy(
                    src_ref=ins[a].at[peer], dst_ref=outs[a].at[me], send_sem=send_sems.at[n],
                    recv_sem=recv_sems.at[n], device_id=(px, py, pc), device_id_type=MESH))
        for cp in copies:
            cp.start()
        for cp in copies:
            cp.wait()

    n_remote = len(flips) * na
    outs = pl.pallas_call(
        body, name=name, in_specs=[ANY] * na, out_specs=[ANY] * na,
        out_shape=[jax.ShapeDtypeStruct(a.shape, a.dtype) for a in arrs],
        scratch_shapes=[pltpu.SemaphoreType.DMA((n_remote,)), pltpu.SemaphoreType.DMA((n_remote,))],
    )(*arrs)
    me = index_of(lax.axis_index("x"), lax.axis_index("y"), lax.axis_index("c"))
    return [lax.dynamic_update_slice(o, lax.dynamic_index_in_dim(a, me, 0, keepdims=True), (me, 0, 0))
            for o, a in zip(outs, arrs)]


def _swap_with_sibling(arrs):
    na = len(arrs)

    def body(*refs):
        ins, outs = refs[:na], refs[na:2 * na]
        send_sems, recv_sems = refs[2 * na:]
        x, y, c = lax.axis_index("x"), lax.axis_index("y"), lax.axis_index("c")
        copies = [pltpu.make_async_remote_copy(
            src_ref=ins[a].at[:, 1 - c], dst_ref=outs[a], send_sem=send_sems.at[a], recv_sem=recv_sems.at[a],
            device_id=(x, y, 1 - c), device_id_type=MESH) for a in range(na)]
        for cp in copies:
            cp.start()
        for cp in copies:
            cp.wait()

    return pl.pallas_call(
        body, name="swap_pieces", in_specs=[ANY] * na, out_specs=[ANY] * na,
        out_shape=[jax.ShapeDtypeStruct((a.shape[0],) + a.shape[2:], a.dtype) for a in arrs],
        scratch_shapes=[pltpu.SemaphoreType.DMA((na,)), pltpu.SemaphoreType.DMA((na,))],
    )(*arrs)


def _add_pair(a, b, *, name):
    n, r, c = a.shape
    tm = _tile(r, max(2 * SUBLANES, SUM_BLOCK_ELEMS // c), 2 * SUBLANES)

    def body(a_ref, b_ref, o_ref):
        o_ref[...] = (a_ref[...].astype(F32) + b_ref[...].astype(F32)).astype(BF16)

    spec = pl.BlockSpec((n, tm, c), lambda i: (0, i, 0))
    return pl.pallas_call(
        body, name=name, grid=(r // tm,), in_specs=[spec, spec], out_specs=spec,
        out_shape=jax.ShapeDtypeStruct(a.shape, BF16), compiler_params=_cparams(("parallel",)),
    )(a, b)


def _share_many(reds, halves):
    na = len(reds)

    def body(*refs):
        ins, outs = refs[:na], refs[na:2 * na]
        send_sems, recv_sems = refs[2 * na:]
        x, y, c = lax.axis_index("x"), lax.axis_index("y"), lax.axis_index("c")
        copies = []
        for a in range(na):
            h = halves[a]
            copies.append(pltpu.make_async_remote_copy(
                src_ref=ins[a].at[pl.ds(0, h), :], dst_ref=outs[a].at[pl.ds(c * h, h), :], send_sem=send_sems.at[a],
                recv_sem=recv_sems.at[a], device_id=(x, y, 1 - c), device_id_type=MESH))
        for cp in copies:
            cp.start()
        for cp in copies:
            cp.wait()

    outs = pl.pallas_call(
        body, name="share_halves", in_specs=[ANY] * na, out_specs=[ANY] * na,
        out_shape=[jax.ShapeDtypeStruct((r.shape[0] + h, r.shape[1]), r.dtype) for r, h in zip(reds, halves)],
        scratch_shapes=[pltpu.SemaphoreType.DMA((na,)), pltpu.SemaphoreType.DMA((na,))],
    )(*reds)
    c = lax.axis_index("c")
    full = []
    for o, r, h in zip(outs, reds, halves):
        o = lax.dynamic_update_slice(o, r[:h], (c * h, 0))
        if r.shape[0] > h:
            o = lax.dynamic_update_slice(o, r[h:], (2 * h, 0))
        full.append(o)
    return full


SUM_BLOCK_ELEMS = 256 * 1024
ADAM_BLOCK_ELEMS = 256 * 1024


def _sum_slots(slots, *, name="sum_grads"):
    n, prow, lanes = slots.shape
    tm = _tile(prow, max(2 * SUBLANES, SUM_BLOCK_ELEMS // lanes), 2 * SUBLANES)

    def body(s_ref, o_ref):
        acc = s_ref[0].astype(F32)
        for s in range(1, n):
            acc = acc + s_ref[s].astype(F32)
        o_ref[...] = acc

    return pl.pallas_call(
        body, name=name, grid=(prow // tm,),
        in_specs=[pl.BlockSpec((n, tm, lanes), lambda i: (0, i, 0))],
        out_specs=pl.BlockSpec((tm, lanes), lambda i: (i, 0)),
        out_shape=jax.ShapeDtypeStruct((prow, lanes), F32),
        compiler_params=_cparams(("parallel",)),
    )(slots)


def _adamw(wp, gp, mp, vp, *, name="adamw"):
    rows, lanes = wp.shape
    tm = _tile(rows, max(SUBLANES, ADAM_BLOCK_ELEMS // lanes), SUBLANES)
    c1 = 1.0 / (1.0 - ADAM_B1 ** ADAM_STEP)
    c2 = 1.0 / (1.0 - ADAM_B2 ** ADAM_STEP)

    def body(w_ref, g_ref, m_ref, v_ref, d_ref, nm_ref, nv_ref):
        g = g_ref[...]
        m = ADAM_B1 * m_ref[...] + (1.0 - ADAM_B1) * g
        v = ADAM_B2 * v_ref[...] + (1.0 - ADAM_B2) * (g * g)
        m_hat = m * c1
        v_hat = v * c2
        d_ref[...] = -ADAM_LR * (m_hat / (jnp.sqrt(v_hat) + ADAM_EPS) + ADAM_WD * w_ref[...])
        nm_ref[...] = m
        nv_ref[...] = v

    spec = pl.BlockSpec((tm, lanes), lambda i: (i, 0))
    shp = jax.ShapeDtypeStruct((rows, lanes), F32)
    return pl.pallas_call(
        body, name=name, grid=(rows // tm,), in_specs=[spec] * 4, out_specs=[spec] * 3,
        out_shape=[shp, shp, shp], compiler_params=_cparams(("parallel",)),
    )(wp, gp, mp, vp)


AG_ROW_MULT = 2 * N_CHUNK * 16
GRAD_ROW_MULT = 2048
REP_ROW_MULT = 512


def _shard_to_front(a, axis):
    n = a.shape[axis]
    a = a.reshape(a.shape[:axis] + (N_CHIP, n // N_CHIP) + a.shape[axis + 1:])
    return jnp.moveaxis(a, axis, 0)


def _shards_to_full(a, axis):
    a = jnp.moveaxis(a, 0, axis)
    return a.reshape(a.shape[:axis] + (a.shape[axis] * a.shape[axis + 1],) + a.shape[axis + 2:])


def kernel(x, p, a_w_in, a_b_f, a_w_out, b_w_in, b_conv_w, b_conv_b, b_w_a, b_b_a, b_w_i, b_b_i, b_lam, b_w_out, f_w_up, f_conv_w, f_conv_b, f_w_down, ln1_g, ln1_b, ln2_g, ln2_b, ple_w, ple_gate_w, ple_gate_b, loss_target, m_a_w_in, m_a_b_f, m_a_w_out, m_b_w_in, m_b_conv_w, m_b_conv_b, m_b_w_a, m_b_b_a, m_b_w_i, m_b_b_i, m_b_lam, m_b_w_out, m_f_w_up, m_f_conv_w, m_f_conv_b, m_f_w_down, m_ln1_g, m_ln1_b, m_ln2_g, m_ln2_b, m_ple_w, m_ple_gate_w, m_ple_gate_b, v_a_w_in, v_a_b_f, v_a_w_out, v_b_w_in, v_b_conv_w, v_b_conv_b, v_b_w_a, v_b_b_a, v_b_w_i, v_b_b_i, v_b_lam, v_b_w_out, v_f_w_up, v_f_conv_w, v_f_conv_b, v_f_w_down, v_ln1_g, v_ln1_b, v_ln2_g, v_ln2_b, v_ple_w, v_ple_gate_w, v_ple_gate_b):
    args = dict(locals())
    swap = lambda n, a: jnp.swapaxes(a, -1, -2) if n in TRANSPOSED else a
    axis_of = lambda n: (3 - SHARD_AXIS[n]) if n in TRANSPOSED else SHARD_AXIS[n]
    local_w = {n: swap(n, args[n]) for n in WEIGHTS}
    local_m = {n: swap(n, args['m_' + n]) for n in WEIGHTS}
    local_v = {n: swap(n, args['v_' + n]) for n in WEIGHTS}

    as_pairs = lambda a: lax.bitcast_convert_type(a, BF16)
    from_pairs = lambda a: lax.bitcast_convert_type(a, F32)
    send = [local_w[n] for n in GATHER_BF16] + [as_pairs(local_w[n]) for n in GATHER_F32]
    gathered = _all_gather_shards(_pack(send, BF16, AG_ROW_MULT))
    shapes = [local_w[n].shape for n in GATHER_BF16] + [local_w[n].shape + (2,) for n in GATHER_F32]
    parts = _unpack(gathered, shapes)
    full_w = {}
    for n, part in zip(GATHER_BF16 + GATHER_F32, parts):
        if n in GATHER_F32:
            part = from_pairs(part)
        full_w[n] = _shards_to_full(part, axis_of(n))
    for n in REPLICATED:
        full_w[n] = local_w[n]

    loss_row, grad_x, g = _local_step(x[0], p[:, 0], loss_target[0], full_w)

    grads = {n: jnp.stack(g[n]) for n in SHARDED}
    small = [n for n in SHARDED if n not in NATIVE]
    shard_rows = _round_up(sum(_round_up(local_w[n].size, LANES) for n in small) // LANES, GRAD_ROW_MULT)
    half_rows = shard_rows // 2

    def native_pieces(n):
        a = _shard_to_front(grads[n], axis_of(n))
        return a.reshape(N_DEV, -1, a.shape[-1]).astype(BF16)

    sharded_g = _pack([_shard_to_front(grads[n], axis_of(n)) for n in small], BF16, GRAD_ROW_MULT, lead=1)
    rep_g = _pack([jnp.stack(g[n]) for n in REPLICATED] + [loss_row], F32, REP_ROW_MULT)
    rep_rows = rep_g.shape[0]
    rep_top = lax.reduce_precision(rep_g, 8, 7)
    rep_hi = rep_top.astype(BF16)
    rep_lo = (rep_g - rep_top).astype(BF16)
    packed_pieces = jnp.concatenate(
        [sharded_g.reshape(N_DEV, half_rows, LANES),
         jnp.broadcast_to(jnp.concatenate([rep_hi, rep_lo]), (N_DEV, 2 * rep_rows, LANES))], axis=1)
    c = lax.axis_index("c")
    mine = [native_pieces(n) for n in NATIVE]
    theirs = _swap_with_sibling([p.reshape((N_CHIP, 2) + p.shape[1:]) for p in mine])
    chip_sums = [_add_pair(lax.dynamic_index_in_dim(p.reshape((N_CHIP, 2) + p.shape[1:]), c, 1, keepdims=False), t,
                           name="pair_" + n) for p, t, n in zip(mine, theirs, NATIVE)]
    slots = _exchange_many(chip_sums, per_chip=True, name="exchange_grads")
    slots += _exchange_many([packed_pieces], per_chip=False, name="exchange_packed")
    reduced = [_sum_slots(s, name="sum_" + n) for s, n in zip(slots, NATIVE + ["packed"])]
    reduced[-1] = jnp.concatenate([reduced[-1][:half_rows], reduced[-1][half_rows:half_rows + rep_rows]
                                   + reduced[-1][half_rows + rep_rows:]])
    full_g = _share_many(reduced, [r.shape[0] for r in reduced[:-1]] + [half_rows])

    rep_shapes = [local_w[n].shape for n in REPLICATED] + [loss_row.shape]

    def packed(d):
        return jnp.concatenate([_pack([d[n] for n in small], F32, GRAD_ROW_MULT),
                                _pack([d[n] for n in REPLICATED] + [jnp.zeros_like(loss_row)], F32, REP_ROW_MULT)])

    def unpacked(buf):
        d = dict(zip(small, _unpack(buf[:shard_rows], [local_w[n].shape for n in small])))
        d.update(zip(REPLICATED, _unpack(buf[shard_rows:], rep_shapes)))
        return d

    outs = [unpacked(b) for b in (full_g[-1],) + tuple(_adamw(packed(local_w), full_g[-1], packed(local_m),
                                                              packed(local_v), name="adamw_packed"))]
    for n, gn in zip(NATIVE, full_g):
        two_d = lambda a: a.reshape(gn.shape)
        res = _adamw(two_d(local_w[n]), gn, two_d(local_m[n]), two_d(local_v[n]), name="adamw_" + n)
        for d, a in zip(outs, (gn,) + tuple(res)):
            d[n] = a.reshape(local_w[n].shape)
    loss = _unpack(full_g[-1][shard_rows:], rep_shapes)[-1][0, 0]
    return (loss, grad_x[None], *[swap(n, d[n]) for d in outs for n in WEIGHTS])
```
